```python
import jax, jax.numpy as jnp
from jax import lax
import numpy as np

D_MODEL = 1024
BATCH = 8
SEQ = 2048
DEPTH = 2
DEC_BATCH = 128
DEC_SEQ = 4
PAST_LEN = 2048
PAGE_SIZE = 128

D_PLE = 256
N_BRANCH = 3
W_BRANCH = D_MODEL // 2
NORM_EPS = 1e-6
DK_A = 128
DV_A = 128
H_A = W_BRANCH // DV_A
QKV_A = H_A * (2 * DK_A + DV_A)
CONV_W = 4
CHUNK_A = 64
N_B = 64
H_B = W_BRANCH // N_B
LORA_W = 64
LORA_A = 64
SHIFT_B = 3 * W_BRANCH + LORA_W + LORA_A
GN_EPS = 64e-5
D_C = 64
H_C = W_BRANCH // D_C
MOBA_BLOCK = 256
MOBA_TOPK = 3
ROPE_DIM = D_C // 4
ROPE_THETA = 500000.0
Q_BLOCK = 128
IN_SPLITS = (QKV_A, W_BRANCH, H_A, H_A, SHIFT_B, W_BRANCH, 3 * W_BRANCH, W_BRANCH, N_BRANCH * D_MODEL)
D_IN = sum(IN_SPLITS)
F32 = jnp.float32

kernel_name = 'hybrid_deltanet_rwkv7_moba_decode_step'


def rmsnorm(x, g, eps=NORM_EPS):
    xf = x.astype(F32)
    y = xf * lax.rsqrt(jnp.mean(xf * xf, axis=-1, keepdims=True) + eps)
    return (y * g.astype(F32)).astype(x.dtype)


def l2norm(x, eps=1e-6):
    xf = x.astype(F32)
    return (xf * lax.rsqrt(jnp.sum(xf * xf, axis=-1, keepdims=True) + eps)).astype(x.dtype)


def rope_partial(x, pos):
    half = ROPE_DIM // 2
    inv = ROPE_THETA ** (-jnp.arange(half, dtype=F32) / half)
    ang = pos.astype(F32)[:, None] * inv[None, :]
    cos = jnp.cos(ang)[None, :, None, :]
    sin = jnp.sin(ang)[None, :, None, :]
    xr = x[..., :ROPE_DIM].astype(F32)
    x1, x2 = xr[..., :half], xr[..., half:]
    rot = jnp.concatenate([x1 * cos - x2 * sin, x2 * cos + x1 * sin], axis=-1)
    return jnp.concatenate([rot.astype(x.dtype), x[..., ROPE_DIM:]], axis=-1)


def causal_conv(u, buf, w):
    L = u.shape[1]
    full = jnp.concatenate([buf.astype(u.dtype), u], axis=1)
    out = full[:, 0:L] * w[0]
    for j in range(1, CONV_W):
        out = out + full[:, j:j + L] * w[j]
    return out, full[:, L:]


def gated_delta_chunked(q, k, v, g, beta, s0):
    B, L, H, _ = q.shape
    DV = v.shape[-1]
    C = CHUNK_A if L % CHUNK_A == 0 else L
    N = L // C

    def chunks(t):
        t = t.astype(F32).reshape((B, N, C, H) + t.shape[3:])
        return jnp.moveaxis(t, (1, 3), (0, 2))

    qc, kc, vc, gc, bc = chunks(q), chunks(k), chunks(v), chunks(g), chunks(beta)
    G = jnp.cumsum(gc, axis=-1)
    tri_incl = jnp.tril(jnp.ones((C, C), bool))
    tri_strict = jnp.tril(jnp.ones((C, C), bool), -1)
    decay = jnp.exp(jnp.where(tri_incl, G[..., :, None] - G[..., None, :], -jnp.inf))
    kk = jnp.einsum('nbhid,nbhjd->nbhij', kc, kc)
    lmat = jnp.where(tri_strict, bc[..., :, None] * kk * decay, 0.0)
    a_mat = jnp.eye(C, dtype=F32) + lmat
    rhs = jnp.concatenate([vc * bc[..., None], kc * (bc * jnp.exp(G))[..., None]], axis=-1)
    sol = lax.linalg.triangular_solve(a_mat, rhs, left_side=True, lower=True, unit_diagonal=True)
    bv, wk = sol[..., :DV], sol[..., DV:]
    qk = jnp.where(tri_incl, jnp.einsum('nbhid,nbhjd->nbhij', qc, kc) * decay, 0.0)

    def step(S, inp):
        qi, ki, bvi, wki, qki, Gi = inp
        u = bvi - jnp.einsum('bhcd,bhde->bhce', wki, S)
        o = jnp.einsum('bhcd,bhde->bhce', qi * jnp.exp(Gi)[..., None], S) + jnp.einsum('bhij,bhje->bhie', qki, u)
        g_last = Gi[..., -1:]
        S = S * jnp.exp(g_last)[..., None] + jnp.einsum('bhcd,bhce->bhde', ki * jnp.exp(g_last - Gi)[..., None], u)
        return S, o

    s_fin, o = lax.scan(step, s0.astype(F32), (qc, kc, bv, wk, qk, G))
    o = jnp.moveaxis(o, (0, 2), (1, 3)).reshape(B, L, H, DV)
    return o, s_fin


def rwkv7_scan(r, logw, k, v, a, b, s0):
    def step(S, inp):
        rt, wt, kt, vt, at, bt = inp
        sa = jnp.einsum('bhvk,bhk->bhv', S, at)
        S = S * jnp.exp(wt)[:, :, None, :] + sa[..., None] * bt[:, :, None, :] + vt[..., None] * kt[:, :, None, :]
        return S, jnp.einsum('bhvk,bhk->bhv', S, rt)

    xs = tuple(jnp.moveaxis(t.astype(F32), 1, 0) for t in (r, logw, k, v, a, b))
    s_fin, y = lax.scan(step, s0.astype(F32), xs)
    return jnp.moveaxis(y, 0, 1), s_fin


def moba_attention(q, q_pos, k_all, v_all):
    B, Lq, H, D = q.shape
    Lk = k_all.shape[1]
    nb = -(-Lk // MOBA_BLOCK)
    pad = nb * MOBA_BLOCK - Lk

    def blocks(t):
        t = jnp.pad(t, ((0, 0), (0, pad), (0, 0), (0, 0)))
        return t.reshape(B, nb, MOBA_BLOCK, H, D).transpose(0, 3, 1, 2, 4)

    kb, vb = blocks(k_all), blocks(v_all)
    kmean = jnp.mean(kb.astype(F32), axis=3)
    ksel = min(MOBA_TOPK, nb)
    qb = Q_BLOCK if Lq % Q_BLOCK == 0 else Lq
    nq = Lq // qb
    qs = q.reshape(B, nq, qb, H, D).transpose(1, 0, 3, 2, 4)
    ps = q_pos.reshape(nq, qb)
    scale = D ** -0.5
    bidx = jnp.arange(B)[:, None, None, None]
    hidx = jnp.arange(H)[None, :, None, None]

    def one_block(args):
        qi, pi = args
        own = pi // MOBA_BLOCK
        gate = jnp.einsum('bhqd,bhnd->bhqn', qi.astype(F32), kmean)
        past = jnp.arange(nb)[None, :] < own[:, None]
        gate = jnp.where(past, gate, -jnp.inf)
        _, top = lax.top_k(gate, ksel)
        sel_ok = jnp.arange(ksel)[None, :] < own[:, None]
        own_idx = jnp.broadcast_to(own[None, None, :, None], top.shape[:3] + (1,))
        idx = jnp.concatenate([top, own_idx.astype(top.dtype)], axis=-1)
        kg = kb[bidx, hidx, idx]
        vg = vb[bidx, hidx, idx]
        s = jnp.einsum('bhqd,bhqskd->bhqsk', qi, kg).astype(F32) * scale
        kpos = idx[..., None] * MOBA_BLOCK + jnp.arange(MOBA_BLOCK)
        slot_ok = jnp.concatenate([sel_ok, jnp.ones((qb, 1), bool)], axis=-1)
        mask = slot_ok[None, None, :, :, None] & (kpos <= pi[None, None, :, None, None])
        s = jnp.where(mask, s, -jnp.inf)
        p = jax.nn.softmax(s.reshape(B, H, qb, -1), axis=-1).reshape(s.shape)
        return jnp.einsum('bhqsk,bhqskd->bhqd', p.astype(vg.dtype), vg)

    out = lax.map(one_block, (qs, ps))
    return out.transpose(1, 0, 3, 2, 4).reshape(B, Lq, H, D)


def gather_pages(pool, page_table):
    pages = pool[page_table]
    b, n, p, h, d = pages.shape
    return pages.reshape(b, n * p, h, d)


def mixer_layer(x, ple, pos, past_k, past_v, dn_s, dn_buf, rw_s, rw_prev, lp):
    B, L, _ = x.shape
    dt = x.dtype
    h = rmsnorm(x, lp['norm_g'])
    proj = h @ lp['w_in']
    offs = np.cumsum(IN_SPLITS)[:-1].tolist()
    qkv_a, z_a, beta_raw, alpha_raw, f_b, z_b, qkv_c, z_c, gate_raw = jnp.split(proj, offs, axis=-1)

    u_a, dn_buf_new = causal_conv(qkv_a, dn_buf, lp['conv_a'])
    u_a = jax.nn.silu(u_a)
    q_a, k_a, v_a = jnp.split(u_a, [H_A * DK_A, 2 * H_A * DK_A], axis=-1)
    q_a = l2norm(q_a.reshape(B, L, H_A, DK_A)) * (DK_A ** -0.5)
    k_a = l2norm(k_a.reshape(B, L, H_A, DK_A))
    v_a = v_a.reshape(B, L, H_A, DV_A)
    beta = jax.nn.sigmoid(beta_raw.astype(F32))
    g_a = -jnp.exp(lp['a_log'].astype(F32)) * jax.nn.softplus(alpha_raw.astype(F32) + lp['dt_bias'].astype(F32))
    o_a, dn_s_new = gated_delta_chunked(q_a, k_a, v_a, g_a, beta, dn_s)
    o_a = rmsnorm(o_a, lp['onorm_a']).reshape(B, L, W_BRANCH).astype(dt) * jax.nn.silu(z_a)

    prev = jnp.concatenate([rw_prev[:, None].astype(dt), f_b[:, :-1]], axis=1)
    f_mix = f_b + (prev - f_b) * lp['mu_b']
    r_b, k_b, v_b, wl_b, al_b = jnp.split(f_mix, [W_BRANCH, 2 * W_BRANCH, 3 * W_BRANCH, 3 * W_BRANCH + LORA_W], axis=-1)
    w_raw = (lp['w0_b'] + jnp.tanh(wl_b) @ lp['w2_b']).astype(F32)
    logw = -jnp.exp(-jax.nn.softplus(-w_raw) - 0.5)
    a_b = jax.nn.sigmoid((lp['a0_b'] + al_b @ lp['a2_b']).astype(F32))

    def heads(t):
        return t.reshape(B, L, H_B, N_B)

    kk = l2norm(heads(k_b.astype(F32) * lp['kk_b'].astype(F32)))
    k_mod = k_b.astype(F32) * (1.0 + (a_b - 1.0) * lp['ka_b'].astype(F32))
    r_h, k_h, v_h, a_h = heads(r_b.astype(F32)), heads(k_mod), heads(v_b.astype(F32)), heads(a_b)
    y_b, rw_s_new = rwkv7_scan(r_h, heads(logw), k_h, v_h, -kk, kk * a_h, rw_s)
    mu = jnp.mean(y_b, axis=-1, keepdims=True)
    var = jnp.mean(jnp.square(y_b - mu), axis=-1, keepdims=True)
    y_b = ((y_b - mu) * lax.rsqrt(var + GN_EPS)).reshape(B, L, W_BRANCH) * lp['gn_g_b'] + lp['gn_b_b']
    bonus = jnp.sum(r_h * k_h * lp['rk_b'], axis=-1, keepdims=True) * v_h
    o_b = (y_b + bonus.reshape(B, L, W_BRANCH)).astype(dt) * jax.nn.silu(z_b)
    rw_prev_new = f_b[:, -1]

    q_c, k_c, v_c = jnp.split(qkv_c, [W_BRANCH, 2 * W_BRANCH], axis=-1)
    q_c = rope_partial(rmsnorm(q_c.reshape(B, L, H_C, D_C), lp['qn_c']), pos)
    k_c = rope_partial(rmsnorm(k_c.reshape(B, L, H_C, D_C), lp['kn_c']), pos)
    v_c = v_c.reshape(B, L, H_C, D_C)
    if past_k is None:
        k_all, v_all = k_c, v_c
    else:
        k_all = jnp.concatenate([past_k.astype(dt), k_c], axis=1)
        v_all = jnp.concatenate([past_v.astype(dt), v_c], axis=1)
    o_c = moba_attention(q_c, pos, k_all, v_all).reshape(B, L, W_BRANCH) * jax.nn.silu(z_c)

    o = jnp.stack([o_a, o_b, o_c], axis=2)
    up = jnp.einsum('blnc,ncd->blnd', o, lp['w_branch'])
    gates = jax.nn.sigmoid(gate_raw + lp['b_gate']).reshape(B, L, N_BRANCH, D_MODEL)
    x = x + jnp.sum(gates * up, axis=2) @ lp['w_out']
    x = x + jax.nn.sigmoid(x @ lp['w_pg'] + lp['b_pg']) * (ple @ lp['w_ple'])
    return x, (k_c, v_c, dn_s_new, dn_buf_new, rw_s_new, rw_prev_new)


def setup_inputs(seed: int = 0) -> dict:
    key = jax.random.key(seed)
    keys = list(jax.random.split(key, 40))

    def nrm(shape, scale=1.0, shift=0.0):
        return jax.random.normal(keys.pop(), shape, F32) * scale + shift

    n_pages = PAST_LEN // PAGE_SIZE
    n_used = DEC_BATCH * n_pages
    n_phys = n_used + max(1, n_used // 4)
    page_table = jax.random.permutation(keys.pop(), n_phys)[:n_used].reshape(DEC_BATCH, n_pages).astype(jnp.int32)
    return {
        'x_prompt': nrm((BATCH, SEQ, D_MODEL)),
        'x_sample': nrm((DEC_BATCH, DEC_SEQ, D_MODEL)),
        'cache_k': nrm((DEPTH, n_phys, PAGE_SIZE, H_C, D_C)),
        'cache_v': nrm((DEPTH, n_phys, PAGE_SIZE, H_C, D_C)),
        'state_dn': nrm((DEPTH, DEC_BATCH, H_A, DK_A, DV_A), 0.5),
        'state_dn_conv': nrm((DEPTH, DEC_BATCH, CONV_W - 1, QKV_A)),
        'state_rwkv': nrm((DEPTH, DEC_BATCH, H_B, N_B, N_B), 0.5),
        'state_rwkv_shift': nrm((DEPTH, DEC_BATCH, SHIFT_B)),
        'page_table': page_table,
        'p_prompt': nrm((DEPTH, BATCH, SEQ, D_PLE)),
        'p_sample': nrm((DEPTH, DEC_BATCH, DEC_SEQ, D_PLE)),
        'norm_g': nrm((DEPTH, D_MODEL), 0.02, 1.0),
        'w_in': nrm((DEPTH, D_MODEL, D_IN), D_MODEL ** -0.5),
        'conv_a': nrm((DEPTH, CONV_W, QKV_A), CONV_W ** -0.5),
        'a_log': jnp.log(jax.random.uniform(keys.pop(), (DEPTH, H_A), F32, 1.0, 16.0)),
        'dt_bias': nrm((DEPTH, H_A), 0.5, -4.0),
        'onorm_a': nrm((DEPTH, DV_A), 0.02, 1.0),
        'mu_b': jax.random.uniform(keys.pop(), (DEPTH, SHIFT_B), F32),
        'w0_b': nrm((DEPTH, W_BRANCH), 0.5, -1.0),
        'w2_b': nrm((DEPTH, LORA_W, W_BRANCH), LORA_W ** -0.5),
        'a0_b': nrm((DEPTH, W_BRANCH), 0.1),
        'a2_b': nrm((DEPTH, LORA_A, W_BRANCH), LORA_A ** -0.5),
        'kk_b': nrm((DEPTH, W_BRANCH), 0.05, 0.85),
        'ka_b': nrm((DEPTH, W_BRANCH), 0.05, 1.0),
        'rk_b': nrm((DEPTH, H_B, N_B), 0.1),
        'gn_g_b': nrm((DEPTH, W_BRANCH), 0.02, 1.0),
        'gn_b_b': nrm((DEPTH, W_BRANCH), 0.02),
        'qn_c': nrm((DEPTH, D_C), 0.02, 1.0),
        'kn_c': nrm((DEPTH, D_C), 0.02, 1.0),
        'w_branch': nrm((DEPTH, N_BRANCH, W_BRANCH, D_MODEL), W_BRANCH ** -0.5),
        'b_gate': nrm((DEPTH, N_BRANCH * D_MODEL), 0.1),
        'w_out': nrm((DEPTH, D_MODEL, D_MODEL), D_MODEL ** -0.5),
        'w_ple': nrm((DEPTH, D_PLE, D_MODEL), D_PLE ** -0.5),
        'w_pg': nrm((DEPTH, D_MODEL, D_MODEL), D_MODEL ** -0.5),
        'b_pg': nrm((DEPTH, D_MODEL), 0.1),
    }


def reference(x_prompt, x_sample, cache_k, cache_v, state_dn, state_dn_conv, state_rwkv,
              state_rwkv_shift, page_table, p_prompt, p_sample, norm_g, w_in, conv_a, a_log,
              dt_bias, onorm_a, mu_b, w0_b, w2_b, a0_b, a2_b, kk_b, ka_b, rk_b, gn_g_b, gn_b_b,
              qn_c, kn_c, w_branch, b_gate, w_out, w_ple, w_pg, b_pg):
    b_p, l_p, _ = x_prompt.shape
    l_s = x_sample.shape[1]
    past_len = page_table.shape[1] * cache_k.shape[2]
    pos_p = jnp.arange(l_p, dtype=jnp.int32)
    pos_s = past_len + jnp.arange(l_s, dtype=jnp.int32)
    dt = x_prompt.dtype
    zero_dn = jnp.zeros((b_p, H_A, DK_A, DV_A), dt)
    zero_conv = jnp.zeros((b_p, CONV_W - 1, QKV_A), dt)
    zero_rw = jnp.zeros((b_p, H_B, N_B, N_B), dt)
    zero_shift = jnp.zeros((b_p, SHIFT_B), dt)
    xp, xs = x_prompt, x_sample
    new_p, new_s = [], []
    for i in range(DEPTH):
        lp = {'norm_g': norm_g[i], 'w_in': w_in[i], 'conv_a': conv_a[i], 'a_log': a_log[i],
              'dt_bias': dt_bias[i], 'onorm_a': onorm_a[i], 'mu_b': mu_b[i], 'w0_b': w0_b[i],
              'w2_b': w2_b[i], 'a0_b': a0_b[i], 'a2_b': a2_b[i], 'kk_b': kk_b[i], 'ka_b': ka_b[i],
              'rk_b': rk_b[i], 'gn_g_b': gn_g_b[i], 'gn_b_b': gn_b_b[i], 'qn_c': qn_c[i],
              'kn_c': kn_c[i], 'w_branch': w_branch[i], 'b_gate': b_gate[i], 'w_out': w_out[i],
              'w_ple': w_ple[i], 'w_pg': w_pg[i], 'b_pg': b_pg[i]}
        xp, st_p = mixer_layer(xp, p_prompt[i], pos_p, None, None, zero_dn, zero_conv, zero_rw, zero_shift, lp)
        new_p.append(st_p)
        past_k = gather_pages(cache_k[i], page_table)
        past_v = gather_pages(cache_v[i], page_table)
        xs, st_s = mixer_layer(xs, p_sample[i], pos_s, past_k, past_v, state_dn[i], state_dn_conv[i],
                               state_rwkv[i], state_rwkv_shift[i], lp)
        new_s.append(st_s)
    k_p, v_p, dn_p, conv_p, rw_p, sh_p = [jnp.stack(t) for t in zip(*new_p)]
    k_s, v_s, dn_s, conv_s, rw_s, sh_s = [jnp.stack(t) for t in zip(*new_s)]
    return (xp, xs, k_p, v_p, dn_p, conv_p, rw_p, sh_p, k_s, v_s, dn_s, conv_s, rw_s, sh_s)
```

```python
import functools

import numpy as np
import jax
import jax.numpy as jnp
from jax import lax
from jax.experimental import pallas as pl
from jax.experimental.pallas import tpu as pltpu

F32 = jnp.float32
BF16 = jnp.bfloat16

D_MODEL = 1024
D_PLE = 256
N_BRANCH = 3
W_BRANCH = 512
NORM_EPS = 1e-6
DK_A = 128
DV_A = 128
H_A = 4
QKV_A = 1536
CONV_W = 4
N_B = 64
H_B = 8
LORA = 64
SHIFT_B = 3 * W_BRANCH + 2 * LORA
GN_EPS = 64e-5
D_C = 64
H_C = 8
MOBA_BLOCK = 256
MOBA_TOPK = 3
ROPE_DIM = 16
ROPE_THETA = 500000.0
IN_SPLITS = (QKV_A, W_BRANCH, H_A, H_A, SHIFT_B, W_BRANCH, 3 * W_BRANCH, W_BRANCH, N_BRANCH * D_MODEL)

C_QKV_A = 0
C_Z_A = 1536
C_RKV = 2048
C_Z_B = 3584
C_QKV_C = 4096
C_Z_C = 5632
C_GATE = 6144
C_WLAL = 9216
C_BA = 9344
N_PROJ = 9600

CHUNK = 64
HIST = 8
LANES = 128
VMEM_LIMIT = 56 * 1024 * 1024

NN = (((1,), (0,)), ((), ()))
NT = (((1,), (1,)), ((), ()))
TN = (((0,), (0,)), ((), ()))
HIGHEST = lax.Precision.HIGHEST


def _mm(a, b, dims=NN):
    return lax.dot_general(a.astype(BF16), b.astype(BF16), dims, preferred_element_type=F32)


def _mm32(a, b, dims=NN):
    return lax.dot_general(a, b, dims, precision=HIGHEST, preferred_element_type=F32)


def _sigmoid(x):
    return 1.0 / (1.0 + jnp.exp(-x))


def _silu(x):
    return x * _sigmoid(x)


def _softplus(x):
    return jnp.maximum(x, 0.0) + jnp.log1p(jnp.exp(-jnp.abs(x)))


def _iota2(shape, dim):
    return lax.broadcasted_iota(jnp.int32, shape, dim)


def _inv_unit_lower(lmat, nilpotent):
    n = lmat.shape[0]
    eye = (_iota2((n, n), 0) == _iota2((n, n), 1)).astype(F32)
    p = -lmat
    acc = eye + p
    span = 2
    while span < nilpotent:
        p = _mm32(p, p)
        acc = acc + _mm32(acc, p)
        span *= 2
    return acc


def _head_sum(x, bd):
    hi = x.astype(BF16)
    lo = (x - hi.astype(F32)).astype(BF16)
    return (jnp.dot(hi, bd, preferred_element_type=F32) + jnp.dot(lo, bd, preferred_element_type=F32))


def _proj_kernel(x_ref, g_ref, w_ref, o_ref):
    x = x_ref[...]
    h = x * lax.rsqrt(jnp.mean(x * x, axis=-1, keepdims=True) + NORM_EPS) * g_ref[...]
    o_ref[...] = jnp.dot(h.astype(BF16), w_ref[...], preferred_element_type=F32)


def _proj(x2d, g, w, tm, tn):
    t, d = x2d.shape
    n = w.shape[1]
    return pl.pallas_call(
        _proj_kernel,
        grid=(n // tn, t // tm),
        in_specs=[pl.BlockSpec((tm, d), lambda j, i: (i, 0)),
                  pl.BlockSpec((1, d), lambda j, i: (0, 0)),
                  pl.BlockSpec((d, tn), lambda j, i: (0, j))],
        out_specs=pl.BlockSpec((tm, tn), lambda j, i: (i, j)),
        out_shape=jax.ShapeDtypeStruct((t, n), F32),
        compiler_params=pltpu.CompilerParams(dimension_semantics=("arbitrary", "arbitrary"),
                                             vmem_limit_bytes=VMEM_LIMIT),
    )(x2d, g, w)


def _chunk_masks(n, ls):
    r = _iota2((n, n), 0)
    c = _iota2((n, n), 1)
    same = (r // ls) == (c // ls)
    return same, same & (r >= c), same & (r > c)


def _valid_rows(l_idx, lt, r0, ls, l_valid):
    t = l_idx * lt + r0 + (_iota2((CHUNK, 1), 0) % ls)
    return t < l_valid


def _gdn_kernel(qkv_ref, z_ref, ba_ref, buf_ref, s0_ref, cw_ref, pa_ref, pb_ref, on_ref,
                o_ref, sout_ref, xs_ref, st_ref, *, nseg, lt, ls, l_valid, l_total):
    l_idx = pl.program_id(1)
    nrow = H_A * CHUNK

    @pl.when(l_idx == 0)
    def _():
        xs_ref[:, 0:HIST, :] = buf_ref[...]
        st_ref[...] = s0_ref[...]

    @pl.when(l_idx > 0)
    def _():
        xs_ref[:, 0:HIST, :] = xs_ref[:, lt:lt + HIST, :]

    xs_ref[:, HIST:lt + HIST, :] = qkv_ref[...]

    _, m_incl, m_strict = _chunk_masks(nrow, ls)
    same_r, tri_r, _ = _chunk_masks(CHUNK, ls)
    tri_r = tri_r.astype(F32)
    same_r = same_r.astype(F32)
    eye_n = (_iota2((nrow, nrow), 0) == _iota2((nrow, nrow), 1)).astype(F32)
    cw = cw_ref[...]
    neg_a = -jnp.exp(pa_ref[...])
    pb = pb_ref[...]
    onorm = on_ref[...]

    def conv(s, r0, col):
        win = xs_ref[s, pl.ds(r0, ls + HIST), col:col + LANES]
        w = cw[:, col:col + LANES]
        acc = pltpu.roll(win, 3, 0)[HIST:] * w[0:1] + pltpu.roll(win, 2, 0)[HIST:] * w[1:2]
        acc = acc + pltpu.roll(win, 1, 0)[HIST:] * w[2:3]
        acc = acc + win[HIST:] * w[3:4]
        return _silu(acc)

    def part(r0, col):
        return jnp.concatenate([conv(s, r0, col) for s in range(nseg)], axis=0)

    def chunk(c, carry):
        r0 = pl.multiple_of(c * ls, ls)
        ba = jnp.concatenate([ba_ref[s, pl.ds(r0, ls), :] for s in range(nseg)], axis=0)
        beta_all = _sigmoid(ba)
        g_all = neg_a * _softplus(ba + pb)
        if l_valid < l_total:
            ok = _valid_rows(l_idx, lt, r0, ls, l_valid)
            beta_all = jnp.where(ok, beta_all, 0.0)
            g_all = jnp.where(ok, g_all, 0.0)
        gcum = _mm32(tri_r, g_all)
        gtot = _mm32(same_r, g_all)

        qs, ks, vs, betas, gcs, gts = [], [], [], [], [], []
        for h in range(H_A):
            q = part(r0, h * DK_A)
            k = part(r0, H_A * DK_A + h * DK_A)
            v = part(r0, 2 * H_A * DK_A + h * DV_A)
            q = q * lax.rsqrt(jnp.sum(q * q, axis=-1, keepdims=True) + 1e-6) * (DK_A ** -0.5)
            k = k * lax.rsqrt(jnp.sum(k * k, axis=-1, keepdims=True) + 1e-6)
            qs.append(q)
            ks.append(k)
            vs.append(v)
            betas.append(beta_all[:, h:h + 1])
            gcs.append(gcum[:, H_A + h:H_A + h + 1])
            gts.append(gtot[:, H_A + h:H_A + h + 1])
        q_st = jnp.concatenate(qs, axis=0)
        k_st = jnp.concatenate(ks, axis=0)
        v_st = jnp.concatenate(vs, axis=0)
        beta = jnp.concatenate(betas, axis=0)
        g_col = jnp.concatenate(gcs, axis=0)
        gt_col = jnp.concatenate(gts, axis=0)
        g_row = jnp.sum(eye_n * g_col, axis=0, keepdims=True)

        decay = jnp.exp(jnp.where(m_incl, g_col - g_row, -jnp.inf))
        kk = _mm(k_st, k_st, NT)
        lmat = jnp.where(m_strict, beta * kk * decay, 0.0)
        ainv = _inv_unit_lower(lmat, ls)
        rhs = jnp.concatenate([v_st * beta, k_st * (beta * jnp.exp(g_col))], axis=1)
        sol = _mm32(ainv, rhs)
        bv = sol[:, :DV_A]
        wk = sol[:, DV_A:]
        qk = jnp.where(m_incl, _mm(q_st, k_st, NT) * decay, 0.0)
        qg = q_st * jnp.exp(g_col)
        kd = k_st * jnp.exp(gt_col - g_col)

        wks, qgs = [], []
        for h in range(H_A):
            for s in range(nseg):
                a = h * CHUNK + s * ls
                st = st_ref[s, h]
                wks.append(_mm(wk[a:a + ls], st))
                qgs.append(_mm(qg[a:a + ls], st))
        u = bv - jnp.concatenate(wks, axis=0)
        o = jnp.concatenate(qgs, axis=0) + _mm(qk, u)

        for h in range(H_A):
            a = h * CHUNK
            for s in range(nseg):
                b = a + s * ls
                if nseg == 1:
                    kd_s = kd[a:a + CHUNK]
                else:
                    rows = _iota2((CHUNK, 1), 0) // ls
                    kd_s = jnp.where(rows == s, kd[a:a + CHUNK], 0.0)
                upd = _mm(kd_s, u[a:a + CHUNK], TN)
                st_ref[s, h] = st_ref[s, h] * jnp.exp(gt_col[b:b + 1]) + upd
            oh = o[a:a + CHUNK]
            oh = oh * lax.rsqrt(jnp.mean(oh * oh, axis=-1, keepdims=True) + NORM_EPS) * onorm
            for s in range(nseg):
                zz = z_ref[s, pl.ds(r0, ls), h * DV_A:(h + 1) * DV_A]
                o_ref[s, pl.ds(r0, ls), h * DV_A:(h + 1) * DV_A] = oh[s * ls:(s + 1) * ls] * _silu(zz)
        return carry

    lax.fori_loop(0, lt // ls, chunk, 0)

    @pl.when(l_idx == pl.num_programs(1) - 1)
    def _():
        sout_ref[...] = st_ref[...]


def _gdn(proj3, buf8, s0, cw, pa, pb, onorm, *, nseg, lt, ls, l_valid):
    b, l, _ = proj3.shape
    kern = functools.partial(_gdn_kernel, nseg=nseg, lt=lt, ls=ls, l_valid=l_valid, l_total=l)
    full = lambda shape: pl.BlockSpec(shape, lambda i, j: (0,) * len(shape))
    return pl.pallas_call(
        kern,
        grid=(b // nseg, l // lt),
        in_specs=[pl.BlockSpec((nseg, lt, QKV_A), lambda i, j: (i, j, C_QKV_A // QKV_A)),
                  pl.BlockSpec((nseg, lt, W_BRANCH), lambda i, j: (i, j, C_Z_A // W_BRANCH)),
                  pl.BlockSpec((nseg, lt, LANES), lambda i, j: (i, j, C_BA // LANES)),
                  pl.BlockSpec((nseg, HIST, QKV_A), lambda i, j: (i, 0, 0)),
                  pl.BlockSpec((nseg, H_A, DK_A, DV_A), lambda i, j: (i, 0, 0, 0)),
                  full((CONV_W, QKV_A)), full((1, LANES)), full((1, LANES)), full((1, DV_A))],
        out_specs=[pl.BlockSpec((nseg, lt, W_BRANCH), lambda i, j: (i, j, 0)),
                   pl.BlockSpec((nseg, H_A, DK_A, DV_A), lambda i, j: (i, 0, 0, 0))],
        out_shape=[jax.ShapeDtypeStruct((b, l, W_BRANCH), F32),
                   jax.ShapeDtypeStruct((b, H_A, DK_A, DV_A), F32)],
        scratch_shapes=[pltpu.VMEM((nseg, lt + HIST, QKV_A), F32),
                        pltpu.VMEM((nseg, H_A, DK_A, DV_A), F32)],
        compiler_params=pltpu.CompilerParams(dimension_semantics=("arbitrary", "arbitrary"),
                                             vmem_limit_bytes=VMEM_LIMIT),
    )(proj3, proj3, proj3, buf8, s0, cw, pa, pb, onorm)


GRP = 4 * N_B


def _rwkv_kernel(r_ref, k_ref, v_ref, wa_ref, z_ref, prev_ref, s0_ref, mu_ref, w0_ref, w2_ref,
                 a0_ref, a2_ref, kkp_ref, kap_ref, rk_ref, gg_ref, gb_ref, bd_ref,
                 o_ref, sout_ref, fs_ref, st_ref, *, nseg, lt, ls, l_valid, l_total):
    l_idx = pl.program_id(1)
    nrow = 4 * CHUNK
    ngrp = W_BRANCH // GRP

    @pl.when(l_idx == 0)
    def _():
        fs_ref[:, 0:HIST, :] = prev_ref[...]
        st_ref[...] = s0_ref[...]

    @pl.when(l_idx > 0)
    def _():
        fs_ref[:, 0:HIST, :] = fs_ref[:, lt:lt + HIST, :]

    fs_ref[:, HIST:lt + HIST, 0:W_BRANCH] = r_ref[...]
    fs_ref[:, HIST:lt + HIST, W_BRANCH:2 * W_BRANCH] = k_ref[...]
    fs_ref[:, HIST:lt + HIST, 2 * W_BRANCH:3 * W_BRANCH] = v_ref[...]
    fs_ref[:, HIST:lt + HIST, 3 * W_BRANCH:SHIFT_B] = wa_ref[...]

    _, m_incl, m_strict = _chunk_masks(nrow, ls)
    same_r, tri_r, _ = _chunk_masks(CHUNK, ls)
    tri_r = tri_r.astype(F32)
    same_r = same_r.astype(F32)
    lane_head = _iota2((1, GRP), 1) // N_B
    head_masks = [(lane_head == j).astype(F32) for j in range(4)]
    bd_state = (_iota2((GRP, GRP), 0) // N_B) == (_iota2((GRP, GRP), 1) // N_B)
    mu = mu_ref[...]
    bd = bd_ref[...]

    def mixed(r0, col, width):
        outs = []
        for s in range(nseg):
            win = fs_ref[s, pl.ds(r0, ls + HIST), col:col + width]
            cur = win[HIST:]
            prev = pltpu.roll(win, 1, 0)[HIST:]
            outs.append(cur + (prev - cur) * mu[:, col:col + width])
        return jnp.concatenate(outs, axis=0)

    def stack_masked(x):
        return jnp.concatenate([x * head_masks[j] for j in range(4)], axis=0)

    def unstack(x):
        return x[0:CHUNK] + x[CHUNK:2 * CHUNK] + x[2 * CHUNK:3 * CHUNK] + x[3 * CHUNK:4 * CHUNK]

    def chunk(c, carry):
        r0 = pl.multiple_of(c * ls, ls)
        r = mixed(r0, 0, W_BRANCH)
        k = mixed(r0, W_BRANCH, W_BRANCH)
        v = mixed(r0, 2 * W_BRANCH, W_BRANCH)
        wa = mixed(r0, 3 * W_BRANCH, 2 * LORA)
        w_raw = w0_ref[...] + _mm(jnp.tanh(wa), w2_ref[...])
        logw = -jnp.exp(-_softplus(-w_raw) - 0.5)
        a_sig = _sigmoid(a0_ref[...] + _mm(wa, a2_ref[...]))
        kk = k * kkp_ref[...]
        kk = kk * lax.rsqrt(_head_sum(kk * kk, bd) + 1e-6)
        k_mod = k * (1.0 + (a_sig - 1.0) * kap_ref[...])
        b_vec = kk * a_sig
        if l_valid < l_total:
            ok = _valid_rows(l_idx, lt, r0, ls, l_valid)
            logw = jnp.where(ok, logw, 0.0)
            b_vec = jnp.where(ok, b_vec, 0.0)
            k_mod = jnp.where(ok, k_mod, 0.0)
        gcum = _mm32(tri_r, logw)
        gtot = _mm32(same_r, logw)
        e_neg = jnp.exp(-gcum)
        e_end = jnp.exp(gtot - gcum)
        a_t = -kk * jnp.exp(gcum - logw)
        b_t = b_vec * e_neg
        k_t = k_mod * e_neg
        r_t = r * jnp.exp(gcum)
        b_h = b_vec * e_end
        k_h = k_mod * e_end

        ys = []
        for g in range(ngrp):
            sl = slice(g * GRP, (g + 1) * GRP)
            a_g, b_g, k_g, r_g, v_g = a_t[:, sl], b_t[:, sl], k_t[:, sl], r_t[:, sl], v[:, sl]
            a_st = stack_masked(a_g)
            r_st = stack_masked(r_g)
            v_st = stack_masked(v_g)
            b_st = jnp.concatenate([b_g] * 4, axis=0)
            k_st = jnp.concatenate([k_g] * 4, axis=0)
            aab = jnp.where(m_strict, _mm(a_st, b_st, NT), 0.0)
            aak = jnp.where(m_strict, _mm(a_st, k_st, NT), 0.0)
            arb = jnp.where(m_incl, _mm(r_st, b_st, NT), 0.0)
            ark = jnp.where(m_incl, _mm(r_st, k_st, NT), 0.0)
            tinv = _inv_unit_lower(-aab, ls)
            ams, rms = [], []
            for s in range(nseg):
                st = st_ref[s, g]
                ams.append(_mm(a_g[s * ls:(s + 1) * ls], st, NT))
                rms.append(_mm(r_g[s * ls:(s + 1) * ls], st, NT))
            am = jnp.concatenate(ams, axis=0)
            rm = jnp.concatenate(rms, axis=0)
            u_st = _mm32(tinv, stack_masked(am) + _mm(aak, v_st))
            y_st = stack_masked(rm) + _mm(arb, u_st) + _mm(ark, v_st)
            u = unstack(u_st)
            ys.append(unstack(y_st))
            for s in range(nseg):
                if nseg == 1:
                    u_s, v_s = u, v_g
                else:
                    rows = _iota2((CHUNK, 1), 0) // ls
                    u_s = jnp.where(rows == s, u, 0.0)
                    v_s = jnp.where(rows == s, v_g, 0.0)
                upd = _mm(u_s, b_h[:, sl], TN) + _mm(v_s, k_h[:, sl], TN)
                dec = jnp.exp(gtot[s * ls:s * ls + 1, sl])
                st_ref[s, g] = st_ref[s, g] * dec + jnp.where(bd_state, upd, 0.0)
        y = jnp.concatenate(ys, axis=1)
        mean = _head_sum(y, bd) * (1.0 / N_B)
        yc = y - mean
        var = _head_sum(yc * yc, bd) * (1.0 / N_B)
        yn = yc * lax.rsqrt(var + GN_EPS) * gg_ref[...] + gb_ref[...]
        bonus = _head_sum(r * k_mod * rk_ref[...], bd) * v
        out = yn + bonus
        for s in range(nseg):
            zz = z_ref[s, pl.ds(r0, ls), :]
            o_ref[s, pl.ds(r0, ls), :] = out[s * ls:(s + 1) * ls] * _silu(zz)
        return carry

    lax.fori_loop(0, lt // ls, chunk, 0)

    @pl.when(l_idx == pl.num_programs(1) - 1)
    def _():
        sout_ref[...] = st_ref[...]


def _rwkv(proj3, prev8, s0, vecs, w2p, a2p, bd, *, nseg, lt, ls, l_valid):
    b, l, _ = proj3.shape
    ngrp = W_BRANCH // GRP
    kern = functools.partial(_rwkv_kernel, nseg=nseg, lt=lt, ls=ls, l_valid=l_valid, l_total=l)
    full = lambda shape: pl.BlockSpec(shape, lambda i, j: (0,) * len(shape))
    blk = lambda col, w: pl.BlockSpec((nseg, lt, w), lambda i, j: (i, j, col // w))
    mu, w0, a0, kkp, kap, rk, gg, gb = vecs
    return pl.pallas_call(
        kern,
        grid=(b // nseg, l // lt),
        in_specs=[blk(C_RKV, W_BRANCH), blk(C_RKV + W_BRANCH, W_BRANCH), blk(C_RKV + 2 * W_BRANCH, W_BRANCH),
                  blk(C_WLAL, 2 * LORA), blk(C_Z_B, W_BRANCH),
                  pl.BlockSpec((nseg, HIST, SHIFT_B), lambda i, j: (i, 0, 0)),
                  pl.BlockSpec((nseg, ngrp, GRP, GRP), lambda i, j: (i, 0, 0, 0)),
                  full((1, SHIFT_B)), full((1, W_BRANCH)), full((2 * LORA, W_BRANCH)),
                  full((1, W_BRANCH)), full((2 * LORA, W_BRANCH)), full((1, W_BRANCH)),
                  full((1, W_BRANCH)), full((1, W_BRANCH)), full((1, W_BRANCH)), full((1, W_BRANCH)),
                  full((W_BRANCH, W_BRANCH))],
        out_specs=[pl.BlockSpec((nseg, lt, W_BRANCH), lambda i, j: (i, j, 0)),
                   pl.BlockSpec((nseg, ngrp, GRP, GRP), lambda i, j: (i, 0, 0, 0))],
        out_shape=[jax.ShapeDtypeStruct((b, l, W_BRANCH), F32),
                   jax.ShapeDtypeStruct((b, ngrp, GRP, GRP), F32)],
        scratch_shapes=[pltpu.VMEM((nseg, lt + HIST, SHIFT_B), F32),
                        pltpu.VMEM((nseg, ngrp, GRP, GRP), F32)],
        compiler_params=pltpu.CompilerParams(dimension_semantics=("arbitrary", "arbitrary"),
                                             vmem_limit_bytes=VMEM_LIMIT),
    )(proj3, proj3, proj3, proj3, proj3, prev8, s0, mu, w0, w2p, a0, a2p, kkp, kap, rk, gg, gb, bd)


def _qkprep_kernel(q_ref, k_ref, qn_ref, kn_ref, inv_ref, bd_ref, qo_ref, ko_ref, *, tm, lseq, pos0):
    pos = (pos0 + (pl.program_id(0) * tm + _iota2((tm, 1), 0)) % lseq).astype(F32)
    ang = pos * inv_ref[...]
    cos = jnp.cos(ang)
    sin = jnp.sin(ang)
    d = _iota2((1, LANES), 1) % D_C
    half = ROPE_DIM // 2
    c_main = jnp.where(d < ROPE_DIM, cos, 1.0)
    c_up = jnp.where(d < half, -sin, 0.0)
    c_dn = jnp.where((d >= half) & (d < ROPE_DIM), sin, 0.0)
    bd = bd_ref[...]

    def one(x_ref, g_ref, out_ref):
        x = x_ref[...]
        y = x * lax.rsqrt(_head_sum(x * x, bd) * (1.0 / D_C) + NORM_EPS) * g_ref[...]
        for blk in range(W_BRANCH // LANES):
            yb = y[:, blk * LANES:(blk + 1) * LANES]
            up = pltpu.roll(yb, LANES - half, 1)
            dn = pltpu.roll(yb, half, 1)
            out_ref[:, blk * LANES:(blk + 1) * LANES] = yb * c_main + up * c_up + dn * c_dn

    one(q_ref, qn_ref, qo_ref)
    one(k_ref, kn_ref, ko_ref)


def _qkprep(proj2, qn, kn, inv, bd, *, tm, lseq, pos0):
    t = proj2.shape[0]
    kern = functools.partial(_qkprep_kernel, tm=tm, lseq=lseq, pos0=pos0)
    full = lambda shape: pl.BlockSpec(shape, lambda i: (0,) * len(shape))
    return pl.pallas_call(
        kern,
        grid=(t // tm,),
        in_specs=[pl.BlockSpec((tm, W_BRANCH), lambda i: (i, C_QKV_C // W_BRANCH)),
                  pl.BlockSpec((tm, W_BRANCH), lambda i: (i, C_QKV_C // W_BRANCH + 1)),
                  full((1, W_BRANCH)), full((1, W_BRANCH)), full((1, LANES)), full((W_BRANCH, W_BRANCH))],
        out_specs=[pl.BlockSpec((tm, W_BRANCH), lambda i: (i, 0)),
                   pl.BlockSpec((tm, W_BRANCH), lambda i: (i, 0))],
        out_shape=[jax.ShapeDtypeStruct((t, W_BRANCH), F32), jax.ShapeDtypeStruct((t, W_BRANCH), F32)],
        compiler_params=pltpu.CompilerParams(dimension_semantics=("arbitrary",), vmem_limit_bytes=VMEM_LIMIT),
    )(proj2, proj2, qn, kn, inv, bd)


def _rank_select(gate, nblk, limit, axis):
    idx = _iota2(gate.shape, axis)
    rank = jnp.zeros(gate.shape, F32)
    for m in range(nblk):
        gm = gate[m:m + 1, :] if axis == 0 else gate[:, m:m + 1]
        beats = (gm > gate) | ((gm == gate) & (m < idx))
        rank = rank + jnp.where(beats & (m < limit), 1.0, 0.0)
    return jnp.where((idx < limit) & (rank < MOBA_TOPK), 1.0, 0.0)


def _moba_prompt_kernel(q_ref, k_ref, v_ref, o_ref, kmean_ref, sel_ref, *, nblk):
    j = pl.program_id(2)
    scale = D_C ** -0.5

    @pl.when(j == 0)
    def _():
        for n in range(nblk):
            kmean_ref[n:n + 1, :] = jnp.mean(k_ref[n], axis=0, keepdims=True)

    q = q_ref[...]
    gate = _mm32(kmean_ref[...], q)
    sel_ref[...] = _rank_select(gate, nblk, j, 0)

    qb = q.astype(BF16)
    kl = _iota2((MOBA_BLOCK, MOBA_BLOCK), 0)
    ql = _iota2((MOBA_BLOCK, MOBA_BLOCK), 1)
    s = jnp.dot(k_ref[j].astype(BF16), qb, preferred_element_type=F32) * scale
    s = jnp.where(kl <= ql, s, -jnp.inf)
    m0 = jnp.max(s, axis=0, keepdims=True)
    p = jnp.exp(s - m0)
    l0 = jnp.sum(p, axis=0, keepdims=True)
    acc0 = jnp.dot(v_ref[j].astype(BF16), p.astype(BF16), preferred_element_type=F32)

    def body(n, carry):
        m, l, acc = carry
        s = jnp.dot(k_ref[n].astype(BF16), qb, preferred_element_type=F32) * scale
        s = jnp.where(sel_ref[pl.ds(n, 1), :] > 0.0, s, -jnp.inf)
        m_new = jnp.maximum(m, jnp.max(s, axis=0, keepdims=True))
        alpha = jnp.exp(m - m_new)
        p = jnp.exp(s - m_new)
        l = l * alpha + jnp.sum(p, axis=0, keepdims=True)
        acc = acc * alpha + jnp.dot(v_ref[n].astype(BF16), p.astype(BF16), preferred_element_type=F32)
        return m_new, l, acc

    _, l, acc = lax.fori_loop(0, j, body, (m0, l0, acc0))
    o_ref[...] = acc / l


def _moba_prompt(q_t, k_b, v_t):
    b, h, _, l = q_t.shape
    nblk = l // MOBA_BLOCK
    kern = functools.partial(_moba_prompt_kernel, nblk=nblk)
    return pl.pallas_call(
        kern,
        grid=(b, h, nblk),
        in_specs=[pl.BlockSpec((None, None, D_C, MOBA_BLOCK), lambda i, hh, j: (i, hh, 0, j)),
                  pl.BlockSpec((None, None, nblk, MOBA_BLOCK, D_C), lambda i, hh, j: (i, hh, 0, 0, 0)),
                  pl.BlockSpec((None, None, nblk, D_C, MOBA_BLOCK), lambda i, hh, j: (i, hh, 0, 0, 0))],
        out_specs=pl.BlockSpec((None, None, D_C, MOBA_BLOCK), lambda i, hh, j: (i, hh, 0, j)),
        out_shape=jax.ShapeDtypeStruct((b, h, D_C, l), F32),
        scratch_shapes=[pltpu.VMEM((nblk, D_C), F32), pltpu.VMEM((nblk, MOBA_BLOCK), F32)],
        compiler_params=pltpu.CompilerParams(dimension_semantics=("arbitrary", "arbitrary", "arbitrary"),
                                             vmem_limit_bytes=VMEM_LIMIT),
    )(q_t, k_b, v_t)


def _moba_decode_kernel(pt_ref, q_ref, kn_ref, vn_ref, *refs, npage, page, lq):
    k_refs = refs[:npage]
    v_refs = refs[npage:2 * npage]
    o_ref = refs[2 * npage]
    del pt_ref
    scale = D_C ** -0.5
    nrow = lq * H_C
    ppb = MOBA_BLOCK // page
    nblk = npage // ppb

    lane_head = _iota2((H_C, W_BRANCH), 1) // D_C
    own_head = (lane_head == _iota2((H_C, W_BRANCH), 0)).astype(F32)
    q = q_ref[...]
    qbd = jnp.concatenate([q[t:t + 1, :] * own_head for t in range(lq)], axis=0)
    qb = qbd.astype(BF16)

    ms, ls_, accs, ksums = [], [], [], []
    for p_i in range(npage):
        kp = k_refs[p_i][...]
        s = lax.dot_general(qb, kp.astype(BF16), NT, preferred_element_type=F32) * scale
        m = jnp.max(s, axis=-1, keepdims=True)
        p = jnp.exp(s - m)
        ms.append(m)
        ls_.append(jnp.sum(p, axis=-1, keepdims=True))
        accs.append(jnp.dot(p.astype(BF16), v_refs[p_i][...].astype(BF16), preferred_element_type=F32))
        ksums.append(jnp.sum(kp, axis=0, keepdims=True))

    kmeans = [sum(ksums[n * ppb:(n + 1) * ppb]) * (1.0 / MOBA_BLOCK) for n in range(nblk)]
    kmean = jnp.concatenate(kmeans + [jnp.zeros((LANES - nblk, W_BRANCH), F32)], axis=0)
    gate = _mm32(qbd, kmean, NT)
    sel = _rank_select(gate, nblk, nblk, 1)

    pad = jnp.zeros((LANES - kn_ref.shape[0], W_BRANCH), F32)
    k_new = jnp.concatenate([kn_ref[...], pad], axis=0)
    v_new = jnp.concatenate([vn_ref[...], pad], axis=0)
    s = lax.dot_general(qb, k_new.astype(BF16), NT, preferred_element_type=F32) * scale
    t_key = _iota2((nrow, LANES), 1)
    t_qry = _iota2((nrow, LANES), 0) // H_C
    s = jnp.where(t_key <= t_qry, s, -jnp.inf)
    m_all = jnp.max(s, axis=-1, keepdims=True)
    p_own = jnp.exp(s - m_all)
    l_own = jnp.sum(p_own, axis=-1, keepdims=True)
    acc_own = jnp.dot(p_own.astype(BF16), v_new.astype(BF16), preferred_element_type=F32)
    m_own = m_all

    m_eff = []
    for p_i in range(npage):
        n = p_i // ppb
        me = jnp.where(sel[:, n:n + 1] > 0.0, ms[p_i], -jnp.inf)
        m_eff.append(me)
        m_all = jnp.maximum(m_all, me)
    w_own = jnp.exp(m_own - m_all)
    l_tot = l_own * w_own
    acc = acc_own * w_own
    for p_i in range(npage):
        w = jnp.exp(m_eff[p_i] - m_all)
        l_tot = l_tot + ls_[p_i] * w
        acc = acc + accs[p_i] * w
    out = acc / l_tot
    own_rows = jnp.concatenate([own_head] * lq, axis=0)
    out = (out * own_rows).reshape(lq, H_C, W_BRANCH).sum(axis=1)
    o_ref[...] = jnp.concatenate([out, jnp.zeros((o_ref.shape[0] - lq, W_BRANCH), F32)], axis=0)


def _moba_decode(page_table, q3, k3, v3, cache_k4, cache_v4, layer, lq):
    b, lp, _ = q3.shape
    npage = page_table.shape[1]
    page = cache_k4.shape[2]
    kern = functools.partial(_moba_decode_kernel, npage=npage, page=page, lq=lq)
    tok = pl.BlockSpec((None, lp, W_BRANCH), lambda i, pt: (i, 0, 0))

    def page_spec(p_i):
        return pl.BlockSpec((None, None, page, W_BRANCH), lambda i, pt: (layer, pt[i, p_i], 0, 0))

    grid_spec = pltpu.PrefetchScalarGridSpec(
        num_scalar_prefetch=1,
        grid=(b,),
        in_specs=[tok, tok, tok] + [page_spec(p_i) for p_i in range(npage)] * 2,
        out_specs=pl.BlockSpec((None, lp, W_BRANCH), lambda i, pt: (i, 0, 0)),
    )
    return pl.pallas_call(
        kern,
        grid_spec=grid_spec,
        out_shape=jax.ShapeDtypeStruct((b, lp, W_BRANCH), F32),
        compiler_params=pltpu.CompilerParams(dimension_semantics=("arbitrary",), vmem_limit_bytes=VMEM_LIMIT),
    )(page_table, q3, k3, v3, *([cache_k4] * npage), *([cache_v4] * npage))


def _merge_kernel(oa_ref, ob_ref, oc_ref, zc_ref, gate_ref, x_ref, ple_ref, wb_ref, bg_ref, wo_ref,
                  wpg_ref, bpg_ref, wple_ref, y_ref):
    branches = (oa_ref[...], ob_ref[...], oc_ref[...] * _silu(zc_ref[...]))
    up = None
    for n, o in enumerate(branches):
        sl = slice(n * D_MODEL, (n + 1) * D_MODEL)
        gate = _sigmoid(gate_ref[:, sl] + bg_ref[:, sl])
        term = gate * jnp.dot(o.astype(BF16), wb_ref[n], preferred_element_type=F32)
        up = term if up is None else up + term
    y = x_ref[...] + jnp.dot(up.astype(BF16), wo_ref[...], preferred_element_type=F32)
    pg = _sigmoid(jnp.dot(y.astype(BF16), wpg_ref[...], preferred_element_type=F32) + bpg_ref[...])
    y_ref[...] = y + pg * jnp.dot(ple_ref[...].astype(BF16), wple_ref[...], preferred_element_type=F32)


def _merge(oa, ob, oc, proj2, x2, ple2, wb, bg, wo, wpg, bpg, wple, tm):
    t = x2.shape[0]
    full = lambda shape: pl.BlockSpec(shape, lambda i: (0,) * len(shape))
    row = lambda w: pl.BlockSpec((tm, w), lambda i: (i, 0))
    return pl.pallas_call(
        _merge_kernel,
        grid=(t // tm,),
        in_specs=[row(W_BRANCH), row(W_BRANCH), row(W_BRANCH),
                  pl.BlockSpec((tm, W_BRANCH), lambda i: (i, C_Z_C // W_BRANCH)),
                  pl.BlockSpec((tm, N_BRANCH * D_MODEL), lambda i: (i, C_GATE // (N_BRANCH * D_MODEL))),
                  row(D_MODEL), row(D_PLE),
                  full((N_BRANCH, W_BRANCH, D_MODEL)), full((1, N_BRANCH * D_MODEL)), full((D_MODEL, D_MODEL)),
                  full((D_MODEL, D_MODEL)), full((1, D_MODEL)), full((D_PLE, D_MODEL))],
        out_specs=row(D_MODEL),
        out_shape=jax.ShapeDtypeStruct((t, D_MODEL), F32),
        compiler_params=pltpu.CompilerParams(dimension_semantics=("arbitrary",), vmem_limit_bytes=VMEM_LIMIT),
    )(oa, ob, oc, proj2, proj2, x2, ple2, wb, bg, wo, wpg, bpg, wple)


def _arrange_w_in(w):
    offs = np.cumsum((0,) + IN_SPLITS)
    qkv_a, z_a, beta, alpha, f_b, z_b, qkv_c, z_c, gate = [w[:, offs[i]:offs[i + 1]] for i in range(9)]
    d = w.shape[0]
    ba = jnp.concatenate([beta, alpha, jnp.zeros((d, LANES - 2 * H_A), w.dtype)], axis=1)
    tail = jnp.zeros((d, N_PROJ - C_BA - LANES), w.dtype)
    out = jnp.concatenate([qkv_a, z_a, f_b[:, :3 * W_BRANCH], z_b, qkv_c, z_c, gate,
                           f_b[:, 3 * W_BRANCH:], ba, tail], axis=1)
    return out.astype(BF16)


def _layer_params(i, p):
    row = lambda v: v.reshape(1, -1)
    lanes_ba = lambda v: jnp.zeros((1, LANES), F32).at[0, H_A:2 * H_A].set(v)
    lora_pad = jnp.zeros((LORA, W_BRANCH), F32)
    head = np.arange(W_BRANCH) // N_B
    half = ROPE_DIM // 2
    inv = ROPE_THETA ** (-jnp.arange(half, dtype=F32) / half)
    d = np.arange(LANES) % D_C
    inv_lanes = jnp.where(d < ROPE_DIM, inv[d % half], 0.0).reshape(1, LANES)
    return dict(
        norm_g=row(p['norm_g'][i]),
        w_in=_arrange_w_in(p['w_in'][i]),
        conv=p['conv_a'][i],
        pa=lanes_ba(p['a_log'][i]), pb=lanes_ba(p['dt_bias'][i]), onorm=row(p['onorm_a'][i]),
        rw_vecs=(row(p['mu_b'][i]), row(p['w0_b'][i]), row(p['a0_b'][i]), row(p['kk_b'][i]), row(p['ka_b'][i]),
                 row(p['rk_b'][i]), row(p['gn_g_b'][i]), row(p['gn_b_b'][i])),
        w2p=jnp.concatenate([p['w2_b'][i], lora_pad], axis=0).astype(BF16),
        a2p=jnp.concatenate([lora_pad, p['a2_b'][i]], axis=0).astype(BF16),
        bd=jnp.asarray(head[:, None] == head[None, :], BF16),
        qn=row(jnp.tile(p['qn_c'][i], H_C)), kn=row(jnp.tile(p['kn_c'][i], H_C)), inv=inv_lanes,
        wb=p['w_branch'][i].astype(BF16), bg=row(p['b_gate'][i]), wo=p['w_out'][i].astype(BF16),
        wpg=p['w_pg'][i].astype(BF16), bpg=row(p['b_pg'][i]), wple=p['w_ple'][i].astype(BF16),
    )


def _rw_state_in(s):
    b = s.shape[0]
    out = jnp.zeros((b, W_BRANCH // GRP, GRP, GRP), F32)
    for h in range(H_B):
        g, j = divmod(h, 4)
        out = out.at[:, g, j * N_B:(j + 1) * N_B, j * N_B:(j + 1) * N_B].set(s[:, h])
    return out


def _rw_state_out(s):
    blocks = []
    for h in range(H_B):
        g, j = divmod(h, 4)
        blocks.append(s[:, g, j * N_B:(j + 1) * N_B, j * N_B:(j + 1) * N_B])
    return jnp.stack(blocks, axis=1)


def _hist_rows(rows):
    b, n, c = rows.shape
    return jnp.concatenate([jnp.zeros((b, HIST - n, c), F32), rows], axis=1)


def _layer(x3, ple3, lp, *, l_valid, pos0, dn_s, dn_buf, rw_s, rw_prev, nseg, lt, ls, tm, attend):
    b, l, d = x3.shape
    t = b * l
    proj2 = _proj(x3.reshape(t, d), lp['norm_g'], lp['w_in'], min(t, 1024), 1920)
    proj3 = proj2.reshape(b, l, N_PROJ)

    oa, dn_new = _gdn(proj3, _hist_rows(dn_buf), dn_s, lp['conv'], lp['pa'], lp['pb'], lp['onorm'],
                      nseg=nseg, lt=lt, ls=ls, l_valid=l_valid)
    ob, rw_new = _rwkv(proj3, _hist_rows(rw_prev[:, None, :]), _rw_state_in(rw_s), lp['rw_vecs'],
                       lp['w2p'], lp['a2p'], lp['bd'], nseg=nseg, lt=lt, ls=ls, l_valid=l_valid)
    qr, kr = _qkprep(proj2, lp['qn'], lp['kn'], lp['inv'], lp['bd'], tm=min(tm, t), lseq=l, pos0=pos0)
    v2 = proj2[:, C_QKV_C + 2 * W_BRANCH:C_QKV_C + 3 * W_BRANCH]
    oc = attend(qr, kr, v2)

    y2 = _merge(oa.reshape(t, W_BRANCH), ob.reshape(t, W_BRANCH), oc.reshape(t, W_BRANCH), proj2,
                x3.reshape(t, d), ple3.reshape(t, D_PLE), lp['wb'], lp['bg'], lp['wo'], lp['wpg'],
                lp['bpg'], lp['wple'], min(tm, t))

    f_last = jnp.concatenate([proj3[:, l_valid - 1, C_RKV:C_RKV + 3 * W_BRANCH],
                              proj3[:, l_valid - 1, C_WLAL:C_WLAL + 2 * LORA]], axis=-1)
    new_state = (kr.reshape(b, l, H_C, D_C)[:, :l_valid], v2.reshape(b, l, H_C, D_C)[:, :l_valid], dn_new,
                 proj3[:, l_valid - (CONV_W - 1):l_valid, C_QKV_A:C_QKV_A + QKV_A],
                 _rw_state_out(rw_new), f_last)
    return y2.reshape(b, l, d), new_state


def _attend_prompt(b, l):
    nblk = l // MOBA_BLOCK

    def attend(qr, kr, v2):
        q_t = qr.reshape(b, l, H_C, D_C).transpose(0, 2, 3, 1)
        k_b = kr.reshape(b, nblk, MOBA_BLOCK, H_C, D_C).transpose(0, 3, 1, 2, 4)
        v_t = v2.reshape(b, nblk, MOBA_BLOCK, H_C, D_C).transpose(0, 3, 1, 4, 2)
        o_t = _moba_prompt(q_t, k_b, v_t)
        return o_t.transpose(0, 3, 1, 2).reshape(b * l, W_BRANCH)
    return attend


def _attend_decode(b, lp_, lq, page_table, cache_k4, cache_v4, layer):
    def attend(qr, kr, v2):
        shp = (b, lp_, W_BRANCH)
        return _moba_decode(page_table, qr.reshape(shp), kr.reshape(shp), v2.reshape(shp),
                            cache_k4, cache_v4, layer, lq)
    return attend


DEC_PAD = 8


def kernel(x_prompt, x_sample, cache_k, cache_v, state_dn, state_dn_conv, state_rwkv, state_rwkv_shift,
           page_table, p_prompt, p_sample, norm_g, w_in, conv_a, a_log, dt_bias, onorm_a, mu_b, w0_b, w2_b,
           a0_b, a2_b, kk_b, ka_b, rk_b, gn_g_b, gn_b_b, qn_c, kn_c, w_branch, b_gate, w_out, w_ple, w_pg,
           b_pg):
    params = dict(norm_g=norm_g, w_in=w_in, conv_a=conv_a, a_log=a_log, dt_bias=dt_bias, onorm_a=onorm_a,
                  mu_b=mu_b, w0_b=w0_b, w2_b=w2_b, a0_b=a0_b, a2_b=a2_b, kk_b=kk_b, ka_b=ka_b, rk_b=rk_b,
                  gn_g_b=gn_g_b, gn_b_b=gn_b_b, qn_c=qn_c, kn_c=kn_c, w_branch=w_branch, b_gate=b_gate,
                  w_out=w_out, w_ple=w_ple, w_pg=w_pg, b_pg=b_pg)
    depth = norm_g.shape[0]
    b_p, l_p, _ = x_prompt.shape
    b_s, l_s, _ = x_sample.shape
    n_phys, page = cache_k.shape[1], cache_k.shape[2]
    past_len = page_table.shape[1] * page
    assert l_p % 512 == 0 and CONV_W - 1 <= l_s <= DEC_PAD and b_s % (CHUNK // DEC_PAD) == 0
    assert past_len % MOBA_BLOCK == 0 and MOBA_BLOCK % page == 0

    pad_rows = lambda a: jnp.pad(a, ((0, 0), (0, DEC_PAD - l_s), (0, 0)))
    xs = pad_rows(x_sample)
    cache_k4 = cache_k.reshape(depth, n_phys, page, W_BRANCH)
    cache_v4 = cache_v.reshape(depth, n_phys, page, W_BRANCH)
    xp = x_prompt
    new_p, new_s = [], []
    for i in range(depth):
        lp = _layer_params(i, params)
        xp, st_p = _layer(xp, p_prompt[i], lp, l_valid=l_p, pos0=0,
                          dn_s=jnp.zeros((b_p, H_A, DK_A, DV_A), F32),
                          dn_buf=jnp.zeros((b_p, CONV_W - 1, QKV_A), F32),
                          rw_s=jnp.zeros((b_p, H_B, N_B, N_B), F32), rw_prev=jnp.zeros((b_p, SHIFT_B), F32),
                          nseg=1, lt=512, ls=CHUNK, tm=512, attend=_attend_prompt(b_p, l_p))
        new_p.append(st_p)
        xs, st_s = _layer(xs, pad_rows(p_sample[i]), lp, l_valid=l_s, pos0=past_len,
                          dn_s=state_dn[i], dn_buf=state_dn_conv[i], rw_s=state_rwkv[i],
                          rw_prev=state_rwkv_shift[i], nseg=CHUNK // DEC_PAD, lt=DEC_PAD, ls=DEC_PAD, tm=512,
                          attend=_attend_decode(b_s, DEC_PAD, l_s, page_table, cache_k4, cache_v4, i))
        new_s.append(st_s)
    outs_p = [jnp.stack(t) for t in zip(*new_p)]
    outs_s = [jnp.stack(t) for t in zip(*new_s)]
    return (xp, xs[:, :l_s], *outs_p, *outs_s)
```

```python
import functools

import numpy as np
import jax
import jax.numpy as jnp
from jax import lax
from jax.experimental import pallas as pl
from jax.experimental.pallas import tpu as pltpu

F32 = jnp.float32
BF16 = jnp.bfloat16

D_MODEL = 1024
D_PLE = 256
N_BRANCH = 3
W_BRANCH = 512
NORM_EPS = 1e-6
DK_A = 128
DV_A = 128
H_A = 4
QKV_A = 1536
CONV_W = 4
N_B = 64
H_B = 8
LORA = 64
SHIFT_B = 3 * W_BRANCH + 2 * LORA
GN_EPS = 64e-5
D_C = 64
H_C = 8
MOBA_BLOCK = 256
MOBA_TOPK = 3
ROPE_DIM = 16
ROPE_THETA = 500000.0
IN_SPLITS = (QKV_A, W_BRANCH, H_A, H_A, SHIFT_B, W_BRANCH, 3 * W_BRANCH, W_BRANCH, N_BRANCH * D_MODEL)

C_QKV_A = 0
C_Z_A = 1536
C_RKV = 2048
C_Z_B = 3584
C_QKV_C = 4096
C_Z_C = 5632
C_GATE = 6144
C_WLAL = 9216
C_BA = 9344
N_PROJ = 9600

CHUNK = 64
HIST = 8
LANES = 128
VMEM_LIMIT = 56 * 1024 * 1024

NN = (((1,), (0,)), ((), ()))
NT = (((1,), (1,)), ((), ()))
TN = (((0,), (0,)), ((), ()))
HIGHEST = lax.Precision.HIGHEST


def _mm(a, b, dims=NN):
    return lax.dot_general(a.astype(BF16), b.astype(BF16), dims, preferred_element_type=F32)


def _mm32(a, b, dims=NN):
    return lax.dot_general(a, b, dims, precision=HIGHEST, preferred_element_type=F32)


def _sigmoid(x):
    return 1.0 / (1.0 + jnp.exp(-x))


def _silu(x):
    return x * _sigmoid(x)


def _softplus(x):
    return jnp.maximum(x, 0.0) + jnp.log1p(jnp.exp(-jnp.abs(x)))


def _iota2(shape, dim):
    return lax.broadcasted_iota(jnp.int32, shape, dim)


def _split(x):
    hi = x.astype(BF16)
    return hi, (x - hi.astype(F32)).astype(BF16)


def _dotb(a, b, dims=NN):
    return lax.dot_general(a, b, dims, preferred_element_type=F32)


def _mm_sel(sel, x):
    hi, lo = _split(x)
    return _dotb(sel, hi) + _dotb(sel, lo)


def _mm3(a, b, dims=NN):
    ah, al = _split(a)
    bh, bl = _split(b)
    return _dotb(ah, bh, dims) + (_dotb(ah, bl, dims) + _dotb(al, bh, dims))


SOLVE_MM_A = _mm
SOLVE_MM_B = _mm


def _inv_unit_lower(lmat, nilpotent, mm):
    n = lmat.shape[0]
    eye = (_iota2((n, n), 0) == _iota2((n, n), 1)).astype(F32)
    p = -lmat
    acc = eye + p
    span = 2
    while span < nilpotent:
        p = mm(p, p)
        acc = acc + mm(acc, p)
        span *= 2
    return acc


def _head_sum(x, bd):
    hi = x.astype(BF16)
    lo = (x - hi.astype(F32)).astype(BF16)
    return (jnp.dot(hi, bd, preferred_element_type=F32) + jnp.dot(lo, bd, preferred_element_type=F32))


def _proj_kernel(x_ref, g_ref, w_ref, o_ref):
    x = x_ref[...]
    h = x * lax.rsqrt(jnp.mean(x * x, axis=-1, keepdims=True) + NORM_EPS) * g_ref[...]
    o_ref[...] = jnp.dot(h.astype(BF16), w_ref[...], preferred_element_type=F32)


def _proj(x2d, g, w, tm, tn):
    t, d = x2d.shape
    n = w.shape[1]
    return pl.pallas_call(
        _proj_kernel,
        grid=(n // tn, t // tm),
        in_specs=[pl.BlockSpec((tm, d), lambda j, i: (i, 0)),
                  pl.BlockSpec((1, d), lambda j, i: (0, 0)),
                  pl.BlockSpec((d, tn), lambda j, i: (0, j))],
        out_specs=pl.BlockSpec((tm, tn), lambda j, i: (i, j)),
        out_shape=jax.ShapeDtypeStruct((t, n), F32),
        compiler_params=pltpu.CompilerParams(dimension_semantics=("arbitrary", "arbitrary"),
                                             vmem_limit_bytes=VMEM_LIMIT),
    )(x2d, g, w)


def _chunk_masks(n, ls):
    r = _iota2((n, n), 0)
    c = _iota2((n, n), 1)
    same = (r // ls) == (c // ls)
    return same, same & (r >= c), same & (r > c)


def _valid_rows(l_idx, lt, r0, ls, l_valid):
    t = l_idx * lt + r0 + (_iota2((CHUNK, 1), 0) % ls)
    return t < l_valid


def _gdn_kernel(qkv_ref, z_ref, ba_ref, buf_ref, s0_ref, cw_ref, pa_ref, pb_ref, on_ref,
                o_ref, sout_ref, xs_ref, st_ref, *, nseg, lt, ls, l_valid, l_total):
    l_idx = pl.program_id(1)
    nrow = H_A * CHUNK

    @pl.when(l_idx == 0)
    def _():
        xs_ref[:, 0:HIST, :] = buf_ref[...]
        st_ref[...] = s0_ref[...]

    @pl.when(l_idx > 0)
    def _():
        xs_ref[:, 0:HIST, :] = xs_ref[:, lt:lt + HIST, :]

    xs_ref[:, HIST:lt + HIST, :] = qkv_ref[...]

    _, m_incl, m_strict = _chunk_masks(nrow, ls)
    same_r, tri_r, _ = _chunk_masks(CHUNK, ls)
    tri_r = tri_r.astype(BF16)
    same_r = same_r.astype(BF16)
    eye_n = (_iota2((nrow, nrow), 0) == _iota2((nrow, nrow), 1)).astype(F32)
    cw = cw_ref[...]
    neg_a = -jnp.exp(pa_ref[...])
    pb = pb_ref[...]
    onorm = on_ref[...]

    def conv(s, r0, col):
        win = xs_ref[s, pl.ds(r0, ls + HIST), col:col + LANES]
        w = cw[:, col:col + LANES]
        acc = pltpu.roll(win, 3, 0)[HIST:] * w[0:1] + pltpu.roll(win, 2, 0)[HIST:] * w[1:2]
        acc = acc + pltpu.roll(win, 1, 0)[HIST:] * w[2:3]
        acc = acc + win[HIST:] * w[3:4]
        return _silu(acc)

    def part(r0, col):
        return jnp.concatenate([conv(s, r0, col) for s in range(nseg)], axis=0)

    def chunk(c, carry):
        r0 = pl.multiple_of(c * ls, ls)
        ba = jnp.concatenate([ba_ref[s, pl.ds(r0, ls), :] for s in range(nseg)], axis=0)
        beta_all = _sigmoid(ba)
        g_all = neg_a * _softplus(ba + pb)
        if l_valid < l_total:
            ok = _valid_rows(l_idx, lt, r0, ls, l_valid)
            beta_all = jnp.where(ok, beta_all, 0.0)
            g_all = jnp.where(ok, g_all, 0.0)
        gcum = _mm_sel(tri_r, g_all)
        gtot = _mm_sel(same_r, g_all)

        qs, ks, vs, betas, gcs, gts = [], [], [], [], [], []
        for h in range(H_A):
            q = part(r0, h * DK_A)
            k = part(r0, H_A * DK_A + h * DK_A)
            v = part(r0, 2 * H_A * DK_A + h * DV_A)
            q = q * lax.rsqrt(jnp.sum(q * q, axis=-1, keepdims=True) + 1e-6) * (DK_A ** -0.5)
            k = k * lax.rsqrt(jnp.sum(k * k, axis=-1, keepdims=True) + 1e-6)
            qs.append(q)
            ks.append(k)
            vs.append(v)
            betas.append(beta_all[:, h:h + 1])
            gcs.append(gcum[:, H_A + h:H_A + h + 1])
            gts.append(gtot[:, H_A + h:H_A + h + 1])
        q_st = jnp.concatenate(qs, axis=0)
        k_st = jnp.concatenate(ks, axis=0)
        v_st = jnp.concatenate(vs, axis=0)
        beta = jnp.concatenate(betas, axis=0)
        g_col = jnp.concatenate(gcs, axis=0)
        gt_col = jnp.concatenate(gts, axis=0)
        g_row = jnp.sum(eye_n * g_col, axis=0, keepdims=True)

        decay = jnp.exp(jnp.where(m_incl, g_col - g_row, -jnp.inf))
        kk = _mm(k_st, k_st, NT)
        lmat = jnp.where(m_strict, beta * kk * decay, 0.0)
        ainv = _inv_unit_lower(lmat, ls, SOLVE_MM_A)
        rhs = jnp.concatenate([v_st * beta, k_st * (beta * jnp.exp(g_col))], axis=1)
        sol = SOLVE_MM_A(ainv, rhs)
        bv = sol[:, :DV_A]
        wk = sol[:, DV_A:]
        qk = jnp.where(m_incl, _mm(q_st, k_st, NT) * decay, 0.0)
        qg = q_st * jnp.exp(g_col)
        kd = k_st * jnp.exp(gt_col - g_col)

        wks, qgs = [], []
        for h in range(H_A):
            for s in range(nseg):
                a = h * CHUNK + s * ls
                st = st_ref[s, h]
                wks.append(_mm(wk[a:a + ls], st))
                qgs.append(_mm(qg[a:a + ls], st))
        u = bv - jnp.concatenate(wks, axis=0)
        o = jnp.concatenate(qgs, axis=0) + _mm(qk, u)

        for h in range(H_A):
            a = h * CHUNK
            for s in range(nseg):
                b = a + s * ls
                if nseg == 1:
                    kd_s = kd[a:a + CHUNK]
                else:
                    rows = _iota2((CHUNK, 1), 0) // ls
                    kd_s = jnp.where(rows == s, kd[a:a + CHUNK], 0.0)
                upd = _mm(kd_s, u[a:a + CHUNK], TN)
                st_ref[s, h] = st_ref[s, h] * jnp.exp(gt_col[b:b + 1]) + upd
            oh = o[a:a + CHUNK]
            oh = oh * lax.rsqrt(jnp.mean(oh * oh, axis=-1, keepdims=True) + NORM_EPS) * onorm
            for s in range(nseg):
                zz = z_ref[s, pl.ds(r0, ls), h * DV_A:(h + 1) * DV_A]
                o_ref[s, pl.ds(r0, ls), h * DV_A:(h + 1) * DV_A] = oh[s * ls:(s + 1) * ls] * _silu(zz)
        return carry

    lax.fori_loop(0, lt // ls, chunk, 0)

    @pl.when(l_idx == pl.num_programs(1) - 1)
    def _():
        sout_ref[...] = st_ref[...]


def _gdn(proj3, buf8, s0, cw, pa, pb, onorm, *, nseg, lt, ls, l_valid):
    b, l, _ = proj3.shape
    kern = functools.partial(_gdn_kernel, nseg=nseg, lt=lt, ls=ls, l_valid=l_valid, l_total=l)
    full = lambda shape: pl.BlockSpec(shape, lambda i, j: (0,) * len(shape))
    return pl.pallas_call(
        kern,
        grid=(b // nseg, l // lt),
        in_specs=[pl.BlockSpec((nseg, lt, QKV_A), lambda i, j: (i, j, C_QKV_A // QKV_A)),
                  pl.BlockSpec((nseg, lt, W_BRANCH), lambda i, j: (i, j, C_Z_A // W_BRANCH)),
                  pl.BlockSpec((nseg, lt, LANES), lambda i, j: (i, j, C_BA // LANES)),
                  pl.BlockSpec((nseg, HIST, QKV_A), lambda i, j: (i, 0, 0)),
                  pl.BlockSpec((nseg, H_A, DK_A, DV_A), lambda i, j: (i, 0, 0, 0)),
                  full((CONV_W, QKV_A)), full((1, LANES)), full((1, LANES)), full((1, DV_A))],
        out_specs=[pl.BlockSpec((nseg, lt, W_BRANCH), lambda i, j: (i, j, 0)),
                   pl.BlockSpec((nseg, H_A, DK_A, DV_A), lambda i, j: (i, 0, 0, 0))],
        out_shape=[jax.ShapeDtypeStruct((b, l, W_BRANCH), F32),
                   jax.ShapeDtypeStruct((b, H_A, DK_A, DV_A), F32)],
        scratch_shapes=[pltpu.VMEM((nseg, lt + HIST, QKV_A), F32),
                        pltpu.VMEM((nseg, H_A, DK_A, DV_A), F32)],
        compiler_params=pltpu.CompilerParams(dimension_semantics=("arbitrary", "arbitrary"),
                                             vmem_limit_bytes=VMEM_LIMIT),
    )(proj3, proj3, proj3, buf8, s0, cw, pa, pb, onorm)


GRP = 4 * N_B


def _rwkv_kernel(r_ref, k_ref, v_ref, wa_ref, z_ref, prev_ref, s0_ref, mu_ref, w0_ref, w2_ref,
                 a0_ref, a2_ref, kkp_ref, kap_ref, rk_ref, gg_ref, gb_ref, bd_ref,
                 o_ref, sout_ref, fs_ref, st_ref, *, nseg, lt, ls, l_valid, l_total):
    l_idx = pl.program_id(1)
    nrow = 4 * CHUNK
    ngrp = W_BRANCH // GRP

    @pl.when(l_idx == 0)
    def _():
        fs_ref[:, 0:HIST, :] = prev_ref[...]
        st_ref[...] = s0_ref[...]

    @pl.when(l_idx > 0)
    def _():
        fs_ref[:, 0:HIST, :] = fs_ref[:, lt:lt + HIST, :]

    fs_ref[:, HIST:lt + HIST, 0:W_BRANCH] = r_ref[...]
    fs_ref[:, HIST:lt + HIST, W_BRANCH:2 * W_BRANCH] = k_ref[...]
    fs_ref[:, HIST:lt + HIST, 2 * W_BRANCH:3 * W_BRANCH] = v_ref[...]
    fs_ref[:, HIST:lt + HIST, 3 * W_BRANCH:SHIFT_B] = wa_ref[...]

    _, m_incl, m_strict = _chunk_masks(nrow, ls)
    same_r, tri_r, _ = _chunk_masks(CHUNK, ls)
    tri_r = tri_r.astype(BF16)
    same_r = same_r.astype(BF16)
    lane_head = _iota2((1, GRP), 1) // N_B
    head_masks = [(lane_head == j).astype(F32) for j in range(4)]
    bd_state = (_iota2((GRP, GRP), 0) // N_B) == (_iota2((GRP, GRP), 1) // N_B)
    mu = mu_ref[...]
    bd = bd_ref[...]

    def mixed(r0, col, width):
        outs = []
        for s in range(nseg):
            win = fs_ref[s, pl.ds(r0, ls + HIST), col:col + width]
            cur = win[HIST:]
            prev = pltpu.roll(win, 1, 0)[HIST:]
            outs.append(cur + (prev - cur) * mu[:, col:col + width])
        return jnp.concatenate(outs, axis=0)

    def stack_masked(x):
        return jnp.concatenate([x * head_masks[j] for j in range(4)], axis=0)

    def unstack(x):
        return x[0:CHUNK] + x[CHUNK:2 * CHUNK] + x[2 * CHUNK:3 * CHUNK] + x[3 * CHUNK:4 * CHUNK]

    def chunk(c, carry):
        r0 = pl.multiple_of(c * ls, ls)
        r = mixed(r0, 0, W_BRANCH)
        k = mixed(r0, W_BRANCH, W_BRANCH)
        v = mixed(r0, 2 * W_BRANCH, W_BRANCH)
        wa = mixed(r0, 3 * W_BRANCH, 2 * LORA)
        w_raw = w0_ref[...] + _mm(jnp.tanh(wa), w2_ref[...])
        logw = -jnp.exp(-_softplus(-w_raw) - 0.5)
        a_sig = _sigmoid(a0_ref[...] + _mm(wa, a2_ref[...]))
        kk = k * kkp_ref[...]
        kk = kk * lax.rsqrt(_head_sum(kk * kk, bd) + 1e-6)
        k_mod = k * (1.0 + (a_sig - 1.0) * kap_ref[...])
        b_vec = kk * a_sig
        if l_valid < l_total:
            ok = _valid_rows(l_idx, lt, r0, ls, l_valid)
            logw = jnp.where(ok, logw, 0.0)
            b_vec = jnp.where(ok, b_vec, 0.0)
            k_mod = jnp.where(ok, k_mod, 0.0)
        gcum = _mm_sel(tri_r, logw)
        gtot = _mm_sel(same_r, logw)
        e_neg = jnp.exp(-gcum)
        e_end = jnp.exp(gtot - gcum)
        a_t = -kk * jnp.exp(gcum - logw)
        b_t = b_vec * e_neg
        k_t = k_mod * e_neg
        r_t = r * jnp.exp(gcum)
        b_h = b_vec * e_end
        k_h = k_mod * e_end

        ys = []
        for g in range(ngrp):
            sl = slice(g * GRP, (g + 1) * GRP)
            a_g, b_g, k_g, r_g, v_g = a_t[:, sl], b_t[:, sl], k_t[:, sl], r_t[:, sl], v[:, sl]
            a_st = stack_masked(a_g)
            r_st = stack_masked(r_g)
            v_st = stack_masked(v_g)
            b_st = jnp.concatenate([b_g] * 4, axis=0)
            k_st = jnp.concatenate([k_g] * 4, axis=0)
            aab = jnp.where(m_strict, _mm(a_st, b_st, NT), 0.0)
            aak = jnp.where(m_strict, _mm(a_st, k_st, NT), 0.0)
            arb = jnp.where(m_incl, _mm(r_st, b_st, NT), 0.0)
            ark = jnp.where(m_incl, _mm(r_st, k_st, NT), 0.0)
            tinv = _inv_unit_lower(-aab, ls, SOLVE_MM_B)
            ams, rms = [], []
            for s in range(nseg):
                st = st_ref[s, g]
                ams.append(_mm(a_g[s * ls:(s + 1) * ls], st, NT))
                rms.append(_mm(r_g[s * ls:(s + 1) * ls], st, NT))
            am = jnp.concatenate(ams, axis=0)
            rm = jnp.concatenate(rms, axis=0)
            u_st = SOLVE_MM_B(tinv, stack_masked(am) + _mm(aak, v_st))
            y_st = stack_masked(rm) + _mm(arb, u_st) + _mm(ark, v_st)
            u = unstack(u_st)
            ys.append(unstack(y_st))
            for s in range(nseg):
                if nseg == 1:
                    u_s, v_s = u, v_g
                else:
                    rows = _iota2((CHUNK, 1), 0) // ls
                    u_s = jnp.where(rows == s, u, 0.0)
                    v_s = jnp.where(rows == s, v_g, 0.0)
                upd = _mm(u_s, b_h[:, sl], TN) + _mm(v_s, k_h[:, sl], TN)
                dec = jnp.exp(gtot[s * ls:s * ls + 1, sl])
                st_ref[s, g] = st_ref[s, g] * dec + jnp.where(bd_state, upd, 0.0)
        y = jnp.concatenate(ys, axis=1)
        mean = _head_sum(y, bd) * (1.0 / N_B)
        yc = y - mean
        var = _head_sum(yc * yc, bd) * (1.0 / N_B)
        yn = yc * lax.rsqrt(var + GN_EPS) * gg_ref[...] + gb_ref[...]
        bonus = _head_sum(r * k_mod * rk_ref[...], bd) * v
        out = yn + bonus
        for s in range(nseg):
            zz = z_ref[s, pl.ds(r0, ls), :]
            o_ref[s, pl.ds(r0, ls), :] = out[s * ls:(s + 1) * ls] * _silu(zz)
        return carry

    lax.fori_loop(0, lt // ls, chunk, 0)

    @pl.when(l_idx == pl.num_programs(1) - 1)
    def _():
        sout_ref[...] = st_ref[...]


def _rwkv(proj3, prev8, s0, vecs, w2p, a2p, bd, *, nseg, lt, ls, l_valid):
    b, l, _ = proj3.shape
    ngrp = W_BRANCH // GRP
    kern = functools.partial(_rwkv_kernel, nseg=nseg, lt=lt, ls=ls, l_valid=l_valid, l_total=l)
    full = lambda shape: pl.BlockSpec(shape, lambda i, j: (0,) * len(shape))
    blk = lambda col, w: pl.BlockSpec((nseg, lt, w), lambda i, j: (i, j, col // w))
    mu, w0, a0, kkp, kap, rk, gg, gb = vecs
    return pl.pallas_call(
        kern,
        grid=(b // nseg, l // lt),
        in_specs=[blk(C_RKV, W_BRANCH), blk(C_RKV + W_BRANCH, W_BRANCH), blk(C_RKV + 2 * W_BRANCH, W_BRANCH),
                  blk(C_WLAL, 2 * LORA), blk(C_Z_B, W_BRANCH),
                  pl.BlockSpec((nseg, HIST, SHIFT_B), lambda i, j: (i, 0, 0)),
                  pl.BlockSpec((nseg, ngrp, GRP, GRP), lambda i, j: (i, 0, 0, 0)),
                  full((1, SHIFT_B)), full((1, W_BRANCH)), full((2 * LORA, W_BRANCH)),
                  full((1, W_BRANCH)), full((2 * LORA, W_BRANCH)), full((1, W_BRANCH)),
                  full((1, W_BRANCH)), full((1, W_BRANCH)), full((1, W_BRANCH)), full((1, W_BRANCH)),
                  full((W_BRANCH, W_BRANCH))],
        out_specs=[pl.BlockSpec((nseg, lt, W_BRANCH), lambda i, j: (i, j, 0)),
                   pl.BlockSpec((nseg, ngrp, GRP, GRP), lambda i, j: (i, 0, 0, 0))],
        out_shape=[jax.ShapeDtypeStruct((b, l, W_BRANCH), F32),
                   jax.ShapeDtypeStruct((b, ngrp, GRP, GRP), F32)],
        scratch_shapes=[pltpu.VMEM((nseg, lt + HIST, SHIFT_B), F32),
                        pltpu.VMEM((nseg, ngrp, GRP, GRP), F32)],
        compiler_params=pltpu.CompilerParams(dimension_semantics=("arbitrary", "arbitrary"),
                                             vmem_limit_bytes=VMEM_LIMIT),
    )(proj3, proj3, proj3, proj3, proj3, prev8, s0, mu, w0, w2p, a0, a2p, kkp, kap, rk, gg, gb, bd)


def _qkprep_kernel(q_ref, k_ref, qn_ref, kn_ref, inv_ref, bd_ref, qo_ref, ko_ref, *, tm, lseq, pos0):
    pos = (pos0 + (pl.program_id(0) * tm + _iota2((tm, 1), 0)) % lseq).astype(F32)
    ang = pos * inv_ref[...]
    cos = jnp.cos(ang)
    sin = jnp.sin(ang)
    d = _iota2((1, LANES), 1) % D_C
    half = ROPE_DIM // 2
    c_main = jnp.where(d < ROPE_DIM, cos, 1.0)
    c_up = jnp.where(d < half, -sin, 0.0)
    c_dn = jnp.where((d >= half) & (d < ROPE_DIM), sin, 0.0)
    bd = bd_ref[...]

    def one(x_ref, g_ref, out_ref):
        x = x_ref[...]
        y = x * lax.rsqrt(_head_sum(x * x, bd) * (1.0 / D_C) + NORM_EPS) * g_ref[...]
        for blk in range(W_BRANCH // LANES):
            yb = y[:, blk * LANES:(blk + 1) * LANES]
            up = pltpu.roll(yb, LANES - half, 1)
            dn = pltpu.roll(yb, half, 1)
            out_ref[:, blk * LANES:(blk + 1) * LANES] = yb * c_main + up * c_up + dn * c_dn

    one(q_ref, qn_ref, qo_ref)
    one(k_ref, kn_ref, ko_ref)


def _qkprep(proj2, qn, kn, inv, bd, *, tm, lseq, pos0):
    t = proj2.shape[0]
    kern = functools.partial(_qkprep_kernel, tm=tm, lseq=lseq, pos0=pos0)
    full = lambda shape: pl.BlockSpec(shape, lambda i: (0,) * len(shape))
    return pl.pallas_call(
        kern,
        grid=(t // tm,),
        in_specs=[pl.BlockSpec((tm, W_BRANCH), lambda i: (i, C_QKV_C // W_BRANCH)),
                  pl.BlockSpec((tm, W_BRANCH), lambda i: (i, C_QKV_C // W_BRANCH + 1)),
                  full((1, W_BRANCH)), full((1, W_BRANCH)), full((1, LANES)), full((W_BRANCH, W_BRANCH))],
        out_specs=[pl.BlockSpec((tm, W_BRANCH), lambda i: (i, 0)),
                   pl.BlockSpec((tm, W_BRANCH), lambda i: (i, 0))],
        out_shape=[jax.ShapeDtypeStruct((t, W_BRANCH), F32), jax.ShapeDtypeStruct((t, W_BRANCH), F32)],
        compiler_params=pltpu.CompilerParams(dimension_semantics=("arbitrary",), vmem_limit_bytes=VMEM_LIMIT),
    )(proj2, proj2, qn, kn, inv, bd)


def _rank_select(gate, nblk, limit, axis):
    idx = _iota2(gate.shape, axis)
    rank = jnp.zeros(gate.shape, F32)
    for m in range(nblk):
        gm = gate[m:m + 1, :] if axis == 0 else gate[:, m:m + 1]
        beats = (gm > gate) | ((gm == gate) & (m < idx))
        rank = rank + jnp.where(beats & (m < limit), 1.0, 0.0)
    return jnp.where((idx < limit) & (rank < MOBA_TOPK), 1.0, 0.0)


def _moba_prompt_kernel(q_ref, k_ref, v_ref, o_ref, *, nblk):
    blk = MOBA_BLOCK
    l = nblk * blk
    scale = D_C ** -0.5
    q = q_ref[...]
    k = k_ref[...]
    kmean = jnp.concatenate([jnp.mean(k[n * blk:(n + 1) * blk], axis=0, keepdims=True) for n in range(nblk)],
                            axis=0)
    gate = _mm32(kmean, q)
    own = _iota2((1, l), 1) // blk
    sel = _rank_select(gate, nblk, own, 0)

    kb = k.astype(BF16)
    qb = (q * scale).astype(BF16)
    causal = _iota2((blk, blk), 0) <= _iota2((blk, blk), 1)
    for j in range(nblk):
        cols = slice(j * blk, (j + 1) * blk)
        nk = (j + 1) * blk
        s = jnp.dot(kb[:nk], qb[:, cols], preferred_element_type=F32)
        parts = [jnp.where(sel[n:n + 1, cols] > 0.0, s[n * blk:(n + 1) * blk], -jnp.inf) for n in range(j)]
        parts.append(jnp.where(causal, s[j * blk:], -jnp.inf))
        s = jnp.concatenate(parts, axis=0)
        m = jnp.max(s, axis=0, keepdims=True)
        p = jnp.exp(s - m)
        denom = jnp.sum(p, axis=0, keepdims=True)
        acc = jnp.dot(v_ref[:, :nk].astype(BF16), p.astype(BF16), preferred_element_type=F32)
        o_ref[:, cols] = acc / denom


def _moba_prompt(q_t, k_b, v_t):
    b, h, _, l = q_t.shape
    nblk = l // MOBA_BLOCK
    assert (4.0 ** round(np.log(D_C) / np.log(4.0))) == D_C
    kern = functools.partial(_moba_prompt_kernel, nblk=nblk)
    lanes = pl.BlockSpec((None, None, D_C, l), lambda i, hh: (i, hh, 0, 0))
    return pl.pallas_call(
        kern,
        grid=(b, h),
        in_specs=[lanes, pl.BlockSpec((None, None, l, D_C), lambda i, hh: (i, hh, 0, 0)), lanes],
        out_specs=lanes,
        out_shape=jax.ShapeDtypeStruct((b, h, D_C, l), F32),
        compiler_params=pltpu.CompilerParams(dimension_semantics=("arbitrary", "arbitrary"),
                                             vmem_limit_bytes=VMEM_LIMIT),
    )(q_t, k_b, v_t)


def _moba_decode_kernel(pt_ref, q_ref, kn_ref, vn_ref, *refs, npage, page, lq):
    k_refs = refs[:npage]
    v_refs = refs[npage:2 * npage]
    o_ref = refs[2 * npage]
    del pt_ref
    scale = D_C ** -0.5
    nrow = lq * H_C
    ppb = MOBA_BLOCK // page
    nblk = npage // ppb
    lp = q_ref.shape[0]

    qall = jnp.concatenate([q_ref[t] for t in range(lq)], axis=0)
    qb = qall.astype(BF16)
    row_head = _iota2((nrow, 1), 0) % H_C

    def own_head_cols(ncol):
        return (_iota2((nrow, ncol), 1) % H_C) == row_head

    page_cols = own_head_cols(page * H_C)
    ms, ls_, accs, ksums = [], [], [], []
    for p_i in range(npage):
        kp = k_refs[p_i][...]
        s = lax.dot_general(qb, kp.astype(BF16), NT, preferred_element_type=F32) * scale
        s = jnp.where(page_cols, s, -jnp.inf)
        m = jnp.max(s, axis=-1, keepdims=True)
        p = jnp.exp(s - m)
        ms.append(m)
        ls_.append(jnp.sum(p, axis=-1, keepdims=True))
        accs.append(jnp.dot(p.astype(BF16), v_refs[p_i][...].astype(BF16), preferred_element_type=F32))
        ksums.append(jnp.sum(kp.reshape(page, H_C, D_C), axis=0))

    kmeans = [sum(ksums[n * ppb:(n + 1) * ppb]) * (1.0 / MOBA_BLOCK) for n in range(nblk)]
    kmean = jnp.concatenate(kmeans, axis=0)
    g_all = _mm32(qall, kmean, NT)
    g_all = jnp.where(own_head_cols(nblk * H_C), g_all, 0.0)
    pick = (_iota2((nblk * H_C, LANES), 0) // H_C == _iota2((nblk * H_C, LANES), 1)).astype(BF16)
    g_hi, g_lo = _split(g_all)
    gate = _dotb(g_hi, pick) + _dotb(g_lo, pick)
    sel = _rank_select(gate, nblk, nblk, 1)

    k_new = kn_ref[...].reshape(lp * H_C, D_C)
    v_new = vn_ref[...].reshape(lp * H_C, D_C)
    s = lax.dot_general(qb, k_new.astype(BF16), NT, preferred_element_type=F32) * scale
    t_key = _iota2((nrow, lp * H_C), 1) // H_C
    t_qry = _iota2((nrow, lp * H_C), 0) // H_C
    s = jnp.where(own_head_cols(lp * H_C) & (t_key <= t_qry), s, -jnp.inf)
    m_all = jnp.max(s, axis=-1, keepdims=True)
    p_own = jnp.exp(s - m_all)
    l_own = jnp.sum(p_own, axis=-1, keepdims=True)
    acc_own = jnp.dot(p_own.astype(BF16), v_new.astype(BF16), preferred_element_type=F32)
    m_own = m_all

    m_eff = []
    for p_i in range(npage):
        n = p_i // ppb
        me = jnp.where(sel[:, n:n + 1] > 0.0, ms[p_i], -jnp.inf)
        m_eff.append(me)
        m_all = jnp.maximum(m_all, me)
    w_own = jnp.exp(m_own - m_all)
    l_tot = l_own * w_own
    acc = acc_own * w_own
    for p_i in range(npage):
        w = jnp.exp(m_eff[p_i] - m_all)
        l_tot = l_tot + ls_[p_i] * w
        acc = acc + accs[p_i] * w
    out = acc / l_tot
    for t in range(lp):
        o_ref[t] = out[t * H_C:(t + 1) * H_C] if t < lq else jnp.zeros((H_C, D_C), F32)


def _moba_decode(page_table, q4, k4, v4, cache_k, cache_v, layer, lq):
    b, lp = q4.shape[:2]
    npage = page_table.shape[1]
    rows = cache_k.shape[2]
    kern = functools.partial(_moba_decode_kernel, npage=npage, page=rows // H_C, lq=lq)
    tok = pl.BlockSpec((None, lp, H_C, D_C), lambda i, pt: (i, 0, 0, 0))

    def page_spec(p_i):
        return pl.BlockSpec((None, None, rows, D_C), lambda i, pt: (layer, pt[i, p_i], 0, 0))

    grid_spec = pltpu.PrefetchScalarGridSpec(
        num_scalar_prefetch=1,
        grid=(b,),
        in_specs=[tok, tok, tok] + [page_spec(p_i) for p_i in range(npage)] * 2,
        out_specs=pl.BlockSpec((None, lp, H_C, D_C), lambda i, pt: (i, 0, 0, 0)),
    )
    return pl.pallas_call(
        kern,
        grid_spec=grid_spec,
        out_shape=jax.ShapeDtypeStruct((b, lp, H_C, D_C), F32),
        compiler_params=pltpu.CompilerParams(dimension_semantics=("arbitrary",), vmem_limit_bytes=VMEM_LIMIT),
    )(page_table, q4, k4, v4, *([cache_k] * npage), *([cache_v] * npage))


def _merge_kernel(oa_ref, ob_ref, oc_ref, zc_ref, gate_ref, x_ref, ple_ref, wb_ref, bg_ref, wo_ref,
                  wpg_ref, bpg_ref, wple_ref, y_ref):
    branches = (oa_ref[...], ob_ref[...], oc_ref[...] * _silu(zc_ref[...]))
    up = None
    for n, o in enumerate(branches):
        sl = slice(n * D_MODEL, (n + 1) * D_MODEL)
        gate = _sigmoid(gate_ref[:, sl] + bg_ref[:, sl])
        term = gate * jnp.dot(o.astype(BF16), wb_ref[n], preferred_element_type=F32)
        up = term if up is None else up + term
    y = x_ref[...] + jnp.dot(up.astype(BF16), wo_ref[...], preferred_element_type=F32)
    pg = _sigmoid(jnp.dot(y.astype(BF16), wpg_ref[...], preferred_element_type=F32) + bpg_ref[...])
    y_ref[...] = y + pg * jnp.dot(ple_ref[...].astype(BF16), wple_ref[...], preferred_element_type=F32)


def _merge(oa, ob, oc, proj2, x2, ple2, wb, bg, wo, wpg, bpg, wple, tm):
    t = x2.shape[0]
    full = lambda shape: pl.BlockSpec(shape, lambda i: (0,) * len(shape))
    row = lambda w: pl.BlockSpec((tm, w), lambda i: (i, 0))
    return pl.pallas_call(
        _merge_kernel,
        grid=(t // tm,),
        in_specs=[row(W_BRANCH), row(W_BRANCH), row(W_BRANCH),
                  pl.BlockSpec((tm, W_BRANCH), lambda i: (i, C_Z_C // W_BRANCH)),
                  pl.BlockSpec((tm, N_BRANCH * D_MODEL), lambda i: (i, C_GATE // (N_BRANCH * D_MODEL))),
                  row(D_MODEL), row(D_PLE),
                  full((N_BRANCH, W_BRANCH, D_MODEL)), full((1, N_BRANCH * D_MODEL)), full((D_MODEL, D_MODEL)),
                  full((D_MODEL, D_MODEL)), full((1, D_MODEL)), full((D_PLE, D_MODEL))],
        out_specs=row(D_MODEL),
        out_shape=jax.ShapeDtypeStruct((t, D_MODEL), F32),
        compiler_params=pltpu.CompilerParams(dimension_semantics=("arbitrary",), vmem_limit_bytes=VMEM_LIMIT),
    )(oa, ob, oc, proj2, proj2, x2, ple2, wb, bg, wo, wpg, bpg, wple)


def _arrange_w_in(w):
    offs = np.cumsum((0,) + IN_SPLITS)
    qkv_a, z_a, beta, alpha, f_b, z_b, qkv_c, z_c, gate = [w[:, offs[i]:offs[i + 1]] for i in range(9)]
    d = w.shape[0]
    ba = jnp.concatenate([beta, alpha, jnp.zeros((d, LANES - 2 * H_A), w.dtype)], axis=1)
    tail = jnp.zeros((d, N_PROJ - C_BA - LANES), w.dtype)
    out = jnp.concatenate([qkv_a, z_a, f_b[:, :3 * W_BRANCH], z_b, qkv_c, z_c, gate,
                           f_b[:, 3 * W_BRANCH:], ba, tail], axis=1)
    return out.astype(BF16)


def _layer_params(i, p):
    row = lambda v: v.reshape(1, -1)
    lanes_ba = lambda v: jnp.zeros((1, LANES), F32).at[0, H_A:2 * H_A].set(v)
    lora_pad = jnp.zeros((LORA, W_BRANCH), F32)
    head = np.arange(W_BRANCH) // N_B
    half = ROPE_DIM // 2
    inv = ROPE_THETA ** (-jnp.arange(half, dtype=F32) / half)
    d = np.arange(LANES) % D_C
    inv_lanes = jnp.where(d < ROPE_DIM, inv[d % half], 0.0).reshape(1, LANES)
    return dict(
        norm_g=row(p['norm_g'][i]),
        w_in=_arrange_w_in(p['w_in'][i]),
        conv=p['conv_a'][i],
        pa=lanes_ba(p['a_log'][i]), pb=lanes_ba(p['dt_bias'][i]), onorm=row(p['onorm_a'][i]),
        rw_vecs=(row(p['mu_b'][i]), row(p['w0_b'][i]), row(p['a0_b'][i]), row(p['kk_b'][i]), row(p['ka_b'][i]),
                 row(p['rk_b'][i]), row(p['gn_g_b'][i]), row(p['gn_b_b'][i])),
        w2p=jnp.concatenate([p['w2_b'][i], lora_pad], axis=0).astype(BF16),
        a2p=jnp.concatenate([lora_pad, p['a2_b'][i]], axis=0).astype(BF16),
        bd=jnp.asarray(head[:, None] == head[None, :], BF16),
        qn=row(jnp.tile(p['qn_c'][i], H_C)), kn=row(jnp.tile(p['kn_c'][i], H_C)), inv=inv_lanes,
        wb=p['w_branch'][i].astype(BF16), bg=row(p['b_gate'][i]), wo=p['w_out'][i].astype(BF16),
        wpg=p['w_pg'][i].astype(BF16), bpg=row(p['b_pg'][i]), wple=p['w_ple'][i].astype(BF16),
    )


def _rw_state_in(s):
    b = s.shape[0]
    ngrp = W_BRANCH // GRP
    eye = jnp.eye(4, dtype=F32).reshape(1, 1, 4, 1, 4, 1)
    return (s.reshape(b, ngrp, 4, N_B, 1, N_B) * eye).reshape(b, ngrp, GRP, GRP)


def _rw_state_out(s):
    b = s.shape[0]
    s6 = s.reshape(b, W_BRANCH // GRP, 4, N_B, 4, N_B)
    return jnp.einsum('bgjvjk->bgjvk', s6).reshape(b, H_B, N_B, N_B)


def _hist_rows(rows):
    b, n, c = rows.shape
    return jnp.concatenate([jnp.zeros((b, HIST - n, c), F32), rows], axis=1)


def _layer(x3, ple3, lp, *, l_valid, pos0, dn_s, dn_buf, rw_s, rw_prev, nseg, lt, ls, tm, attend):
    b, l, d = x3.shape
    t = b * l
    proj2 = _proj(x3.reshape(t, d), lp['norm_g'], lp['w_in'], min(t, 1024), 1920)
    proj3 = proj2.reshape(b, l, N_PROJ)

    oa, dn_new = _gdn(proj3, _hist_rows(dn_buf), dn_s, lp['conv'], lp['pa'], lp['pb'], lp['onorm'],
                      nseg=nseg, lt=lt, ls=ls, l_valid=l_valid)
    ob, rw_new = _rwkv(proj3, _hist_rows(rw_prev[:, None, :]), _rw_state_in(rw_s), lp['rw_vecs'],
                       lp['w2p'], lp['a2p'], lp['bd'], nseg=nseg, lt=lt, ls=ls, l_valid=l_valid)
    qr, kr = _qkprep(proj2, lp['qn'], lp['kn'], lp['inv'], lp['bd'], tm=min(tm, t), lseq=l, pos0=pos0)
    v2 = proj2[:, C_QKV_C + 2 * W_BRANCH:C_QKV_C + 3 * W_BRANCH]
    oc = attend(qr, kr, v2)

    y2 = _merge(oa.reshape(t, W_BRANCH), ob.reshape(t, W_BRANCH), oc.reshape(t, W_BRANCH), proj2,
                x3.reshape(t, d), ple3.reshape(t, D_PLE), lp['wb'], lp['bg'], lp['wo'], lp['wpg'],
                lp['bpg'], lp['wple'], min(tm, t))

    f_last = jnp.concatenate([proj3[:, l_valid - 1, C_RKV:C_RKV + 3 * W_BRANCH],
                              proj3[:, l_valid - 1, C_WLAL:C_WLAL + 2 * LORA]], axis=-1)
    new_state = (kr.reshape(b, l, H_C, D_C)[:, :l_valid], v2.reshape(b, l, H_C, D_C)[:, :l_valid], dn_new,
                 proj3[:, l_valid - (CONV_W - 1):l_valid, C_QKV_A:C_QKV_A + QKV_A],
                 _rw_state_out(rw_new), f_last)
    return y2.reshape(b, l, d), new_state


def _attend_prompt(b, l):
    nblk = l // MOBA_BLOCK

    def attend(qr, kr, v2):
        q_t = qr.reshape(b, l, H_C, D_C).transpose(0, 2, 3, 1)
        k_b = kr.reshape(b, l, H_C, D_C).transpose(0, 2, 1, 3)
        v_t = v2.reshape(b, l, H_C, D_C).transpose(0, 2, 3, 1)
        o_t = _moba_prompt(q_t, k_b, v_t)
        return o_t.transpose(0, 3, 1, 2).reshape(b * l, W_BRANCH)
    return attend


def _attend_decode(b, lp_, lq, page_table, cache_k4, cache_v4, layer):
    def attend(qr, kr, v2):
        shp = (b, lp_, H_C, D_C)
        o4 = _moba_decode(page_table, qr.reshape(shp), kr.reshape(shp), v2.reshape(shp),
                          cache_k4, cache_v4, layer, lq)
        return o4.reshape(b * lp_, W_BRANCH)
    return attend


DEC_PAD = 8


def kernel(x_prompt, x_sample, cache_k, cache_v, state_dn, state_dn_conv, state_rwkv, state_rwkv_shift,
           page_table, p_prompt, p_sample, norm_g, w_in, conv_a, a_log, dt_bias, onorm_a, mu_b, w0_b, w2_b,
           a0_b, a2_b, kk_b, ka_b, rk_b, gn_g_b, gn_b_b, qn_c, kn_c, w_branch, b_gate, w_out, w_ple, w_pg,
           b_pg):
    params = dict(norm_g=norm_g, w_in=w_in, conv_a=conv_a, a_log=a_log, dt_bias=dt_bias, onorm_a=onorm_a,
                  mu_b=mu_b, w0_b=w0_b, w2_b=w2_b, a0_b=a0_b, a2_b=a2_b, kk_b=kk_b, ka_b=ka_b, rk_b=rk_b,
                  gn_g_b=gn_g_b, gn_b_b=gn_b_b, qn_c=qn_c, kn_c=kn_c, w_branch=w_branch, b_gate=b_gate,
                  w_out=w_out, w_ple=w_ple, w_pg=w_pg, b_pg=b_pg)
    depth = norm_g.shape[0]
    b_p, l_p, _ = x_prompt.shape
    b_s, l_s, _ = x_sample.shape
    n_phys, page = cache_k.shape[1], cache_k.shape[2]
    past_len = page_table.shape[1] * page
    assert l_p % 512 == 0 and CONV_W - 1 <= l_s <= DEC_PAD and b_s % (CHUNK // DEC_PAD) == 0
    assert past_len % MOBA_BLOCK == 0 and MOBA_BLOCK % page == 0

    pad_rows = lambda a: jnp.pad(a, ((0, 0), (0, DEC_PAD - l_s), (0, 0)))
    xs = pad_rows(x_sample)
    cache_k4 = cache_k.reshape(depth, n_phys, page * H_C, D_C)
    cache_v4 = cache_v.reshape(depth, n_phys, page * H_C, D_C)
    xp = x_prompt
    new_p, new_s = [], []
    for i in range(depth):
        lp = _layer_params(i, params)
        xp, st_p = _layer(xp, p_prompt[i], lp, l_valid=l_p, pos0=0,
                          dn_s=jnp.zeros((b_p, H_A, DK_A, DV_A), F32),
                          dn_buf=jnp.zeros((b_p, CONV_W - 1, QKV_A), F32),
                          rw_s=jnp.zeros((b_p, H_B, N_B, N_B), F32), rw_prev=jnp.zeros((b_p, SHIFT_B), F32),
                          nseg=1, lt=512, ls=CHUNK, tm=512, attend=_attend_prompt(b_p, l_p))
        new_p.append(st_p)
        xs, st_s = _layer(xs, pad_rows(p_sample[i]), lp, l_valid=l_s, pos0=past_len,
                          dn_s=state_dn[i], dn_buf=state_dn_conv[i], rw_s=state_rwkv[i],
                          rw_prev=state_rwkv_shift[i], nseg=CHUNK // DEC_PAD, lt=DEC_PAD, ls=DEC_PAD, tm=512,
                          attend=_attend_decode(b_s, DEC_PAD, l_s, page_table, cache_k4, cache_v4, i))
        new_s.append(st_s)
    outs_p = [jnp.stack(t) for t in zip(*new_p)]
    outs_s = [jnp.stack(t) for t in zip(*new_s)]
    return (xp, xs[:, :l_s], *outs_p, *outs_s)
```

```python
import functools

import numpy as np
import jax
import jax.numpy as jnp
from jax import lax
from jax.experimental import pallas as pl
from jax.experimental.pallas import tpu as pltpu

F32 = jnp.float32
BF16 = jnp.bfloat16

D_MODEL = 1024
D_PLE = 256
N_BRANCH = 3
W_BRANCH = 512
NORM_EPS = 1e-6
DK_A = 128
DV_A = 128
H_A = 4
QKV_A = 1536
CONV_W = 4
N_B = 64
H_B = 8
LORA = 64
SHIFT_B = 3 * W_BRANCH + 2 * LORA
GN_EPS = 64e-5
D_C = 64
H_C = 8
MOBA_BLOCK = 256
MOBA_TOPK = 3
ROPE_DIM = 16
ROPE_THETA = 500000.0
IN_SPLITS = (QKV_A, W_BRANCH, H_A, H_A, SHIFT_B, W_BRANCH, 3 * W_BRANCH, W_BRANCH, N_BRANCH * D_MODEL)

C_QKV_A = 0
C_Z_A = 1536
C_RKV = 2048
C_Z_B = 3584
C_QKV_C = 4096
C_Z_C = 5632
C_GATE = 6144
C_WLAL = 9216
C_BA = 9344
N_PROJ = 9600

CHUNK = 64
HIST = 8
CHUNKS_PER_TRIP_A = 8
CHUNKS_PER_TRIP_B = 4
LANES = 128
VMEM_LIMIT = 56 * 1024 * 1024

NN = (((1,), (0,)), ((), ()))
NT = (((1,), (1,)), ((), ()))
TN = (((0,), (0,)), ((), ()))
HIGHEST = lax.Precision.HIGHEST


def _mm(a, b, dims=NN):
    return lax.dot_general(a.astype(BF16), b.astype(BF16), dims, preferred_element_type=F32)


def _mm32(a, b, dims=NN):
    return lax.dot_general(a, b, dims, precision=HIGHEST, preferred_element_type=F32)


def _sigmoid(x):
    return 1.0 / (1.0 + jnp.exp(-x))


def _silu(x):
    return x * _sigmoid(x)


def _softplus(x):
    return jnp.maximum(x, 0.0) + jnp.log1p(jnp.exp(-jnp.abs(x)))


def _iota2(shape, dim):
    return lax.broadcasted_iota(jnp.int32, shape, dim)


def _split(x):
    hi = x.astype(BF16)
    return hi, (x - hi.astype(F32)).astype(BF16)


def _dotb(a, b, dims=NN):
    return lax.dot_general(a, b, dims, preferred_element_type=F32)


def _mm_sel(sel, x):
    hi, lo = _split(x)
    return _dotb(sel, hi) + _dotb(sel, lo)


def _mm3(a, b, dims=NN):
    ah, al = _split(a)
    bh, bl = _split(b)
    return _dotb(ah, bh, dims) + (_dotb(ah, bl, dims) + _dotb(al, bh, dims))


SOLVE_MM_A = _mm
SOLVE_MM_B = _mm


def _inv_unit_lower(lmats, nilpotent, mm):
    n = lmats[0].shape[0]
    eye = (_iota2((n, n), 0) == _iota2((n, n), 1)).astype(F32)
    ps = [-lmat for lmat in lmats]
    accs = [eye + p for p in ps]
    span = 2
    while span < nilpotent:
        ps = [mm(p, p) for p in ps]
        accs = [acc + mm(acc, p) for acc, p in zip(accs, ps)]
        span *= 2
    return accs


def _head_sums(xs, bd):
    n = xs[0].shape[0]
    parts = []
    for x in xs:
        parts.extend(_split(x))
    out = jnp.dot(jnp.concatenate(parts, axis=0), bd, preferred_element_type=F32)
    return [out[2 * i * n:(2 * i + 1) * n] + out[(2 * i + 1) * n:(2 * i + 2) * n] for i in range(len(xs))]


def _head_sum(x, bd):
    return _head_sums([x], bd)[0]


def _proj_kernel(x_ref, g_ref, w_ref, o_ref):
    x = x_ref[...]
    h = x * lax.rsqrt(jnp.mean(x * x, axis=-1, keepdims=True) + NORM_EPS) * g_ref[...]
    o_ref[...] = jnp.dot(h.astype(BF16), w_ref[...], preferred_element_type=F32)


def _proj(x2d, g, w, tm, tn):
    t, d = x2d.shape
    n = w.shape[1]
    return pl.pallas_call(
        _proj_kernel,
        grid=(n // tn, t // tm),
        in_specs=[pl.BlockSpec((tm, d), lambda j, i: (i, 0)),
                  pl.BlockSpec((1, d), lambda j, i: (0, 0)),
                  pl.BlockSpec((d, tn), lambda j, i: (0, j))],
        out_specs=pl.BlockSpec((tm, tn), lambda j, i: (i, j)),
        out_shape=jax.ShapeDtypeStruct((t, n), F32),
        compiler_params=pltpu.CompilerParams(dimension_semantics=("arbitrary", "arbitrary"),
                                             vmem_limit_bytes=VMEM_LIMIT),
    )(x2d, g, w)


def _chunk_masks(n, ls):
    r = _iota2((n, n), 0)
    c = _iota2((n, n), 1)
    same = (r // ls) == (c // ls)
    return same, same & (r >= c), same & (r > c)


def _valid_rows(l_idx, lt, r0, ls, l_valid):
    t = l_idx * lt + r0 + (_iota2((CHUNK, 1), 0) % ls)
    return t < l_valid


def _gdn_kernel(qkv_ref, z_ref, ba_ref, buf_ref, s0_ref, cw_ref, pa_ref, pb_ref, on_ref,
                o_ref, sout_ref, xs_ref, st_ref, *, nseg, lt, ls, l_valid, l_total):
    l_idx = pl.program_id(1)
    nrow = H_A * CHUNK

    @pl.when(l_idx == 0)
    def _():
        xs_ref[:, 0:HIST, :] = buf_ref[...]
        st_ref[...] = s0_ref[...]

    @pl.when(l_idx > 0)
    def _():
        xs_ref[:, 0:HIST, :] = xs_ref[:, lt:lt + HIST, :]

    xs_ref[:, HIST:lt + HIST, :] = qkv_ref[...]

    _, m_incl, m_strict = _chunk_masks(nrow, ls)
    same_r, tri_r, _ = _chunk_masks(CHUNK, ls)
    tri_r = tri_r.astype(BF16)
    same_r = same_r.astype(BF16)
    eye_n = (_iota2((nrow, nrow), 0) == _iota2((nrow, nrow), 1)).astype(F32)
    cw = cw_ref[...]
    neg_a = -jnp.exp(pa_ref[...])
    pb = pb_ref[...]
    onorm = on_ref[...]

    def conv(s, r0, col):
        win = xs_ref[s, pl.ds(r0, ls + HIST), col:col + LANES]
        w = cw[:, col:col + LANES]
        acc = pltpu.roll(win, 3, 0)[HIST:] * w[0:1] + pltpu.roll(win, 2, 0)[HIST:] * w[1:2]
        acc = acc + pltpu.roll(win, 1, 0)[HIST:] * w[2:3]
        acc = acc + win[HIST:] * w[3:4]
        return _silu(acc)

    def part(r0, col):
        return jnp.concatenate([conv(s, r0, col) for s in range(nseg)], axis=0)

    def local(c):
        r0 = pl.multiple_of(c * ls, ls)
        ba = jnp.concatenate([ba_ref[s, pl.ds(r0, ls), :] for s in range(nseg)], axis=0)
        beta_all = _sigmoid(ba)
        g_all = neg_a * _softplus(ba + pb)
        if l_valid < l_total:
            ok = _valid_rows(l_idx, lt, r0, ls, l_valid)
            beta_all = jnp.where(ok, beta_all, 0.0)
            g_all = jnp.where(ok, g_all, 0.0)
        gcum = _mm_sel(tri_r, g_all)
        if nseg == 1:
            gtot = jnp.broadcast_to(gcum[CHUNK - 1:CHUNK], gcum.shape)
        else:
            gtot = _mm_sel(same_r, g_all)

        qs, ks, vs, betas, gcs, gts = [], [], [], [], [], []
        for h in range(H_A):
            q = part(r0, h * DK_A)
            k = part(r0, H_A * DK_A + h * DK_A)
            v = part(r0, 2 * H_A * DK_A + h * DV_A)
            q = q * lax.rsqrt(jnp.sum(q * q, axis=-1, keepdims=True) + 1e-6) * (DK_A ** -0.5)
            k = k * lax.rsqrt(jnp.sum(k * k, axis=-1, keepdims=True) + 1e-6)
            qs.append(q)
            ks.append(k)
            vs.append(v)
            betas.append(beta_all[:, h:h + 1])
            gcs.append(gcum[:, H_A + h:H_A + h + 1])
            gts.append(gtot[:, H_A + h:H_A + h + 1])
        q_st = jnp.concatenate(qs, axis=0)
        k_st = jnp.concatenate(ks, axis=0)
        v_st = jnp.concatenate(vs, axis=0)
        beta = jnp.concatenate(betas, axis=0)
        g_col = jnp.concatenate(gcs, axis=0)
        gt_col = jnp.concatenate(gts, axis=0)
        g_row = jnp.sum(eye_n * g_col, axis=0, keepdims=True)

        decay = jnp.exp(jnp.where(m_incl, g_col - g_row, -jnp.inf))
        kq_k = _mm(jnp.concatenate([k_st, q_st], axis=0), k_st, NT)
        return dict(
            r0=r0, gt_col=gt_col,
            lmat=jnp.where(m_strict, beta * kq_k[:nrow] * decay, 0.0),
            rhs=jnp.concatenate([v_st * beta, k_st * (beta * jnp.exp(g_col))], axis=1),
            qk=jnp.where(m_incl, kq_k[nrow:] * decay, 0.0),
            qg=q_st * jnp.exp(g_col),
            kd=k_st * jnp.exp(gt_col - g_col))

    def advance(loc, ainv):
        r0, gt_col, qk, qg, kd = loc['r0'], loc['gt_col'], loc['qk'], loc['qg'], loc['kd']
        sol = SOLVE_MM_A(ainv, loc['rhs'])
        bv = sol[:, :DV_A]
        wk = sol[:, DV_A:]
        wks, qgs = [], []
        for h in range(H_A):
            for s in range(nseg):
                a = h * CHUNK + s * ls
                both = _mm(jnp.concatenate([wk[a:a + ls], qg[a:a + ls]], axis=0), st_ref[s, h])
                wks.append(both[:ls])
                qgs.append(both[ls:])
        u = bv - jnp.concatenate(wks, axis=0)
        o = jnp.concatenate(qgs, axis=0) + _mm(qk, u)

        for h in range(H_A):
            a = h * CHUNK
            for s in range(nseg):
                b = a + s * ls
                if nseg == 1:
                    kd_s = kd[a:a + CHUNK]
                else:
                    rows = _iota2((CHUNK, 1), 0) // ls
                    kd_s = jnp.where(rows == s, kd[a:a + CHUNK], 0.0)
                upd = _mm(kd_s, u[a:a + CHUNK], TN)
                st_ref[s, h] = st_ref[s, h] * jnp.exp(gt_col[b:b + 1]) + upd
            oh = o[a:a + CHUNK]
            oh = oh * lax.rsqrt(jnp.mean(oh * oh, axis=-1, keepdims=True) + NORM_EPS) * onorm
            for s in range(nseg):
                zz = z_ref[s, pl.ds(r0, ls), h * DV_A:(h + 1) * DV_A]
                o_ref[s, pl.ds(r0, ls), h * DV_A:(h + 1) * DV_A] = oh[s * ls:(s + 1) * ls] * _silu(zz)

    nchunk = lt // ls
    per_trip = min(CHUNKS_PER_TRIP_A, nchunk)

    def trip(i, carry):
        locs = [local(i * per_trip + j) for j in range(per_trip)]
        ainvs = _inv_unit_lower([loc['lmat'] for loc in locs], ls, SOLVE_MM_A)
        for loc, ainv in zip(locs, ainvs):
            advance(loc, ainv)
        return carry

    lax.fori_loop(0, nchunk // per_trip, trip, 0)

    @pl.when(l_idx == pl.num_programs(1) - 1)
    def _():
        sout_ref[...] = st_ref[...]


def _gdn(proj3, buf8, s0, cw, pa, pb, onorm, *, nseg, lt, ls, l_valid):
    b, l, _ = proj3.shape
    kern = functools.partial(_gdn_kernel, nseg=nseg, lt=lt, ls=ls, l_valid=l_valid, l_total=l)
    full = lambda shape: pl.BlockSpec(shape, lambda i, j: (0,) * len(shape))
    return pl.pallas_call(
        kern,
        grid=(b // nseg, l // lt),
        in_specs=[pl.BlockSpec((nseg, lt, QKV_A), lambda i, j: (i, j, C_QKV_A // QKV_A)),
                  pl.BlockSpec((nseg, lt, W_BRANCH), lambda i, j: (i, j, C_Z_A // W_BRANCH)),
                  pl.BlockSpec((nseg, lt, LANES), lambda i, j: (i, j, C_BA // LANES)),
                  pl.BlockSpec((nseg, HIST, QKV_A), lambda i, j: (i, 0, 0)),
                  pl.BlockSpec((nseg, H_A, DK_A, DV_A), lambda i, j: (i, 0, 0, 0)),
                  full((CONV_W, QKV_A)), full((1, LANES)), full((1, LANES)), full((1, DV_A))],
        out_specs=[pl.BlockSpec((nseg, lt, W_BRANCH), lambda i, j: (i, j, 0)),
                   pl.BlockSpec((nseg, H_A, DK_A, DV_A), lambda i, j: (i, 0, 0, 0))],
        out_shape=[jax.ShapeDtypeStruct((b, l, W_BRANCH), F32),
                   jax.ShapeDtypeStruct((b, H_A, DK_A, DV_A), F32)],
        scratch_shapes=[pltpu.VMEM((nseg, lt + HIST, QKV_A), F32),
                        pltpu.VMEM((nseg, H_A, DK_A, DV_A), F32)],
        compiler_params=pltpu.CompilerParams(dimension_semantics=("arbitrary", "arbitrary"),
                                             vmem_limit_bytes=VMEM_LIMIT),
    )(proj3, proj3, proj3, buf8, s0, cw, pa, pb, onorm)


GRP = 4 * N_B


def _rwkv_kernel(r_ref, k_ref, v_ref, wa_ref, z_ref, prev_ref, s0_ref, mu_ref, w0_ref, w2_ref,
                 a0_ref, a2_ref, kkp_ref, kap_ref, rk_ref, gg_ref, gb_ref, bd_ref,
                 o_ref, sout_ref, fs_ref, st_ref, *, nseg, lt, ls, l_valid, l_total):
    l_idx = pl.program_id(1)
    nrow = 4 * CHUNK
    ngrp = W_BRANCH // GRP

    @pl.when(l_idx == 0)
    def _():
        fs_ref[:, 0:HIST, :] = prev_ref[...]
        st_ref[...] = s0_ref[...]

    @pl.when(l_idx > 0)
    def _():
        fs_ref[:, 0:HIST, :] = fs_ref[:, lt:lt + HIST, :]

    fs_ref[:, HIST:lt + HIST, 0:W_BRANCH] = r_ref[...]
    fs_ref[:, HIST:lt + HIST, W_BRANCH:2 * W_BRANCH] = k_ref[...]
    fs_ref[:, HIST:lt + HIST, 2 * W_BRANCH:3 * W_BRANCH] = v_ref[...]
    fs_ref[:, HIST:lt + HIST, 3 * W_BRANCH:SHIFT_B] = wa_ref[...]

    _, m_incl, m_strict = _chunk_masks(nrow, ls)
    same_r, tri_r, _ = _chunk_masks(CHUNK, ls)
    tri_r = tri_r.astype(BF16)
    same_r = same_r.astype(BF16)
    lane_head = _iota2((1, GRP), 1) // N_B
    head_masks = [(lane_head == j).astype(F32) for j in range(4)]
    bd_state = (_iota2((GRP, GRP), 0) // N_B) == (_iota2((GRP, GRP), 1) // N_B)
    head_rows = ((_iota2((nrow, GRP), 0) // CHUNK) == (_iota2((nrow, GRP), 1) // N_B)).astype(F32)
    t_row = _iota2((nrow, CHUNK), 0) % CHUNK
    t_col = _iota2((nrow, CHUNK), 1)
    t_same = (t_row // ls) == (t_col // ls)
    t_incl = t_same & (t_col <= t_row)
    t_strict = t_same & (t_col < t_row)
    mu = mu_ref[...]
    bd = bd_ref[...]

    def mixed(r0, col, width):
        outs = []
        for s in range(nseg):
            win = fs_ref[s, pl.ds(r0, ls + HIST), col:col + width]
            cur = win[HIST:]
            prev = pltpu.roll(win, 1, 0)[HIST:]
            outs.append(cur + (prev - cur) * mu[:, col:col + width])
        return jnp.concatenate(outs, axis=0)

    def stack_masked(x):
        return jnp.concatenate([x * head_masks[j] for j in range(4)], axis=0)

    def unstack(x):
        return x[0:CHUNK] + x[CHUNK:2 * CHUNK] + x[2 * CHUNK:3 * CHUNK] + x[3 * CHUNK:4 * CHUNK]

    def local(c):
        r0 = pl.multiple_of(c * ls, ls)
        r = mixed(r0, 0, W_BRANCH)
        k = mixed(r0, W_BRANCH, W_BRANCH)
        v = mixed(r0, 2 * W_BRANCH, W_BRANCH)
        wa = mixed(r0, 3 * W_BRANCH, 2 * LORA)
        w_raw = w0_ref[...] + _mm(jnp.tanh(wa), w2_ref[...])
        logw = -jnp.exp(-_softplus(-w_raw) - 0.5)
        a_sig = _sigmoid(a0_ref[...] + _mm(wa, a2_ref[...]))
        kk = k * kkp_ref[...]
        k_mod = k * (1.0 + (a_sig - 1.0) * kap_ref[...])
        kk_ss, rk_sum = _head_sums([kk * kk, r * k_mod * rk_ref[...]], bd)
        kk = kk * lax.rsqrt(kk_ss + 1e-6)
        b_vec = kk * a_sig
        if l_valid < l_total:
            ok = _valid_rows(l_idx, lt, r0, ls, l_valid)
            logw = jnp.where(ok, logw, 0.0)
            b_vec = jnp.where(ok, b_vec, 0.0)
            k_mod = jnp.where(ok, k_mod, 0.0)
        gcum = _mm_sel(tri_r, logw)
        if nseg == 1:
            gtot = jnp.broadcast_to(gcum[CHUNK - 1:CHUNK], gcum.shape)
        else:
            gtot = _mm_sel(same_r, logw)
        e_neg = jnp.exp(-gcum)
        e_end = jnp.exp(gtot - gcum)
        a_t = -kk * jnp.exp(gcum - logw)
        b_t = b_vec * e_neg
        k_t = k_mod * e_neg
        r_t = r * jnp.exp(gcum)
        b_h = b_vec * e_end
        k_h = k_mod * e_end

        groups = []
        for g in range(ngrp):
            sl = slice(g * GRP, (g + 1) * GRP)
            a_g, b_g, k_g, r_g, v_g = a_t[:, sl], b_t[:, sl], k_t[:, sl], r_t[:, sl], v[:, sl]
            a_st = stack_masked(a_g)
            r_st = stack_masked(r_g)
            b_st = jnp.concatenate([b_g] * 4, axis=0)
            ar_k = _mm(jnp.concatenate([a_st, r_st], axis=0), k_g, NT)
            groups.append(dict(
                a_g=a_g, r_g=r_g, v_g=v_g, b_h=b_h[:, sl], k_h=k_h[:, sl], gtot=gtot[:, sl],
                neg_aab=-jnp.where(m_strict, _mm(a_st, b_st, NT), 0.0),
                aak=jnp.where(t_strict, ar_k[:nrow], 0.0),
                ark=jnp.where(t_incl, ar_k[nrow:], 0.0),
                arb=jnp.where(t_incl, _mm(r_st, b_g, NT), 0.0)))
        return dict(r0=r0, v=v, rk_sum=rk_sum, groups=groups)

    def advance(loc, tinvs):
        ys = []
        for g, (grp, tinv) in enumerate(zip(loc['groups'], tinvs)):
            a_g, r_g, v_g = grp['a_g'], grp['r_g'], grp['v_g']
            ams, rms = [], []
            for s in range(nseg):
                rows_s = slice(s * ls, (s + 1) * ls)
                both = _mm(jnp.concatenate([a_g[rows_s], r_g[rows_s]], axis=0), st_ref[s, g], NT)
                ams.append(both[:ls])
                rms.append(both[ls:])
            am = jnp.concatenate(ams, axis=0)
            rm = jnp.concatenate(rms, axis=0)
            kv = _mm(jnp.concatenate([grp['aak'], grp['ark']], axis=0), v_g)
            u_st = SOLVE_MM_B(tinv, stack_masked(am) + kv[:nrow] * head_rows)
            u = unstack(u_st)
            ys.append(rm + unstack((_mm(grp['arb'], u) + kv[nrow:]) * head_rows))
            for s in range(nseg):
                if nseg == 1:
                    u_s, v_s = u, v_g
                else:
                    rows = _iota2((CHUNK, 1), 0) // ls
                    u_s = jnp.where(rows == s, u, 0.0)
                    v_s = jnp.where(rows == s, v_g, 0.0)
                upd = _mm(jnp.concatenate([u_s, v_s], axis=0),
                          jnp.concatenate([grp['b_h'], grp['k_h']], axis=0), TN)
                dec = jnp.exp(grp['gtot'][s * ls:s * ls + 1])
                st_ref[s, g] = st_ref[s, g] * dec + jnp.where(bd_state, upd, 0.0)
        y = jnp.concatenate(ys, axis=1)
        mean = _head_sums([y], bd)[0] * (1.0 / N_B)
        yc = y - mean
        var = _head_sums([yc * yc], bd)[0] * (1.0 / N_B)
        yn = yc * lax.rsqrt(var + GN_EPS) * gg_ref[...] + gb_ref[...]
        out = yn + loc['rk_sum'] * loc['v']
        for s in range(nseg):
            zz = z_ref[s, pl.ds(loc['r0'], ls), :]
            o_ref[s, pl.ds(loc['r0'], ls), :] = out[s * ls:(s + 1) * ls] * _silu(zz)

    nchunk = lt // ls
    per_trip = min(CHUNKS_PER_TRIP_B, nchunk)

    def trip(i, carry):
        locs = [local(i * per_trip + j) for j in range(per_trip)]
        tinvs = _inv_unit_lower([grp['neg_aab'] for loc in locs for grp in loc['groups']], ls, SOLVE_MM_B)
        for j, loc in enumerate(locs):
            advance(loc, tinvs[j * ngrp:(j + 1) * ngrp])
        return carry

    lax.fori_loop(0, nchunk // per_trip, trip, 0)

    @pl.when(l_idx == pl.num_programs(1) - 1)
    def _():
        sout_ref[...] = st_ref[...]


def _rwkv(proj3, prev8, s0, vecs, w2p, a2p, bd, *, nseg, lt, ls, l_valid):
    b, l, _ = proj3.shape
    ngrp = W_BRANCH // GRP
    kern = functools.partial(_rwkv_kernel, nseg=nseg, lt=lt, ls=ls, l_valid=l_valid, l_total=l)
    full = lambda shape: pl.BlockSpec(shape, lambda i, j: (0,) * len(shape))
    blk = lambda col, w: pl.BlockSpec((nseg, lt, w), lambda i, j: (i, j, col // w))
    mu, w0, a0, kkp, kap, rk, gg, gb = vecs
    return pl.pallas_call(
        kern,
        grid=(b // nseg, l // lt),
        in_specs=[blk(C_RKV, W_BRANCH), blk(C_RKV + W_BRANCH, W_BRANCH), blk(C_RKV + 2 * W_BRANCH, W_BRANCH),
                  blk(C_WLAL, 2 * LORA), blk(C_Z_B, W_BRANCH),
                  pl.BlockSpec((nseg, HIST, SHIFT_B), lambda i, j: (i, 0, 0)),
                  pl.BlockSpec((nseg, ngrp, GRP, GRP), lambda i, j: (i, 0, 0, 0)),
                  full((1, SHIFT_B)), full((1, W_BRANCH)), full((2 * LORA, W_BRANCH)),
                  full((1, W_BRANCH)), full((2 * LORA, W_BRANCH)), full((1, W_BRANCH)),
                  full((1, W_BRANCH)), full((1, W_BRANCH)), full((1, W_BRANCH)), full((1, W_BRANCH)),
                  full((W_BRANCH, W_BRANCH))],
        out_specs=[pl.BlockSpec((nseg, lt, W_BRANCH), lambda i, j: (i, j, 0)),
                   pl.BlockSpec((nseg, ngrp, GRP, GRP), lambda i, j: (i, 0, 0, 0))],
        out_shape=[jax.ShapeDtypeStruct((b, l, W_BRANCH), F32),
                   jax.ShapeDtypeStruct((b, ngrp, GRP, GRP), F32)],
        scratch_shapes=[pltpu.VMEM((nseg, lt + HIST, SHIFT_B), F32),
                        pltpu.VMEM((nseg, ngrp, GRP, GRP), F32)],
        compiler_params=pltpu.CompilerParams(dimension_semantics=("arbitrary", "arbitrary"),
                                             vmem_limit_bytes=VMEM_LIMIT),
    )(proj3, proj3, proj3, proj3, proj3, prev8, s0, mu, w0, w2p, a0, a2p, kkp, kap, rk, gg, gb, bd)


def _qkprep_kernel(q_ref, k_ref, qn_ref, kn_ref, inv_ref, bd_ref, qo_ref, ko_ref, *, tm, lseq, pos0):
    pos = (pos0 + (pl.program_id(0) * tm + _iota2((tm, 1), 0)) % lseq).astype(F32)
    ang = pos * inv_ref[...]
    cos = jnp.cos(ang)
    sin = jnp.sin(ang)
    d = _iota2((1, LANES), 1) % D_C
    half = ROPE_DIM // 2
    c_main = jnp.where(d < ROPE_DIM, cos, 1.0)
    c_up = jnp.where(d < half, -sin, 0.0)
    c_dn = jnp.where((d >= half) & (d < ROPE_DIM), sin, 0.0)
    bd = bd_ref[...]

    def one(x_ref, g_ref, out_ref):
        x = x_ref[...]
        y = x * lax.rsqrt(_head_sum(x * x, bd) * (1.0 / D_C) + NORM_EPS) * g_ref[...]
        for blk in range(W_BRANCH // LANES):
            yb = y[:, blk * LANES:(blk + 1) * LANES]
            up = pltpu.roll(yb, LANES - half, 1)
            dn = pltpu.roll(yb, half, 1)
            out_ref[:, blk * LANES:(blk + 1) * LANES] = yb * c_main + up * c_up + dn * c_dn

    one(q_ref, qn_ref, qo_ref)
    one(k_ref, kn_ref, ko_ref)


def _qkprep(proj2, qn, kn, inv, bd, *, tm, lseq, pos0):
    t = proj2.shape[0]
    kern = functools.partial(_qkprep_kernel, tm=tm, lseq=lseq, pos0=pos0)
    full = lambda shape: pl.BlockSpec(shape, lambda i: (0,) * len(shape))
    return pl.pallas_call(
        kern,
        grid=(t // tm,),
        in_specs=[pl.BlockSpec((tm, W_BRANCH), lambda i: (i, C_QKV_C // W_BRANCH)),
                  pl.BlockSpec((tm, W_BRANCH), lambda i: (i, C_QKV_C // W_BRANCH + 1)),
                  full((1, W_BRANCH)), full((1, W_BRANCH)), full((1, LANES)), full((W_BRANCH, W_BRANCH))],
        out_specs=[pl.BlockSpec((tm, W_BRANCH), lambda i: (i, 0)),
                   pl.BlockSpec((tm, W_BRANCH), lambda i: (i, 0))],
        out_shape=[jax.ShapeDtypeStruct((t, W_BRANCH), F32), jax.ShapeDtypeStruct((t, W_BRANCH), F32)],
        compiler_params=pltpu.CompilerParams(dimension_semantics=("arbitrary",), vmem_limit_bytes=VMEM_LIMIT),
    )(proj2, proj2, qn, kn, inv, bd)


def _rank_select(gate, nblk, limit, axis):
    idx = _iota2(gate.shape, axis)
    rank = jnp.zeros(gate.shape, F32)
    for m in range(nblk):
        gm = gate[m:m + 1, :] if axis == 0 else gate[:, m:m + 1]
        beats = (gm > gate) | ((gm == gate) & (m < idx))
        rank = rank + jnp.where(beats & (m < limit), 1.0, 0.0)
    return jnp.where((idx < limit) & (rank < MOBA_TOPK), 1.0, 0.0)


def _moba_prompt_kernel(q_ref, k_ref, v_ref, o_ref, *, nblk):
    blk = MOBA_BLOCK
    l = nblk * blk
    scale = D_C ** -0.5
    q = q_ref[...]
    k = k_ref[...]
    kmean = jnp.concatenate([jnp.mean(k[n * blk:(n + 1) * blk], axis=0, keepdims=True) for n in range(nblk)],
                            axis=0)
    gate = _mm32(kmean, q)
    own = _iota2((1, l), 1) // blk
    sel = _rank_select(gate, nblk, own, 0)

    kb = k.astype(BF16)
    qb = (q * scale).astype(BF16)
    causal = _iota2((blk, blk), 0) <= _iota2((blk, blk), 1)
    for j in range(nblk):
        cols = slice(j * blk, (j + 1) * blk)
        nk = (j + 1) * blk
        s = jnp.dot(kb[:nk], qb[:, cols], preferred_element_type=F32)
        parts = [jnp.where(sel[n:n + 1, cols] > 0.0, s[n * blk:(n + 1) * blk], -jnp.inf) for n in range(j)]
        parts.append(jnp.where(causal, s[j * blk:], -jnp.inf))
        s = jnp.concatenate(parts, axis=0)
        m = jnp.max(s, axis=0, keepdims=True)
        p = jnp.exp(s - m)
        denom = jnp.sum(p, axis=0, keepdims=True)
        acc = jnp.dot(v_ref[:, :nk].astype(BF16), p.astype(BF16), preferred_element_type=F32)
        o_ref[:, cols] = acc / denom


def _moba_prompt(q_t, k_b, v_t):
    b, h, _, l = q_t.shape
    nblk = l // MOBA_BLOCK
    assert (4.0 ** round(np.log(D_C) / np.log(4.0))) == D_C
    kern = functools.partial(_moba_prompt_kernel, nblk=nblk)
    lanes = pl.BlockSpec((None, None, D_C, l), lambda i, hh: (i, hh, 0, 0))
    return pl.pallas_call(
        kern,
        grid=(b, h),
        in_specs=[lanes, pl.BlockSpec((None, None, l, D_C), lambda i, hh: (i, hh, 0, 0)), lanes],
        out_specs=lanes,
        out_shape=jax.ShapeDtypeStruct((b, h, D_C, l), F32),
        compiler_params=pltpu.CompilerParams(dimension_semantics=("arbitrary", "arbitrary"),
                                             vmem_limit_bytes=VMEM_LIMIT),
    )(q_t, k_b, v_t)


def _moba_decode_kernel(pt_ref, q_ref, kn_ref, vn_ref, *refs, npage, page, lq):
    k_refs = refs[:npage]
    v_refs = refs[npage:2 * npage]
    o_ref = refs[2 * npage]
    del pt_ref
    scale = D_C ** -0.5
    nrow = lq * H_C
    ppb = MOBA_BLOCK // page
    nblk = npage // ppb
    lp = q_ref.shape[0]

    lane_head = _iota2((H_C, W_BRANCH), 1) // D_C
    own_head = (lane_head == _iota2((H_C, W_BRANCH), 0)).astype(F32)
    q = q_ref[...]
    qbd = jnp.concatenate([q[t:t + 1, :] * own_head for t in range(lq)], axis=0)
    qb = (qbd * scale).astype(BF16)

    lane = _iota2((1, LANES), 1)
    ms, ls_, accs = [], [], []
    ksum = jnp.zeros((W_BRANCH, LANES), F32)
    for n in range(nblk):
        pages = range(n * ppb, (n + 1) * ppb)
        k_t = jnp.concatenate([k_refs[i][...].reshape(W_BRANCH, page) for i in pages], axis=1)
        v_t = jnp.concatenate([v_refs[i][...].reshape(W_BRANCH, page) for i in pages], axis=1)
        s = jnp.dot(qb, k_t.astype(BF16), preferred_element_type=F32)
        m = jnp.max(s, axis=-1, keepdims=True)
        p = jnp.exp(s - m)
        ms.append(m)
        ls_.append(jnp.sum(p, axis=-1, keepdims=True))
        accs.append(lax.dot_general(p.astype(BF16), v_t.astype(BF16), NT, preferred_element_type=F32))
        ksum = ksum + jnp.sum(k_t, axis=-1, keepdims=True) * (lane == n).astype(F32)

    gate = _mm3(qbd, ksum) * (1.0 / MOBA_BLOCK)
    sel = _rank_select(gate, nblk, nblk, 1)

    pad = jnp.zeros((LANES - lp, W_BRANCH), F32)
    k_new = jnp.concatenate([kn_ref[...], pad], axis=0)
    v_new = jnp.concatenate([vn_ref[...], pad], axis=0)
    s = lax.dot_general(qb, k_new.astype(BF16), NT, preferred_element_type=F32)
    t_key = _iota2((nrow, LANES), 1)
    t_qry = _iota2((nrow, LANES), 0) // H_C
    s = jnp.where(t_key <= t_qry, s, -jnp.inf)
    m_all = jnp.max(s, axis=-1, keepdims=True)
    p_own = jnp.exp(s - m_all)
    l_own = jnp.sum(p_own, axis=-1, keepdims=True)
    acc_own = jnp.dot(p_own.astype(BF16), v_new.astype(BF16), preferred_element_type=F32)
    m_own = m_all

    m_eff = []
    for n in range(nblk):
        me = jnp.where(sel[:, n:n + 1] > 0.0, ms[n], -jnp.inf)
        m_eff.append(me)
        m_all = jnp.maximum(m_all, me)
    w_own = jnp.exp(m_own - m_all)
    l_tot = l_own * w_own
    acc = acc_own * w_own
    for n in range(nblk):
        w = jnp.exp(m_eff[n] - m_all)
        l_tot = l_tot + ls_[n] * w
        acc = acc + accs[n] * w
    out = acc / l_tot
    own_rows = jnp.concatenate([own_head] * lq, axis=0)
    out = (out * own_rows).reshape(lq, H_C, W_BRANCH).sum(axis=1)
    o_ref[...] = jnp.concatenate([out, jnp.zeros((lp - lq, W_BRANCH), F32)], axis=0)


def _moba_decode(page_table, q3, k3, v3, cache_k, cache_v, layer, lq):
    b, lp, _ = q3.shape
    npage = page_table.shape[1]
    page = cache_k.shape[-1]
    kern = functools.partial(_moba_decode_kernel, npage=npage, page=page, lq=lq)
    tok = pl.BlockSpec((None, lp, W_BRANCH), lambda i, pt: (i, 0, 0))

    def page_spec(p_i):
        return pl.BlockSpec((None, None, H_C, D_C, page), lambda i, pt: (layer, pt[i, p_i], 0, 0, 0))

    grid_spec = pltpu.PrefetchScalarGridSpec(
        num_scalar_prefetch=1,
        grid=(b,),
        in_specs=[tok, tok, tok] + [page_spec(p_i) for p_i in range(npage)] * 2,
        out_specs=pl.BlockSpec((None, lp, W_BRANCH), lambda i, pt: (i, 0, 0)),
    )
    return pl.pallas_call(
        kern,
        grid_spec=grid_spec,
        out_shape=jax.ShapeDtypeStruct((b, lp, W_BRANCH), F32),
        compiler_params=pltpu.CompilerParams(dimension_semantics=("arbitrary",), vmem_limit_bytes=VMEM_LIMIT),
    )(page_table, q3, k3, v3, *([cache_k] * npage), *([cache_v] * npage))


def _merge_kernel(oa_ref, ob_ref, oc_ref, zc_ref, gate_ref, x_ref, ple_ref, wb_ref, bg_ref, wo_ref,
                  wpg_ref, bpg_ref, wple_ref, y_ref):
    branches = (oa_ref[...], ob_ref[...], oc_ref[...] * _silu(zc_ref[...]))
    up = None
    for n, o in enumerate(branches):
        sl = slice(n * D_MODEL, (n + 1) * D_MODEL)
        gate = _sigmoid(gate_ref[:, sl] + bg_ref[:, sl])
        term = gate * jnp.dot(o.astype(BF16), wb_ref[n], preferred_element_type=F32)
        up = term if up is None else up + term
    y = x_ref[...] + jnp.dot(up.astype(BF16), wo_ref[...], preferred_element_type=F32)
    pg = _sigmoid(jnp.dot(y.astype(BF16), wpg_ref[...], preferred_element_type=F32) + bpg_ref[...])
    y_ref[...] = y + pg * jnp.dot(ple_ref[...].astype(BF16), wple_ref[...], preferred_element_type=F32)


def _merge(oa, ob, oc, proj2, x2, ple2, wb, bg, wo, wpg, bpg, wple, tm):
    t = x2.shape[0]
    full = lambda shape: pl.BlockSpec(shape, lambda i: (0,) * len(shape))
    row = lambda w: pl.BlockSpec((tm, w), lambda i: (i, 0))
    return pl.pallas_call(
        _merge_kernel,
        grid=(t // tm,),
        in_specs=[row(W_BRANCH), row(W_BRANCH), row(W_BRANCH),
                  pl.BlockSpec((tm, W_BRANCH), lambda i: (i, C_Z_C // W_BRANCH)),
                  pl.BlockSpec((tm, N_BRANCH * D_MODEL), lambda i: (i, C_GATE // (N_BRANCH * D_MODEL))),
                  row(D_MODEL), row(D_PLE),
                  full((N_BRANCH, W_BRANCH, D_MODEL)), full((1, N_BRANCH * D_MODEL)), full((D_MODEL, D_MODEL)),
                  full((D_MODEL, D_MODEL)), full((1, D_MODEL)), full((D_PLE, D_MODEL))],
        out_specs=row(D_MODEL),
        out_shape=jax.ShapeDtypeStruct((t, D_MODEL), F32),
        compiler_params=pltpu.CompilerParams(dimension_semantics=("arbitrary",), vmem_limit_bytes=VMEM_LIMIT),
    )(oa, ob, oc, proj2, proj2, x2, ple2, wb, bg, wo, wpg, bpg, wple)


def _arrange_w_in(w):
    offs = np.cumsum((0,) + IN_SPLITS)
    qkv_a, z_a, beta, alpha, f_b, z_b, qkv_c, z_c, gate = [w[:, offs[i]:offs[i + 1]] for i in range(9)]
    d = w.shape[0]
    ba = jnp.concatenate([beta, alpha, jnp.zeros((d, LANES - 2 * H_A), w.dtype)], axis=1)
    tail = jnp.zeros((d, N_PROJ - C_BA - LANES), w.dtype)
    out = jnp.concatenate([qkv_a, z_a, f_b[:, :3 * W_BRANCH], z_b, qkv_c, z_c, gate,
                           f_b[:, 3 * W_BRANCH:], ba, tail], axis=1)
    return out.astype(BF16)


def _layer_params(i, p):
    row = lambda v: v.reshape(1, -1)
    lanes_ba = lambda v: jnp.zeros((1, LANES), F32).at[0, H_A:2 * H_A].set(v)
    lora_pad = jnp.zeros((LORA, W_BRANCH), F32)
    head = np.arange(W_BRANCH) // N_B
    half = ROPE_DIM // 2
    inv = ROPE_THETA ** (-jnp.arange(half, dtype=F32) / half)
    d = np.arange(LANES) % D_C
    inv_lanes = jnp.where(d < ROPE_DIM, inv[d % half], 0.0).reshape(1, LANES)
    return dict(
        norm_g=row(p['norm_g'][i]),
        w_in=_arrange_w_in(p['w_in'][i]),
        conv=p['conv_a'][i],
        pa=lanes_ba(p['a_log'][i]), pb=lanes_ba(p['dt_bias'][i]), onorm=row(p['onorm_a'][i]),
        rw_vecs=(row(p['mu_b'][i]), row(p['w0_b'][i]), row(p['a0_b'][i]), row(p['kk_b'][i]), row(p['ka_b'][i]),
                 row(p['rk_b'][i]), row(p['gn_g_b'][i]), row(p['gn_b_b'][i])),
        w2p=jnp.concatenate([p['w2_b'][i], lora_pad], axis=0).astype(BF16),
        a2p=jnp.concatenate([lora_pad, p['a2_b'][i]], axis=0).astype(BF16),
        bd=jnp.asarray(head[:, None] == head[None, :], BF16),
        qn=row(jnp.tile(p['qn_c'][i], H_C)), kn=row(jnp.tile(p['kn_c'][i], H_C)), inv=inv_lanes,
        wb=p['w_branch'][i].astype(BF16), bg=row(p['b_gate'][i]), wo=p['w_out'][i].astype(BF16),
        wpg=p['w_pg'][i].astype(BF16), bpg=row(p['b_pg'][i]), wple=p['w_ple'][i].astype(BF16),
    )


def _rw_state_in(s):
    b = s.shape[0]
    ngrp = W_BRANCH // GRP
    eye = jnp.eye(4, dtype=F32).reshape(1, 1, 4, 1, 4, 1)
    return (s.reshape(b, ngrp, 4, N_B, 1, N_B) * eye).reshape(b, ngrp, GRP, GRP)


def _rw_state_out(s):
    b = s.shape[0]
    s6 = s.reshape(b, W_BRANCH // GRP, 4, N_B, 4, N_B)
    return jnp.einsum('bgjvjk->bgjvk', s6).reshape(b, H_B, N_B, N_B)


def _hist_rows(rows):
    b, n, c = rows.shape
    return jnp.concatenate([jnp.zeros((b, HIST - n, c), F32), rows], axis=1)


def _layer(x3, ple3, lp, *, l_valid, pos0, dn_s, dn_buf, rw_s, rw_prev, nseg, lt, ls, tm, attend):
    b, l, d = x3.shape
    t = b * l
    proj2 = _proj(x3.reshape(t, d), lp['norm_g'], lp['w_in'], min(t, 1024), 1920)
    proj3 = proj2.reshape(b, l, N_PROJ)

    oa, dn_new = _gdn(proj3, _hist_rows(dn_buf), dn_s, lp['conv'], lp['pa'], lp['pb'], lp['onorm'],
                      nseg=nseg, lt=lt, ls=ls, l_valid=l_valid)
    ob, rw_new = _rwkv(proj3, _hist_rows(rw_prev[:, None, :]), _rw_state_in(rw_s), lp['rw_vecs'],
                       lp['w2p'], lp['a2p'], lp['bd'], nseg=nseg, lt=lt, ls=ls, l_valid=l_valid)
    qr, kr = _qkprep(proj2, lp['qn'], lp['kn'], lp['inv'], lp['bd'], tm=min(tm, t), lseq=l, pos0=pos0)
    v2 = proj2[:, C_QKV_C + 2 * W_BRANCH:C_QKV_C + 3 * W_BRANCH]
    oc = attend(qr, kr, v2)

    y2 = _merge(oa.reshape(t, W_BRANCH), ob.reshape(t, W_BRANCH), oc.reshape(t, W_BRANCH), proj2,
                x3.reshape(t, d), ple3.reshape(t, D_PLE), lp['wb'], lp['bg'], lp['wo'], lp['wpg'],
                lp['bpg'], lp['wple'], min(tm, t))

    f_last = jnp.concatenate([proj3[:, l_valid - 1, C_RKV:C_RKV + 3 * W_BRANCH],
                              proj3[:, l_valid - 1, C_WLAL:C_WLAL + 2 * LORA]], axis=-1)
    new_state = (kr.reshape(b, l, H_C, D_C)[:, :l_valid], v2.reshape(b, l, H_C, D_C)[:, :l_valid], dn_new,
                 proj3[:, l_valid - (CONV_W - 1):l_valid, C_QKV_A:C_QKV_A + QKV_A],
                 _rw_state_out(rw_new), f_last)
    return y2.reshape(b, l, d), new_state


def _attend_prompt(b, l):
    nblk = l // MOBA_BLOCK

    def attend(qr, kr, v2):
        q_t = qr.reshape(b, l, H_C, D_C).transpose(0, 2, 3, 1)
        k_b = kr.reshape(b, l, H_C, D_C).transpose(0, 2, 1, 3)
        v_t = v2.reshape(b, l, H_C, D_C).transpose(0, 2, 3, 1)
        o_t = _moba_prompt(q_t, k_b, v_t)
        return o_t.transpose(0, 3, 1, 2).reshape(b * l, W_BRANCH)
    return attend


def _attend_decode(b, lp_, lq, page_table, cache_k4, cache_v4, layer):
    def attend(qr, kr, v2):
        shp = (b, lp_, W_BRANCH)
        return _moba_decode(page_table, qr.reshape(shp), kr.reshape(shp), v2.reshape(shp),
                            cache_k4, cache_v4, layer, lq)
    return attend


DEC_PAD = 8


def kernel(x_prompt, x_sample, cache_k, cache_v, state_dn, state_dn_conv, state_rwkv, state_rwkv_shift,
           page_table, p_prompt, p_sample, norm_g, w_in, conv_a, a_log, dt_bias, onorm_a, mu_b, w0_b, w2_b,
           a0_b, a2_b, kk_b, ka_b, rk_b, gn_g_b, gn_b_b, qn_c, kn_c, w_branch, b_gate, w_out, w_ple, w_pg,
           b_pg):
    params = dict(norm_g=norm_g, w_in=w_in, conv_a=conv_a, a_log=a_log, dt_bias=dt_bias, onorm_a=onorm_a,
                  mu_b=mu_b, w0_b=w0_b, w2_b=w2_b, a0_b=a0_b, a2_b=a2_b, kk_b=kk_b, ka_b=ka_b, rk_b=rk_b,
                  gn_g_b=gn_g_b, gn_b_b=gn_b_b, qn_c=qn_c, kn_c=kn_c, w_branch=w_branch, b_gate=b_gate,
                  w_out=w_out, w_ple=w_ple, w_pg=w_pg, b_pg=b_pg)
    depth = norm_g.shape[0]
    b_p, l_p, _ = x_prompt.shape
    b_s, l_s, _ = x_sample.shape
    n_phys, page = cache_k.shape[1], cache_k.shape[2]
    past_len = page_table.shape[1] * page
    assert l_p % 512 == 0 and CONV_W - 1 <= l_s <= DEC_PAD and b_s % (CHUNK // DEC_PAD) == 0
    assert past_len % MOBA_BLOCK == 0 and MOBA_BLOCK % page == 0

    pad_rows = lambda a: jnp.pad(a, ((0, 0), (0, DEC_PAD - l_s), (0, 0)))
    xs = pad_rows(x_sample)
    cache_k4 = cache_k.transpose(0, 1, 3, 4, 2)
    cache_v4 = cache_v.transpose(0, 1, 3, 4, 2)
    xp = x_prompt
    new_p, new_s = [], []
    for i in range(depth):
        lp = _layer_params(i, params)
        xp, st_p = _layer(xp, p_prompt[i], lp, l_valid=l_p, pos0=0,
                          dn_s=jnp.zeros((b_p, H_A, DK_A, DV_A), F32),
                          dn_buf=jnp.zeros((b_p, CONV_W - 1, QKV_A), F32),
                          rw_s=jnp.zeros((b_p, H_B, N_B, N_B), F32), rw_prev=jnp.zeros((b_p, SHIFT_B), F32),
                          nseg=1, lt=512, ls=CHUNK, tm=512, attend=_attend_prompt(b_p, l_p))
        new_p.append(st_p)
        xs, st_s = _layer(xs, pad_rows(p_sample[i]), lp, l_valid=l_s, pos0=past_len,
                          dn_s=state_dn[i], dn_buf=state_dn_conv[i], rw_s=state_rwkv[i],
                          rw_prev=state_rwkv_shift[i], nseg=CHUNK // DEC_PAD, lt=DEC_PAD, ls=DEC_PAD, tm=512,
                          attend=_attend_decode(b_s, DEC_PAD, l_s, page_table, cache_k4, cache_v4, i))
        new_s.append(st_s)
    outs_p = [jnp.stack(t) for t in zip(*new_p)]
    outs_s = [jnp.stack(t) for t in zip(*new_s)]
    return (xp, xs[:, :l_s], *outs_p, *outs_s)
```

```python
import functools

import numpy as np
import jax
import jax.numpy as jnp
from jax import lax
from jax.experimental import pallas as pl
from jax.experimental.pallas import tpu as pltpu

F32 = jnp.float32
BF16 = jnp.bfloat16

D_MODEL = 1024
D_PLE = 256
N_BRANCH = 3
W_BRANCH = 512
NORM_EPS = 1e-6
DK_A = 128
DV_A = 128
H_A = 4
QKV_A = 1536
CONV_W = 4
N_B = 64
H_B = 8
LORA = 64
SHIFT_B = 3 * W_BRANCH + 2 * LORA
GN_EPS = 64e-5
D_C = 64
H_C = 8
MOBA_BLOCK = 256
MOBA_TOPK = 3
ROPE_DIM = 16
ROPE_THETA = 500000.0
IN_SPLITS = (QKV_A, W_BRANCH, H_A, H_A, SHIFT_B, W_BRANCH, 3 * W_BRANCH, W_BRANCH, N_BRANCH * D_MODEL)

C_QKV_A = 0
C_Z_A = 1536
C_RKV = 2048
C_Z_B = 3584
C_QKV_C = 4096
C_Z_C = 5632
C_GATE = 6144
C_WLAL = 9216
C_BA = 9344
N_PROJ = 9600

CHUNK = 64
HIST = 8
CHUNKS_PER_TRIP_A = 8
CHUNKS_PER_TRIP_B = 4
LANES = 128
VMEM_LIMIT = 56 * 1024 * 1024

NN = (((1,), (0,)), ((), ()))
NT = (((1,), (1,)), ((), ()))
TN = (((0,), (0,)), ((), ()))
HIGHEST = lax.Precision.HIGHEST


def _mm(a, b, dims=NN):
    return lax.dot_general(a.astype(BF16), b.astype(BF16), dims, preferred_element_type=F32)


def _mm32(a, b, dims=NN):
    return lax.dot_general(a, b, dims, precision=HIGHEST, preferred_element_type=F32)


def _sigmoid(x):
    return 1.0 / (1.0 + jnp.exp(-x))


def _silu(x):
    return x * _sigmoid(x)


def _softplus(x):
    return jnp.maximum(x, 0.0) + jnp.log1p(jnp.exp(-jnp.abs(x)))


def _iota2(shape, dim):
    return lax.broadcasted_iota(jnp.int32, shape, dim)


def _split(x):
    hi = x.astype(BF16)
    return hi, (x - hi.astype(F32)).astype(BF16)


def _dotb(a, b, dims=NN):
    return lax.dot_general(a, b, dims, preferred_element_type=F32)


def _mm_sel(sel, x):
    hi, lo = _split(x)
    return _dotb(sel, hi) + _dotb(sel, lo)


def _mm3(a, b, dims=NN):
    ah, al = _split(a)
    bh, bl = _split(b)
    return _dotb(ah, bh, dims) + (_dotb(ah, bl, dims) + _dotb(al, bh, dims))


SOLVE_MM_A = _mm
SOLVE_MM_B = _mm


def _inv_unit_lower(lmats, nilpotent, mm):
    n = lmats[0].shape[0]
    eye = (_iota2((n, n), 0) == _iota2((n, n), 1)).astype(F32)
    ps = [-lmat for lmat in lmats]
    accs = [eye + p for p in ps]
    span = 2
    while span < nilpotent:
        ps = [mm(p, p) for p in ps]
        accs = [acc + mm(acc, p) for acc, p in zip(accs, ps)]
        span *= 2
    return accs


def _head_sums(xs, bd):
    n = xs[0].shape[0]
    parts = []
    for x in xs:
        parts.extend(_split(x))
    out = jnp.dot(jnp.concatenate(parts, axis=0), bd, preferred_element_type=F32)
    return [out[2 * i * n:(2 * i + 1) * n] + out[(2 * i + 1) * n:(2 * i + 2) * n] for i in range(len(xs))]


def _head_sum(x, bd):
    return _head_sums([x], bd)[0]


def _proj_kernel(x_ref, g_ref, w_ref, o_ref):
    x = x_ref[...]
    h = x * lax.rsqrt(jnp.mean(x * x, axis=-1, keepdims=True) + NORM_EPS) * g_ref[...]
    o_ref[...] = jnp.dot(h.astype(BF16), w_ref[...], preferred_element_type=F32)


def _proj(x2d, g, w, tm, tn):
    t, d = x2d.shape
    n = w.shape[1]
    return pl.pallas_call(
        _proj_kernel,
        grid=(n // tn, t // tm),
        in_specs=[pl.BlockSpec((tm, d), lambda j, i: (i, 0)),
                  pl.BlockSpec((1, d), lambda j, i: (0, 0)),
                  pl.BlockSpec((d, tn), lambda j, i: (0, j))],
        out_specs=pl.BlockSpec((tm, tn), lambda j, i: (i, j)),
        out_shape=jax.ShapeDtypeStruct((t, n), F32),
        compiler_params=pltpu.CompilerParams(dimension_semantics=("arbitrary", "arbitrary"),
                                             vmem_limit_bytes=VMEM_LIMIT),
    )(x2d, g, w)


def _chunk_masks(n, ls):
    r = _iota2((n, n), 0)
    c = _iota2((n, n), 1)
    same = (r // ls) == (c // ls)
    return same, same & (r >= c), same & (r > c)


def _valid_rows(l_idx, lt, r0, ls, l_valid):
    t = l_idx * lt + r0 + (_iota2((CHUNK, 1), 0) % ls)
    return t < l_valid


def _gdn_kernel(qkv_ref, z_ref, ba_ref, buf_ref, s0_ref, cw_ref, pa_ref, pb_ref, on_ref,
                o_ref, sout_ref, xs_ref, st_ref, *, nseg, lt, ls, l_valid, l_total):
    l_idx = pl.program_id(1)
    nrow = H_A * CHUNK

    @pl.when(l_idx == 0)
    def _():
        xs_ref[:, 0:HIST, :] = buf_ref[...]
        st_ref[...] = s0_ref[...]

    @pl.when(l_idx > 0)
    def _():
        xs_ref[:, 0:HIST, :] = xs_ref[:, lt:lt + HIST, :]

    xs_ref[:, HIST:lt + HIST, :] = qkv_ref[...]

    _, m_incl, m_strict = _chunk_masks(nrow, ls)
    same_r, tri_r, _ = _chunk_masks(CHUNK, ls)
    tri_r = tri_r.astype(BF16)
    same_r = same_r.astype(BF16)
    eye_n = (_iota2((nrow, nrow), 0) == _iota2((nrow, nrow), 1)).astype(F32)
    cw = cw_ref[...]
    neg_a = -jnp.exp(pa_ref[...])
    pb = pb_ref[...]
    onorm = on_ref[...]

    def conv(s, r0, col):
        win = xs_ref[s, pl.ds(r0, ls + HIST), col:col + LANES]
        w = cw[:, col:col + LANES]
        acc = pltpu.roll(win, 3, 0)[HIST:] * w[0:1] + pltpu.roll(win, 2, 0)[HIST:] * w[1:2]
        acc = acc + pltpu.roll(win, 1, 0)[HIST:] * w[2:3]
        acc = acc + win[HIST:] * w[3:4]
        return _silu(acc)

    def part(r0, col):
        return jnp.concatenate([conv(s, r0, col) for s in range(nseg)], axis=0)

    def local(c):
        r0 = pl.multiple_of(c * ls, ls)
        ba = jnp.concatenate([ba_ref[s, pl.ds(r0, ls), :] for s in range(nseg)], axis=0)
        beta_all = _sigmoid(ba)
        g_all = neg_a * _softplus(ba + pb)
        if l_valid < l_total:
            ok = _valid_rows(l_idx, lt, r0, ls, l_valid)
            beta_all = jnp.where(ok, beta_all, 0.0)
            g_all = jnp.where(ok, g_all, 0.0)
        gcum = _mm_sel(tri_r, g_all)
        if nseg == 1:
            gtot = jnp.broadcast_to(gcum[CHUNK - 1:CHUNK], gcum.shape)
        else:
            gtot = _mm_sel(same_r, g_all)

        qs, ks, vs, betas, gcs, gts = [], [], [], [], [], []
        for h in range(H_A):
            q = part(r0, h * DK_A)
            k = part(r0, H_A * DK_A + h * DK_A)
            v = part(r0, 2 * H_A * DK_A + h * DV_A)
            q = q * lax.rsqrt(jnp.sum(q * q, axis=-1, keepdims=True) + 1e-6) * (DK_A ** -0.5)
            k = k * lax.rsqrt(jnp.sum(k * k, axis=-1, keepdims=True) + 1e-6)
            qs.append(q)
            ks.append(k)
            vs.append(v)
            betas.append(beta_all[:, h:h + 1])
            gcs.append(gcum[:, H_A + h:H_A + h + 1])
            gts.append(gtot[:, H_A + h:H_A + h + 1])
        q_st = jnp.concatenate(qs, axis=0)
        k_st = jnp.concatenate(ks, axis=0)
        v_st = jnp.concatenate(vs, axis=0)
        beta = jnp.concatenate(betas, axis=0)
        g_col = jnp.concatenate(gcs, axis=0)
        gt_col = jnp.concatenate(gts, axis=0)
        g_row = jnp.sum(eye_n * g_col, axis=0, keepdims=True)

        decay = jnp.exp(jnp.where(m_incl, g_col - g_row, -jnp.inf))
        kq_k = _mm(jnp.concatenate([k_st, q_st], axis=0), k_st, NT)
        return dict(
            r0=r0, gt_col=gt_col,
            lmat=jnp.where(m_strict, beta * kq_k[:nrow] * decay, 0.0),
            rhs=jnp.concatenate([v_st * beta, k_st * (beta * jnp.exp(g_col))], axis=1),
            qk=jnp.where(m_incl, kq_k[nrow:] * decay, 0.0),
            qg=q_st * jnp.exp(g_col),
            kd=k_st * jnp.exp(gt_col - g_col))

    def advance(loc, ainv):
        r0, gt_col, qk, qg, kd = loc['r0'], loc['gt_col'], loc['qk'], loc['qg'], loc['kd']
        sol = SOLVE_MM_A(ainv, loc['rhs'])
        bv = sol[:, :DV_A]
        wk = sol[:, DV_A:]
        wks, qgs = [], []
        for h in range(H_A):
            for s in range(nseg):
                a = h * CHUNK + s * ls
                both = _mm(jnp.concatenate([wk[a:a + ls], qg[a:a + ls]], axis=0), st_ref[s, h])
                wks.append(both[:ls])
                qgs.append(both[ls:])
        u = bv - jnp.concatenate(wks, axis=0)
        o = jnp.concatenate(qgs, axis=0) + _mm(qk, u)

        for h in range(H_A):
            a = h * CHUNK
            for s in range(nseg):
                b = a + s * ls
                if nseg == 1:
                    kd_s = kd[a:a + CHUNK]
                else:
                    rows = _iota2((CHUNK, 1), 0) // ls
                    kd_s = jnp.where(rows == s, kd[a:a + CHUNK], 0.0)
                upd = _mm(kd_s, u[a:a + CHUNK], TN)
                st_ref[s, h] = st_ref[s, h] * jnp.exp(gt_col[b:b + 1]) + upd
            oh = o[a:a + CHUNK]
            oh = oh * lax.rsqrt(jnp.mean(oh * oh, axis=-1, keepdims=True) + NORM_EPS) * onorm
            for s in range(nseg):
                zz = z_ref[s, pl.ds(r0, ls), h * DV_A:(h + 1) * DV_A]
                o_ref[s, pl.ds(r0, ls), h * DV_A:(h + 1) * DV_A] = oh[s * ls:(s + 1) * ls] * _silu(zz)

    nchunk = lt // ls
    per_trip = min(CHUNKS_PER_TRIP_A, nchunk)

    def trip(i, carry):
        locs = [local(i * per_trip + j) for j in range(per_trip)]
        ainvs = _inv_unit_lower([loc['lmat'] for loc in locs], ls, SOLVE_MM_A)
        for loc, ainv in zip(locs, ainvs):
            advance(loc, ainv)
        return carry

    lax.fori_loop(0, nchunk // per_trip, trip, 0)

    @pl.when(l_idx == pl.num_programs(1) - 1)
    def _():
        sout_ref[...] = st_ref[...]


def _gdn(proj3, buf8, s0, layer, cw, pa, pb, onorm, *, nseg, lt, ls, l_valid):
    b, l, _ = proj3.shape
    kern = functools.partial(_gdn_kernel, nseg=nseg, lt=lt, ls=ls, l_valid=l_valid, l_total=l)
    full = lambda shape: pl.BlockSpec(shape, lambda i, j: (0,) * len(shape))
    return pl.pallas_call(
        kern,
        grid=(b // nseg, l // lt),
        in_specs=[pl.BlockSpec((nseg, lt, QKV_A), lambda i, j: (i, j, C_QKV_A // QKV_A)),
                  pl.BlockSpec((nseg, lt, W_BRANCH), lambda i, j: (i, j, C_Z_A // W_BRANCH)),
                  pl.BlockSpec((nseg, lt, LANES), lambda i, j: (i, j, C_BA // LANES)),
                  pl.BlockSpec((nseg, HIST, QKV_A), lambda i, j: (i, 0, 0)),
                  pl.BlockSpec((None, nseg, H_A, DK_A, DV_A), lambda i, j: (layer, i, 0, 0, 0)),
                  full((CONV_W, QKV_A)), full((1, LANES)), full((1, LANES)), full((1, DV_A))],
        out_specs=[pl.BlockSpec((nseg, lt, W_BRANCH), lambda i, j: (i, j, 0)),
                   pl.BlockSpec((nseg, H_A, DK_A, DV_A), lambda i, j: (i, 0, 0, 0))],
        out_shape=[jax.ShapeDtypeStruct((b, l, W_BRANCH), F32),
                   jax.ShapeDtypeStruct((b, H_A, DK_A, DV_A), F32)],
        scratch_shapes=[pltpu.VMEM((nseg, lt + HIST, QKV_A), F32),
                        pltpu.VMEM((nseg, H_A, DK_A, DV_A), F32)],
        compiler_params=pltpu.CompilerParams(dimension_semantics=("arbitrary", "arbitrary"),
                                             vmem_limit_bytes=VMEM_LIMIT),
    )(proj3, proj3, proj3, buf8, s0, cw, pa, pb, onorm)


GRP = 4 * N_B


def _rwkv_kernel(r_ref, k_ref, v_ref, wa_ref, z_ref, prev_ref, s0_ref, mu_ref, w0_ref, w2_ref,
                 a0_ref, a2_ref, kkp_ref, kap_ref, rk_ref, gg_ref, gb_ref, bd_ref,
                 o_ref, sout_ref, fs_ref, st_ref, *, nseg, lt, ls, l_valid, l_total):
    l_idx = pl.program_id(1)
    nrow = 4 * CHUNK
    ngrp = W_BRANCH // GRP

    @pl.when(l_idx == 0)
    def _():
        fs_ref[:, 0:HIST, :] = prev_ref[...]
        st_ref[...] = jnp.zeros(st_ref.shape, F32)
        for s in range(nseg):
            for h in range(H_B):
                g, j = divmod(h, 4)
                st_ref[s, g, j * N_B:(j + 1) * N_B, j * N_B:(j + 1) * N_B] = s0_ref[s, h]

    @pl.when(l_idx > 0)
    def _():
        fs_ref[:, 0:HIST, :] = fs_ref[:, lt:lt + HIST, :]

    fs_ref[:, HIST:lt + HIST, 0:W_BRANCH] = r_ref[...]
    fs_ref[:, HIST:lt + HIST, W_BRANCH:2 * W_BRANCH] = k_ref[...]
    fs_ref[:, HIST:lt + HIST, 2 * W_BRANCH:3 * W_BRANCH] = v_ref[...]
    fs_ref[:, HIST:lt + HIST, 3 * W_BRANCH:SHIFT_B] = wa_ref[...]

    _, m_incl, m_strict = _chunk_masks(nrow, ls)
    same_r, tri_r, _ = _chunk_masks(CHUNK, ls)
    tri_r = tri_r.astype(BF16)
    same_r = same_r.astype(BF16)
    lane_head = _iota2((1, GRP), 1) // N_B
    head_masks = [(lane_head == j).astype(F32) for j in range(4)]
    bd_state = (_iota2((GRP, GRP), 0) // N_B) == (_iota2((GRP, GRP), 1) // N_B)
    head_rows = ((_iota2((nrow, GRP), 0) // CHUNK) == (_iota2((nrow, GRP), 1) // N_B)).astype(F32)
    t_row = _iota2((nrow, CHUNK), 0) % CHUNK
    t_col = _iota2((nrow, CHUNK), 1)
    t_same = (t_row // ls) == (t_col // ls)
    t_incl = t_same & (t_col <= t_row)
    t_strict = t_same & (t_col < t_row)
    mu = mu_ref[...]
    bd = bd_ref[...]

    def mixed(r0, col, width):
        outs = []
        for s in range(nseg):
            win = fs_ref[s, pl.ds(r0, ls + HIST), col:col + width]
            cur = win[HIST:]
            prev = pltpu.roll(win, 1, 0)[HIST:]
            outs.append(cur + (prev - cur) * mu[:, col:col + width])
        return jnp.concatenate(outs, axis=0)

    def stack_masked(x):
        return jnp.concatenate([x * head_masks[j] for j in range(4)], axis=0)

    def unstack(x):
        return x[0:CHUNK] + x[CHUNK:2 * CHUNK] + x[2 * CHUNK:3 * CHUNK] + x[3 * CHUNK:4 * CHUNK]

    def local(c):
        r0 = pl.multiple_of(c * ls, ls)
        r = mixed(r0, 0, W_BRANCH)
        k = mixed(r0, W_BRANCH, W_BRANCH)
        v = mixed(r0, 2 * W_BRANCH, W_BRANCH)
        wa = mixed(r0, 3 * W_BRANCH, 2 * LORA)
        w_raw = w0_ref[...] + _mm(jnp.tanh(wa), w2_ref[...])
        logw = -jnp.exp(-_softplus(-w_raw) - 0.5)
        a_sig = _sigmoid(a0_ref[...] + _mm(wa, a2_ref[...]))
        kk = k * kkp_ref[...]
        k_mod = k * (1.0 + (a_sig - 1.0) * kap_ref[...])
        kk_ss, rk_sum = _head_sums([kk * kk, r * k_mod * rk_ref[...]], bd)
        kk = kk * lax.rsqrt(kk_ss + 1e-6)
        b_vec = kk * a_sig
        if l_valid < l_total:
            ok = _valid_rows(l_idx, lt, r0, ls, l_valid)
            logw = jnp.where(ok, logw, 0.0)
            b_vec = jnp.where(ok, b_vec, 0.0)
            k_mod = jnp.where(ok, k_mod, 0.0)
        gcum = _mm_sel(tri_r, logw)
        if nseg == 1:
            gtot = jnp.broadcast_to(gcum[CHUNK - 1:CHUNK], gcum.shape)
        else:
            gtot = _mm_sel(same_r, logw)
        e_neg = jnp.exp(-gcum)
        e_end = jnp.exp(gtot - gcum)
        a_t = -kk * jnp.exp(gcum - logw)
        b_t = b_vec * e_neg
        k_t = k_mod * e_neg
        r_t = r * jnp.exp(gcum)
        b_h = b_vec * e_end
        k_h = k_mod * e_end

        groups = []
        for g in range(ngrp):
            sl = slice(g * GRP, (g + 1) * GRP)
            a_g, b_g, k_g, r_g, v_g = a_t[:, sl], b_t[:, sl], k_t[:, sl], r_t[:, sl], v[:, sl]
            a_st = stack_masked(a_g)
            r_st = stack_masked(r_g)
            b_st = jnp.concatenate([b_g] * 4, axis=0)
            ar_k = _mm(jnp.concatenate([a_st, r_st], axis=0), k_g, NT)
            groups.append(dict(
                a_g=a_g, r_g=r_g, v_g=v_g, b_h=b_h[:, sl], k_h=k_h[:, sl], gtot=gtot[:, sl],
                neg_aab=-jnp.where(m_strict, _mm(a_st, b_st, NT), 0.0),
                aak=jnp.where(t_strict, ar_k[:nrow], 0.0),
                ark=jnp.where(t_incl, ar_k[nrow:], 0.0),
                arb=jnp.where(t_incl, _mm(r_st, b_g, NT), 0.0)))
        return dict(r0=r0, v=v, rk_sum=rk_sum, groups=groups)

    def advance(loc, tinvs):
        ys = []
        for g, (grp, tinv) in enumerate(zip(loc['groups'], tinvs)):
            a_g, r_g, v_g = grp['a_g'], grp['r_g'], grp['v_g']
            ams, rms = [], []
            for s in range(nseg):
                rows_s = slice(s * ls, (s + 1) * ls)
                both = _mm(jnp.concatenate([a_g[rows_s], r_g[rows_s]], axis=0), st_ref[s, g], NT)
                ams.append(both[:ls])
                rms.append(both[ls:])
            am = jnp.concatenate(ams, axis=0)
            rm = jnp.concatenate(rms, axis=0)
            kv = _mm(jnp.concatenate([grp['aak'], grp['ark']], axis=0), v_g)
            u_st = SOLVE_MM_B(tinv, stack_masked(am) + kv[:nrow] * head_rows)
            u = unstack(u_st)
            ys.append(rm + unstack((_mm(grp['arb'], u) + kv[nrow:]) * head_rows))
            for s in range(nseg):
                if nseg == 1:
                    u_s, v_s = u, v_g
                else:
                    rows = _iota2((CHUNK, 1), 0) // ls
                    u_s = jnp.where(rows == s, u, 0.0)
                    v_s = jnp.where(rows == s, v_g, 0.0)
                upd = _mm(jnp.concatenate([u_s, v_s], axis=0),
                          jnp.concatenate([grp['b_h'], grp['k_h']], axis=0), TN)
                dec = jnp.exp(grp['gtot'][s * ls:s * ls + 1])
                st_ref[s, g] = st_ref[s, g] * dec + jnp.where(bd_state, upd, 0.0)
        y = jnp.concatenate(ys, axis=1)
        mean = _head_sums([y], bd)[0] * (1.0 / N_B)
        yc = y - mean
        var = _head_sums([yc * yc], bd)[0] * (1.0 / N_B)
        yn = yc * lax.rsqrt(var + GN_EPS) * gg_ref[...] + gb_ref[...]
        out = yn + loc['rk_sum'] * loc['v']
        for s in range(nseg):
            zz = z_ref[s, pl.ds(loc['r0'], ls), :]
            o_ref[s, pl.ds(loc['r0'], ls), :] = out[s * ls:(s + 1) * ls] * _silu(zz)

    nchunk = lt // ls
    per_trip = min(CHUNKS_PER_TRIP_B, nchunk)

    def trip(i, carry):
        locs = [local(i * per_trip + j) for j in range(per_trip)]
        tinvs = _inv_unit_lower([grp['neg_aab'] for loc in locs for grp in loc['groups']], ls, SOLVE_MM_B)
        for j, loc in enumerate(locs):
            advance(loc, tinvs[j * ngrp:(j + 1) * ngrp])
        return carry

    lax.fori_loop(0, nchunk // per_trip, trip, 0)

    @pl.when(l_idx == pl.num_programs(1) - 1)
    def _():
        for s in range(nseg):
            for h in range(H_B):
                g, j = divmod(h, 4)
                sout_ref[s, h] = st_ref[s, g, j * N_B:(j + 1) * N_B, j * N_B:(j + 1) * N_B]


def _rwkv(proj3, prev8, s0, layer, vecs, w2p, a2p, bd, *, nseg, lt, ls, l_valid):
    b, l, _ = proj3.shape
    ngrp = W_BRANCH // GRP
    kern = functools.partial(_rwkv_kernel, nseg=nseg, lt=lt, ls=ls, l_valid=l_valid, l_total=l)
    full = lambda shape: pl.BlockSpec(shape, lambda i, j: (0,) * len(shape))
    blk = lambda col, w: pl.BlockSpec((nseg, lt, w), lambda i, j: (i, j, col // w))
    mu, w0, a0, kkp, kap, rk, gg, gb = vecs
    return pl.pallas_call(
        kern,
        grid=(b // nseg, l // lt),
        in_specs=[blk(C_RKV, W_BRANCH), blk(C_RKV + W_BRANCH, W_BRANCH), blk(C_RKV + 2 * W_BRANCH, W_BRANCH),
                  blk(C_WLAL, 2 * LORA), blk(C_Z_B, W_BRANCH),
                  pl.BlockSpec((nseg, HIST, SHIFT_B), lambda i, j: (i, 0, 0)),
                  pl.BlockSpec((None, nseg, H_B, N_B, N_B), lambda i, j: (layer, i, 0, 0, 0)),
                  full((1, SHIFT_B)), full((1, W_BRANCH)), full((2 * LORA, W_BRANCH)),
                  full((1, W_BRANCH)), full((2 * LORA, W_BRANCH)), full((1, W_BRANCH)),
                  full((1, W_BRANCH)), full((1, W_BRANCH)), full((1, W_BRANCH)), full((1, W_BRANCH)),
                  full((W_BRANCH, W_BRANCH))],
        out_specs=[pl.BlockSpec((nseg, lt, W_BRANCH), lambda i, j: (i, j, 0)),
                   pl.BlockSpec((nseg, H_B, N_B, N_B), lambda i, j: (i, 0, 0, 0))],
        out_shape=[jax.ShapeDtypeStruct((b, l, W_BRANCH), F32),
                   jax.ShapeDtypeStruct((b, H_B, N_B, N_B), F32)],
        scratch_shapes=[pltpu.VMEM((nseg, lt + HIST, SHIFT_B), F32),
                        pltpu.VMEM((nseg, ngrp, GRP, GRP), F32)],
        compiler_params=pltpu.CompilerParams(dimension_semantics=("arbitrary", "arbitrary"),
                                             vmem_limit_bytes=VMEM_LIMIT),
    )(proj3, proj3, proj3, proj3, proj3, prev8, s0, mu, w0, w2p, a0, a2p, kkp, kap, rk, gg, gb, bd)


def _qkprep_kernel(q_ref, k_ref, qn_ref, kn_ref, inv_ref, bd_ref, qo_ref, ko_ref, *, tm, lseq, pos0):
    pos = (pos0 + (pl.program_id(0) * tm + _iota2((tm, 1), 0)) % lseq).astype(F32)
    ang = pos * inv_ref[...]
    cos = jnp.cos(ang)
    sin = jnp.sin(ang)
    d = _iota2((1, LANES), 1) % D_C
    half = ROPE_DIM // 2
    c_main = jnp.where(d < ROPE_DIM, cos, 1.0)
    c_up = jnp.where(d < half, -sin, 0.0)
    c_dn = jnp.where((d >= half) & (d < ROPE_DIM), sin, 0.0)
    bd = bd_ref[...]

    def one(x_ref, g_ref, out_ref):
        x = x_ref[...]
        y = x * lax.rsqrt(_head_sum(x * x, bd) * (1.0 / D_C) + NORM_EPS) * g_ref[...]
        for blk in range(W_BRANCH // LANES):
            yb = y[:, blk * LANES:(blk + 1) * LANES]
            up = pltpu.roll(yb, LANES - half, 1)
            dn = pltpu.roll(yb, half, 1)
            out_ref[:, blk * LANES:(blk + 1) * LANES] = yb * c_main + up * c_up + dn * c_dn

    one(q_ref, qn_ref, qo_ref)
    one(k_ref, kn_ref, ko_ref)


def _qkprep(proj2, qn, kn, inv, bd, *, tm, lseq, pos0):
    t = proj2.shape[0]
    kern = functools.partial(_qkprep_kernel, tm=tm, lseq=lseq, pos0=pos0)
    full = lambda shape: pl.BlockSpec(shape, lambda i: (0,) * len(shape))
    return pl.pallas_call(
        kern,
        grid=(t // tm,),
        in_specs=[pl.BlockSpec((tm, W_BRANCH), lambda i: (i, C_QKV_C // W_BRANCH)),
                  pl.BlockSpec((tm, W_BRANCH), lambda i: (i, C_QKV_C // W_BRANCH + 1)),
                  full((1, W_BRANCH)), full((1, W_BRANCH)), full((1, LANES)), full((W_BRANCH, W_BRANCH))],
        out_specs=[pl.BlockSpec((tm, W_BRANCH), lambda i: (i, 0)),
                   pl.BlockSpec((tm, W_BRANCH), lambda i: (i, 0))],
        out_shape=[jax.ShapeDtypeStruct((t, W_BRANCH), F32), jax.ShapeDtypeStruct((t, W_BRANCH), F32)],
        compiler_params=pltpu.CompilerParams(dimension_semantics=("arbitrary",), vmem_limit_bytes=VMEM_LIMIT),
    )(proj2, proj2, qn, kn, inv, bd)


def _rank_select(gate, nblk, limit, axis):
    idx = _iota2(gate.shape, axis)
    rank = jnp.zeros(gate.shape, F32)
    for m in range(nblk):
        gm = gate[m:m + 1, :] if axis == 0 else gate[:, m:m + 1]
        beats = (gm > gate) | ((gm == gate) & (m < idx))
        rank = rank + jnp.where(beats & (m < limit), 1.0, 0.0)
    return jnp.where((idx < limit) & (rank < MOBA_TOPK), 1.0, 0.0)


HEADS_PER_STEP = LANES // D_C


def _moba_prompt_kernel(q_ref, k_ref, v_ref, o_ref, *, nblk):
    blk = MOBA_BLOCK
    l = nblk * blk
    scale = D_C ** -0.5
    k = k_ref[...]
    q_t = q_ref[...].T
    v_t = v_ref[...].T.astype(BF16)
    kmean = jnp.concatenate([jnp.mean(k[n * blk:(n + 1) * blk], axis=0, keepdims=True) for n in range(nblk)],
                            axis=0)
    own = _iota2((1, l), 1) // blk
    kb = k.astype(BF16)
    causal = _iota2((blk, blk), 0) <= _iota2((blk, blk), 1)
    row_head = _iota2((LANES, 1), 0) // D_C
    outs = []
    for e in range(HEADS_PER_STEP):
        q_e = jnp.where(row_head == e, q_t, 0.0)
        sel = _rank_select(_mm32(kmean, q_e), nblk, own, 0)
        qb = (q_e * scale).astype(BF16)
        tiles = []
        for j in range(nblk):
            cols = slice(j * blk, (j + 1) * blk)
            nk = (j + 1) * blk
            s = jnp.dot(kb[:nk], qb[:, cols], preferred_element_type=F32)
            parts = [jnp.where(sel[n:n + 1, cols] > 0.0, s[n * blk:(n + 1) * blk], -jnp.inf) for n in range(j)]
            parts.append(jnp.where(causal, s[j * blk:], -jnp.inf))
            s = jnp.concatenate(parts, axis=0)
            m = jnp.max(s, axis=0, keepdims=True)
            p = jnp.exp(s - m)
            denom = jnp.sum(p, axis=0, keepdims=True)
            acc = jnp.dot(v_t[e * D_C:(e + 1) * D_C, :nk], p.astype(BF16), preferred_element_type=F32)
            tiles.append(acc / denom)
        outs.append(jnp.concatenate(tiles, axis=1))
    o_ref[...] = jnp.concatenate(outs, axis=0).T


def _moba_prompt(q3, k3, proj3):
    b, l, _ = q3.shape
    nblk = l // MOBA_BLOCK
    assert (4.0 ** round(np.log(D_C) / np.log(4.0))) == D_C
    kern = functools.partial(_moba_prompt_kernel, nblk=nblk)
    v_col = (C_QKV_C + 2 * W_BRANCH) // LANES
    cols = pl.BlockSpec((None, l, LANES), lambda i, hp: (i, 0, hp))
    return pl.pallas_call(
        kern,
        grid=(b, W_BRANCH // LANES),
        in_specs=[cols, cols, pl.BlockSpec((None, l, LANES), lambda i, hp: (i, 0, v_col + hp))],
        out_specs=cols,
        out_shape=jax.ShapeDtypeStruct((b, l, W_BRANCH), F32),
        compiler_params=pltpu.CompilerParams(dimension_semantics=("arbitrary", "arbitrary"),
                                             vmem_limit_bytes=VMEM_LIMIT),
    )(q3, k3, proj3)


def _moba_decode_kernel(pt_ref, q_ref, kn_ref, vn_ref, *refs, npage, page, lq):
    k_refs = refs[:npage]
    v_refs = refs[npage:2 * npage]
    o_ref = refs[2 * npage]
    del pt_ref
    scale = D_C ** -0.5
    nrow = lq * H_C
    ppb = MOBA_BLOCK // page
    nblk = npage // ppb
    lp = q_ref.shape[0]

    lane_head = _iota2((H_C, W_BRANCH), 1) // D_C
    own_head = (lane_head == _iota2((H_C, W_BRANCH), 0)).astype(F32)
    q = q_ref[...]
    qbd = jnp.concatenate([q[t:t + 1, :] * own_head for t in range(lq)], axis=0)
    qb = (qbd * scale).astype(BF16)

    lane = _iota2((1, LANES), 1)
    ms, ls_, accs = [], [], []
    ksum = jnp.zeros((W_BRANCH, LANES), F32)
    for n in range(nblk):
        pages = range(n * ppb, (n + 1) * ppb)
        k_t = jnp.concatenate([k_refs[i][...].reshape(W_BRANCH, page) for i in pages], axis=1)
        v_t = jnp.concatenate([v_refs[i][...].reshape(W_BRANCH, page) for i in pages], axis=1)
        s = jnp.dot(qb, k_t.astype(BF16), preferred_element_type=F32)
        m = jnp.max(s, axis=-1, keepdims=True)
        p = jnp.exp(s - m)
        ms.append(m)
        ls_.append(jnp.sum(p, axis=-1, keepdims=True))
        accs.append(lax.dot_general(p.astype(BF16), v_t.astype(BF16), NT, preferred_element_type=F32))
        ksum = ksum + jnp.sum(k_t, axis=-1, keepdims=True) * (lane == n).astype(F32)

    gate = _mm3(qbd, ksum) * (1.0 / MOBA_BLOCK)
    sel = _rank_select(gate, nblk, nblk, 1)

    pad = jnp.zeros((LANES - lp, W_BRANCH), F32)
    k_new = jnp.concatenate([kn_ref[...], pad], axis=0)
    v_new = jnp.concatenate([vn_ref[...], pad], axis=0)
    s = lax.dot_general(qb, k_new.astype(BF16), NT, preferred_element_type=F32)
    t_key = _iota2((nrow, LANES), 1)
    t_qry = _iota2((nrow, LANES), 0) // H_C
    s = jnp.where(t_key <= t_qry, s, -jnp.inf)
    m_all = jnp.max(s, axis=-1, keepdims=True)
    p_own = jnp.exp(s - m_all)
    l_own = jnp.sum(p_own, axis=-1, keepdims=True)
    acc_own = jnp.dot(p_own.astype(BF16), v_new.astype(BF16), preferred_element_type=F32)
    m_own = m_all

    m_eff = []
    for n in range(nblk):
        me = jnp.where(sel[:, n:n + 1] > 0.0, ms[n], -jnp.inf)
        m_eff.append(me)
        m_all = jnp.maximum(m_all, me)
    w_own = jnp.exp(m_own - m_all)
    l_tot = l_own * w_own
    acc = acc_own * w_own
    for n in range(nblk):
        w = jnp.exp(m_eff[n] - m_all)
        l_tot = l_tot + ls_[n] * w
        acc = acc + accs[n] * w
    out = acc / l_tot
    own_rows = jnp.concatenate([own_head] * lq, axis=0)
    out = (out * own_rows).reshape(lq, H_C, W_BRANCH).sum(axis=1)
    o_ref[...] = jnp.concatenate([out, jnp.zeros((lp - lq, W_BRANCH), F32)], axis=0)


def _moba_decode(page_table, q3, k3, v3, cache_k, cache_v, layer, lq):
    b, lp, _ = q3.shape
    npage = page_table.shape[1]
    page = cache_k.shape[-1]
    kern = functools.partial(_moba_decode_kernel, npage=npage, page=page, lq=lq)
    tok = pl.BlockSpec((None, lp, W_BRANCH), lambda i, pt: (i, 0, 0))

    def page_spec(p_i):
        return pl.BlockSpec((None, None, H_C, D_C, page), lambda i, pt: (layer, pt[i, p_i], 0, 0, 0))

    grid_spec = pltpu.PrefetchScalarGridSpec(
        num_scalar_prefetch=1,
        grid=(b,),
        in_specs=[tok, tok, tok] + [page_spec(p_i) for p_i in range(npage)] * 2,
        out_specs=pl.BlockSpec((None, lp, W_BRANCH), lambda i, pt: (i, 0, 0)),
    )
    return pl.pallas_call(
        kern,
        grid_spec=grid_spec,
        out_shape=jax.ShapeDtypeStruct((b, lp, W_BRANCH), F32),
        compiler_params=pltpu.CompilerParams(dimension_semantics=("arbitrary",), vmem_limit_bytes=VMEM_LIMIT),
    )(page_table, q3, k3, v3, *([cache_k] * npage), *([cache_v] * npage))


def _merge_kernel(oa_ref, ob_ref, oc_ref, zc_ref, gate_ref, x_ref, ple_ref, wb_ref, bg_ref, wo_ref,
                  wpg_ref, bpg_ref, wple_ref, y_ref):
    branches = (oa_ref[...], ob_ref[...], oc_ref[...] * _silu(zc_ref[...]))
    up = None
    for n, o in enumerate(branches):
        sl = slice(n * D_MODEL, (n + 1) * D_MODEL)
        gate = _sigmoid(gate_ref[:, sl] + bg_ref[:, sl])
        term = gate * jnp.dot(o.astype(BF16), wb_ref[n], preferred_element_type=F32)
        up = term if up is None else up + term
    y = x_ref[...] + jnp.dot(up.astype(BF16), wo_ref[...], preferred_element_type=F32)
    pg = _sigmoid(jnp.dot(y.astype(BF16), wpg_ref[...], preferred_element_type=F32) + bpg_ref[...])
    y_ref[...] = y + pg * jnp.dot(ple_ref[...].astype(BF16), wple_ref[...], preferred_element_type=F32)


def _merge(oa, ob, oc, proj2, x2, ple3, ple_layer, wb, bg, wo, wpg, bpg, wple, tm):
    t = x2.shape[0]
    full = lambda shape: pl.BlockSpec(shape, lambda i: (0,) * len(shape))
    row = lambda w: pl.BlockSpec((tm, w), lambda i: (i, 0))
    return pl.pallas_call(
        _merge_kernel,
        grid=(t // tm,),
        in_specs=[row(W_BRANCH), row(W_BRANCH), row(W_BRANCH),
                  pl.BlockSpec((tm, W_BRANCH), lambda i: (i, C_Z_C // W_BRANCH)),
                  pl.BlockSpec((tm, N_BRANCH * D_MODEL), lambda i: (i, C_GATE // (N_BRANCH * D_MODEL))),
                  row(D_MODEL), pl.BlockSpec((None, tm, D_PLE), lambda i: (ple_layer, i, 0)),
                  full((N_BRANCH, W_BRANCH, D_MODEL)), full((1, N_BRANCH * D_MODEL)), full((D_MODEL, D_MODEL)),
                  full((D_MODEL, D_MODEL)), full((1, D_MODEL)), full((D_PLE, D_MODEL))],
        out_specs=row(D_MODEL),
        out_shape=jax.ShapeDtypeStruct((t, D_MODEL), F32),
        compiler_params=pltpu.CompilerParams(dimension_semantics=("arbitrary",), vmem_limit_bytes=VMEM_LIMIT),
    )(oa, ob, oc, proj2, proj2, x2, ple3, wb, bg, wo, wpg, bpg, wple)


def _arrange_w_in(w):
    offs = np.cumsum((0,) + IN_SPLITS)
    qkv_a, z_a, beta, alpha, f_b, z_b, qkv_c, z_c, gate = [w[:, offs[i]:offs[i + 1]] for i in range(9)]
    d = w.shape[0]
    ba = jnp.concatenate([beta, alpha, jnp.zeros((d, LANES - 2 * H_A), w.dtype)], axis=1)
    tail = jnp.zeros((d, N_PROJ - C_BA - LANES), w.dtype)
    out = jnp.concatenate([qkv_a, z_a, f_b[:, :3 * W_BRANCH], z_b, qkv_c, z_c, gate,
                           f_b[:, 3 * W_BRANCH:], ba, tail], axis=1)
    return out.astype(BF16)


def _layer_params(i, p):
    row = lambda v: v.reshape(1, -1)
    lanes_ba = lambda v: jnp.zeros((1, LANES), F32).at[0, H_A:2 * H_A].set(v)
    lora_pad = jnp.zeros((LORA, W_BRANCH), F32)
    head = np.arange(W_BRANCH) // N_B
    half = ROPE_DIM // 2
    inv = ROPE_THETA ** (-jnp.arange(half, dtype=F32) / half)
    d = np.arange(LANES) % D_C
    inv_lanes = jnp.where(d < ROPE_DIM, inv[d % half], 0.0).reshape(1, LANES)
    return dict(
        norm_g=row(p['norm_g'][i]),
        w_in=_arrange_w_in(p['w_in'][i]),
        conv=p['conv_a'][i],
        pa=lanes_ba(p['a_log'][i]), pb=lanes_ba(p['dt_bias'][i]), onorm=row(p['onorm_a'][i]),
        rw_vecs=(row(p['mu_b'][i]), row(p['w0_b'][i]), row(p['a0_b'][i]), row(p['kk_b'][i]), row(p['ka_b'][i]),
                 row(p['rk_b'][i]), row(p['gn_g_b'][i]), row(p['gn_b_b'][i])),
        w2p=jnp.concatenate([p['w2_b'][i], lora_pad], axis=0).astype(BF16),
        a2p=jnp.concatenate([lora_pad, p['a2_b'][i]], axis=0).astype(BF16),
        bd=jnp.asarray(head[:, None] == head[None, :], BF16),
        qn=row(jnp.tile(p['qn_c'][i], H_C)), kn=row(jnp.tile(p['kn_c'][i], H_C)), inv=inv_lanes,
        wb=p['w_branch'][i].astype(BF16), bg=row(p['b_gate'][i]), wo=p['w_out'][i].astype(BF16),
        wpg=p['w_pg'][i].astype(BF16), bpg=row(p['b_pg'][i]), wple=p['w_ple'][i].astype(BF16),
    )


def _hist_rows(rows):
    b, n, c = rows.shape
    return jnp.concatenate([jnp.zeros((b, HIST - n, c), F32), rows], axis=1)


def _layer(x3, ple3, ple_layer, lp, *, l_valid, pos0, dn_s, rw_s, state_layer, dn_buf, rw_prev, nseg, lt, ls, tm, attend):
    b, l, d = x3.shape
    t = b * l
    proj2 = _proj(x3.reshape(t, d), lp['norm_g'], lp['w_in'], min(t, 1024), 1920)
    proj3 = proj2.reshape(b, l, N_PROJ)

    oa, dn_new = _gdn(proj3, _hist_rows(dn_buf), dn_s, state_layer, lp['conv'], lp['pa'], lp['pb'], lp['onorm'],
                      nseg=nseg, lt=lt, ls=ls, l_valid=l_valid)
    ob, rw_new = _rwkv(proj3, _hist_rows(rw_prev[:, None, :]), rw_s, state_layer, lp['rw_vecs'],
                       lp['w2p'], lp['a2p'], lp['bd'], nseg=nseg, lt=lt, ls=ls, l_valid=l_valid)
    qr, kr = _qkprep(proj2, lp['qn'], lp['kn'], lp['inv'], lp['bd'], tm=min(tm, t), lseq=l, pos0=pos0)
    v2 = proj2[:, C_QKV_C + 2 * W_BRANCH:C_QKV_C + 3 * W_BRANCH]
    oc = attend(qr, kr, v2, proj3)

    y2 = _merge(oa.reshape(t, W_BRANCH), ob.reshape(t, W_BRANCH), oc.reshape(t, W_BRANCH), proj2,
                x3.reshape(t, d), ple3, ple_layer, lp['wb'], lp['bg'], lp['wo'], lp['wpg'],
                lp['bpg'], lp['wple'], min(tm, t))

    f_last = jnp.concatenate([proj3[:, l_valid - 1, C_RKV:C_RKV + 3 * W_BRANCH],
                              proj3[:, l_valid - 1, C_WLAL:C_WLAL + 2 * LORA]], axis=-1)
    new_state = (kr.reshape(b, l, H_C, D_C)[:, :l_valid], v2.reshape(b, l, H_C, D_C)[:, :l_valid], dn_new,
                 proj3[:, l_valid - (CONV_W - 1):l_valid, C_QKV_A:C_QKV_A + QKV_A],
                 rw_new, f_last)
    return y2.reshape(b, l, d), new_state


def _attend_prompt(b, l):
    def attend(qr, kr, v2, proj3):
        shp = (b, l, W_BRANCH)
        return _moba_prompt(qr.reshape(shp), kr.reshape(shp), proj3).reshape(b * l, W_BRANCH)
    return attend


def _attend_decode(b, lp_, lq, page_table, cache_k4, cache_v4, layer):
    def attend(qr, kr, v2, proj3):
        shp = (b, lp_, W_BRANCH)
        return _moba_decode(page_table, qr.reshape(shp), kr.reshape(shp), v2.reshape(shp),
                            cache_k4, cache_v4, layer, lq)
    return attend


DEC_PAD = 8


def kernel(x_prompt, x_sample, cache_k, cache_v, state_dn, state_dn_conv, state_rwkv, state_rwkv_shift,
           page_table, p_prompt, p_sample, norm_g, w_in, conv_a, a_log, dt_bias, onorm_a, mu_b, w0_b, w2_b,
           a0_b, a2_b, kk_b, ka_b, rk_b, gn_g_b, gn_b_b, qn_c, kn_c, w_branch, b_gate, w_out, w_ple, w_pg,
           b_pg):
    params = dict(norm_g=norm_g, w_in=w_in, conv_a=conv_a, a_log=a_log, dt_bias=dt_bias, onorm_a=onorm_a,
                  mu_b=mu_b, w0_b=w0_b, w2_b=w2_b, a0_b=a0_b, a2_b=a2_b, kk_b=kk_b, ka_b=ka_b, rk_b=rk_b,
                  gn_g_b=gn_g_b, gn_b_b=gn_b_b, qn_c=qn_c, kn_c=kn_c, w_branch=w_branch, b_gate=b_gate,
                  w_out=w_out, w_ple=w_ple, w_pg=w_pg, b_pg=b_pg)
    depth = norm_g.shape[0]
    b_p, l_p, _ = x_prompt.shape
    b_s, l_s, _ = x_sample.shape
    n_phys, page = cache_k.shape[1], cache_k.shape[2]
    past_len = page_table.shape[1] * page
    assert l_p % 512 == 0 and CONV_W - 1 <= l_s <= DEC_PAD and b_s % (CHUNK // DEC_PAD) == 0
    assert past_len % MOBA_BLOCK == 0 and MOBA_BLOCK % page == 0

    pad_rows = lambda a: jnp.pad(a, ((0, 0), (0, DEC_PAD - l_s), (0, 0)))
    xs = pad_rows(x_sample)
    cache_k4 = cache_k.transpose(0, 1, 3, 4, 2)
    cache_v4 = cache_v.transpose(0, 1, 3, 4, 2)
    xp = x_prompt
    new_p, new_s = [], []
    for i in range(depth):
        lp = _layer_params(i, params)
        xp, st_p = _layer(xp, p_prompt.reshape(depth, b_p * l_p, D_PLE), i, lp, l_valid=l_p, pos0=0,
                          dn_s=jnp.zeros((1, b_p, H_A, DK_A, DV_A), F32),
                          rw_s=jnp.zeros((1, b_p, H_B, N_B, N_B), F32), state_layer=0,
                          dn_buf=jnp.zeros((b_p, CONV_W - 1, QKV_A), F32), rw_prev=jnp.zeros((b_p, SHIFT_B), F32),
                          nseg=1, lt=512, ls=CHUNK, tm=512, attend=_attend_prompt(b_p, l_p))
        new_p.append(st_p)
        xs, st_s = _layer(xs, pad_rows(p_sample[i]).reshape(1, b_s * DEC_PAD, D_PLE), 0, lp, l_valid=l_s, pos0=past_len,
                          dn_s=state_dn, rw_s=state_rwkv, state_layer=i, dn_buf=state_dn_conv[i],
                          rw_prev=state_rwkv_shift[i], nseg=CHUNK // DEC_PAD, lt=DEC_PAD, ls=DEC_PAD, tm=512,
                          attend=_attend_decode(b_s, DEC_PAD, l_s, page_table, cache_k4, cache_v4, i))
        new_s.append(st_s)
    outs_p = [jnp.stack(t) for t in zip(*new_p)]
    outs_s = [jnp.stack(t) for t in zip(*new_s)]
    return (xp, xs[:, :l_s], *outs_p, *outs_s)
```

```python
import functools

import numpy as np
import jax
import jax.numpy as jnp
from jax import lax
from jax.experimental import pallas as pl
from jax.experimental.pallas import tpu as pltpu

F32 = jnp.float32
BF16 = jnp.bfloat16

D_MODEL = 1024
D_PLE = 256
N_BRANCH = 3
W_BRANCH = 512
NORM_EPS = 1e-6
DK_A = 128
DV_A = 128
H_A = 4
QKV_A = 1536
CONV_W = 4
N_B = 64
H_B = 8
LORA = 64
SHIFT_B = 3 * W_BRANCH + 2 * LORA
GN_EPS = 64e-5
D_C = 64
H_C = 8
MOBA_BLOCK = 256
MOBA_TOPK = 3
ROPE_DIM = 16
ROPE_THETA = 500000.0
IN_SPLITS = (QKV_A, W_BRANCH, H_A, H_A, SHIFT_B, W_BRANCH, 3 * W_BRANCH, W_BRANCH, N_BRANCH * D_MODEL)

C_QKV_A = 0
C_Z_A = 1536
C_RKV = 2048
C_Z_B = 3584
C_QKV_C = 4096
C_Z_C = 5632
C_GATE = 6144
C_WLAL = 9216
C_BA = 9344
N_PROJ = 9600

CHUNK = 64
HIST = 8
CHUNKS_PER_TRIP_A = 8
CHUNKS_PER_TRIP_B = 8
LANES = 128
VMEM_LIMIT = 56 * 1024 * 1024

NN = (((1,), (0,)), ((), ()))
BNN = (((2,), (1,)), ((0,), (0,)))
NT = (((1,), (1,)), ((), ()))
TN = (((0,), (0,)), ((), ()))
HIGHEST = lax.Precision.HIGHEST


def _mm(a, b, dims=NN):
    return lax.dot_general(a.astype(BF16), b.astype(BF16), dims, preferred_element_type=F32)


def _mm32(a, b, dims=NN):
    return lax.dot_general(a, b, dims, precision=HIGHEST, preferred_element_type=F32)


def _sigmoid(x):
    return 1.0 / (1.0 + jnp.exp(-x))


def _silu(x):
    return x * _sigmoid(x)


def _softplus(x):
    return jnp.maximum(x, 0.0) + jnp.log1p(jnp.exp(-jnp.abs(x)))


def _iota2(shape, dim):
    return lax.broadcasted_iota(jnp.int32, shape, dim)


def _split(x):
    hi = x.astype(BF16)
    return hi, (x - hi.astype(F32)).astype(BF16)


def _dotb(a, b, dims=NN):
    return lax.dot_general(a, b, dims, preferred_element_type=F32)


def _mm_sel(sel, x):
    hi, lo = _split(x)
    return _dotb(sel, hi) + _dotb(sel, lo)


def _mm3(a, b, dims=NN):
    ah, al = _split(a)
    bh, bl = _split(b)
    return _dotb(ah, bh, dims) + (_dotb(ah, bl, dims) + _dotb(al, bh, dims))


SOLVE_MM_A = _mm
SOLVE_MM_B = _mm


def _inv_unit_lower(lmats, nilpotent, mm):
    n = lmats[0].shape[-1]
    dims = NN if lmats[0].ndim == 2 else BNN
    eye = (_iota2((n, n), 0) == _iota2((n, n), 1)).astype(F32)
    ps = [-lmat for lmat in lmats]
    accs = [eye + p for p in ps]
    span = 2
    while span < nilpotent:
        ps = [mm(p, p, dims) for p in ps]
        accs = [acc + mm(acc, p, dims) for acc, p in zip(accs, ps)]
        span *= 2
    return accs


def _head_sums(xs, bd):
    n = xs[0].shape[0]
    parts = []
    for x in xs:
        parts.extend(_split(x))
    out = jnp.dot(jnp.concatenate(parts, axis=0), bd, preferred_element_type=F32)
    return [out[2 * i * n:(2 * i + 1) * n] + out[(2 * i + 1) * n:(2 * i + 2) * n] for i in range(len(xs))]


def _head_sum(x, bd):
    return _head_sums([x], bd)[0]


def _proj_kernel(x_ref, g_ref, w_ref, o_ref):
    x = x_ref[...]
    h = x * lax.rsqrt(jnp.mean(x * x, axis=-1, keepdims=True) + NORM_EPS) * g_ref[...]
    o_ref[...] = jnp.dot(h.astype(BF16), w_ref[...], preferred_element_type=F32)


def _proj(x2d, g, w, tm, tn):
    t, d = x2d.shape
    n = w.shape[1]
    return pl.pallas_call(
        _proj_kernel,
        grid=(n // tn, t // tm),
        in_specs=[pl.BlockSpec((tm, d), lambda j, i: (i, 0)),
                  pl.BlockSpec((1, d), lambda j, i: (0, 0)),
                  pl.BlockSpec((d, tn), lambda j, i: (0, j))],
        out_specs=pl.BlockSpec((tm, tn), lambda j, i: (i, j)),
        out_shape=jax.ShapeDtypeStruct((t, n), F32),
        compiler_params=pltpu.CompilerParams(dimension_semantics=("arbitrary", "arbitrary"),
                                             vmem_limit_bytes=VMEM_LIMIT),
    )(x2d, g, w)


def _chunk_masks(n, ls):
    r = _iota2((n, n), 0)
    c = _iota2((n, n), 1)
    same = (r // ls) == (c // ls)
    return same, same & (r >= c), same & (r > c)


def _valid_rows(l_idx, lt, r0, ls, l_valid):
    t = l_idx * lt + r0 + (_iota2((CHUNK, 1), 0) % ls)
    return t < l_valid


def _gdn_kernel(qkv_ref, z_ref, ba_ref, buf_ref, s0_ref, cw_ref, pa_ref, pb_ref, on_ref,
                o_ref, sout_ref, xs_ref, st_ref, *, nseg, lt, ls, l_valid, l_total):
    l_idx = pl.program_id(1)
    nrow = H_A * CHUNK

    @pl.when(l_idx == 0)
    def _():
        xs_ref[:, 0:HIST, :] = buf_ref[...]
        st_ref[...] = s0_ref[...]

    @pl.when(l_idx > 0)
    def _():
        xs_ref[:, 0:HIST, :] = xs_ref[:, lt:lt + HIST, :]

    xs_ref[:, HIST:lt + HIST, :] = qkv_ref[...]

    same_r, c_incl, c_strict = _chunk_masks(CHUNK, ls)
    tri_r = c_incl.astype(BF16)
    same_r = same_r.astype(BF16)
    eye_c = (_iota2((CHUNK, CHUNK), 0) == _iota2((CHUNK, CHUNK), 1)).astype(F32)
    cw = cw_ref[...]
    neg_a = -jnp.exp(pa_ref[...])
    pb = pb_ref[...]
    onorm = on_ref[...]

    def conv(s, r0, col):
        win = xs_ref[s, pl.ds(r0, ls + HIST), col:col + LANES]
        w = cw[:, col:col + LANES]
        acc = pltpu.roll(win, 3, 0)[HIST:] * w[0:1] + pltpu.roll(win, 2, 0)[HIST:] * w[1:2]
        acc = acc + pltpu.roll(win, 1, 0)[HIST:] * w[2:3]
        acc = acc + win[HIST:] * w[3:4]
        return _silu(acc)

    def part(r0, col):
        return jnp.concatenate([conv(s, r0, col) for s in range(nseg)], axis=0)

    def local(c):
        r0 = pl.multiple_of(c * ls, ls)
        ba = jnp.concatenate([ba_ref[s, pl.ds(r0, ls), :] for s in range(nseg)], axis=0)
        beta_all = _sigmoid(ba)
        g_all = neg_a * _softplus(ba + pb)
        if l_valid < l_total:
            ok = _valid_rows(l_idx, lt, r0, ls, l_valid)
            beta_all = jnp.where(ok, beta_all, 0.0)
            g_all = jnp.where(ok, g_all, 0.0)
        gcum = _mm_sel(tri_r, g_all)
        if nseg == 1:
            gtot = jnp.broadcast_to(gcum[CHUNK - 1:CHUNK], gcum.shape)
        else:
            gtot = _mm_sel(same_r, g_all)

        qs, ks, vs, betas, gcs, gts, lmats, qks = [], [], [], [], [], [], [], []
        for h in range(H_A):
            q = part(r0, h * DK_A)
            k = part(r0, H_A * DK_A + h * DK_A)
            v = part(r0, 2 * H_A * DK_A + h * DV_A)
            q = q * lax.rsqrt(jnp.sum(q * q, axis=-1, keepdims=True) + 1e-6) * (DK_A ** -0.5)
            k = k * lax.rsqrt(jnp.sum(k * k, axis=-1, keepdims=True) + 1e-6)
            beta_h = beta_all[:, h:h + 1]
            g_h = gcum[:, H_A + h:H_A + h + 1]
            g_row = jnp.sum(eye_c * g_h, axis=0, keepdims=True)
            decay = jnp.exp(jnp.where(c_incl, g_h - g_row, -jnp.inf))
            kq_k = _mm(jnp.concatenate([k, q], axis=0), k, NT)
            lmats.append(jnp.where(c_strict, beta_h * kq_k[:CHUNK] * decay, 0.0))
            qks.append(jnp.where(c_incl, kq_k[CHUNK:] * decay, 0.0))
            qs.append(q)
            ks.append(k)
            vs.append(v)
            betas.append(beta_h)
            gcs.append(g_h)
            gts.append(gtot[:, H_A + h:H_A + h + 1])
        q_st = jnp.concatenate(qs, axis=0)
        k_st = jnp.concatenate(ks, axis=0)
        v_st = jnp.concatenate(vs, axis=0)
        beta = jnp.concatenate(betas, axis=0)
        g_col = jnp.concatenate(gcs, axis=0)
        gt_col = jnp.concatenate(gts, axis=0)
        return dict(
            r0=r0, gt_col=gt_col,
            lmat=jnp.stack(lmats),
            rhs=jnp.concatenate([v_st * beta, k_st * (beta * jnp.exp(g_col))], axis=1),
            qk=jnp.stack(qks),
            qg=q_st * jnp.exp(g_col),
            kd=k_st * jnp.exp(gt_col - g_col))

    def advance(loc, ainv):
        r0, gt_col, qk, qg, kd = loc['r0'], loc['gt_col'], loc['qk'], loc['qg'], loc['kd']
        sol = SOLVE_MM_A(ainv, loc['rhs'].reshape(H_A, CHUNK, DK_A + DV_A), BNN).reshape(nrow, DK_A + DV_A)
        bv = sol[:, :DV_A]
        wk = sol[:, DV_A:]
        wks, qgs = [], []
        for h in range(H_A):
            for s in range(nseg):
                a = h * CHUNK + s * ls
                both = _mm(jnp.concatenate([wk[a:a + ls], qg[a:a + ls]], axis=0), st_ref[s, h])
                wks.append(both[:ls])
                qgs.append(both[ls:])
        u = bv - jnp.concatenate(wks, axis=0)
        o = jnp.concatenate(qgs, axis=0) + _mm(qk, u.reshape(H_A, CHUNK, DV_A), BNN).reshape(nrow, DV_A)

        for h in range(H_A):
            a = h * CHUNK
            for s in range(nseg):
                b = a + s * ls
                if nseg == 1:
                    kd_s = kd[a:a + CHUNK]
                else:
                    rows = _iota2((CHUNK, 1), 0) // ls
                    kd_s = jnp.where(rows == s, kd[a:a + CHUNK], 0.0)
                upd = _mm(kd_s, u[a:a + CHUNK], TN)
                st_ref[s, h] = st_ref[s, h] * jnp.exp(gt_col[b:b + 1]) + upd
            oh = o[a:a + CHUNK]
            oh = oh * lax.rsqrt(jnp.mean(oh * oh, axis=-1, keepdims=True) + NORM_EPS) * onorm
            for s in range(nseg):
                zz = z_ref[s, pl.ds(r0, ls), h * DV_A:(h + 1) * DV_A]
                o_ref[s, pl.ds(r0, ls), h * DV_A:(h + 1) * DV_A] = oh[s * ls:(s + 1) * ls] * _silu(zz)

    nchunk = lt // ls
    per_trip = min(CHUNKS_PER_TRIP_A, nchunk)

    def trip(i, carry):
        locs = [local(i * per_trip + j) for j in range(per_trip)]
        ainvs = _inv_unit_lower([loc['lmat'] for loc in locs], ls, SOLVE_MM_A)
        for loc, ainv in zip(locs, ainvs):
            advance(loc, ainv)
        return carry

    lax.fori_loop(0, nchunk // per_trip, trip, 0)

    @pl.when(l_idx == pl.num_programs(1) - 1)
    def _():
        sout_ref[...] = st_ref[...]


def _gdn(proj3, buf8, s0, layer, cw, pa, pb, onorm, *, nseg, lt, ls, l_valid):
    b, l, _ = proj3.shape
    kern = functools.partial(_gdn_kernel, nseg=nseg, lt=lt, ls=ls, l_valid=l_valid, l_total=l)
    full = lambda shape: pl.BlockSpec(shape, lambda i, j: (0,) * len(shape))
    return pl.pallas_call(
        kern,
        grid=(b // nseg, l // lt),
        in_specs=[pl.BlockSpec((nseg, lt, QKV_A), lambda i, j: (i, j, C_QKV_A // QKV_A)),
                  pl.BlockSpec((nseg, lt, W_BRANCH), lambda i, j: (i, j, C_Z_A // W_BRANCH)),
                  pl.BlockSpec((nseg, lt, LANES), lambda i, j: (i, j, C_BA // LANES)),
                  pl.BlockSpec((nseg, HIST, QKV_A), lambda i, j: (i, 0, 0)),
                  pl.BlockSpec((None, nseg, H_A, DK_A, DV_A), lambda i, j: (layer, i, 0, 0, 0)),
                  full((CONV_W, QKV_A)), full((1, LANES)), full((1, LANES)), full((1, DV_A))],
        out_specs=[pl.BlockSpec((nseg, lt, W_BRANCH), lambda i, j: (i, j, 0)),
                   pl.BlockSpec((nseg, H_A, DK_A, DV_A), lambda i, j: (i, 0, 0, 0))],
        out_shape=[jax.ShapeDtypeStruct((b, l, W_BRANCH), F32),
                   jax.ShapeDtypeStruct((b, H_A, DK_A, DV_A), F32)],
        scratch_shapes=[pltpu.VMEM((nseg, lt + HIST, QKV_A), F32),
                        pltpu.VMEM((nseg, H_A, DK_A, DV_A), F32)],
        compiler_params=pltpu.CompilerParams(dimension_semantics=("arbitrary", "arbitrary"),
                                             vmem_limit_bytes=VMEM_LIMIT),
    )(proj3, proj3, proj3, buf8, s0, cw, pa, pb, onorm)


GRP = 4 * N_B


def _rwkv_kernel(r_ref, k_ref, v_ref, wa_ref, z_ref, prev_ref, s0_ref, mu_ref, w0_ref, w2_ref,
                 a0_ref, a2_ref, kkp_ref, kap_ref, rk_ref, gg_ref, gb_ref, bd_ref,
                 o_ref, sout_ref, fs_ref, st_ref, *, nseg, lt, ls, l_valid, l_total):
    l_idx = pl.program_id(1)
    nrow = 4 * CHUNK
    ngrp = W_BRANCH // GRP

    @pl.when(l_idx == 0)
    def _():
        fs_ref[:, 0:HIST, :] = prev_ref[...]
        st_ref[...] = jnp.zeros(st_ref.shape, F32)
        for s in range(nseg):
            for h in range(H_B):
                g, j = divmod(h, 4)
                st_ref[s, g, j * N_B:(j + 1) * N_B, j * N_B:(j + 1) * N_B] = s0_ref[s, h]

    @pl.when(l_idx > 0)
    def _():
        fs_ref[:, 0:HIST, :] = fs_ref[:, lt:lt + HIST, :]

    fs_ref[:, HIST:lt + HIST, 0:W_BRANCH] = r_ref[...]
    fs_ref[:, HIST:lt + HIST, W_BRANCH:2 * W_BRANCH] = k_ref[...]
    fs_ref[:, HIST:lt + HIST, 2 * W_BRANCH:3 * W_BRANCH] = v_ref[...]
    fs_ref[:, HIST:lt + HIST, 3 * W_BRANCH:SHIFT_B] = wa_ref[...]

    same_r, tri_r, _ = _chunk_masks(CHUNK, ls)
    tri_r = tri_r.astype(BF16)
    same_r = same_r.astype(BF16)
    lane_head = _iota2((1, GRP), 1) // N_B
    head_masks = [(lane_head == j).astype(F32) for j in range(4)]
    bd_state = (_iota2((GRP, GRP), 0) // N_B) == (_iota2((GRP, GRP), 1) // N_B)
    head_rows = ((_iota2((nrow, GRP), 0) // CHUNK) == (_iota2((nrow, GRP), 1) // N_B)).astype(F32)
    t_row = _iota2((nrow, CHUNK), 0) % CHUNK
    t_col = _iota2((nrow, CHUNK), 1)
    t_same = (t_row // ls) == (t_col // ls)
    t_incl = t_same & (t_col <= t_row)
    t_strict = t_same & (t_col < t_row)
    mu = mu_ref[...]
    bd = bd_ref[...]

    def mixed(r0, col, width):
        outs = []
        for s in range(nseg):
            win = fs_ref[s, pl.ds(r0, ls + HIST), col:col + width]
            cur = win[HIST:]
            prev = pltpu.roll(win, 1, 0)[HIST:]
            outs.append(cur + (prev - cur) * mu[:, col:col + width])
        return jnp.concatenate(outs, axis=0)

    def stack_masked(x):
        return jnp.concatenate([x * head_masks[j] for j in range(4)], axis=0)

    def unstack(x):
        return x[0:CHUNK] + x[CHUNK:2 * CHUNK] + x[2 * CHUNK:3 * CHUNK] + x[3 * CHUNK:4 * CHUNK]

    def local(c):
        r0 = pl.multiple_of(c * ls, ls)
        r = mixed(r0, 0, W_BRANCH)
        k = mixed(r0, W_BRANCH, W_BRANCH)
        v = mixed(r0, 2 * W_BRANCH, W_BRANCH)
        wa = mixed(r0, 3 * W_BRANCH, 2 * LORA)
        w_raw = w0_ref[...] + _mm(jnp.tanh(wa), w2_ref[...])
        logw = -jnp.exp(-_softplus(-w_raw) - 0.5)
        a_sig = _sigmoid(a0_ref[...] + _mm(wa, a2_ref[...]))
        kk = k * kkp_ref[...]
        k_mod = k * (1.0 + (a_sig - 1.0) * kap_ref[...])
        kk_ss, rk_sum = _head_sums([kk * kk, r * k_mod * rk_ref[...]], bd)
        kk = kk * lax.rsqrt(kk_ss + 1e-6)
        b_vec = kk * a_sig
        if l_valid < l_total:
            ok = _valid_rows(l_idx, lt, r0, ls, l_valid)
            logw = jnp.where(ok, logw, 0.0)
            b_vec = jnp.where(ok, b_vec, 0.0)
            k_mod = jnp.where(ok, k_mod, 0.0)
        gcum = _mm_sel(tri_r, logw)
        if nseg == 1:
            gtot = jnp.broadcast_to(gcum[CHUNK - 1:CHUNK], gcum.shape)
        else:
            gtot = _mm_sel(same_r, logw)
        e_neg = jnp.exp(-gcum)
        e_end = jnp.exp(gtot - gcum)
        a_t = -kk * jnp.exp(gcum - logw)
        b_t = b_vec * e_neg
        k_t = k_mod * e_neg
        r_t = r * jnp.exp(gcum)
        b_h = b_vec * e_end
        k_h = k_mod * e_end

        groups = []
        for g in range(ngrp):
            sl = slice(g * GRP, (g + 1) * GRP)
            a_g, b_g, k_g, r_g, v_g = a_t[:, sl], b_t[:, sl], k_t[:, sl], r_t[:, sl], v[:, sl]
            a_st = stack_masked(a_g)
            r_st = stack_masked(r_g)
            ar_k = _mm(jnp.concatenate([a_st, r_st], axis=0), k_g, NT)
            groups.append(dict(
                a_g=a_g, r_g=r_g, v_g=v_g, b_h=b_h[:, sl], k_h=k_h[:, sl], gtot=gtot[:, sl],
                neg_aab=-jnp.where(t_strict, _mm(a_st, b_g, NT), 0.0).reshape(4, CHUNK, CHUNK),
                aak=jnp.where(t_strict, ar_k[:nrow], 0.0),
                ark=jnp.where(t_incl, ar_k[nrow:], 0.0),
                arb=jnp.where(t_incl, _mm(r_st, b_g, NT), 0.0)))
        return dict(r0=r0, v=v, rk_sum=rk_sum, groups=groups)

    def advance(loc, tinvs):
        ys = []
        for g, (grp, tinv) in enumerate(zip(loc['groups'], tinvs)):
            a_g, r_g, v_g = grp['a_g'], grp['r_g'], grp['v_g']
            ams, rms = [], []
            for s in range(nseg):
                rows_s = slice(s * ls, (s + 1) * ls)
                both = _mm(jnp.concatenate([a_g[rows_s], r_g[rows_s]], axis=0), st_ref[s, g], NT)
                ams.append(both[:ls])
                rms.append(both[ls:])
            am = jnp.concatenate(ams, axis=0)
            rm = jnp.concatenate(rms, axis=0)
            kv = _mm(jnp.concatenate([grp['aak'], grp['ark']], axis=0), v_g)
            rhs = stack_masked(am) + kv[:nrow] * head_rows
            u_st = SOLVE_MM_B(tinv, rhs.reshape(4, CHUNK, GRP), BNN).reshape(nrow, GRP)
            u = unstack(u_st)
            ys.append(rm + unstack((_mm(grp['arb'], u) + kv[nrow:]) * head_rows))
            for s in range(nseg):
                if nseg == 1:
                    u_s, v_s = u, v_g
                else:
                    rows = _iota2((CHUNK, 1), 0) // ls
                    u_s = jnp.where(rows == s, u, 0.0)
                    v_s = jnp.where(rows == s, v_g, 0.0)
                upd = _mm(jnp.concatenate([u_s, v_s], axis=0),
                          jnp.concatenate([grp['b_h'], grp['k_h']], axis=0), TN)
                dec = jnp.exp(grp['gtot'][s * ls:s * ls + 1])
                st_ref[s, g] = st_ref[s, g] * dec + jnp.where(bd_state, upd, 0.0)
        y = jnp.concatenate(ys, axis=1)
        mean = _head_sums([y], bd)[0] * (1.0 / N_B)
        yc = y - mean
        var = _head_sums([yc * yc], bd)[0] * (1.0 / N_B)
        yn = yc * lax.rsqrt(var + GN_EPS) * gg_ref[...] + gb_ref[...]
        out = yn + loc['rk_sum'] * loc['v']
        for s in range(nseg):
            zz = z_ref[s, pl.ds(loc['r0'], ls), :]
            o_ref[s, pl.ds(loc['r0'], ls), :] = out[s * ls:(s + 1) * ls] * _silu(zz)

    nchunk = lt // ls
    per_trip = min(CHUNKS_PER_TRIP_B, nchunk)

    def trip(i, carry):
        locs = [local(i * per_trip + j) for j in range(per_trip)]
        tinvs = _inv_unit_lower([grp['neg_aab'] for loc in locs for grp in loc['groups']], ls, SOLVE_MM_B)
        for j, loc in enumerate(locs):
            advance(loc, tinvs[j * ngrp:(j + 1) * ngrp])
        return carry

    lax.fori_loop(0, nchunk // per_trip, trip, 0)

    @pl.when(l_idx == pl.num_programs(1) - 1)
    def _():
        for s in range(nseg):
            for h in range(H_B):
                g, j = divmod(h, 4)
                sout_ref[s, h] = st_ref[s, g, j * N_B:(j + 1) * N_B, j * N_B:(j + 1) * N_B]


def _rwkv(proj3, prev8, s0, layer, vecs, w2p, a2p, bd, *, nseg, lt, ls, l_valid):
    b, l, _ = proj3.shape
    ngrp = W_BRANCH // GRP
    kern = functools.partial(_rwkv_kernel, nseg=nseg, lt=lt, ls=ls, l_valid=l_valid, l_total=l)
    full = lambda shape: pl.BlockSpec(shape, lambda i, j: (0,) * len(shape))
    blk = lambda col, w: pl.BlockSpec((nseg, lt, w), lambda i, j: (i, j, col // w))
    mu, w0, a0, kkp, kap, rk, gg, gb = vecs
    return pl.pallas_call(
        kern,
        grid=(b // nseg, l // lt),
        in_specs=[blk(C_RKV, W_BRANCH), blk(C_RKV + W_BRANCH, W_BRANCH), blk(C_RKV + 2 * W_BRANCH, W_BRANCH),
                  blk(C_WLAL, 2 * LORA), blk(C_Z_B, W_BRANCH),
                  pl.BlockSpec((nseg, HIST, SHIFT_B), lambda i, j: (i, 0, 0)),
                  pl.BlockSpec((None, nseg, H_B, N_B, N_B), lambda i, j: (layer, i, 0, 0, 0)),
                  full((1, SHIFT_B)), full((1, W_BRANCH)), full((2 * LORA, W_BRANCH)),
                  full((1, W_BRANCH)), full((2 * LORA, W_BRANCH)), full((1, W_BRANCH)),
                  full((1, W_BRANCH)), full((1, W_BRANCH)), full((1, W_BRANCH)), full((1, W_BRANCH)),
                  full((W_BRANCH, W_BRANCH))],
        out_specs=[pl.BlockSpec((nseg, lt, W_BRANCH), lambda i, j: (i, j, 0)),
                   pl.BlockSpec((nseg, H_B, N_B, N_B), lambda i, j: (i, 0, 0, 0))],
        out_shape=[jax.ShapeDtypeStruct((b, l, W_BRANCH), F32),
                   jax.ShapeDtypeStruct((b, H_B, N_B, N_B), F32)],
        scratch_shapes=[pltpu.VMEM((nseg, lt + HIST, SHIFT_B), F32),
                        pltpu.VMEM((nseg, ngrp, GRP, GRP), F32)],
        compiler_params=pltpu.CompilerParams(dimension_semantics=("arbitrary", "arbitrary"),
                                             vmem_limit_bytes=VMEM_LIMIT),
    )(proj3, proj3, proj3, proj3, proj3, prev8, s0, mu, w0, w2p, a0, a2p, kkp, kap, rk, gg, gb, bd)


def _qkprep_kernel(q_ref, k_ref, qn_ref, kn_ref, inv_ref, bd_ref, qo_ref, ko_ref, *, tm, lseq, pos0):
    pos = (pos0 + (pl.program_id(0) * tm + _iota2((tm, 1), 0)) % lseq).astype(F32)
    ang = pos * inv_ref[...]
    cos = jnp.cos(ang)
    sin = jnp.sin(ang)
    d = _iota2((1, LANES), 1) % D_C
    half = ROPE_DIM // 2
    c_main = jnp.where(d < ROPE_DIM, cos, 1.0)
    c_up = jnp.where(d < half, -sin, 0.0)
    c_dn = jnp.where((d >= half) & (d < ROPE_DIM), sin, 0.0)
    bd = bd_ref[...]

    def one(x_ref, g_ref, out_ref):
        x = x_ref[...]
        y = x * lax.rsqrt(_head_sum(x * x, bd) * (1.0 / D_C) + NORM_EPS) * g_ref[...]
        for blk in range(W_BRANCH // LANES):
            yb = y[:, blk * LANES:(blk + 1) * LANES]
            up = pltpu.roll(yb, LANES - half, 1)
            dn = pltpu.roll(yb, half, 1)
            out_ref[:, blk * LANES:(blk + 1) * LANES] = yb * c_main + up * c_up + dn * c_dn

    one(q_ref, qn_ref, qo_ref)
    one(k_ref, kn_ref, ko_ref)


def _qkprep(proj2, qn, kn, inv, bd, *, tm, lseq, pos0):
    t = proj2.shape[0]
    kern = functools.partial(_qkprep_kernel, tm=tm, lseq=lseq, pos0=pos0)
    full = lambda shape: pl.BlockSpec(shape, lambda i: (0,) * len(shape))
    return pl.pallas_call(
        kern,
        grid=(t // tm,),
        in_specs=[pl.BlockSpec((tm, W_BRANCH), lambda i: (i, C_QKV_C // W_BRANCH)),
                  pl.BlockSpec((tm, W_BRANCH), lambda i: (i, C_QKV_C // W_BRANCH + 1)),
                  full((1, W_BRANCH)), full((1, W_BRANCH)), full((1, LANES)), full((W_BRANCH, W_BRANCH))],
        out_specs=[pl.BlockSpec((tm, W_BRANCH), lambda i: (i, 0)),
                   pl.BlockSpec((tm, W_BRANCH), lambda i: (i, 0))],
        out_shape=[jax.ShapeDtypeStruct((t, W_BRANCH), F32), jax.ShapeDtypeStruct((t, W_BRANCH), F32)],
        compiler_params=pltpu.CompilerParams(dimension_semantics=("arbitrary",), vmem_limit_bytes=VMEM_LIMIT),
    )(proj2, proj2, qn, kn, inv, bd)


def _rank_select(gate, nblk, limit, axis):
    idx = _iota2(gate.shape, axis)
    rank = jnp.zeros(gate.shape, F32)
    for m in range(nblk):
        gm = gate[m:m + 1, :] if axis == 0 else gate[:, m:m + 1]
        beats = (gm > gate) | ((gm == gate) & (m < idx))
        rank = rank + jnp.where(beats & (m < limit), 1.0, 0.0)
    return jnp.where((idx < limit) & (rank < MOBA_TOPK), 1.0, 0.0)


HEADS_PER_STEP = LANES // D_C


def _moba_prompt_kernel(q_ref, k_ref, v_ref, o_ref, *, nblk):
    blk = MOBA_BLOCK
    l = nblk * blk
    scale = D_C ** -0.5
    k = k_ref[...]
    q_t = q_ref[...].T
    v_t = v_ref[...].T.astype(BF16)
    kmean = jnp.concatenate([jnp.mean(k[n * blk:(n + 1) * blk], axis=0, keepdims=True) for n in range(nblk)],
                            axis=0)
    own = _iota2((1, l), 1) // blk
    kb = k.astype(BF16)
    causal = _iota2((blk, blk), 0) <= _iota2((blk, blk), 1)
    row_head = _iota2((LANES, 1), 0) // D_C
    outs = []
    for e in range(HEADS_PER_STEP):
        q_e = jnp.where(row_head == e, q_t, 0.0)
        sel = _rank_select(_mm32(kmean, q_e), nblk, own, 0)
        qb = (q_e * scale).astype(BF16)
        tiles = []
        for j in range(nblk):
            cols = slice(j * blk, (j + 1) * blk)
            nk = (j + 1) * blk
            s = jnp.dot(kb[:nk], qb[:, cols], preferred_element_type=F32)
            parts = [jnp.where(sel[n:n + 1, cols] > 0.0, s[n * blk:(n + 1) * blk], -jnp.inf) for n in range(j)]
            parts.append(jnp.where(causal, s[j * blk:], -jnp.inf))
            s = jnp.concatenate(parts, axis=0)
            m = jnp.max(s, axis=0, keepdims=True)
            p = jnp.exp(s - m)
            denom = jnp.sum(p, axis=0, keepdims=True)
            acc = jnp.dot(v_t[e * D_C:(e + 1) * D_C, :nk], p.astype(BF16), preferred_element_type=F32)
            tiles.append(acc / denom)
        outs.append(jnp.concatenate(tiles, axis=1))
    o_ref[...] = jnp.concatenate(outs, axis=0).T


def _moba_prompt(q3, k3, proj3):
    b, l, _ = q3.shape
    nblk = l // MOBA_BLOCK
    assert (4.0 ** round(np.log(D_C) / np.log(4.0))) == D_C
    kern = functools.partial(_moba_prompt_kernel, nblk=nblk)
    v_col = (C_QKV_C + 2 * W_BRANCH) // LANES
    cols = pl.BlockSpec((None, l, LANES), lambda i, hp: (i, 0, hp))
    return pl.pallas_call(
        kern,
        grid=(b, W_BRANCH // LANES),
        in_specs=[cols, cols, pl.BlockSpec((None, l, LANES), lambda i, hp: (i, 0, v_col + hp))],
        out_specs=cols,
        out_shape=jax.ShapeDtypeStruct((b, l, W_BRANCH), F32),
        compiler_params=pltpu.CompilerParams(dimension_semantics=("arbitrary", "arbitrary"),
                                             vmem_limit_bytes=VMEM_LIMIT),
    )(q3, k3, proj3)


def _moba_decode_kernel(pt_ref, q_ref, kn_ref, vn_ref, *refs, npage, page, lq):
    k_refs = refs[:npage]
    v_refs = refs[npage:2 * npage]
    o_ref = refs[2 * npage]
    del pt_ref
    scale = D_C ** -0.5
    nrow = lq * H_C
    ppb = MOBA_BLOCK // page
    nblk = npage // ppb
    lp = q_ref.shape[0]

    lane_head = _iota2((H_C, W_BRANCH), 1) // D_C
    own_head = (lane_head == _iota2((H_C, W_BRANCH), 0)).astype(F32)
    q = q_ref[...]
    qbd = jnp.concatenate([q[t:t + 1, :] * own_head for t in range(lq)], axis=0)
    qb = (qbd * scale).astype(BF16)

    k_t = jnp.concatenate([k_refs[i][...].reshape(W_BRANCH, page) for i in range(npage)], axis=1)
    v_t = jnp.concatenate([v_refs[i][...].reshape(W_BRANCH, page) for i in range(npage)], axis=1)
    s_past = jnp.dot(qb, k_t.astype(BF16), preferred_element_type=F32)

    lane = _iota2((1, LANES), 1)
    ksum = jnp.zeros((W_BRANCH, LANES), F32)
    for n in range(nblk):
        k_n = k_t[:, n * MOBA_BLOCK:(n + 1) * MOBA_BLOCK]
        folded = sum(k_n[:, i * LANES:(i + 1) * LANES] for i in range(MOBA_BLOCK // LANES))
        ksum = ksum + jnp.sum(folded, axis=-1, keepdims=True) * (lane == n).astype(F32)
    gate = _mm3(qbd, ksum) * (1.0 / MOBA_BLOCK)
    sel = _rank_select(gate, nblk, nblk, 1)
    s_past = jnp.concatenate(
        [jnp.where(sel[:, n:n + 1] > 0.0, s_past[:, n * MOBA_BLOCK:(n + 1) * MOBA_BLOCK], -jnp.inf)
         for n in range(nblk)], axis=1)

    pad = jnp.zeros((LANES - lp, W_BRANCH), F32)
    k_new = jnp.concatenate([kn_ref[...], pad], axis=0)
    v_new = jnp.concatenate([vn_ref[...], pad], axis=0)
    s_own = lax.dot_general(qb, k_new.astype(BF16), NT, preferred_element_type=F32)
    t_key = _iota2((nrow, LANES), 1)
    t_qry = _iota2((nrow, LANES), 0) // H_C
    s_own = jnp.where(t_key <= t_qry, s_own, -jnp.inf)

    m = jnp.maximum(jnp.max(s_past, axis=-1, keepdims=True), jnp.max(s_own, axis=-1, keepdims=True))
    p_past = jnp.exp(s_past - m)
    p_own = jnp.exp(s_own - m)
    denom = jnp.sum(p_past, axis=-1, keepdims=True) + jnp.sum(p_own, axis=-1, keepdims=True)
    acc = lax.dot_general(p_past.astype(BF16), v_t.astype(BF16), NT, preferred_element_type=F32)
    acc = acc + jnp.dot(p_own.astype(BF16), v_new.astype(BF16), preferred_element_type=F32)
    out = acc / denom
    own_rows = jnp.concatenate([own_head] * lq, axis=0)
    out = (out * own_rows).reshape(lq, H_C, W_BRANCH).sum(axis=1)
    o_ref[...] = jnp.concatenate([out, jnp.zeros((lp - lq, W_BRANCH), F32)], axis=0)


def _moba_decode(page_table, q3, k3, v3, cache_k, cache_v, layer, lq):
    b, lp, _ = q3.shape
    npage = page_table.shape[1]
    page = cache_k.shape[-1]
    kern = functools.partial(_moba_decode_kernel, npage=npage, page=page, lq=lq)
    tok = pl.BlockSpec((None, lp, W_BRANCH), lambda i, pt: (i, 0, 0))

    def page_spec(p_i):
        return pl.BlockSpec((None, None, H_C, D_C, page), lambda i, pt: (layer, pt[i, p_i], 0, 0, 0))

    grid_spec = pltpu.PrefetchScalarGridSpec(
        num_scalar_prefetch=1,
        grid=(b,),
        in_specs=[tok, tok, tok] + [page_spec(p_i) for p_i in range(npage)] * 2,
        out_specs=pl.BlockSpec((None, lp, W_BRANCH), lambda i, pt: (i, 0, 0)),
    )
    return pl.pallas_call(
        kern,
        grid_spec=grid_spec,
        out_shape=jax.ShapeDtypeStruct((b, lp, W_BRANCH), F32),
        compiler_params=pltpu.CompilerParams(dimension_semantics=("arbitrary",), vmem_limit_bytes=VMEM_LIMIT),
    )(page_table, q3, k3, v3, *([cache_k] * npage), *([cache_v] * npage))


def _merge_kernel(oa_ref, ob_ref, oc_ref, zc_ref, gate_ref, x_ref, ple_ref, wb_ref, bg_ref, wo_ref,
                  wpg_ref, bpg_ref, wple_ref, y_ref):
    branches = (oa_ref[...], ob_ref[...], oc_ref[...] * _silu(zc_ref[...]))
    up = None
    for n, o in enumerate(branches):
        sl = slice(n * D_MODEL, (n + 1) * D_MODEL)
        gate = _sigmoid(gate_ref[:, sl] + bg_ref[:, sl])
        term = gate * jnp.dot(o.astype(BF16), wb_ref[n], preferred_element_type=F32)
        up = term if up is None else up + term
    y = x_ref[...] + jnp.dot(up.astype(BF16), wo_ref[...], preferred_element_type=F32)
    pg = _sigmoid(jnp.dot(y.astype(BF16), wpg_ref[...], preferred_element_type=F32) + bpg_ref[...])
    y_ref[...] = y + pg * jnp.dot(ple_ref[...].astype(BF16), wple_ref[...], preferred_element_type=F32)


def _merge(oa, ob, oc, proj2, x2, ple3, ple_layer, wb, bg, wo, wpg, bpg, wple, tm):
    t = x2.shape[0]
    full = lambda shape: pl.BlockSpec(shape, lambda i: (0,) * len(shape))
    row = lambda w: pl.BlockSpec((tm, w), lambda i: (i, 0))
    return pl.pallas_call(
        _merge_kernel,
        grid=(t // tm,),
        in_specs=[row(W_BRANCH), row(W_BRANCH), row(W_BRANCH),
                  pl.BlockSpec((tm, W_BRANCH), lambda i: (i, C_Z_C // W_BRANCH)),
                  pl.BlockSpec((tm, N_BRANCH * D_MODEL), lambda i: (i, C_GATE // (N_BRANCH * D_MODEL))),
                  row(D_MODEL), pl.BlockSpec((None, tm, D_PLE), lambda i: (ple_layer, i, 0)),
                  full((N_BRANCH, W_BRANCH, D_MODEL)), full((1, N_BRANCH * D_MODEL)), full((D_MODEL, D_MODEL)),
                  full((D_MODEL, D_MODEL)), full((1, D_MODEL)), full((D_PLE, D_MODEL))],
        out_specs=row(D_MODEL),
        out_shape=jax.ShapeDtypeStruct((t, D_MODEL), F32),
        compiler_params=pltpu.CompilerParams(dimension_semantics=("arbitrary",), vmem_limit_bytes=VMEM_LIMIT),
    )(oa, ob, oc, proj2, proj2, x2, ple3, wb, bg, wo, wpg, bpg, wple)


def _arrange_w_in(w):
    offs = np.cumsum((0,) + IN_SPLITS)
    qkv_a, z_a, beta, alpha, f_b, z_b, qkv_c, z_c, gate = [w[:, offs[i]:offs[i + 1]] for i in range(9)]
    d = w.shape[0]
    ba = jnp.concatenate([beta, alpha, jnp.zeros((d, LANES - 2 * H_A), w.dtype)], axis=1)
    tail = jnp.zeros((d, N_PROJ - C_BA - LANES), w.dtype)
    out = jnp.concatenate([qkv_a, z_a, f_b[:, :3 * W_BRANCH], z_b, qkv_c, z_c, gate,
                           f_b[:, 3 * W_BRANCH:], ba, tail], axis=1)
    return out.astype(BF16)


def _layer_params(i, p):
    row = lambda v: v.reshape(1, -1)
    lanes_ba = lambda v: jnp.zeros((1, LANES), F32).at[0, H_A:2 * H_A].set(v)
    lora_pad = jnp.zeros((LORA, W_BRANCH), F32)
    head = np.arange(W_BRANCH) // N_B
    half = ROPE_DIM // 2
    inv = ROPE_THETA ** (-jnp.arange(half, dtype=F32) / half)
    d = np.arange(LANES) % D_C
    inv_lanes = jnp.where(d < ROPE_DIM, inv[d % half], 0.0).reshape(1, LANES)
    return dict(
        norm_g=row(p['norm_g'][i]),
        w_in=_arrange_w_in(p['w_in'][i]),
        conv=p['conv_a'][i],
        pa=lanes_ba(p['a_log'][i]), pb=lanes_ba(p['dt_bias'][i]), onorm=row(p['onorm_a'][i]),
        rw_vecs=(row(p['mu_b'][i]), row(p['w0_b'][i]), row(p['a0_b'][i]), row(p['kk_b'][i]), row(p['ka_b'][i]),
                 row(p['rk_b'][i]), row(p['gn_g_b'][i]), row(p['gn_b_b'][i])),
        w2p=jnp.concatenate([p['w2_b'][i], lora_pad], axis=0).astype(BF16),
        a2p=jnp.concatenate([lora_pad, p['a2_b'][i]], axis=0).astype(BF16),
        bd=jnp.asarray(head[:, None] == head[None, :], BF16),
        qn=row(jnp.tile(p['qn_c'][i], H_C)), kn=row(jnp.tile(p['kn_c'][i], H_C)), inv=inv_lanes,
        wb=p['w_branch'][i].astype(BF16), bg=row(p['b_gate'][i]), wo=p['w_out'][i].astype(BF16),
        wpg=p['w_pg'][i].astype(BF16), bpg=row(p['b_pg'][i]), wple=p['w_ple'][i].astype(BF16),
    )


def _hist_rows(rows):
    b, n, c = rows.shape
    return jnp.concatenate([jnp.zeros((b, HIST - n, c), F32), rows], axis=1)


def _layer(x3, ple3, ple_layer, lp, *, l_valid, pos0, dn_s, rw_s, state_layer, dn_buf, rw_prev, nseg, lt, ls, tm, attend):
    b, l, d = x3.shape
    t = b * l
    proj2 = _proj(x3.reshape(t, d), lp['norm_g'], lp['w_in'], min(t, 1024), 1920)
    proj3 = proj2.reshape(b, l, N_PROJ)

    oa, dn_new = _gdn(proj3, _hist_rows(dn_buf), dn_s, state_layer, lp['conv'], lp['pa'], lp['pb'], lp['onorm'],
                      nseg=nseg, lt=lt, ls=ls, l_valid=l_valid)
    ob, rw_new = _rwkv(proj3, _hist_rows(rw_prev[:, None, :]), rw_s, state_layer, lp['rw_vecs'],
                       lp['w2p'], lp['a2p'], lp['bd'], nseg=nseg, lt=lt, ls=ls, l_valid=l_valid)
    qr, kr = _qkprep(proj2, lp['qn'], lp['kn'], lp['inv'], lp['bd'], tm=min(tm, t), lseq=l, pos0=pos0)
    v2 = proj2[:, C_QKV_C + 2 * W_BRANCH:C_QKV_C + 3 * W_BRANCH]
    oc = attend(qr, kr, v2, proj3)

    y2 = _merge(oa.reshape(t, W_BRANCH), ob.reshape(t, W_BRANCH), oc.reshape(t, W_BRANCH), proj2,
                x3.reshape(t, d), ple3, ple_layer, lp['wb'], lp['bg'], lp['wo'], lp['wpg'],
                lp['bpg'], lp['wple'], min(tm, t))

    f_last = jnp.concatenate([proj3[:, l_valid - 1, C_RKV:C_RKV + 3 * W_BRANCH],
                              proj3[:, l_valid - 1, C_WLAL:C_WLAL + 2 * LORA]], axis=-1)
    new_state = (kr.reshape(b, l, H_C, D_C)[:, :l_valid], v2.reshape(b, l, H_C, D_C)[:, :l_valid], dn_new,
                 proj3[:, l_valid - (CONV_W - 1):l_valid, C_QKV_A:C_QKV_A + QKV_A],
                 rw_new, f_last)
    return y2.reshape(b, l, d), new_state


def _attend_prompt(b, l):
    def attend(qr, kr, v2, proj3):
        shp = (b, l, W_BRANCH)
        return _moba_prompt(qr.reshape(shp), kr.reshape(shp), proj3).reshape(b * l, W_BRANCH)
    return attend


def _attend_decode(b, lp_, lq, page_table, cache_k4, cache_v4, layer):
    def attend(qr, kr, v2, proj3):
        shp = (b, lp_, W_BRANCH)
        return _moba_decode(page_table, qr.reshape(shp), kr.reshape(shp), v2.reshape(shp),
                            cache_k4, cache_v4, layer, lq)
    return attend


DEC_PAD = 8


def kernel(x_prompt, x_sample, cache_k, cache_v, state_dn, state_dn_conv, state_rwkv, state_rwkv_shift,
           page_table, p_prompt, p_sample, norm_g, w_in, conv_a, a_log, dt_bias, onorm_a, mu_b, w0_b, w2_b,
           a0_b, a2_b, kk_b, ka_b, rk_b, gn_g_b, gn_b_b, qn_c, kn_c, w_branch, b_gate, w_out, w_ple, w_pg,
           b_pg):
    params = dict(norm_g=norm_g, w_in=w_in, conv_a=conv_a, a_log=a_log, dt_bias=dt_bias, onorm_a=onorm_a,
                  mu_b=mu_b, w0_b=w0_b, w2_b=w2_b, a0_b=a0_b, a2_b=a2_b, kk_b=kk_b, ka_b=ka_b, rk_b=rk_b,
                  gn_g_b=gn_g_b, gn_b_b=gn_b_b, qn_c=qn_c, kn_c=kn_c, w_branch=w_branch, b_gate=b_gate,
                  w_out=w_out, w_ple=w_ple, w_pg=w_pg, b_pg=b_pg)
    depth = norm_g.shape[0]
    b_p, l_p, _ = x_prompt.shape
    b_s, l_s, _ = x_sample.shape
    n_phys, page = cache_k.shape[1], cache_k.shape[2]
    past_len = page_table.shape[1] * page
    assert l_p % 512 == 0 and CONV_W - 1 <= l_s <= DEC_PAD and b_s % (CHUNK // DEC_PAD) == 0
    assert past_len % MOBA_BLOCK == 0 and MOBA_BLOCK % page == 0

    pad_rows = lambda a: jnp.pad(a, ((0, 0), (0, DEC_PAD - l_s), (0, 0)))
    xs = pad_rows(x_sample)
    cache_k4 = cache_k.transpose(0, 1, 3, 4, 2)
    cache_v4 = cache_v.transpose(0, 1, 3, 4, 2)
    xp = x_prompt
    new_p, new_s = [], []
    for i in range(depth):
        lp = _layer_params(i, params)
        xp, st_p = _layer(xp, p_prompt.reshape(depth, b_p * l_p, D_PLE), i, lp, l_valid=l_p, pos0=0,
                          dn_s=jnp.zeros((1, b_p, H_A, DK_A, DV_A), F32),
                          rw_s=jnp.zeros((1, b_p, H_B, N_B, N_B), F32), state_layer=0,
                          dn_buf=jnp.zeros((b_p, CONV_W - 1, QKV_A), F32), rw_prev=jnp.zeros((b_p, SHIFT_B), F32),
                          nseg=1, lt=512, ls=CHUNK, tm=512, attend=_attend_prompt(b_p, l_p))
        new_p.append(st_p)
        xs, st_s = _layer(xs, pad_rows(p_sample[i]).reshape(1, b_s * DEC_PAD, D_PLE), 0, lp, l_valid=l_s, pos0=past_len,
                          dn_s=state_dn, rw_s=state_rwkv, state_layer=i, dn_buf=state_dn_conv[i],
                          rw_prev=state_rwkv_shift[i], nseg=CHUNK // DEC_PAD, lt=DEC_PAD, ls=DEC_PAD, tm=512,
                          attend=_attend_decode(b_s, DEC_PAD, l_s, page_table, cache_k4, cache_v4, i))
        new_s.append(st_s)
    outs_p = [jnp.stack(t) for t in zip(*new_p)]
    outs_s = [jnp.stack(t) for t in zip(*new_s)]
    return (xp, xs[:, :l_s], *outs_p, *outs_s)
```

```python
import functools

import numpy as np
import jax
import jax.numpy as jnp
from jax import lax
from jax.experimental import pallas as pl
from jax.experimental.pallas import tpu as pltpu

F32 = jnp.float32
BF16 = jnp.bfloat16

D_MODEL = 1024
D_PLE = 256
N_BRANCH = 3
W_BRANCH = 512
NORM_EPS = 1e-6
DK_A = 128
DV_A = 128
H_A = 4
QKV_A = 1536
CONV_W = 4
N_B = 64
H_B = 8
LORA = 64
SHIFT_B = 3 * W_BRANCH + 2 * LORA
GN_EPS = 64e-5
D_C = 64
H_C = 8
MOBA_BLOCK = 256
MOBA_TOPK = 3
ROPE_DIM = 16
ROPE_THETA = 500000.0
IN_SPLITS = (QKV_A, W_BRANCH, H_A, H_A, SHIFT_B, W_BRANCH, 3 * W_BRANCH, W_BRANCH, N_BRANCH * D_MODEL)

C_QKV_A = 0
C_Z_A = 1536
C_RKV = 2048
C_Z_B = 3584
C_QKV_C = 4096
C_Z_C = 5632
C_GATE = 6144
C_WLAL = 9216
C_BA = 9344
N_PROJ = 9600

CHUNK = 64
HIST = 8
CHUNKS_PER_TRIP_A = 8
CHUNKS_PER_TRIP_B = 8
LANES = 128
VMEM_LIMIT = 56 * 1024 * 1024

NN = (((1,), (0,)), ((), ()))
BNN = (((2,), (1,)), ((0,), (0,)))
NT = (((1,), (1,)), ((), ()))
TN = (((0,), (0,)), ((), ()))
HIGHEST = lax.Precision.HIGHEST


def _mm(a, b, dims=NN):
    return lax.dot_general(a.astype(BF16), b.astype(BF16), dims, preferred_element_type=F32)


def _mm32(a, b, dims=NN):
    return lax.dot_general(a, b, dims, precision=HIGHEST, preferred_element_type=F32)


def _sigmoid(x):
    return 1.0 / (1.0 + jnp.exp(-x))


def _silu(x):
    return x * _sigmoid(x)


def _softplus(x):
    return jnp.maximum(x, 0.0) + jnp.log1p(jnp.exp(-jnp.abs(x)))


def _iota2(shape, dim):
    return lax.broadcasted_iota(jnp.int32, shape, dim)


def _split(x):
    hi = x.astype(BF16)
    return hi, (x - hi.astype(F32)).astype(BF16)


def _dotb(a, b, dims=NN):
    return lax.dot_general(a, b, dims, preferred_element_type=F32)


def _mm_sel(sel, x):
    hi, lo = _split(x)
    return _dotb(sel, hi) + _dotb(sel, lo)


def _mm3(a, b, dims=NN):
    ah, al = _split(a)
    bh, bl = _split(b)
    return _dotb(ah, bh, dims) + (_dotb(ah, bl, dims) + _dotb(al, bh, dims))


SOLVE_MM_A = _mm
SOLVE_MM_B = _mm


def _inv_unit_lower(lmats, nilpotent, mm):
    n = lmats[0].shape[-1]
    dims = NN if lmats[0].ndim == 2 else BNN
    eye = (_iota2((n, n), 0) == _iota2((n, n), 1)).astype(F32)
    ps = [-lmat for lmat in lmats]
    accs = [eye + p for p in ps]
    span = 2
    while span < nilpotent:
        ps = [mm(p, p, dims) for p in ps]
        accs = [acc + mm(acc, p, dims) for acc, p in zip(accs, ps)]
        span *= 2
    return accs


def _head_sums(xs, bd):
    n = xs[0].shape[0]
    parts = []
    for x in xs:
        parts.extend(_split(x))
    out = jnp.dot(jnp.concatenate(parts, axis=0), bd, preferred_element_type=F32)
    return [out[2 * i * n:(2 * i + 1) * n] + out[(2 * i + 1) * n:(2 * i + 2) * n] for i in range(len(xs))]


def _head_sum(x, bd):
    return _head_sums([x], bd)[0]


def _proj_kernel(x_ref, g_ref, w_ref, o_ref):
    x = x_ref[...]
    h = x * lax.rsqrt(jnp.mean(x * x, axis=-1, keepdims=True) + NORM_EPS) * g_ref[...]
    o_ref[...] = jnp.dot(h.astype(BF16), w_ref[...], preferred_element_type=F32)


def _proj(x2d, g, w, tm, tn):
    t, d = x2d.shape
    n = w.shape[1]
    return pl.pallas_call(
        _proj_kernel,
        grid=(n // tn, t // tm),
        in_specs=[pl.BlockSpec((tm, d), lambda j, i: (i, 0)),
                  pl.BlockSpec((1, d), lambda j, i: (0, 0)),
                  pl.BlockSpec((d, tn), lambda j, i: (0, j))],
        out_specs=pl.BlockSpec((tm, tn), lambda j, i: (i, j)),
        out_shape=jax.ShapeDtypeStruct((t, n), F32),
        compiler_params=pltpu.CompilerParams(dimension_semantics=("arbitrary", "arbitrary"),
                                             vmem_limit_bytes=VMEM_LIMIT),
    )(x2d, g, w)


def _chunk_masks(n, ls):
    r = _iota2((n, n), 0)
    c = _iota2((n, n), 1)
    same = (r // ls) == (c // ls)
    return same, same & (r >= c), same & (r > c)


def _valid_rows(l_idx, lt, r0, ls, l_valid):
    t = l_idx * lt + r0 + (_iota2((CHUNK, 1), 0) % ls)
    return t < l_valid


def _gdn_kernel(qkv_ref, z_ref, ba_ref, buf_ref, s0_ref, cw_ref, pa_ref, pb_ref, on_ref,
                o_ref, sout_ref, xs_ref, st_ref, *, nseg, lt, ls, l_valid, l_total):
    l_idx = pl.program_id(1)
    nrow = H_A * CHUNK

    @pl.when(l_idx == 0)
    def _():
        xs_ref[:, 0:HIST, :] = buf_ref[...]
        st_ref[...] = s0_ref[...]

    @pl.when(l_idx > 0)
    def _():
        xs_ref[:, 0:HIST, :] = xs_ref[:, lt:lt + HIST, :]

    xs_ref[:, HIST:lt + HIST, :] = qkv_ref[...]

    same_r, c_incl, c_strict = _chunk_masks(CHUNK, ls)
    tri_r = c_incl.astype(BF16)
    same_r = same_r.astype(BF16)
    eye_c = (_iota2((CHUNK, CHUNK), 0) == _iota2((CHUNK, CHUNK), 1)).astype(F32)
    cw = cw_ref[...]
    neg_a = -jnp.exp(pa_ref[...])
    pb = pb_ref[...]
    onorm = on_ref[...]

    def conv(s, r0, col):
        win = xs_ref[s, pl.ds(r0, ls + HIST), col:col + LANES]
        w = cw[:, col:col + LANES]
        acc = pltpu.roll(win, 3, 0)[HIST:] * w[0:1] + pltpu.roll(win, 2, 0)[HIST:] * w[1:2]
        acc = acc + pltpu.roll(win, 1, 0)[HIST:] * w[2:3]
        acc = acc + win[HIST:] * w[3:4]
        return _silu(acc)

    def part(r0, col):
        return jnp.concatenate([conv(s, r0, col) for s in range(nseg)], axis=0)

    def local(c):
        r0 = pl.multiple_of(c * ls, ls)
        ba = jnp.concatenate([ba_ref[s, pl.ds(r0, ls), :] for s in range(nseg)], axis=0)
        beta_all = _sigmoid(ba)
        g_all = neg_a * _softplus(ba + pb)
        if l_valid < l_total:
            ok = _valid_rows(l_idx, lt, r0, ls, l_valid)
            beta_all = jnp.where(ok, beta_all, 0.0)
            g_all = jnp.where(ok, g_all, 0.0)
        gcum = _mm_sel(tri_r, g_all)
        if nseg == 1:
            gtot = jnp.broadcast_to(gcum[CHUNK - 1:CHUNK], gcum.shape)
        else:
            gtot = _mm_sel(same_r, g_all)

        qs, ks, vs, betas, gcs, gts, lmats, qks = [], [], [], [], [], [], [], []
        for h in range(H_A):
            q = part(r0, h * DK_A)
            k = part(r0, H_A * DK_A + h * DK_A)
            v = part(r0, 2 * H_A * DK_A + h * DV_A)
            q = q * lax.rsqrt(jnp.sum(q * q, axis=-1, keepdims=True) + 1e-6) * (DK_A ** -0.5)
            k = k * lax.rsqrt(jnp.sum(k * k, axis=-1, keepdims=True) + 1e-6)
            beta_h = beta_all[:, h:h + 1]
            g_h = gcum[:, H_A + h:H_A + h + 1]
            g_row = jnp.sum(eye_c * g_h, axis=0, keepdims=True)
            decay = jnp.exp(jnp.where(c_incl, g_h - g_row, -jnp.inf))
            kq_k = _mm(jnp.concatenate([k, q], axis=0), k, NT)
            lmats.append(jnp.where(c_strict, beta_h * kq_k[:CHUNK] * decay, 0.0))
            qks.append(jnp.where(c_incl, kq_k[CHUNK:] * decay, 0.0))
            qs.append(q)
            ks.append(k)
            vs.append(v)
            betas.append(beta_h)
            gcs.append(g_h)
            gts.append(gtot[:, H_A + h:H_A + h + 1])
        q_st = jnp.concatenate(qs, axis=0)
        k_st = jnp.concatenate(ks, axis=0)
        v_st = jnp.concatenate(vs, axis=0)
        beta = jnp.concatenate(betas, axis=0)
        g_col = jnp.concatenate(gcs, axis=0)
        gt_col = jnp.concatenate(gts, axis=0)
        return dict(
            r0=r0, gt_col=gt_col,
            lmat=jnp.stack(lmats),
            rhs=jnp.concatenate([v_st * beta, k_st * (beta * jnp.exp(g_col))], axis=1),
            qk=jnp.stack(qks),
            qg=q_st * jnp.exp(g_col),
            kd=k_st * jnp.exp(gt_col - g_col))

    def advance(loc, sol):
        r0, gt_col, qk, qg, kd = loc['r0'], loc['gt_col'], loc['qk'], loc['qg'], loc['kd']
        bv = sol[:, :DV_A]
        wk = sol[:, DV_A:]
        wks, qgs = [], []
        for h in range(H_A):
            for s in range(nseg):
                a = h * CHUNK + s * ls
                both = _mm(jnp.concatenate([wk[a:a + ls], qg[a:a + ls]], axis=0), st_ref[s, h])
                wks.append(both[:ls])
                qgs.append(both[ls:])
        u = bv - jnp.concatenate(wks, axis=0)
        o = jnp.concatenate(qgs, axis=0) + _mm(qk, u.reshape(H_A, CHUNK, DV_A), BNN).reshape(nrow, DV_A)

        for h in range(H_A):
            a = h * CHUNK
            for s in range(nseg):
                b = a + s * ls
                if nseg == 1:
                    kd_s = kd[a:a + CHUNK]
                else:
                    rows = _iota2((CHUNK, 1), 0) // ls
                    kd_s = jnp.where(rows == s, kd[a:a + CHUNK], 0.0)
                upd = _mm(kd_s, u[a:a + CHUNK], TN)
                st_ref[s, h] = st_ref[s, h] * jnp.exp(gt_col[b:b + 1]) + upd
            oh = o[a:a + CHUNK]
            oh = oh * lax.rsqrt(jnp.mean(oh * oh, axis=-1, keepdims=True) + NORM_EPS) * onorm
            for s in range(nseg):
                zz = z_ref[s, pl.ds(r0, ls), h * DV_A:(h + 1) * DV_A]
                o_ref[s, pl.ds(r0, ls), h * DV_A:(h + 1) * DV_A] = oh[s * ls:(s + 1) * ls] * _silu(zz)

    nchunk = lt // ls
    per_trip = min(CHUNKS_PER_TRIP_A, nchunk)

    def trip(i, carry):
        locs = [local(i * per_trip + j) for j in range(per_trip)]
        ainvs = _inv_unit_lower([loc['lmat'] for loc in locs], ls, SOLVE_MM_A)
        sols = [SOLVE_MM_A(ainv, loc['rhs'].reshape(H_A, CHUNK, DK_A + DV_A), BNN).reshape(nrow, DK_A + DV_A)
                for loc, ainv in zip(locs, ainvs)]
        for loc, sol in zip(locs, sols):
            advance(loc, sol)
        return carry

    lax.fori_loop(0, nchunk // per_trip, trip, 0)

    @pl.when(l_idx == pl.num_programs(1) - 1)
    def _():
        sout_ref[...] = st_ref[...]


def _gdn(proj3, buf8, s0, layer, cw, pa, pb, onorm, *, nseg, lt, ls, l_valid):
    b, l, _ = proj3.shape
    kern = functools.partial(_gdn_kernel, nseg=nseg, lt=lt, ls=ls, l_valid=l_valid, l_total=l)
    full = lambda shape: pl.BlockSpec(shape, lambda i, j: (0,) * len(shape))
    return pl.pallas_call(
        kern,
        grid=(b // nseg, l // lt),
        in_specs=[pl.BlockSpec((nseg, lt, QKV_A), lambda i, j: (i, j, C_QKV_A // QKV_A)),
                  pl.BlockSpec((nseg, lt, W_BRANCH), lambda i, j: (i, j, C_Z_A // W_BRANCH)),
                  pl.BlockSpec((nseg, lt, LANES), lambda i, j: (i, j, C_BA // LANES)),
                  pl.BlockSpec((nseg, HIST, QKV_A), lambda i, j: (i, 0, 0)),
                  pl.BlockSpec((None, nseg, H_A, DK_A, DV_A), lambda i, j: (layer, i, 0, 0, 0)),
                  full((CONV_W, QKV_A)), full((1, LANES)), full((1, LANES)), full((1, DV_A))],
        out_specs=[pl.BlockSpec((nseg, lt, W_BRANCH), lambda i, j: (i, j, 0)),
                   pl.BlockSpec((nseg, H_A, DK_A, DV_A), lambda i, j: (i, 0, 0, 0))],
        out_shape=[jax.ShapeDtypeStruct((b, l, W_BRANCH), F32),
                   jax.ShapeDtypeStruct((b, H_A, DK_A, DV_A), F32)],
        scratch_shapes=[pltpu.VMEM((nseg, lt + HIST, QKV_A), F32),
                        pltpu.VMEM((nseg, H_A, DK_A, DV_A), F32)],
        compiler_params=pltpu.CompilerParams(dimension_semantics=("arbitrary", "arbitrary"),
                                             vmem_limit_bytes=VMEM_LIMIT),
    )(proj3, proj3, proj3, buf8, s0, cw, pa, pb, onorm)


GRP = 4 * N_B


def _rwkv_kernel(r_ref, k_ref, v_ref, wa_ref, z_ref, prev_ref, s0_ref, mu_ref, w0_ref, w2_ref,
                 a0_ref, a2_ref, kkp_ref, kap_ref, rk_ref, gg_ref, gb_ref, bd_ref,
                 o_ref, sout_ref, fs_ref, st_ref, *, nseg, lt, ls, l_valid, l_total):
    l_idx = pl.program_id(1)
    nrow = 4 * CHUNK
    ngrp = W_BRANCH // GRP

    @pl.when(l_idx == 0)
    def _():
        fs_ref[:, 0:HIST, :] = prev_ref[...]
        st_ref[...] = jnp.zeros(st_ref.shape, F32)
        for s in range(nseg):
            for h in range(H_B):
                g, j = divmod(h, 4)
                st_ref[s, g, j * N_B:(j + 1) * N_B, j * N_B:(j + 1) * N_B] = s0_ref[s, h]

    @pl.when(l_idx > 0)
    def _():
        fs_ref[:, 0:HIST, :] = fs_ref[:, lt:lt + HIST, :]

    fs_ref[:, HIST:lt + HIST, 0:W_BRANCH] = r_ref[...]
    fs_ref[:, HIST:lt + HIST, W_BRANCH:2 * W_BRANCH] = k_ref[...]
    fs_ref[:, HIST:lt + HIST, 2 * W_BRANCH:3 * W_BRANCH] = v_ref[...]
    fs_ref[:, HIST:lt + HIST, 3 * W_BRANCH:SHIFT_B] = wa_ref[...]

    same_r, tri_r, _ = _chunk_masks(CHUNK, ls)
    tri_r = tri_r.astype(BF16)
    same_r = same_r.astype(BF16)
    lane_head = _iota2((1, GRP), 1) // N_B
    head_masks = [(lane_head == j).astype(F32) for j in range(4)]
    bd_state = (_iota2((GRP, GRP), 0) // N_B) == (_iota2((GRP, GRP), 1) // N_B)
    head_rows = ((_iota2((nrow, GRP), 0) // CHUNK) == (_iota2((nrow, GRP), 1) // N_B)).astype(F32)
    t_row = _iota2((nrow, CHUNK), 0) % CHUNK
    t_col = _iota2((nrow, CHUNK), 1)
    t_same = (t_row // ls) == (t_col // ls)
    t_incl = t_same & (t_col <= t_row)
    t_strict = t_same & (t_col < t_row)
    mu = mu_ref[...]
    bd = bd_ref[...]

    def mixed(r0, col, width):
        outs = []
        for s in range(nseg):
            win = fs_ref[s, pl.ds(r0, ls + HIST), col:col + width]
            cur = win[HIST:]
            prev = pltpu.roll(win, 1, 0)[HIST:]
            outs.append(cur + (prev - cur) * mu[:, col:col + width])
        return jnp.concatenate(outs, axis=0)

    def stack_masked(x):
        return jnp.concatenate([x * head_masks[j] for j in range(4)], axis=0)

    def unstack(x):
        return x[0:CHUNK] + x[CHUNK:2 * CHUNK] + x[2 * CHUNK:3 * CHUNK] + x[3 * CHUNK:4 * CHUNK]

    def local(c):
        r0 = pl.multiple_of(c * ls, ls)
        r = mixed(r0, 0, W_BRANCH)
        k = mixed(r0, W_BRANCH, W_BRANCH)
        v = mixed(r0, 2 * W_BRANCH, W_BRANCH)
        wa = mixed(r0, 3 * W_BRANCH, 2 * LORA)
        w_raw = w0_ref[...] + _mm(jnp.tanh(wa), w2_ref[...])
        logw = -jnp.exp(-_softplus(-w_raw) - 0.5)
        a_sig = _sigmoid(a0_ref[...] + _mm(wa, a2_ref[...]))
        kk = k * kkp_ref[...]
        k_mod = k * (1.0 + (a_sig - 1.0) * kap_ref[...])
        kk_ss, rk_sum = _head_sums([kk * kk, r * k_mod * rk_ref[...]], bd)
        kk = kk * lax.rsqrt(kk_ss + 1e-6)
        b_vec = kk * a_sig
        if l_valid < l_total:
            ok = _valid_rows(l_idx, lt, r0, ls, l_valid)
            logw = jnp.where(ok, logw, 0.0)
            b_vec = jnp.where(ok, b_vec, 0.0)
            k_mod = jnp.where(ok, k_mod, 0.0)
        gcum = _mm_sel(tri_r, logw)
        if nseg == 1:
            gtot = jnp.broadcast_to(gcum[CHUNK - 1:CHUNK], gcum.shape)
        else:
            gtot = _mm_sel(same_r, logw)
        e_neg = jnp.exp(-gcum)
        e_end = jnp.exp(gtot - gcum)
        a_t = -kk * jnp.exp(gcum - logw)
        b_t = b_vec * e_neg
        k_t = k_mod * e_neg
        r_t = r * jnp.exp(gcum)
        b_h = b_vec * e_end
        k_h = k_mod * e_end

        groups = []
        for g in range(ngrp):
            sl = slice(g * GRP, (g + 1) * GRP)
            a_g, b_g, k_g, r_g, v_g = a_t[:, sl], b_t[:, sl], k_t[:, sl], r_t[:, sl], v[:, sl]
            a_st = stack_masked(a_g)
            r_st = stack_masked(r_g)
            ar_k = _mm(jnp.concatenate([a_st, r_st], axis=0), k_g, NT)
            aak = jnp.where(t_strict, ar_k[:nrow], 0.0)
            ark = jnp.where(t_incl, ar_k[nrow:], 0.0)
            kv = _mm(jnp.concatenate([aak, ark], axis=0), v_g)
            groups.append(dict(
                a_g=a_g, r_g=r_g, v_g=v_g, b_h=b_h[:, sl], k_h=k_h[:, sl], gtot=gtot[:, sl],
                neg_aab=-jnp.where(t_strict, _mm(a_st, b_g, NT), 0.0).reshape(4, CHUNK, CHUNK),
                kv_u=kv[:nrow] * head_rows, kv_y=kv[nrow:],
                arb=jnp.where(t_incl, _mm(r_st, b_g, NT), 0.0)))
        return dict(r0=r0, v=v, rk_sum=rk_sum, groups=groups)

    def advance(loc, tinvs):
        grps = loc['groups']
        ams, rms = [], []
        for g, grp in enumerate(grps):
            parts = [_mm(jnp.concatenate([grp['a_g'][s * ls:(s + 1) * ls], grp['r_g'][s * ls:(s + 1) * ls]], axis=0),
                         st_ref[s, g], NT) for s in range(nseg)]
            ams.append(jnp.concatenate([p[:ls] for p in parts], axis=0))
            rms.append(jnp.concatenate([p[ls:] for p in parts], axis=0))
        us = [unstack(SOLVE_MM_B(tinv, (stack_masked(am) + grp['kv_u']).reshape(4, CHUNK, GRP), BNN)
                      .reshape(nrow, GRP)) for grp, tinv, am in zip(grps, tinvs, ams)]
        ys = [rm + unstack((_mm(grp['arb'], u) + grp['kv_y']) * head_rows) for grp, u, rm in zip(grps, us, rms)]
        for g, (grp, u) in enumerate(zip(grps, us)):
            v_g = grp['v_g']
            for s in range(nseg):
                if nseg == 1:
                    u_s, v_s = u, v_g
                else:
                    rows = _iota2((CHUNK, 1), 0) // ls
                    u_s = jnp.where(rows == s, u, 0.0)
                    v_s = jnp.where(rows == s, v_g, 0.0)
                upd = _mm(jnp.concatenate([u_s, v_s], axis=0),
                          jnp.concatenate([grp['b_h'], grp['k_h']], axis=0), TN)
                dec = jnp.exp(grp['gtot'][s * ls:s * ls + 1])
                st_ref[s, g] = st_ref[s, g] * dec + jnp.where(bd_state, upd, 0.0)
        y = jnp.concatenate(ys, axis=1)
        mean = _head_sums([y], bd)[0] * (1.0 / N_B)
        yc = y - mean
        var = _head_sums([yc * yc], bd)[0] * (1.0 / N_B)
        yn = yc * lax.rsqrt(var + GN_EPS) * gg_ref[...] + gb_ref[...]
        out = yn + loc['rk_sum'] * loc['v']
        for s in range(nseg):
            zz = z_ref[s, pl.ds(loc['r0'], ls), :]
            o_ref[s, pl.ds(loc['r0'], ls), :] = out[s * ls:(s + 1) * ls] * _silu(zz)

    nchunk = lt // ls
    per_trip = min(CHUNKS_PER_TRIP_B, nchunk)

    def trip(i, carry):
        locs = [local(i * per_trip + j) for j in range(per_trip)]
        tinvs = _inv_unit_lower([grp['neg_aab'] for loc in locs for grp in loc['groups']], ls, SOLVE_MM_B)
        for j, loc in enumerate(locs):
            advance(loc, tinvs[j * ngrp:(j + 1) * ngrp])
        return carry

    lax.fori_loop(0, nchunk // per_trip, trip, 0)

    @pl.when(l_idx == pl.num_programs(1) - 1)
    def _():
        for s in range(nseg):
            for h in range(H_B):
                g, j = divmod(h, 4)
                sout_ref[s, h] = st_ref[s, g, j * N_B:(j + 1) * N_B, j * N_B:(j + 1) * N_B]


def _rwkv(proj3, prev8, s0, layer, vecs, w2p, a2p, bd, *, nseg, lt, ls, l_valid):
    b, l, _ = proj3.shape
    ngrp = W_BRANCH // GRP
    kern = functools.partial(_rwkv_kernel, nseg=nseg, lt=lt, ls=ls, l_valid=l_valid, l_total=l)
    full = lambda shape: pl.BlockSpec(shape, lambda i, j: (0,) * len(shape))
    blk = lambda col, w: pl.BlockSpec((nseg, lt, w), lambda i, j: (i, j, col // w))
    mu, w0, a0, kkp, kap, rk, gg, gb = vecs
    return pl.pallas_call(
        kern,
        grid=(b // nseg, l // lt),
        in_specs=[blk(C_RKV, W_BRANCH), blk(C_RKV + W_BRANCH, W_BRANCH), blk(C_RKV + 2 * W_BRANCH, W_BRANCH),
                  blk(C_WLAL, 2 * LORA), blk(C_Z_B, W_BRANCH),
                  pl.BlockSpec((nseg, HIST, SHIFT_B), lambda i, j: (i, 0, 0)),
                  pl.BlockSpec((None, nseg, H_B, N_B, N_B), lambda i, j: (layer, i, 0, 0, 0)),
                  full((1, SHIFT_B)), full((1, W_BRANCH)), full((2 * LORA, W_BRANCH)),
                  full((1, W_BRANCH)), full((2 * LORA, W_BRANCH)), full((1, W_BRANCH)),
                  full((1, W_BRANCH)), full((1, W_BRANCH)), full((1, W_BRANCH)), full((1, W_BRANCH)),
                  full((W_BRANCH, W_BRANCH))],
        out_specs=[pl.BlockSpec((nseg, lt, W_BRANCH), lambda i, j: (i, j, 0)),
                   pl.BlockSpec((nseg, H_B, N_B, N_B), lambda i, j: (i, 0, 0, 0))],
        out_shape=[jax.ShapeDtypeStruct((b, l, W_BRANCH), F32),
                   jax.ShapeDtypeStruct((b, H_B, N_B, N_B), F32)],
        scratch_shapes=[pltpu.VMEM((nseg, lt + HIST, SHIFT_B), F32),
                        pltpu.VMEM((nseg, ngrp, GRP, GRP), F32)],
        compiler_params=pltpu.CompilerParams(dimension_semantics=("arbitrary", "arbitrary"),
                                             vmem_limit_bytes=VMEM_LIMIT),
    )(proj3, proj3, proj3, proj3, proj3, prev8, s0, mu, w0, w2p, a0, a2p, kkp, kap, rk, gg, gb, bd)


def _qkprep_kernel(q_ref, k_ref, qn_ref, kn_ref, inv_ref, bd_ref, qo_ref, ko_ref, *, tm, lseq, pos0):
    pos = (pos0 + (pl.program_id(0) * tm + _iota2((tm, 1), 0)) % lseq).astype(F32)
    ang = pos * inv_ref[...]
    cos = jnp.cos(ang)
    sin = jnp.sin(ang)
    d = _iota2((1, LANES), 1) % D_C
    half = ROPE_DIM // 2
    c_main = jnp.where(d < ROPE_DIM, cos, 1.0)
    c_up = jnp.where(d < half, -sin, 0.0)
    c_dn = jnp.where((d >= half) & (d < ROPE_DIM), sin, 0.0)
    bd = bd_ref[...]

    def one(x_ref, g_ref, out_ref):
        x = x_ref[...]
        y = x * lax.rsqrt(_head_sum(x * x, bd) * (1.0 / D_C) + NORM_EPS) * g_ref[...]
        for blk in range(W_BRANCH // LANES):
            yb = y[:, blk * LANES:(blk + 1) * LANES]
            up = pltpu.roll(yb, LANES - half, 1)
            dn = pltpu.roll(yb, half, 1)
            out_ref[:, blk * LANES:(blk + 1) * LANES] = yb * c_main + up * c_up + dn * c_dn

    one(q_ref, qn_ref, qo_ref)
    one(k_ref, kn_ref, ko_ref)


def _qkprep(proj2, qn, kn, inv, bd, *, tm, lseq, pos0):
    t = proj2.shape[0]
    kern = functools.partial(_qkprep_kernel, tm=tm, lseq=lseq, pos0=pos0)
    full = lambda shape: pl.BlockSpec(shape, lambda i: (0,) * len(shape))
    return pl.pallas_call(
        kern,
        grid=(t // tm,),
        in_specs=[pl.BlockSpec((tm, W_BRANCH), lambda i: (i, C_QKV_C // W_BRANCH)),
                  pl.BlockSpec((tm, W_BRANCH), lambda i: (i, C_QKV_C // W_BRANCH + 1)),
                  full((1, W_BRANCH)), full((1, W_BRANCH)), full((1, LANES)), full((W_BRANCH, W_BRANCH))],
        out_specs=[pl.BlockSpec((tm, W_BRANCH), lambda i: (i, 0)),
                   pl.BlockSpec((tm, W_BRANCH), lambda i: (i, 0))],
        out_shape=[jax.ShapeDtypeStruct((t, W_BRANCH), F32), jax.ShapeDtypeStruct((t, W_BRANCH), F32)],
        compiler_params=pltpu.CompilerParams(dimension_semantics=("arbitrary",), vmem_limit_bytes=VMEM_LIMIT),
    )(proj2, proj2, qn, kn, inv, bd)


def _rank_select(gate, nblk, limit, axis):
    idx = _iota2(gate.shape, axis)
    rank = jnp.zeros(gate.shape, F32)
    for m in range(nblk):
        gm = gate[m:m + 1, :] if axis == 0 else gate[:, m:m + 1]
        beats = (gm > gate) | ((gm == gate) & (m < idx))
        rank = rank + jnp.where(beats & (m < limit), 1.0, 0.0)
    return jnp.where((idx < limit) & (rank < MOBA_TOPK), 1.0, 0.0)


HEADS_PER_STEP = LANES // D_C


def _moba_prompt_kernel(q_ref, k_ref, v_ref, o_ref, *, nblk):
    blk = MOBA_BLOCK
    l = nblk * blk
    scale = D_C ** -0.5
    k = k_ref[...]
    q_t = q_ref[...].T
    v_t = v_ref[...].T.astype(BF16)
    kmean = jnp.concatenate([jnp.mean(k[n * blk:(n + 1) * blk], axis=0, keepdims=True) for n in range(nblk)],
                            axis=0)
    own = _iota2((1, l), 1) // blk
    kb = k.astype(BF16)
    causal = _iota2((blk, blk), 0) <= _iota2((blk, blk), 1)
    row_head = _iota2((LANES, 1), 0) // D_C
    units = [(e, j) for e in range(HEADS_PER_STEP) for j in range(nblk)]
    sels, qbs = [], []
    for e in range(HEADS_PER_STEP):
        q_e = jnp.where(row_head == e, q_t, 0.0)
        sels.append(_rank_select(_mm32(kmean, q_e), nblk, own, 0))
        qbs.append((q_e * scale).astype(BF16))
    scores = [jnp.dot(kb[:(j + 1) * blk], qbs[e][:, j * blk:(j + 1) * blk], preferred_element_type=F32)
              for e, j in units]
    probs, denoms = [], []
    for (e, j), s in zip(units, scores):
        cols = slice(j * blk, (j + 1) * blk)
        parts = [jnp.where(sels[e][n:n + 1, cols] > 0.0, s[n * blk:(n + 1) * blk], -jnp.inf) for n in range(j)]
        parts.append(jnp.where(causal, s[j * blk:], -jnp.inf))
        s = jnp.concatenate(parts, axis=0)
        p = jnp.exp(s - jnp.max(s, axis=0, keepdims=True))
        denoms.append(jnp.sum(p, axis=0, keepdims=True))
        probs.append(p.astype(BF16))
    tiles = [jnp.dot(v_t[e * D_C:(e + 1) * D_C, :(j + 1) * blk], p, preferred_element_type=F32) / d
             for (e, j), p, d in zip(units, probs, denoms)]
    outs = [jnp.concatenate(tiles[e * nblk:(e + 1) * nblk], axis=1) for e in range(HEADS_PER_STEP)]
    o_ref[...] = jnp.concatenate(outs, axis=0).T


def _moba_prompt(q3, k3, proj3):
    b, l, _ = q3.shape
    nblk = l // MOBA_BLOCK
    assert (4.0 ** round(np.log(D_C) / np.log(4.0))) == D_C
    kern = functools.partial(_moba_prompt_kernel, nblk=nblk)
    v_col = (C_QKV_C + 2 * W_BRANCH) // LANES
    cols = pl.BlockSpec((None, l, LANES), lambda i, hp: (i, 0, hp))
    return pl.pallas_call(
        kern,
        grid=(b, W_BRANCH // LANES),
        in_specs=[cols, cols, pl.BlockSpec((None, l, LANES), lambda i, hp: (i, 0, v_col + hp))],
        out_specs=cols,
        out_shape=jax.ShapeDtypeStruct((b, l, W_BRANCH), F32),
        compiler_params=pltpu.CompilerParams(dimension_semantics=("arbitrary", "arbitrary"),
                                             vmem_limit_bytes=VMEM_LIMIT),
    )(q3, k3, proj3)


SEQ_PER_STEP = 2


def _moba_decode_kernel(pt_ref, q_ref, kn_ref, vn_ref, *refs, npage, page, lq):
    nseq = q_ref.shape[0]
    k_refs = refs[:nseq * npage]
    v_refs = refs[nseq * npage:2 * nseq * npage]
    o_ref = refs[2 * nseq * npage]
    del pt_ref
    scale = D_C ** -0.5
    nrow = lq * H_C
    nblk = npage * page // MOBA_BLOCK
    lp = q_ref.shape[1]
    seqs = range(nseq)

    lane_head = _iota2((H_C, W_BRANCH), 1) // D_C
    own_head = (lane_head == _iota2((H_C, W_BRANCH), 0)).astype(F32)
    own_rows = jnp.concatenate([own_head] * lq, axis=0)
    lane = _iota2((1, LANES), 1)
    t_key = _iota2((nrow, LANES), 1)
    t_qry = _iota2((nrow, LANES), 0) // H_C
    pad = jnp.zeros((LANES - lp, W_BRANCH), F32)

    qbds = [jnp.concatenate([q_ref[r, t:t + 1, :] * own_head for t in range(lq)], axis=0) for r in seqs]
    qbs = [(qbd * scale).astype(BF16) for qbd in qbds]
    k_ts = [jnp.concatenate([k_refs[r * npage + i][...].reshape(W_BRANCH, page) for i in range(npage)], axis=1)
            for r in seqs]
    v_ts = [jnp.concatenate([v_refs[r * npage + i][...].reshape(W_BRANCH, page) for i in range(npage)], axis=1)
            for r in seqs]
    s_pasts = [jnp.dot(qb, k_t.astype(BF16), preferred_element_type=F32) for qb, k_t in zip(qbs, k_ts)]

    def block_key_sums(k_t):
        ksum = jnp.zeros((W_BRANCH, LANES), F32)
        for n in range(nblk):
            k_n = k_t[:, n * MOBA_BLOCK:(n + 1) * MOBA_BLOCK]
            folded = sum(k_n[:, i * LANES:(i + 1) * LANES] for i in range(MOBA_BLOCK // LANES))
            ksum = ksum + jnp.sum(folded, axis=-1, keepdims=True) * (lane == n).astype(F32)
        return ksum

    ksums = [block_key_sums(k_t) for k_t in k_ts]
    gates = [_mm3(qbd, ksum) * (1.0 / MOBA_BLOCK) for qbd, ksum in zip(qbds, ksums)]
    sels = [_rank_select(gate, nblk, nblk, 1) for gate in gates]
    s_pasts = [jnp.concatenate(
        [jnp.where(sel[:, n:n + 1] > 0.0, s[:, n * MOBA_BLOCK:(n + 1) * MOBA_BLOCK], -jnp.inf)
         for n in range(nblk)], axis=1) for sel, s in zip(sels, s_pasts)]

    k_news = [jnp.concatenate([kn_ref[r], pad], axis=0) for r in seqs]
    v_news = [jnp.concatenate([vn_ref[r], pad], axis=0) for r in seqs]
    s_owns = [jnp.where(t_key <= t_qry, lax.dot_general(qb, k_new.astype(BF16), NT, preferred_element_type=F32),
                        -jnp.inf) for qb, k_new in zip(qbs, k_news)]
    ms = [jnp.maximum(jnp.max(sp, axis=-1, keepdims=True), jnp.max(so, axis=-1, keepdims=True))
          for sp, so in zip(s_pasts, s_owns)]
    p_pasts = [jnp.exp(sp - m) for sp, m in zip(s_pasts, ms)]
    p_owns = [jnp.exp(so - m) for so, m in zip(s_owns, ms)]
    denoms = [jnp.sum(pp, axis=-1, keepdims=True) + jnp.sum(po, axis=-1, keepdims=True)
              for pp, po in zip(p_pasts, p_owns)]
    accs = [lax.dot_general(pp.astype(BF16), v_t.astype(BF16), NT, preferred_element_type=F32)
            for pp, v_t in zip(p_pasts, v_ts)]
    accs = [acc + jnp.dot(po.astype(BF16), v_new.astype(BF16), preferred_element_type=F32)
            for acc, po, v_new in zip(accs, p_owns, v_news)]
    for r in seqs:
        out = accs[r] / denoms[r]
        out = (out * own_rows).reshape(lq, H_C, W_BRANCH).sum(axis=1)
        o_ref[r] = jnp.concatenate([out, jnp.zeros((lp - lq, W_BRANCH), F32)], axis=0)


def _moba_decode(page_table, q3, k3, v3, cache_k, cache_v, layer, lq):
    b, lp, _ = q3.shape
    npage = page_table.shape[1]
    page = cache_k.shape[-1]
    nseq = SEQ_PER_STEP
    kern = functools.partial(_moba_decode_kernel, npage=npage, page=page, lq=lq)
    tok = pl.BlockSpec((nseq, lp, W_BRANCH), lambda i, pt: (i, 0, 0))

    def page_spec(r, p_i):
        return pl.BlockSpec((None, None, H_C, D_C, page), lambda i, pt: (layer, pt[i * nseq + r, p_i], 0, 0, 0))

    pages = [page_spec(r, p_i) for r in range(nseq) for p_i in range(npage)]
    grid_spec = pltpu.PrefetchScalarGridSpec(
        num_scalar_prefetch=1,
        grid=(b // nseq,),
        in_specs=[tok, tok, tok] + pages * 2,
        out_specs=pl.BlockSpec((nseq, lp, W_BRANCH), lambda i, pt: (i, 0, 0)),
    )
    return pl.pallas_call(
        kern,
        grid_spec=grid_spec,
        out_shape=jax.ShapeDtypeStruct((b, lp, W_BRANCH), F32),
        compiler_params=pltpu.CompilerParams(dimension_semantics=("arbitrary",), vmem_limit_bytes=VMEM_LIMIT),
    )(page_table, q3, k3, v3, *([cache_k] * (nseq * npage)), *([cache_v] * (nseq * npage)))


def _merge_kernel(oa_ref, ob_ref, oc_ref, zc_ref, gate_ref, x_ref, ple_ref, wb_ref, bg_ref, wo_ref,
                  wpg_ref, bpg_ref, wple_ref, y_ref):
    branches = (oa_ref[...], ob_ref[...], oc_ref[...] * _silu(zc_ref[...]))
    up = None
    for n, o in enumerate(branches):
        sl = slice(n * D_MODEL, (n + 1) * D_MODEL)
        gate = _sigmoid(gate_ref[:, sl] + bg_ref[:, sl])
        term = gate * jnp.dot(o.astype(BF16), wb_ref[n], preferred_element_type=F32)
        up = term if up is None else up + term
    y = x_ref[...] + jnp.dot(up.astype(BF16), wo_ref[...], preferred_element_type=F32)
    pg = _sigmoid(jnp.dot(y.astype(BF16), wpg_ref[...], preferred_element_type=F32) + bpg_ref[...])
    y_ref[...] = y + pg * jnp.dot(ple_ref[...].astype(BF16), wple_ref[...], preferred_element_type=F32)


def _merge(oa, ob, oc, proj2, x2, ple3, ple_layer, wb, bg, wo, wpg, bpg, wple, tm):
    t = x2.shape[0]
    full = lambda shape: pl.BlockSpec(shape, lambda i: (0,) * len(shape))
    row = lambda w: pl.BlockSpec((tm, w), lambda i: (i, 0))
    return pl.pallas_call(
        _merge_kernel,
        grid=(t // tm,),
        in_specs=[row(W_BRANCH), row(W_BRANCH), row(W_BRANCH),
                  pl.BlockSpec((tm, W_BRANCH), lambda i: (i, C_Z_C // W_BRANCH)),
                  pl.BlockSpec((tm, N_BRANCH * D_MODEL), lambda i: (i, C_GATE // (N_BRANCH * D_MODEL))),
                  row(D_MODEL), pl.BlockSpec((None, tm, D_PLE), lambda i: (ple_layer, i, 0)),
                  full((N_BRANCH, W_BRANCH, D_MODEL)), full((1, N_BRANCH * D_MODEL)), full((D_MODEL, D_MODEL)),
                  full((D_MODEL, D_MODEL)), full((1, D_MODEL)), full((D_PLE, D_MODEL))],
        out_specs=row(D_MODEL),
        out_shape=jax.ShapeDtypeStruct((t, D_MODEL), F32),
        compiler_params=pltpu.CompilerParams(dimension_semantics=("arbitrary",), vmem_limit_bytes=VMEM_LIMIT),
    )(oa, ob, oc, proj2, proj2, x2, ple3, wb, bg, wo, wpg, bpg, wple)


def _arrange_w_in(w):
    offs = np.cumsum((0,) + IN_SPLITS)
    qkv_a, z_a, beta, alpha, f_b, z_b, qkv_c, z_c, gate = [w[:, offs[i]:offs[i + 1]] for i in range(9)]
    d = w.shape[0]
    ba = jnp.concatenate([beta, alpha, jnp.zeros((d, LANES - 2 * H_A), w.dtype)], axis=1)
    tail = jnp.zeros((d, N_PROJ - C_BA - LANES), w.dtype)
    out = jnp.concatenate([qkv_a, z_a, f_b[:, :3 * W_BRANCH], z_b, qkv_c, z_c, gate,
                           f_b[:, 3 * W_BRANCH:], ba, tail], axis=1)
    return out.astype(BF16)


def _layer_params(i, p):
    row = lambda v: v.reshape(1, -1)
    lanes_ba = lambda v: jnp.zeros((1, LANES), F32).at[0, H_A:2 * H_A].set(v)
    lora_pad = jnp.zeros((LORA, W_BRANCH), F32)
    head = np.arange(W_BRANCH) // N_B
    half = ROPE_DIM // 2
    inv = ROPE_THETA ** (-jnp.arange(half, dtype=F32) / half)
    d = np.arange(LANES) % D_C
    inv_lanes = jnp.where(d < ROPE_DIM, inv[d % half], 0.0).reshape(1, LANES)
    return dict(
        norm_g=row(p['norm_g'][i]),
        w_in=_arrange_w_in(p['w_in'][i]),
        conv=p['conv_a'][i],
        pa=lanes_ba(p['a_log'][i]), pb=lanes_ba(p['dt_bias'][i]), onorm=row(p['onorm_a'][i]),
        rw_vecs=(row(p['mu_b'][i]), row(p['w0_b'][i]), row(p['a0_b'][i]), row(p['kk_b'][i]), row(p['ka_b'][i]),
                 row(p['rk_b'][i]), row(p['gn_g_b'][i]), row(p['gn_b_b'][i])),
        w2p=jnp.concatenate([p['w2_b'][i], lora_pad], axis=0).astype(BF16),
        a2p=jnp.concatenate([lora_pad, p['a2_b'][i]], axis=0).astype(BF16),
        bd=jnp.asarray(head[:, None] == head[None, :], BF16),
        qn=row(jnp.tile(p['qn_c'][i], H_C)), kn=row(jnp.tile(p['kn_c'][i], H_C)), inv=inv_lanes,
        wb=p['w_branch'][i].astype(BF16), bg=row(p['b_gate'][i]), wo=p['w_out'][i].astype(BF16),
        wpg=p['w_pg'][i].astype(BF16), bpg=row(p['b_pg'][i]), wple=p['w_ple'][i].astype(BF16),
    )


def _hist_rows(rows):
    b, n, c = rows.shape
    return jnp.concatenate([jnp.zeros((b, HIST - n, c), F32), rows], axis=1)


def _layer(x3, ple3, ple_layer, lp, *, l_valid, pos0, dn_s, rw_s, state_layer, dn_buf, rw_prev, nseg, lt, ls, tm, attend):
    b, l, d = x3.shape
    t = b * l
    proj2 = _proj(x3.reshape(t, d), lp['norm_g'], lp['w_in'], min(t, 1024), 1920)
    proj3 = proj2.reshape(b, l, N_PROJ)

    oa, dn_new = _gdn(proj3, _hist_rows(dn_buf), dn_s, state_layer, lp['conv'], lp['pa'], lp['pb'], lp['onorm'],
                      nseg=nseg, lt=lt, ls=ls, l_valid=l_valid)
    ob, rw_new = _rwkv(proj3, _hist_rows(rw_prev[:, None, :]), rw_s, state_layer, lp['rw_vecs'],
                       lp['w2p'], lp['a2p'], lp['bd'], nseg=nseg, lt=lt, ls=ls, l_valid=l_valid)
    qr, kr = _qkprep(proj2, lp['qn'], lp['kn'], lp['inv'], lp['bd'], tm=min(tm, t), lseq=l, pos0=pos0)
    v2 = proj2[:, C_QKV_C + 2 * W_BRANCH:C_QKV_C + 3 * W_BRANCH]
    oc = attend(qr, kr, v2, proj3)

    y2 = _merge(oa.reshape(t, W_BRANCH), ob.reshape(t, W_BRANCH), oc.reshape(t, W_BRANCH), proj2,
                x3.reshape(t, d), ple3, ple_layer, lp['wb'], lp['bg'], lp['wo'], lp['wpg'],
                lp['bpg'], lp['wple'], min(tm, t))

    f_last = jnp.concatenate([proj3[:, l_valid - 1, C_RKV:C_RKV + 3 * W_BRANCH],
                              proj3[:, l_valid - 1, C_WLAL:C_WLAL + 2 * LORA]], axis=-1)
    new_state = (kr.reshape(b, l, H_C, D_C)[:, :l_valid], v2.reshape(b, l, H_C, D_C)[:, :l_valid], dn_new,
                 proj3[:, l_valid - (CONV_W - 1):l_valid, C_QKV_A:C_QKV_A + QKV_A],
                 rw_new, f_last)
    return y2.reshape(b, l, d), new_state


def _attend_prompt(b, l):
    def attend(qr, kr, v2, proj3):
        shp = (b, l, W_BRANCH)
        return _moba_prompt(qr.reshape(shp), kr.reshape(shp), proj3).reshape(b * l, W_BRANCH)
    return attend


def _attend_decode(b, lp_, lq, page_table, cache_k4, cache_v4, layer):
    def attend(qr, kr, v2, proj3):
        shp = (b, lp_, W_BRANCH)
        return _moba_decode(page_table, qr.reshape(shp), kr.reshape(shp), v2.reshape(shp),
                            cache_k4, cache_v4, layer, lq)
    return attend


DEC_PAD = 8


def kernel(x_prompt, x_sample, cache_k, cache_v, state_dn, state_dn_conv, state_rwkv, state_rwkv_shift,
           page_table, p_prompt, p_sample, norm_g, w_in, conv_a, a_log, dt_bias, onorm_a, mu_b, w0_b, w2_b,
           a0_b, a2_b, kk_b, ka_b, rk_b, gn_g_b, gn_b_b, qn_c, kn_c, w_branch, b_gate, w_out, w_ple, w_pg,
           b_pg):
    params = dict(norm_g=norm_g, w_in=w_in, conv_a=conv_a, a_log=a_log, dt_bias=dt_bias, onorm_a=onorm_a,
                  mu_b=mu_b, w0_b=w0_b, w2_b=w2_b, a0_b=a0_b, a2_b=a2_b, kk_b=kk_b, ka_b=ka_b, rk_b=rk_b,
                  gn_g_b=gn_g_b, gn_b_b=gn_b_b, qn_c=qn_c, kn_c=kn_c, w_branch=w_branch, b_gate=b_gate,
                  w_out=w_out, w_ple=w_ple, w_pg=w_pg, b_pg=b_pg)
    depth = norm_g.shape[0]
    b_p, l_p, _ = x_prompt.shape
    b_s, l_s, _ = x_sample.shape
    n_phys, page = cache_k.shape[1], cache_k.shape[2]
    past_len = page_table.shape[1] * page
    assert l_p % 512 == 0 and CONV_W - 1 <= l_s <= DEC_PAD and b_s % (CHUNK // DEC_PAD) == 0
    assert past_len % MOBA_BLOCK == 0 and MOBA_BLOCK % page == 0

    pad_rows = lambda a: jnp.pad(a, ((0, 0), (0, DEC_PAD - l_s), (0, 0)))
    xs = pad_rows(x_sample)
    cache_k4 = cache_k.transpose(0, 1, 3, 4, 2)
    cache_v4 = cache_v.transpose(0, 1, 3, 4, 2)
    xp = x_prompt
    new_p, new_s = [], []
    for i in range(depth):
        lp = _layer_params(i, params)
        xp, st_p = _layer(xp, p_prompt.reshape(depth, b_p * l_p, D_PLE), i, lp, l_valid=l_p, pos0=0,
                          dn_s=jnp.zeros((1, b_p, H_A, DK_A, DV_A), F32),
                          rw_s=jnp.zeros((1, b_p, H_B, N_B, N_B), F32), state_layer=0,
                          dn_buf=jnp.zeros((b_p, CONV_W - 1, QKV_A), F32), rw_prev=jnp.zeros((b_p, SHIFT_B), F32),
                          nseg=1, lt=512, ls=CHUNK, tm=512, attend=_attend_prompt(b_p, l_p))
        new_p.append(st_p)
        xs, st_s = _layer(xs, pad_rows(p_sample[i]).reshape(1, b_s * DEC_PAD, D_PLE), 0, lp, l_valid=l_s, pos0=past_len,
                          dn_s=state_dn, rw_s=state_rwkv, state_layer=i, dn_buf=state_dn_conv[i],
                          rw_prev=state_rwkv_shift[i], nseg=CHUNK // DEC_PAD, lt=DEC_PAD, ls=DEC_PAD, tm=512,
                          attend=_attend_decode(b_s, DEC_PAD, l_s, page_table, cache_k4, cache_v4, i))
        new_s.append(st_s)
    outs_p = [jnp.stack(t) for t in zip(*new_p)]
    outs_s = [jnp.stack(t) for t in zip(*new_s)]
    return (xp, xs[:, :l_s], *outs_p, *outs_s)
```

```python
import functools

import numpy as np
import jax
import jax.numpy as jnp
from jax import lax
from jax.experimental import pallas as pl
from jax.experimental.pallas import tpu as pltpu

F32 = jnp.float32
BF16 = jnp.bfloat16

D_MODEL = 1024
D_PLE = 256
N_BRANCH = 3
W_BRANCH = 512
NORM_EPS = 1e-6
DK_A = 128
DV_A = 128
H_A = 4
QKV_A = 1536
CONV_W = 4
N_B = 64
H_B = 8
LORA = 64
SHIFT_B = 3 * W_BRANCH + 2 * LORA
GN_EPS = 64e-5
D_C = 64
H_C = 8
MOBA_BLOCK = 256
MOBA_TOPK = 3
ROPE_DIM = 16
ROPE_THETA = 500000.0
IN_SPLITS = (QKV_A, W_BRANCH, H_A, H_A, SHIFT_B, W_BRANCH, 3 * W_BRANCH, W_BRANCH, N_BRANCH * D_MODEL)

C_QKV_A = 0
C_Z_A = 1536
C_RKV = 2048
C_Z_B = 3584
C_QKV_C = 4096
C_Z_C = 5632
C_GATE = 6144
C_WLAL = 9216
C_BA = 9344
N_PROJ = 9600

CHUNK = 64
HIST = 8
CHUNKS_PER_TRIP_A = 8
CHUNKS_PER_TRIP_B = 8
LANES = 128
VMEM_LIMIT = 56 * 1024 * 1024

NN = (((1,), (0,)), ((), ()))
BNN = (((2,), (1,)), ((0,), (0,)))
NT = (((1,), (1,)), ((), ()))
TN = (((0,), (0,)), ((), ()))
HIGHEST = lax.Precision.HIGHEST


def _mm(a, b, dims=NN):
    return lax.dot_general(a.astype(BF16), b.astype(BF16), dims, preferred_element_type=F32)


def _mm32(a, b, dims=NN):
    return lax.dot_general(a, b, dims, precision=HIGHEST, preferred_element_type=F32)


def _sigmoid(x):
    return 1.0 / (1.0 + jnp.exp(-x))


def _silu(x):
    return x * _sigmoid(x)


def _softplus(x):
    return jnp.maximum(x, 0.0) + jnp.log1p(jnp.exp(-jnp.abs(x)))


def _iota2(shape, dim):
    return lax.broadcasted_iota(jnp.int32, shape, dim)


def _split(x):
    hi = x.astype(BF16)
    return hi, (x - hi.astype(F32)).astype(BF16)


def _dotb(a, b, dims=NN):
    return lax.dot_general(a, b, dims, preferred_element_type=F32)


def _mm_sel(sel, x):
    hi, lo = _split(x)
    return _dotb(sel, hi) + _dotb(sel, lo)


def _mm3(a, b, dims=NN):
    ah, al = _split(a)
    bh, bl = _split(b)
    return _dotb(ah, bh, dims) + (_dotb(ah, bl, dims) + _dotb(al, bh, dims))


SOLVE_MM_A = _mm
SOLVE_MM_B = _mm


def _inv_unit_lower(lmats, nilpotent, mm):
    n = lmats[0].shape[-1]
    dims = NN if lmats[0].ndim == 2 else BNN
    eye = (_iota2((n, n), 0) == _iota2((n, n), 1)).astype(F32)
    ps = [-lmat for lmat in lmats]
    accs = [eye + p for p in ps]
    span = 2
    while span < nilpotent:
        ps = [mm(p, p, dims) for p in ps]
        accs = [acc + mm(acc, p, dims) for acc, p in zip(accs, ps)]
        span *= 2
    return accs


def _head_sums(xs, bd):
    n = xs[0].shape[0]
    out = jnp.dot(jnp.concatenate([x.astype(BF16) for x in xs], axis=0), bd, preferred_element_type=F32)
    return [out[i * n:(i + 1) * n] for i in range(len(xs))]


def _head_sum(x, bd):
    return _head_sums([x], bd)[0]


def _proj_kernel(x_ref, g_ref, w_ref, o_ref):
    x = x_ref[...]
    h = x * lax.rsqrt(jnp.mean(x * x, axis=-1, keepdims=True) + NORM_EPS) * g_ref[...]
    o_ref[...] = jnp.dot(h.astype(BF16), w_ref[...], preferred_element_type=F32)


def _proj(x2d, g, w, tm, tn):
    t, d = x2d.shape
    n = w.shape[1]
    return pl.pallas_call(
        _proj_kernel,
        grid=(n // tn, t // tm),
        in_specs=[pl.BlockSpec((tm, d), lambda j, i: (i, 0)),
                  pl.BlockSpec((1, d), lambda j, i: (0, 0)),
                  pl.BlockSpec((d, tn), lambda j, i: (0, j))],
        out_specs=pl.BlockSpec((tm, tn), lambda j, i: (i, j)),
        out_shape=jax.ShapeDtypeStruct((t, n), F32),
        compiler_params=pltpu.CompilerParams(dimension_semantics=("arbitrary", "arbitrary"),
                                             vmem_limit_bytes=VMEM_LIMIT),
    )(x2d, g, w)


def _chunk_masks(n, ls):
    r = _iota2((n, n), 0)
    c = _iota2((n, n), 1)
    same = (r // ls) == (c // ls)
    return same, same & (r >= c), same & (r > c)


def _valid_rows(l_idx, lt, r0, ls, l_valid):
    t = l_idx * lt + r0 + (_iota2((CHUNK, 1), 0) % ls)
    return t < l_valid


def _gdn_kernel(qkv_ref, z_ref, ba_ref, buf_ref, s0_ref, cw_ref, pa_ref, pb_ref, on_ref,
                o_ref, sout_ref, xs_ref, st_ref, *, nseg, lt, ls, l_valid, l_total):
    l_idx = pl.program_id(1)
    nrow = H_A * CHUNK

    @pl.when(l_idx == 0)
    def _():
        xs_ref[:, 0:HIST, :] = buf_ref[...]
        st_ref[...] = s0_ref[...]

    @pl.when(l_idx > 0)
    def _():
        xs_ref[:, 0:HIST, :] = xs_ref[:, lt:lt + HIST, :]

    xs_ref[:, HIST:lt + HIST, :] = qkv_ref[...]

    same_r, c_incl, c_strict = _chunk_masks(CHUNK, ls)
    tri_r = c_incl.astype(BF16)
    same_r = same_r.astype(BF16)
    eye_c = (_iota2((CHUNK, CHUNK), 0) == _iota2((CHUNK, CHUNK), 1)).astype(F32)
    cw = cw_ref[...]
    neg_a = -jnp.exp(pa_ref[...])
    pb = pb_ref[...]
    onorm = on_ref[...]

    def conv(s, r0, col):
        win = xs_ref[s, pl.ds(r0, ls + HIST), col:col + LANES]
        w = cw[:, col:col + LANES]
        acc = pltpu.roll(win, 3, 0)[HIST:] * w[0:1] + pltpu.roll(win, 2, 0)[HIST:] * w[1:2]
        acc = acc + pltpu.roll(win, 1, 0)[HIST:] * w[2:3]
        acc = acc + win[HIST:] * w[3:4]
        return _silu(acc)

    def part(r0, col):
        return jnp.concatenate([conv(s, r0, col) for s in range(nseg)], axis=0)

    def local(c):
        r0 = pl.multiple_of(c * ls, ls)
        ba = jnp.concatenate([ba_ref[s, pl.ds(r0, ls), :] for s in range(nseg)], axis=0)
        beta_all = _sigmoid(ba)
        g_all = neg_a * _softplus(ba + pb)
        if l_valid < l_total:
            ok = _valid_rows(l_idx, lt, r0, ls, l_valid)
            beta_all = jnp.where(ok, beta_all, 0.0)
            g_all = jnp.where(ok, g_all, 0.0)
        gcum = _mm_sel(tri_r, g_all)
        if nseg == 1:
            gtot = jnp.broadcast_to(gcum[CHUNK - 1:CHUNK], gcum.shape)
        else:
            gtot = _mm_sel(same_r, g_all)

        qs, ks, vs, betas, gcs, gts, lmats, qks = [], [], [], [], [], [], [], []
        for h in range(H_A):
            q = part(r0, h * DK_A)
            k = part(r0, H_A * DK_A + h * DK_A)
            v = part(r0, 2 * H_A * DK_A + h * DV_A)
            q = q * lax.rsqrt(jnp.sum(q * q, axis=-1, keepdims=True) + 1e-6) * (DK_A ** -0.5)
            k = k * lax.rsqrt(jnp.sum(k * k, axis=-1, keepdims=True) + 1e-6)
            beta_h = beta_all[:, h:h + 1]
            g_h = gcum[:, H_A + h:H_A + h + 1]
            g_row = jnp.sum(eye_c * g_h, axis=0, keepdims=True)
            decay = jnp.exp(jnp.where(c_incl, g_h - g_row, -jnp.inf))
            kq_k = _mm(jnp.concatenate([k, q], axis=0), k, NT)
            lmats.append(jnp.where(c_strict, beta_h * kq_k[:CHUNK] * decay, 0.0))
            qks.append(jnp.where(c_incl, kq_k[CHUNK:] * decay, 0.0))
            qs.append(q)
            ks.append(k)
            vs.append(v)
            betas.append(beta_h)
            gcs.append(g_h)
            gts.append(gtot[:, H_A + h:H_A + h + 1])
        q_st = jnp.concatenate(qs, axis=0)
        k_st = jnp.concatenate(ks, axis=0)
        v_st = jnp.concatenate(vs, axis=0)
        beta = jnp.concatenate(betas, axis=0)
        g_col = jnp.concatenate(gcs, axis=0)
        gt_col = jnp.concatenate(gts, axis=0)
        return dict(
            r0=r0, gt_col=gt_col,
            lmat=jnp.stack(lmats),
            rhs=jnp.concatenate([v_st * beta, k_st * (beta * jnp.exp(g_col))], axis=1),
            qk=jnp.stack(qks),
            qg=q_st * jnp.exp(g_col),
            kd=k_st * jnp.exp(gt_col - g_col))

    def advance(loc, sol):
        r0, gt_col, qk, qg, kd = loc['r0'], loc['gt_col'], loc['qk'], loc['qg'], loc['kd']
        bv = sol[:, :DV_A]
        wk = sol[:, DV_A:]
        wks, qgs = [], []
        for h in range(H_A):
            for s in range(nseg):
                a = h * CHUNK + s * ls
                both = _mm(jnp.concatenate([wk[a:a + ls], qg[a:a + ls]], axis=0), st_ref[s, h])
                wks.append(both[:ls])
                qgs.append(both[ls:])
        u = bv - jnp.concatenate(wks, axis=0)
        o = jnp.concatenate(qgs, axis=0) + _mm(qk, u.reshape(H_A, CHUNK, DV_A), BNN).reshape(nrow, DV_A)

        for h in range(H_A):
            a = h * CHUNK
            for s in range(nseg):
                b = a + s * ls
                if nseg == 1:
                    kd_s = kd[a:a + CHUNK]
                else:
                    rows = _iota2((CHUNK, 1), 0) // ls
                    kd_s = jnp.where(rows == s, kd[a:a + CHUNK], 0.0)
                upd = _mm(kd_s, u[a:a + CHUNK], TN)
                st_ref[s, h] = st_ref[s, h] * jnp.exp(gt_col[b:b + 1]) + upd
            oh = o[a:a + CHUNK]
            oh = oh * lax.rsqrt(jnp.mean(oh * oh, axis=-1, keepdims=True) + NORM_EPS) * onorm
            for s in range(nseg):
                zz = z_ref[s, pl.ds(r0, ls), h * DV_A:(h + 1) * DV_A]
                o_ref[s, pl.ds(r0, ls), h * DV_A:(h + 1) * DV_A] = oh[s * ls:(s + 1) * ls] * _silu(zz)

    nchunk = lt // ls
    per_trip = min(CHUNKS_PER_TRIP_A, nchunk)

    def trip(i, carry):
        locs = [local(i * per_trip + j) for j in range(per_trip)]
        ainvs = _inv_unit_lower([loc['lmat'] for loc in locs], ls, SOLVE_MM_A)
        sols = [SOLVE_MM_A(ainv, loc['rhs'].reshape(H_A, CHUNK, DK_A + DV_A), BNN).reshape(nrow, DK_A + DV_A)
                for loc, ainv in zip(locs, ainvs)]
        for loc, sol in zip(locs, sols):
            advance(loc, sol)
        return carry

    lax.fori_loop(0, nchunk // per_trip, trip, 0)

    @pl.when(l_idx == pl.num_programs(1) - 1)
    def _():
        sout_ref[...] = st_ref[...]


def _gdn(proj3, buf8, s0, layer, cw, pa, pb, onorm, *, nseg, lt, ls, l_valid):
    b, l, _ = proj3.shape
    kern = functools.partial(_gdn_kernel, nseg=nseg, lt=lt, ls=ls, l_valid=l_valid, l_total=l)
    full = lambda shape: pl.BlockSpec(shape, lambda i, j: (0,) * len(shape))
    return pl.pallas_call(
        kern,
        grid=(b // nseg, l // lt),
        in_specs=[pl.BlockSpec((nseg, lt, QKV_A), lambda i, j: (i, j, C_QKV_A // QKV_A)),
                  pl.BlockSpec((nseg, lt, W_BRANCH), lambda i, j: (i, j, C_Z_A // W_BRANCH)),
                  pl.BlockSpec((nseg, lt, LANES), lambda i, j: (i, j, C_BA // LANES)),
                  pl.BlockSpec((nseg, HIST, QKV_A), lambda i, j: (i, 0, 0)),
                  pl.BlockSpec((None, nseg, H_A, DK_A, DV_A), lambda i, j: (layer, i, 0, 0, 0)),
                  full((CONV_W, QKV_A)), full((1, LANES)), full((1, LANES)), full((1, DV_A))],
        out_specs=[pl.BlockSpec((nseg, lt, W_BRANCH), lambda i, j: (i, j, 0)),
                   pl.BlockSpec((nseg, H_A, DK_A, DV_A), lambda i, j: (i, 0, 0, 0))],
        out_shape=[jax.ShapeDtypeStruct((b, l, W_BRANCH), F32),
                   jax.ShapeDtypeStruct((b, H_A, DK_A, DV_A), F32)],
        scratch_shapes=[pltpu.VMEM((nseg, lt + HIST, QKV_A), F32),
                        pltpu.VMEM((nseg, H_A, DK_A, DV_A), F32)],
        compiler_params=pltpu.CompilerParams(dimension_semantics=("arbitrary", "arbitrary"),
                                             vmem_limit_bytes=VMEM_LIMIT),
    )(proj3, proj3, proj3, buf8, s0, cw, pa, pb, onorm)


GRP = 4 * N_B


def _rwkv_kernel(r_ref, k_ref, v_ref, wa_ref, z_ref, prev_ref, s0_ref, mu_ref, w0_ref, w2_ref,
                 a0_ref, a2_ref, kkp_ref, kap_ref, rk_ref, gg_ref, gb_ref, bd_ref,
                 o_ref, sout_ref, fs_ref, st_ref, *, nseg, lt, ls, l_valid, l_total):
    l_idx = pl.program_id(1)
    nrow = 4 * CHUNK
    ngrp = W_BRANCH // GRP

    @pl.when(l_idx == 0)
    def _():
        fs_ref[:, 0:HIST, :] = prev_ref[...]
        st_ref[...] = jnp.zeros(st_ref.shape, F32)
        for s in range(nseg):
            for h in range(H_B):
                g, j = divmod(h, 4)
                st_ref[s, g, j * N_B:(j + 1) * N_B, j * N_B:(j + 1) * N_B] = s0_ref[s, h]

    @pl.when(l_idx > 0)
    def _():
        fs_ref[:, 0:HIST, :] = fs_ref[:, lt:lt + HIST, :]

    fs_ref[:, HIST:lt + HIST, 0:W_BRANCH] = r_ref[...]
    fs_ref[:, HIST:lt + HIST, W_BRANCH:2 * W_BRANCH] = k_ref[...]
    fs_ref[:, HIST:lt + HIST, 2 * W_BRANCH:3 * W_BRANCH] = v_ref[...]
    fs_ref[:, HIST:lt + HIST, 3 * W_BRANCH:SHIFT_B] = wa_ref[...]

    same_r, tri_r, _ = _chunk_masks(CHUNK, ls)
    tri_r = tri_r.astype(BF16)
    same_r = same_r.astype(BF16)
    lane_head = _iota2((1, GRP), 1) // N_B
    head_masks = [(lane_head == j).astype(F32) for j in range(4)]
    bd_state = (_iota2((GRP, GRP), 0) // N_B) == (_iota2((GRP, GRP), 1) // N_B)
    head_rows = ((_iota2((nrow, GRP), 0) // CHUNK) == (_iota2((nrow, GRP), 1) // N_B)).astype(F32)
    t_row = _iota2((nrow, CHUNK), 0) % CHUNK
    t_col = _iota2((nrow, CHUNK), 1)
    t_same = (t_row // ls) == (t_col // ls)
    t_incl = t_same & (t_col <= t_row)
    t_strict = t_same & (t_col < t_row)
    mu = mu_ref[...]
    bd = bd_ref[...]

    def mixed(r0, col, width):
        outs = []
        for s in range(nseg):
            win = fs_ref[s, pl.ds(r0, ls + HIST), col:col + width]
            cur = win[HIST:]
            prev = pltpu.roll(win, 1, 0)[HIST:]
            outs.append(cur + (prev - cur) * mu[:, col:col + width])
        return jnp.concatenate(outs, axis=0)

    def stack_masked(x):
        return jnp.concatenate([x * head_masks[j] for j in range(4)], axis=0)

    def unstack(x):
        return x[0:CHUNK] + x[CHUNK:2 * CHUNK] + x[2 * CHUNK:3 * CHUNK] + x[3 * CHUNK:4 * CHUNK]

    def local(c):
        r0 = pl.multiple_of(c * ls, ls)
        r = mixed(r0, 0, W_BRANCH)
        k = mixed(r0, W_BRANCH, W_BRANCH)
        v = mixed(r0, 2 * W_BRANCH, W_BRANCH)
        wa = mixed(r0, 3 * W_BRANCH, 2 * LORA)
        w_raw = w0_ref[...] + _mm(jnp.tanh(wa), w2_ref[...])
        logw = -jnp.exp(-_softplus(-w_raw) - 0.5)
        a_sig = _sigmoid(a0_ref[...] + _mm(wa, a2_ref[...]))
        kk = k * kkp_ref[...]
        k_mod = k * (1.0 + (a_sig - 1.0) * kap_ref[...])
        kk_ss, rk_sum = _head_sums([kk * kk, r * k_mod * rk_ref[...]], bd)
        kk = kk * lax.rsqrt(kk_ss + 1e-6)
        b_vec = kk * a_sig
        if l_valid < l_total:
            ok = _valid_rows(l_idx, lt, r0, ls, l_valid)
            logw = jnp.where(ok, logw, 0.0)
            b_vec = jnp.where(ok, b_vec, 0.0)
            k_mod = jnp.where(ok, k_mod, 0.0)
        gcum = _mm_sel(tri_r, logw)
        if nseg == 1:
            gtot = jnp.broadcast_to(gcum[CHUNK - 1:CHUNK], gcum.shape)
        else:
            gtot = _mm_sel(same_r, logw)
        e_neg = jnp.exp(-gcum)
        e_end = jnp.exp(gtot - gcum)
        a_t = -kk * jnp.exp(gcum - logw)
        b_t = b_vec * e_neg
        k_t = k_mod * e_neg
        r_t = r * jnp.exp(gcum)
        b_h = b_vec * e_end
        k_h = k_mod * e_end

        groups = []
        for g in range(ngrp):
            sl = slice(g * GRP, (g + 1) * GRP)
            a_g, b_g, k_g, r_g, v_g = a_t[:, sl], b_t[:, sl], k_t[:, sl], r_t[:, sl], v[:, sl]
            a_st = stack_masked(a_g)
            r_st = stack_masked(r_g)
            ar_k = _mm(jnp.concatenate([a_st, r_st], axis=0), k_g, NT)
            aak = jnp.where(t_strict, ar_k[:nrow], 0.0)
            ark = jnp.where(t_incl, ar_k[nrow:], 0.0)
            kv = _mm(jnp.concatenate([aak, ark], axis=0), v_g)
            groups.append(dict(
                a_g=a_g, r_g=r_g, v_g=v_g, b_h=b_h[:, sl], k_h=k_h[:, sl], gtot=gtot[:, sl],
                neg_aab=-jnp.where(t_strict, _mm(a_st, b_g, NT), 0.0).reshape(4, CHUNK, CHUNK),
                kv_u=kv[:nrow] * head_rows, kv_y=kv[nrow:],
                arb=jnp.where(t_incl, _mm(r_st, b_g, NT), 0.0)))
        return dict(r0=r0, v=v, rk_sum=rk_sum, groups=groups)

    def advance(loc, tinvs):
        grps = loc['groups']
        ams, rms = [], []
        for g, grp in enumerate(grps):
            parts = [_mm(jnp.concatenate([grp['a_g'][s * ls:(s + 1) * ls], grp['r_g'][s * ls:(s + 1) * ls]], axis=0),
                         st_ref[s, g], NT) for s in range(nseg)]
            ams.append(jnp.concatenate([p[:ls] for p in parts], axis=0))
            rms.append(jnp.concatenate([p[ls:] for p in parts], axis=0))
        us = [unstack(SOLVE_MM_B(tinv, (stack_masked(am) + grp['kv_u']).reshape(4, CHUNK, GRP), BNN)
                      .reshape(nrow, GRP)) for grp, tinv, am in zip(grps, tinvs, ams)]
        ys = [rm + unstack((_mm(grp['arb'], u) + grp['kv_y']) * head_rows) for grp, u, rm in zip(grps, us, rms)]
        for g, (grp, u) in enumerate(zip(grps, us)):
            v_g = grp['v_g']
            for s in range(nseg):
                if nseg == 1:
                    u_s, v_s = u, v_g
                else:
                    rows = _iota2((CHUNK, 1), 0) // ls
                    u_s = jnp.where(rows == s, u, 0.0)
                    v_s = jnp.where(rows == s, v_g, 0.0)
                upd = _mm(jnp.concatenate([u_s, v_s], axis=0),
                          jnp.concatenate([grp['b_h'], grp['k_h']], axis=0), TN)
                dec = jnp.exp(grp['gtot'][s * ls:s * ls + 1])
                st_ref[s, g] = st_ref[s, g] * dec + jnp.where(bd_state, upd, 0.0)
        y = jnp.concatenate(ys, axis=1)
        mean = _head_sums([y], bd)[0] * (1.0 / N_B)
        yc = y - mean
        var = _head_sums([yc * yc], bd)[0] * (1.0 / N_B)
        yn = yc * lax.rsqrt(var + GN_EPS) * gg_ref[...] + gb_ref[...]
        out = yn + loc['rk_sum'] * loc['v']
        for s in range(nseg):
            zz = z_ref[s, pl.ds(loc['r0'], ls), :]
            o_ref[s, pl.ds(loc['r0'], ls), :] = out[s * ls:(s + 1) * ls] * _silu(zz)

    nchunk = lt // ls
    per_trip = min(CHUNKS_PER_TRIP_B, nchunk)

    def trip(i, carry):
        locs = [local(i * per_trip + j) for j in range(per_trip)]
        tinvs = _inv_unit_lower([grp['neg_aab'] for loc in locs for grp in loc['groups']], ls, SOLVE_MM_B)
        for j, loc in enumerate(locs):
            advance(loc, tinvs[j * ngrp:(j + 1) * ngrp])
        return carry

    lax.fori_loop(0, nchunk // per_trip, trip, 0)

    @pl.when(l_idx == pl.num_programs(1) - 1)
    def _():
        for s in range(nseg):
            for h in range(H_B):
                g, j = divmod(h, 4)
                sout_ref[s, h] = st_ref[s, g, j * N_B:(j + 1) * N_B, j * N_B:(j + 1) * N_B]


def _rwkv(proj3, prev8, s0, layer, vecs, w2p, a2p, bd, *, nseg, lt, ls, l_valid):
    b, l, _ = proj3.shape
    ngrp = W_BRANCH // GRP
    kern = functools.partial(_rwkv_kernel, nseg=nseg, lt=lt, ls=ls, l_valid=l_valid, l_total=l)
    full = lambda shape: pl.BlockSpec(shape, lambda i, j: (0,) * len(shape))
    blk = lambda col, w: pl.BlockSpec((nseg, lt, w), lambda i, j: (i, j, col // w))
    mu, w0, a0, kkp, kap, rk, gg, gb = vecs
    return pl.pallas_call(
        kern,
        grid=(b // nseg, l // lt),
        in_specs=[blk(C_RKV, W_BRANCH), blk(C_RKV + W_BRANCH, W_BRANCH), blk(C_RKV + 2 * W_BRANCH, W_BRANCH),
                  blk(C_WLAL, 2 * LORA), blk(C_Z_B, W_BRANCH),
                  pl.BlockSpec((nseg, HIST, SHIFT_B), lambda i, j: (i, 0, 0)),
                  pl.BlockSpec((None, nseg, H_B, N_B, N_B), lambda i, j: (layer, i, 0, 0, 0)),
                  full((1, SHIFT_B)), full((1, W_BRANCH)), full((2 * LORA, W_BRANCH)),
                  full((1, W_BRANCH)), full((2 * LORA, W_BRANCH)), full((1, W_BRANCH)),
                  full((1, W_BRANCH)), full((1, W_BRANCH)), full((1, W_BRANCH)), full((1, W_BRANCH)),
                  full((W_BRANCH, W_BRANCH))],
        out_specs=[pl.BlockSpec((nseg, lt, W_BRANCH), lambda i, j: (i, j, 0)),
                   pl.BlockSpec((nseg, H_B, N_B, N_B), lambda i, j: (i, 0, 0, 0))],
        out_shape=[jax.ShapeDtypeStruct((b, l, W_BRANCH), F32),
                   jax.ShapeDtypeStruct((b, H_B, N_B, N_B), F32)],
        scratch_shapes=[pltpu.VMEM((nseg, lt + HIST, SHIFT_B), F32),
                        pltpu.VMEM((nseg, ngrp, GRP, GRP), F32)],
        compiler_params=pltpu.CompilerParams(dimension_semantics=("arbitrary", "arbitrary"),
                                             vmem_limit_bytes=VMEM_LIMIT),
    )(proj3, proj3, proj3, proj3, proj3, prev8, s0, mu, w0, w2p, a0, a2p, kkp, kap, rk, gg, gb, bd)


def _rope_table_kernel(inv_ref, cos_ref, sin_ref, *, pos0):
    n = cos_ref.shape[0]
    pos = (pos0 + pl.program_id(0) * n + _iota2((n, 1), 0)).astype(F32)
    ang = pos * inv_ref[...]
    cos_ref[...] = jnp.cos(ang)
    sin_ref[...] = jnp.sin(ang)


def _rope_table(inv, n, pos0, tile=512):
    tile = min(tile, n)
    blk = pl.BlockSpec((tile, LANES), lambda i: (i, 0))
    return pl.pallas_call(
        functools.partial(_rope_table_kernel, pos0=pos0),
        grid=(n // tile,),
        in_specs=[pl.BlockSpec((1, LANES), lambda i: (0, 0))],
        out_specs=[blk, blk],
        out_shape=[jax.ShapeDtypeStruct((n, LANES), F32)] * 2,
        compiler_params=pltpu.CompilerParams(dimension_semantics=("arbitrary",), vmem_limit_bytes=VMEM_LIMIT),
    )(inv)


def _qkprep_kernel(q_ref, k_ref, qn_ref, kn_ref, cos_ref, sin_ref, bd_ref, qo_ref, ko_ref):
    cos = cos_ref[...]
    sin = sin_ref[...]
    d = _iota2((1, LANES), 1) % D_C
    half = ROPE_DIM // 2
    c_main = jnp.where(d < ROPE_DIM, cos, 1.0)
    c_up = jnp.where(d < half, -sin, 0.0)
    c_dn = jnp.where((d >= half) & (d < ROPE_DIM), sin, 0.0)
    bd = bd_ref[...]

    def one(x_ref, g_ref, out_ref):
        x = x_ref[...]
        y = x * lax.rsqrt(_head_sum(x * x, bd) * (1.0 / D_C) + NORM_EPS) * g_ref[...]
        for blk in range(W_BRANCH // LANES):
            yb = y[:, blk * LANES:(blk + 1) * LANES]
            up = pltpu.roll(yb, LANES - half, 1)
            dn = pltpu.roll(yb, half, 1)
            out_ref[:, blk * LANES:(blk + 1) * LANES] = yb * c_main + up * c_up + dn * c_dn

    one(q_ref, qn_ref, qo_ref)
    one(k_ref, kn_ref, ko_ref)


def _qkprep(proj2, qn, kn, rope, bd, *, tm):
    t = proj2.shape[0]
    cos, sin = rope
    period = cos.shape[0] // tm
    full = lambda shape: pl.BlockSpec(shape, lambda i: (0,) * len(shape))
    table = pl.BlockSpec((tm, LANES), lambda i: (i % period, 0))
    return pl.pallas_call(
        _qkprep_kernel,
        grid=(t // tm,),
        in_specs=[pl.BlockSpec((tm, W_BRANCH), lambda i: (i, C_QKV_C // W_BRANCH)),
                  pl.BlockSpec((tm, W_BRANCH), lambda i: (i, C_QKV_C // W_BRANCH + 1)),
                  full((1, W_BRANCH)), full((1, W_BRANCH)), table, table, full((W_BRANCH, W_BRANCH))],
        out_specs=[pl.BlockSpec((tm, W_BRANCH), lambda i: (i, 0)),
                   pl.BlockSpec((tm, W_BRANCH), lambda i: (i, 0))],
        out_shape=[jax.ShapeDtypeStruct((t, W_BRANCH), F32), jax.ShapeDtypeStruct((t, W_BRANCH), F32)],
        compiler_params=pltpu.CompilerParams(dimension_semantics=("arbitrary",), vmem_limit_bytes=VMEM_LIMIT),
    )(proj2, proj2, qn, kn, cos, sin, bd)


def _rank_select(gate, nblk, limit, axis):
    idx = _iota2(gate.shape, axis)
    rank = jnp.zeros(gate.shape, F32)
    for m in range(nblk):
        gm = gate[m:m + 1, :] if axis == 0 else gate[:, m:m + 1]
        beats = (gm > gate) | ((gm == gate) & (m < idx))
        rank = rank + jnp.where(beats & (m < limit), 1.0, 0.0)
    return jnp.where((idx < limit) & (rank < MOBA_TOPK), 1.0, 0.0)


HEADS_PER_STEP = LANES // D_C


def _moba_prompt_kernel(q_ref, k_ref, v_ref, o_ref, vo_ref, *, nblk):
    blk = MOBA_BLOCK
    l = nblk * blk
    scale = D_C ** -0.5
    k = k_ref[...]
    q_t = q_ref[...].T
    v = v_ref[...]
    vo_ref[...] = v
    v_t = v.T.astype(BF16)
    kmean = jnp.concatenate([jnp.mean(k[n * blk:(n + 1) * blk], axis=0, keepdims=True) for n in range(nblk)],
                            axis=0)
    own = _iota2((1, l), 1) // blk
    kb = k.astype(BF16)
    causal = _iota2((blk, blk), 0) <= _iota2((blk, blk), 1)
    row_head = _iota2((LANES, 1), 0) // D_C
    units = [(e, j) for e in range(HEADS_PER_STEP) for j in range(nblk)]
    sels, qbs = [], []
    for e in range(HEADS_PER_STEP):
        q_e = jnp.where(row_head == e, q_t, 0.0)
        sels.append(_rank_select(_mm32(kmean, q_e), nblk, own, 0))
        qbs.append((q_e * scale).astype(BF16))
    scores = [jnp.dot(kb[:(j + 1) * blk], qbs[e][:, j * blk:(j + 1) * blk], preferred_element_type=F32)
              for e, j in units]
    probs, denoms = [], []
    for (e, j), s in zip(units, scores):
        cols = slice(j * blk, (j + 1) * blk)
        parts = [jnp.where(sels[e][n:n + 1, cols] > 0.0, s[n * blk:(n + 1) * blk], -jnp.inf) for n in range(j)]
        parts.append(jnp.where(causal, s[j * blk:], -jnp.inf))
        s = jnp.concatenate(parts, axis=0)
        p = jnp.exp(s - jnp.max(s, axis=0, keepdims=True))
        denoms.append(jnp.sum(p, axis=0, keepdims=True))
        probs.append(p.astype(BF16))
    tiles = [jnp.dot(v_t[e * D_C:(e + 1) * D_C, :(j + 1) * blk], p, preferred_element_type=F32) / d
             for (e, j), p, d in zip(units, probs, denoms)]
    outs = [jnp.concatenate(tiles[e * nblk:(e + 1) * nblk], axis=1) for e in range(HEADS_PER_STEP)]
    o_ref[...] = jnp.concatenate(outs, axis=0).T


def _moba_prompt(q3, k3, proj3):
    b, l, _ = q3.shape
    nblk = l // MOBA_BLOCK
    assert (4.0 ** round(np.log(D_C) / np.log(4.0))) == D_C
    kern = functools.partial(_moba_prompt_kernel, nblk=nblk)
    v_col = (C_QKV_C + 2 * W_BRANCH) // LANES
    cols = pl.BlockSpec((None, l, LANES), lambda i, hp: (i, 0, hp))
    return pl.pallas_call(
        kern,
        grid=(b, W_BRANCH // LANES),
        in_specs=[cols, cols, pl.BlockSpec((None, l, LANES), lambda i, hp: (i, 0, v_col + hp))],
        out_specs=[cols, cols],
        out_shape=[jax.ShapeDtypeStruct((b, l, W_BRANCH), F32)] * 2,
        compiler_params=pltpu.CompilerParams(dimension_semantics=("arbitrary", "arbitrary"),
                                             vmem_limit_bytes=VMEM_LIMIT),
    )(q3, k3, proj3)


SEQ_PER_STEP = 2


def _moba_decode_kernel(pt_ref, q_ref, kn_ref, vn_ref, *refs, npage, page, lq):
    nseq = q_ref.shape[0]
    k_refs = refs[:nseq * npage]
    v_refs = refs[nseq * npage:2 * nseq * npage]
    o_ref = refs[2 * nseq * npage]
    del pt_ref
    scale = D_C ** -0.5
    nrow = lq * H_C
    nblk = npage * page // MOBA_BLOCK
    lp = q_ref.shape[1]
    seqs = range(nseq)

    lane_head = _iota2((H_C, W_BRANCH), 1) // D_C
    own_head = (lane_head == _iota2((H_C, W_BRANCH), 0)).astype(F32)
    own_rows = jnp.concatenate([own_head] * lq, axis=0)
    lane = _iota2((1, LANES), 1)
    t_key = _iota2((nrow, LANES), 1)
    t_qry = _iota2((nrow, LANES), 0) // H_C
    pad = jnp.zeros((LANES - lp, W_BRANCH), F32)

    qbds = [jnp.concatenate([q_ref[r, t:t + 1, :] * own_head for t in range(lq)], axis=0) for r in seqs]
    qbs = [(qbd * scale).astype(BF16) for qbd in qbds]
    k_ts = [jnp.concatenate([k_refs[r * npage + i][...].reshape(W_BRANCH, page) for i in range(npage)], axis=1)
            for r in seqs]
    v_ts = [jnp.concatenate([v_refs[r * npage + i][...].reshape(W_BRANCH, page) for i in range(npage)], axis=1)
            for r in seqs]
    s_pasts = [jnp.dot(qb, k_t.astype(BF16), preferred_element_type=F32) for qb, k_t in zip(qbs, k_ts)]

    def block_key_sums(k_t):
        ksum = jnp.zeros((W_BRANCH, LANES), F32)
        for n in range(nblk):
            k_n = k_t[:, n * MOBA_BLOCK:(n + 1) * MOBA_BLOCK]
            folded = sum(k_n[:, i * LANES:(i + 1) * LANES] for i in range(MOBA_BLOCK // LANES))
            ksum = ksum + jnp.sum(folded, axis=-1, keepdims=True) * (lane == n).astype(F32)
        return ksum

    ksums = [block_key_sums(k_t) for k_t in k_ts]
    gates = [_mm3(qbd, ksum) * (1.0 / MOBA_BLOCK) for qbd, ksum in zip(qbds, ksums)]
    sels = [_rank_select(gate, nblk, nblk, 1) for gate in gates]
    s_pasts = [jnp.concatenate(
        [jnp.where(sel[:, n:n + 1] > 0.0, s[:, n * MOBA_BLOCK:(n + 1) * MOBA_BLOCK], -jnp.inf)
         for n in range(nblk)], axis=1) for sel, s in zip(sels, s_pasts)]

    k_news = [jnp.concatenate([kn_ref[r], pad], axis=0) for r in seqs]
    v_news = [jnp.concatenate([vn_ref[r], pad], axis=0) for r in seqs]
    s_owns = [jnp.where(t_key <= t_qry, lax.dot_general(qb, k_new.astype(BF16), NT, preferred_element_type=F32),
                        -jnp.inf) for qb, k_new in zip(qbs, k_news)]
    ms = [jnp.maximum(jnp.max(sp, axis=-1, keepdims=True), jnp.max(so, axis=-1, keepdims=True))
          for sp, so in zip(s_pasts, s_owns)]
    p_pasts = [jnp.exp(sp - m) for sp, m in zip(s_pasts, ms)]
    p_owns = [jnp.exp(so - m) for so, m in zip(s_owns, ms)]
    denoms = [jnp.sum(pp, axis=-1, keepdims=True) + jnp.sum(po, axis=-1, keepdims=True)
              for pp, po in zip(p_pasts, p_owns)]
    accs = [lax.dot_general(pp.astype(BF16), v_t.astype(BF16), NT, preferred_element_type=F32)
            for pp, v_t in zip(p_pasts, v_ts)]
    accs = [acc + jnp.dot(po.astype(BF16), v_new.astype(BF16), preferred_element_type=F32)
            for acc, po, v_new in zip(accs, p_owns, v_news)]
    for r in seqs:
        out = accs[r] / denoms[r]
        out = (out * own_rows).reshape(lq, H_C, W_BRANCH).sum(axis=1)
        o_ref[r] = jnp.concatenate([out, jnp.zeros((lp - lq, W_BRANCH), F32)], axis=0)


def _moba_decode(page_table, q3, k3, v3, cache_k, cache_v, layer, lq):
    b, lp, _ = q3.shape
    npage = page_table.shape[1]
    page = cache_k.shape[-1]
    nseq = SEQ_PER_STEP
    kern = functools.partial(_moba_decode_kernel, npage=npage, page=page, lq=lq)
    tok = pl.BlockSpec((nseq, lp, W_BRANCH), lambda i, pt: (i, 0, 0))

    def page_spec(r, p_i):
        return pl.BlockSpec((None, None, H_C, D_C, page), lambda i, pt: (layer, pt[i * nseq + r, p_i], 0, 0, 0))

    pages = [page_spec(r, p_i) for r in range(nseq) for p_i in range(npage)]
    grid_spec = pltpu.PrefetchScalarGridSpec(
        num_scalar_prefetch=1,
        grid=(b // nseq,),
        in_specs=[tok, tok, tok] + pages * 2,
        out_specs=pl.BlockSpec((nseq, lp, W_BRANCH), lambda i, pt: (i, 0, 0)),
    )
    return pl.pallas_call(
        kern,
        grid_spec=grid_spec,
        out_shape=jax.ShapeDtypeStruct((b, lp, W_BRANCH), F32),
        compiler_params=pltpu.CompilerParams(dimension_semantics=("arbitrary",), vmem_limit_bytes=VMEM_LIMIT),
    )(page_table, q3, k3, v3, *([cache_k] * (nseq * npage)), *([cache_v] * (nseq * npage)))


def _merge_kernel(oa_ref, ob_ref, oc_ref, zc_ref, gate_ref, x_ref, ple_ref, wb_ref, bg_ref, wo_ref,
                  wpg_ref, bpg_ref, wple_ref, y_ref):
    branches = (oa_ref[...], ob_ref[...], oc_ref[...] * _silu(zc_ref[...]))
    up = None
    for n, o in enumerate(branches):
        sl = slice(n * D_MODEL, (n + 1) * D_MODEL)
        gate = _sigmoid(gate_ref[:, sl] + bg_ref[:, sl])
        term = gate * jnp.dot(o.astype(BF16), wb_ref[n], preferred_element_type=F32)
        up = term if up is None else up + term
    y = x_ref[...] + jnp.dot(up.astype(BF16), wo_ref[...], preferred_element_type=F32)
    pg = _sigmoid(jnp.dot(y.astype(BF16), wpg_ref[...], preferred_element_type=F32) + bpg_ref[...])
    y_ref[...] = y + pg * jnp.dot(ple_ref[...].astype(BF16), wple_ref[...], preferred_element_type=F32)


def _merge(oa, ob, oc, proj2, x2, ple3, ple_layer, wb, bg, wo, wpg, bpg, wple, tm):
    t = x2.shape[0]
    full = lambda shape: pl.BlockSpec(shape, lambda i: (0,) * len(shape))
    row = lambda w: pl.BlockSpec((tm, w), lambda i: (i, 0))
    return pl.pallas_call(
        _merge_kernel,
        grid=(t // tm,),
        in_specs=[row(W_BRANCH), row(W_BRANCH), row(W_BRANCH),
                  pl.BlockSpec((tm, W_BRANCH), lambda i: (i, C_Z_C // W_BRANCH)),
                  pl.BlockSpec((tm, N_BRANCH * D_MODEL), lambda i: (i, C_GATE // (N_BRANCH * D_MODEL))),
                  row(D_MODEL), pl.BlockSpec((None, tm, D_PLE), lambda i: (ple_layer, i, 0)),
                  full((N_BRANCH, W_BRANCH, D_MODEL)), full((1, N_BRANCH * D_MODEL)), full((D_MODEL, D_MODEL)),
                  full((D_MODEL, D_MODEL)), full((1, D_MODEL)), full((D_PLE, D_MODEL))],
        out_specs=row(D_MODEL),
        out_shape=jax.ShapeDtypeStruct((t, D_MODEL), F32),
        compiler_params=pltpu.CompilerParams(dimension_semantics=("arbitrary",), vmem_limit_bytes=VMEM_LIMIT),
    )(oa, ob, oc, proj2, proj2, x2, ple3, wb, bg, wo, wpg, bpg, wple)


def _arrange_w_in(w):
    offs = np.cumsum((0,) + IN_SPLITS)
    qkv_a, z_a, beta, alpha, f_b, z_b, qkv_c, z_c, gate = [w[:, offs[i]:offs[i + 1]] for i in range(9)]
    d = w.shape[0]
    ba = jnp.concatenate([beta, alpha, jnp.zeros((d, LANES - 2 * H_A), w.dtype)], axis=1)
    tail = jnp.zeros((d, N_PROJ - C_BA - LANES), w.dtype)
    out = jnp.concatenate([qkv_a, z_a, f_b[:, :3 * W_BRANCH], z_b, qkv_c, z_c, gate,
                           f_b[:, 3 * W_BRANCH:], ba, tail], axis=1)
    return out.astype(BF16)


def _rope_frequencies():
    half = ROPE_DIM // 2
    inv = ROPE_THETA ** (-jnp.arange(half, dtype=F32) / half)
    d = np.arange(LANES) % D_C
    return jnp.where(d < ROPE_DIM, inv[d % half], 0.0).reshape(1, LANES)


def _layer_params(i, p):
    row = lambda v: v.reshape(1, -1)
    lanes_ba = lambda v: jnp.zeros((1, LANES), F32).at[0, H_A:2 * H_A].set(v)
    lora_pad = jnp.zeros((LORA, W_BRANCH), F32)
    head = np.arange(W_BRANCH) // N_B
    return dict(
        norm_g=row(p['norm_g'][i]),
        w_in=_arrange_w_in(p['w_in'][i]),
        conv=p['conv_a'][i],
        pa=lanes_ba(p['a_log'][i]), pb=lanes_ba(p['dt_bias'][i]), onorm=row(p['onorm_a'][i]),
        rw_vecs=(row(p['mu_b'][i]), row(p['w0_b'][i]), row(p['a0_b'][i]), row(p['kk_b'][i]), row(p['ka_b'][i]),
                 row(p['rk_b'][i]), row(p['gn_g_b'][i]), row(p['gn_b_b'][i])),
        w2p=jnp.concatenate([p['w2_b'][i], lora_pad], axis=0).astype(BF16),
        a2p=jnp.concatenate([lora_pad, p['a2_b'][i]], axis=0).astype(BF16),
        bd=jnp.asarray(head[:, None] == head[None, :], BF16),
        qn=row(jnp.tile(p['qn_c'][i], H_C)), kn=row(jnp.tile(p['kn_c'][i], H_C)),
        wb=p['w_branch'][i].astype(BF16), bg=row(p['b_gate'][i]), wo=p['w_out'][i].astype(BF16),
        wpg=p['w_pg'][i].astype(BF16), bpg=row(p['b_pg'][i]), wple=p['w_ple'][i].astype(BF16),
    )


def _hist_rows(rows):
    b, n, c = rows.shape
    return jnp.concatenate([jnp.zeros((b, HIST - n, c), F32), rows], axis=1)


def _layer(x3, ple3, ple_layer, lp, *, l_valid, rope, dn_s, rw_s, state_layer, dn_buf, rw_prev, nseg, lt, ls, tm, attend):
    b, l, d = x3.shape
    t = b * l
    proj2 = _proj(x3.reshape(t, d), lp['norm_g'], lp['w_in'], min(t, 1024), 1920)
    proj3 = proj2.reshape(b, l, N_PROJ)

    oa, dn_new = _gdn(proj3, _hist_rows(dn_buf), dn_s, state_layer, lp['conv'], lp['pa'], lp['pb'], lp['onorm'],
                      nseg=nseg, lt=lt, ls=ls, l_valid=l_valid)
    ob, rw_new = _rwkv(proj3, _hist_rows(rw_prev[:, None, :]), rw_s, state_layer, lp['rw_vecs'],
                       lp['w2p'], lp['a2p'], lp['bd'], nseg=nseg, lt=lt, ls=ls, l_valid=l_valid)
    qr, kr = _qkprep(proj2, lp['qn'], lp['kn'], rope, lp['bd'], tm=min(tm, t))
    oc, v2 = attend(qr, kr, proj3)

    y2 = _merge(oa.reshape(t, W_BRANCH), ob.reshape(t, W_BRANCH), oc.reshape(t, W_BRANCH), proj2,
                x3.reshape(t, d), ple3, ple_layer, lp['wb'], lp['bg'], lp['wo'], lp['wpg'],
                lp['bpg'], lp['wple'], min(tm, t))

    f_last = jnp.concatenate([proj3[:, l_valid - 1, C_RKV:C_RKV + 3 * W_BRANCH],
                              proj3[:, l_valid - 1, C_WLAL:C_WLAL + 2 * LORA]], axis=-1)
    new_state = (kr.reshape(b, l, H_C, D_C)[:, :l_valid], v2.reshape(b, l, H_C, D_C)[:, :l_valid], dn_new,
                 proj3[:, l_valid - (CONV_W - 1):l_valid, C_QKV_A:C_QKV_A + QKV_A],
                 rw_new, f_last)
    return y2.reshape(b, l, d), new_state


def _attend_prompt(b, l):
    def attend(qr, kr, proj3):
        shp = (b, l, W_BRANCH)
        o3, v3 = _moba_prompt(qr.reshape(shp), kr.reshape(shp), proj3)
        return o3.reshape(b * l, W_BRANCH), v3.reshape(b * l, W_BRANCH)
    return attend


def _attend_decode(b, lp_, lq, page_table, cache_k4, cache_v4, layer):
    def attend(qr, kr, proj3):
        shp = (b, lp_, W_BRANCH)
        v3 = proj3[:, :, C_QKV_C + 2 * W_BRANCH:C_QKV_C + 3 * W_BRANCH]
        o3 = _moba_decode(page_table, qr.reshape(shp), kr.reshape(shp), v3, cache_k4, cache_v4, layer, lq)
        return o3.reshape(b * lp_, W_BRANCH), v3.reshape(b * lp_, W_BRANCH)
    return attend


DEC_PAD = 8


def kernel(x_prompt, x_sample, cache_k, cache_v, state_dn, state_dn_conv, state_rwkv, state_rwkv_shift,
           page_table, p_prompt, p_sample, norm_g, w_in, conv_a, a_log, dt_bias, onorm_a, mu_b, w0_b, w2_b,
           a0_b, a2_b, kk_b, ka_b, rk_b, gn_g_b, gn_b_b, qn_c, kn_c, w_branch, b_gate, w_out, w_ple, w_pg,
           b_pg):
    params = dict(norm_g=norm_g, w_in=w_in, conv_a=conv_a, a_log=a_log, dt_bias=dt_bias, onorm_a=onorm_a,
                  mu_b=mu_b, w0_b=w0_b, w2_b=w2_b, a0_b=a0_b, a2_b=a2_b, kk_b=kk_b, ka_b=ka_b, rk_b=rk_b,
                  gn_g_b=gn_g_b, gn_b_b=gn_b_b, qn_c=qn_c, kn_c=kn_c, w_branch=w_branch, b_gate=b_gate,
                  w_out=w_out, w_ple=w_ple, w_pg=w_pg, b_pg=b_pg)
    depth = norm_g.shape[0]
    b_p, l_p, _ = x_prompt.shape
    b_s, l_s, _ = x_sample.shape
    n_phys, page = cache_k.shape[1], cache_k.shape[2]
    past_len = page_table.shape[1] * page
    assert l_p % 512 == 0 and CONV_W - 1 <= l_s <= DEC_PAD and b_s % (CHUNK // DEC_PAD) == 0
    assert past_len % MOBA_BLOCK == 0 and MOBA_BLOCK % page == 0

    pad_rows = lambda a: jnp.pad(a, ((0, 0), (0, DEC_PAD - l_s), (0, 0)))
    xs = pad_rows(x_sample)
    cache_k4 = cache_k.transpose(0, 1, 3, 4, 2)
    cache_v4 = cache_v.transpose(0, 1, 3, 4, 2)
    inv = _rope_frequencies()
    tm_s = min(512, b_s * DEC_PAD)
    rope_p = _rope_table(inv, l_p, 0)
    rope_s = tuple(jnp.tile(tab, (tm_s // DEC_PAD, 1)) for tab in _rope_table(inv, DEC_PAD, past_len))
    xp = x_prompt
    new_p, new_s = [], []
    for i in range(depth):
        lp = _layer_params(i, params)
        xp, st_p = _layer(xp, p_prompt.reshape(depth, b_p * l_p, D_PLE), i, lp, l_valid=l_p, rope=rope_p,
                          dn_s=jnp.zeros((1, b_p, H_A, DK_A, DV_A), F32),
                          rw_s=jnp.zeros((1, b_p, H_B, N_B, N_B), F32), state_layer=0,
                          dn_buf=jnp.zeros((b_p, CONV_W - 1, QKV_A), F32), rw_prev=jnp.zeros((b_p, SHIFT_B), F32),
                          nseg=1, lt=512, ls=CHUNK, tm=512, attend=_attend_prompt(b_p, l_p))
        new_p.append(st_p)
        xs, st_s = _layer(xs, pad_rows(p_sample[i]).reshape(1, b_s * DEC_PAD, D_PLE), 0, lp, l_valid=l_s, rope=rope_s,
                          dn_s=state_dn, rw_s=state_rwkv, state_layer=i, dn_buf=state_dn_conv[i],
                          rw_prev=state_rwkv_shift[i], nseg=CHUNK // DEC_PAD, lt=DEC_PAD, ls=DEC_PAD, tm=512,
                          attend=_attend_decode(b_s, DEC_PAD, l_s, page_table, cache_k4, cache_v4, i))
        new_s.append(st_s)
    outs_p = [jnp.stack(t) for t in zip(*new_p)]
    outs_s = [jnp.stack(t) for t in zip(*new_s)]
    return (xp, xs[:, :l_s], *outs_p, *outs_s)
```

```python
import functools

import numpy as np
import jax
import jax.numpy as jnp
from jax import lax
from jax.experimental import pallas as pl
from jax.experimental.pallas import tpu as pltpu

F32 = jnp.float32
BF16 = jnp.bfloat16

D_MODEL = 1024
D_PLE = 256
N_BRANCH = 3
W_BRANCH = 512
NORM_EPS = 1e-6
DK_A = 128
DV_A = 128
H_A = 4
QKV_A = 1536
CONV_W = 4
N_B = 64
H_B = 8
LORA = 64
SHIFT_B = 3 * W_BRANCH + 2 * LORA
GN_EPS = 64e-5
D_C = 64
H_C = 8
MOBA_BLOCK = 256
MOBA_TOPK = 3
ROPE_DIM = 16
ROPE_THETA = 500000.0
IN_SPLITS = (QKV_A, W_BRANCH, H_A, H_A, SHIFT_B, W_BRANCH, 3 * W_BRANCH, W_BRANCH, N_BRANCH * D_MODEL)

C_QKV_A = 0
C_Z_A = 1536
C_RKV = 2048
C_Z_B = 3584
C_QKV_C = 4096
C_Z_C = 5632
C_GATE = 6144
C_WLAL = 9216
C_BA = 9344
N_PROJ = 9600

CHUNK = 64
HIST = 8
CHUNKS_PER_TRIP_A = 8
CHUNKS_PER_TRIP_B = 8
LANES = 128
VMEM_LIMIT = 56 * 1024 * 1024

NN = (((1,), (0,)), ((), ()))
BNN = (((2,), (1,)), ((0,), (0,)))
NT = (((1,), (1,)), ((), ()))
TN = (((0,), (0,)), ((), ()))
HIGHEST = lax.Precision.HIGHEST


def _mm(a, b, dims=NN):
    return lax.dot_general(a.astype(BF16), b.astype(BF16), dims, preferred_element_type=F32)


def _mm32(a, b, dims=NN):
    return lax.dot_general(a, b, dims, precision=HIGHEST, preferred_element_type=F32)


def _sigmoid(x):
    return 1.0 / (1.0 + jnp.exp(-x))


def _silu(x):
    return x * _sigmoid(x)


def _softplus(x):
    return jnp.maximum(x, 0.0) + jnp.log1p(jnp.exp(-jnp.abs(x)))


def _iota2(shape, dim):
    return lax.broadcasted_iota(jnp.int32, shape, dim)


def _split(x):
    hi = x.astype(BF16)
    return hi, (x - hi.astype(F32)).astype(BF16)


def _dotb(a, b, dims=NN):
    return lax.dot_general(a, b, dims, preferred_element_type=F32)


def _mm_sel(sel, x):
    hi, lo = _split(x)
    return _dotb(sel, hi) + _dotb(sel, lo)


def _mm3(a, b, dims=NN):
    ah, al = _split(a)
    bh, bl = _split(b)
    return _dotb(ah, bh, dims) + (_dotb(ah, bl, dims) + _dotb(al, bh, dims))


SOLVE_MM_A = _mm
SOLVE_MM_B = _mm


def _inv_unit_lower(lmats, nilpotent, mm):
    n = lmats[0].shape[-1]
    dims = NN if lmats[0].ndim == 2 else BNN
    eye = (_iota2((n, n), 0) == _iota2((n, n), 1)).astype(F32)
    ps = [-lmat for lmat in lmats]
    accs = [eye + p for p in ps]
    span = 2
    while span < nilpotent:
        ps = [mm(p, p, dims) for p in ps]
        accs = [acc + mm(acc, p, dims) for acc, p in zip(accs, ps)]
        span *= 2
    return accs


def _head_sums(xs, bd):
    n = xs[0].shape[0]
    out = jnp.dot(jnp.concatenate([x.astype(BF16) for x in xs], axis=0), bd, preferred_element_type=F32)
    return [out[i * n:(i + 1) * n] for i in range(len(xs))]


def _head_sum(x, bd):
    return _head_sums([x], bd)[0]


def _proj_kernel(x_ref, g_ref, w_ref, o_ref):
    x = x_ref[...]
    h = x * lax.rsqrt(jnp.mean(x * x, axis=-1, keepdims=True) + NORM_EPS) * g_ref[...]
    o_ref[...] = jnp.dot(h.astype(BF16), w_ref[...], preferred_element_type=F32)


def _proj(x2d, g, w, tm, tn):
    t, d = x2d.shape
    n = w.shape[1]
    return pl.pallas_call(
        _proj_kernel,
        grid=(n // tn, t // tm),
        in_specs=[pl.BlockSpec((tm, d), lambda j, i: (i, 0)),
                  pl.BlockSpec((1, d), lambda j, i: (0, 0)),
                  pl.BlockSpec((d, tn), lambda j, i: (0, j))],
        out_specs=pl.BlockSpec((tm, tn), lambda j, i: (i, j)),
        out_shape=jax.ShapeDtypeStruct((t, n), F32),
        compiler_params=pltpu.CompilerParams(dimension_semantics=("arbitrary", "arbitrary"),
                                             vmem_limit_bytes=VMEM_LIMIT),
    )(x2d, g, w)


def _chunk_masks(n, ls):
    r = _iota2((n, n), 0)
    c = _iota2((n, n), 1)
    same = (r // ls) == (c // ls)
    return same, same & (r >= c), same & (r > c)


def _valid_rows(l_idx, lt, r0, ls, l_valid):
    t = l_idx * lt + r0 + (_iota2((CHUNK, 1), 0) % ls)
    return t < l_valid


def _gdn_kernel(qkv_ref, z_ref, ba_ref, buf_ref, s0_ref, cw_ref, pa_ref, pb_ref, on_ref,
                o_ref, sout_ref, xs_ref, st_ref, *, nseg, lt, ls, l_valid, l_total):
    l_idx = pl.program_id(1)
    nrow = H_A * CHUNK

    @pl.when(l_idx == 0)
    def _():
        xs_ref[:, 0:HIST, :] = buf_ref[...]
        st_ref[...] = s0_ref[...]

    @pl.when(l_idx > 0)
    def _():
        xs_ref[:, 0:HIST, :] = xs_ref[:, lt:lt + HIST, :]

    xs_ref[:, HIST:lt + HIST, :] = qkv_ref[...]

    same_r, c_incl, c_strict = _chunk_masks(CHUNK, ls)
    tri_r = c_incl.astype(BF16)
    same_r = same_r.astype(BF16)
    eye_c = (_iota2((CHUNK, CHUNK), 0) == _iota2((CHUNK, CHUNK), 1)).astype(F32)
    cw = cw_ref[...]
    neg_a = -jnp.exp(pa_ref[...])
    pb = pb_ref[...]
    onorm = on_ref[...]

    def conv(s, r0, col):
        win = xs_ref[s, pl.ds(r0, ls + HIST), col:col + LANES]
        w = cw[:, col:col + LANES]
        acc = pltpu.roll(win, 3, 0)[HIST:] * w[0:1] + pltpu.roll(win, 2, 0)[HIST:] * w[1:2]
        acc = acc + pltpu.roll(win, 1, 0)[HIST:] * w[2:3]
        acc = acc + win[HIST:] * w[3:4]
        return _silu(acc)

    def part(r0, col):
        return jnp.concatenate([conv(s, r0, col) for s in range(nseg)], axis=0)

    def local(c):
        r0 = pl.multiple_of(c * ls, ls)
        ba = jnp.concatenate([ba_ref[s, pl.ds(r0, ls), :] for s in range(nseg)], axis=0)
        beta_all = _sigmoid(ba)
        g_all = neg_a * _softplus(ba + pb)
        if l_valid < l_total:
            ok = _valid_rows(l_idx, lt, r0, ls, l_valid)
            beta_all = jnp.where(ok, beta_all, 0.0)
            g_all = jnp.where(ok, g_all, 0.0)
        gcum = _mm_sel(tri_r, g_all)
        if nseg == 1:
            gtot = jnp.broadcast_to(gcum[CHUNK - 1:CHUNK], gcum.shape)
        else:
            gtot = _mm_sel(same_r, g_all)

        qs, ks, vs, betas, gcs, gts, lmats, qks = [], [], [], [], [], [], [], []
        for h in range(H_A):
            q = part(r0, h * DK_A)
            k = part(r0, H_A * DK_A + h * DK_A)
            v = part(r0, 2 * H_A * DK_A + h * DV_A)
            q = q * lax.rsqrt(jnp.sum(q * q, axis=-1, keepdims=True) + 1e-6) * (DK_A ** -0.5)
            k = k * lax.rsqrt(jnp.sum(k * k, axis=-1, keepdims=True) + 1e-6)
            beta_h = beta_all[:, h:h + 1]
            g_h = gcum[:, H_A + h:H_A + h + 1]
            g_row = jnp.sum(eye_c * g_h, axis=0, keepdims=True)
            decay = jnp.exp(jnp.where(c_incl, g_h - g_row, -jnp.inf))
            kq_k = _mm(jnp.concatenate([k, q], axis=0), k, NT)
            lmats.append(jnp.where(c_strict, beta_h * kq_k[:CHUNK] * decay, 0.0))
            qks.append(jnp.where(c_incl, kq_k[CHUNK:] * decay, 0.0))
            qs.append(q)
            ks.append(k)
            vs.append(v)
            betas.append(beta_h)
            gcs.append(g_h)
            gts.append(gtot[:, H_A + h:H_A + h + 1])
        q_st = jnp.concatenate(qs, axis=0)
        k_st = jnp.concatenate(ks, axis=0)
        v_st = jnp.concatenate(vs, axis=0)
        beta = jnp.concatenate(betas, axis=0)
        g_col = jnp.concatenate(gcs, axis=0)
        gt_col = jnp.concatenate(gts, axis=0)
        return dict(
            r0=r0, gt_col=gt_col,
            lmat=jnp.stack(lmats),
            rhs=jnp.concatenate([v_st * beta, k_st * (beta * jnp.exp(g_col))], axis=1),
            qk=jnp.stack(qks),
            qg=q_st * jnp.exp(g_col),
            kd=k_st * jnp.exp(gt_col - g_col))

    def advance(loc, sol):
        r0, gt_col, qk, qg, kd = loc['r0'], loc['gt_col'], loc['qk'], loc['qg'], loc['kd']
        bv = sol[:, :DV_A]
        wk = sol[:, DV_A:]
        wks, qgs = [], []
        for h in range(H_A):
            for s in range(nseg):
                a = h * CHUNK + s * ls
                both = _mm(jnp.concatenate([wk[a:a + ls], qg[a:a + ls]], axis=0), st_ref[s, h])
                wks.append(both[:ls])
                qgs.append(both[ls:])
        u = bv - jnp.concatenate(wks, axis=0)
        o = jnp.concatenate(qgs, axis=0) + _mm(qk, u.reshape(H_A, CHUNK, DV_A), BNN).reshape(nrow, DV_A)

        for h in range(H_A):
            a = h * CHUNK
            for s in range(nseg):
                b = a + s * ls
                if nseg == 1:
                    kd_s = kd[a:a + CHUNK]
                else:
                    rows = _iota2((CHUNK, 1), 0) // ls
                    kd_s = jnp.where(rows == s, kd[a:a + CHUNK], 0.0)
                upd = _mm(kd_s, u[a:a + CHUNK], TN)
                st_ref[s, h] = st_ref[s, h] * jnp.exp(gt_col[b:b + 1]) + upd
            oh = o[a:a + CHUNK]
            oh = oh * lax.rsqrt(jnp.mean(oh * oh, axis=-1, keepdims=True) + NORM_EPS) * onorm
            for s in range(nseg):
                zz = z_ref[s, pl.ds(r0, ls), h * DV_A:(h + 1) * DV_A]
                o_ref[s, pl.ds(r0, ls), h * DV_A:(h + 1) * DV_A] = oh[s * ls:(s + 1) * ls] * _silu(zz)

    nchunk = lt // ls
    per_trip = min(CHUNKS_PER_TRIP_A, nchunk)

    def trip(i, carry):
        locs = [local(i * per_trip + j) for j in range(per_trip)]
        ainvs = _inv_unit_lower([loc['lmat'] for loc in locs], ls, SOLVE_MM_A)
        sols = [SOLVE_MM_A(ainv, loc['rhs'].reshape(H_A, CHUNK, DK_A + DV_A), BNN).reshape(nrow, DK_A + DV_A)
                for loc, ainv in zip(locs, ainvs)]
        for loc, sol in zip(locs, sols):
            advance(loc, sol)
        return carry

    lax.fori_loop(0, nchunk // per_trip, trip, 0)

    @pl.when(l_idx == pl.num_programs(1) - 1)
    def _():
        sout_ref[...] = st_ref[...]


def _gdn(proj3, buf8, s0, layer, cw, pa, pb, onorm, *, nseg, lt, ls, l_valid):
    b, l, _ = proj3.shape
    kern = functools.partial(_gdn_kernel, nseg=nseg, lt=lt, ls=ls, l_valid=l_valid, l_total=l)
    full = lambda shape: pl.BlockSpec(shape, lambda i, j: (0,) * len(shape))
    return pl.pallas_call(
        kern,
        grid=(b // nseg, l // lt),
        in_specs=[pl.BlockSpec((nseg, lt, QKV_A), lambda i, j: (i, j, C_QKV_A // QKV_A)),
                  pl.BlockSpec((nseg, lt, W_BRANCH), lambda i, j: (i, j, C_Z_A // W_BRANCH)),
                  pl.BlockSpec((nseg, lt, LANES), lambda i, j: (i, j, C_BA // LANES)),
                  pl.BlockSpec((nseg, HIST, QKV_A), lambda i, j: (i, 0, 0)),
                  pl.BlockSpec((None, nseg, H_A, DK_A, DV_A), lambda i, j: (layer, i, 0, 0, 0)),
                  full((CONV_W, QKV_A)), full((1, LANES)), full((1, LANES)), full((1, DV_A))],
        out_specs=[pl.BlockSpec((nseg, lt, W_BRANCH), lambda i, j: (i, j, 0)),
                   pl.BlockSpec((nseg, H_A, DK_A, DV_A), lambda i, j: (i, 0, 0, 0))],
        out_shape=[jax.ShapeDtypeStruct((b, l, W_BRANCH), F32),
                   jax.ShapeDtypeStruct((b, H_A, DK_A, DV_A), F32)],
        scratch_shapes=[pltpu.VMEM((nseg, lt + HIST, QKV_A), F32),
                        pltpu.VMEM((nseg, H_A, DK_A, DV_A), F32)],
        compiler_params=pltpu.CompilerParams(dimension_semantics=("arbitrary", "arbitrary"),
                                             vmem_limit_bytes=VMEM_LIMIT),
    )(proj3, proj3, proj3, buf8, s0, cw, pa, pb, onorm)


GRP = 4 * N_B


def _rwkv_kernel(r_ref, k_ref, v_ref, wa_ref, z_ref, prev_ref, s0_ref, mu_ref, w0_ref, w2_ref,
                 a0_ref, a2_ref, kkp_ref, kap_ref, rk_ref, gg_ref, gb_ref, bd_ref,
                 o_ref, sout_ref, fs_ref, st_ref, *, nseg, lt, ls, l_valid, l_total):
    l_idx = pl.program_id(1)
    nrow = 4 * CHUNK
    ngrp = W_BRANCH // GRP

    @pl.when(l_idx == 0)
    def _():
        fs_ref[:, 0:HIST, :] = prev_ref[...]
        st_ref[...] = jnp.zeros(st_ref.shape, F32)
        for s in range(nseg):
            for h in range(H_B):
                g, j = divmod(h, 4)
                st_ref[s, g, j * N_B:(j + 1) * N_B, j * N_B:(j + 1) * N_B] = s0_ref[s, h]

    @pl.when(l_idx > 0)
    def _():
        fs_ref[:, 0:HIST, :] = fs_ref[:, lt:lt + HIST, :]

    fs_ref[:, HIST:lt + HIST, 0:W_BRANCH] = r_ref[...]
    fs_ref[:, HIST:lt + HIST, W_BRANCH:2 * W_BRANCH] = k_ref[...]
    fs_ref[:, HIST:lt + HIST, 2 * W_BRANCH:3 * W_BRANCH] = v_ref[...]
    fs_ref[:, HIST:lt + HIST, 3 * W_BRANCH:SHIFT_B] = wa_ref[...]

    same_r, tri_r, _ = _chunk_masks(CHUNK, ls)
    tri_r = tri_r.astype(BF16)
    same_r = same_r.astype(BF16)
    lane_head = _iota2((1, GRP), 1) // N_B
    head_masks = [(lane_head == j).astype(F32) for j in range(4)]
    bd_state = (_iota2((GRP, GRP), 0) // N_B) == (_iota2((GRP, GRP), 1) // N_B)
    head_rows = ((_iota2((nrow, GRP), 0) // CHUNK) == (_iota2((nrow, GRP), 1) // N_B)).astype(F32)
    t_row = _iota2((nrow, 2 * CHUNK), 0) % CHUNK
    t_col = _iota2((nrow, 2 * CHUNK), 1) % CHUNK
    t_same = (t_row // ls) == (t_col // ls)
    t_incl = t_same & (t_col <= t_row)
    t_strict = t_same & (t_col < t_row)
    mu = mu_ref[...]
    bd = bd_ref[...]

    def mixed(r0, col, width):
        outs = []
        for s in range(nseg):
            win = fs_ref[s, pl.ds(r0, ls + HIST), col:col + width]
            cur = win[HIST:]
            prev = pltpu.roll(win, 1, 0)[HIST:]
            outs.append(cur + (prev - cur) * mu[:, col:col + width])
        return jnp.concatenate(outs, axis=0)

    def stack_masked(x):
        return jnp.concatenate([x * head_masks[j] for j in range(4)], axis=0)

    def unstack(x):
        return x[0:CHUNK] + x[CHUNK:2 * CHUNK] + x[2 * CHUNK:3 * CHUNK] + x[3 * CHUNK:4 * CHUNK]

    def local(c):
        r0 = pl.multiple_of(c * ls, ls)
        r = mixed(r0, 0, W_BRANCH)
        k = mixed(r0, W_BRANCH, W_BRANCH)
        v = mixed(r0, 2 * W_BRANCH, W_BRANCH)
        wa = mixed(r0, 3 * W_BRANCH, 2 * LORA)
        w_raw = w0_ref[...] + _mm(jnp.tanh(wa), w2_ref[...])
        logw = -jnp.exp(-_softplus(-w_raw) - 0.5)
        a_sig = _sigmoid(a0_ref[...] + _mm(wa, a2_ref[...]))
        kk = k * kkp_ref[...]
        k_mod = k * (1.0 + (a_sig - 1.0) * kap_ref[...])
        kk_ss, rk_sum = _head_sums([kk * kk, r * k_mod * rk_ref[...]], bd)
        kk = kk * lax.rsqrt(kk_ss + 1e-6)
        b_vec = kk * a_sig
        if l_valid < l_total:
            ok = _valid_rows(l_idx, lt, r0, ls, l_valid)
            logw = jnp.where(ok, logw, 0.0)
            b_vec = jnp.where(ok, b_vec, 0.0)
            k_mod = jnp.where(ok, k_mod, 0.0)
        gcum = _mm_sel(tri_r, logw)
        if nseg == 1:
            gtot = jnp.broadcast_to(gcum[CHUNK - 1:CHUNK], gcum.shape)
        else:
            gtot = _mm_sel(same_r, logw)
        e_neg = jnp.exp(-gcum)
        e_end = jnp.exp(gtot - gcum)
        a_t = -kk * jnp.exp(gcum - logw)
        b_t = b_vec * e_neg
        k_t = k_mod * e_neg
        r_t = r * jnp.exp(gcum)
        b_h = b_vec * e_end
        k_h = k_mod * e_end

        groups = []
        for g in range(ngrp):
            sl = slice(g * GRP, (g + 1) * GRP)
            a_g, b_g, k_g, r_g, v_g = a_t[:, sl], b_t[:, sl], k_t[:, sl], r_t[:, sl], v[:, sl]
            a_st = stack_masked(a_g)
            r_st = stack_masked(r_g)
            both = _mm(jnp.concatenate([a_st, r_st], axis=0), jnp.concatenate([b_g, k_g], axis=0), NT)
            a_bk = jnp.where(t_strict, both[:nrow], 0.0)
            r_bk = jnp.where(t_incl, both[nrow:], 0.0)
            zero_v = jnp.concatenate([jnp.zeros_like(v_g), v_g], axis=0)
            groups.append(dict(
                a_g=a_g, r_g=r_g, v_g=v_g, b_h=b_h[:, sl], k_h=k_h[:, sl], gtot=gtot[:, sl],
                neg_aab=-a_bk[:, :CHUNK].reshape(4, CHUNK, CHUNK),
                kv_u=_mm(a_bk, zero_v) * head_rows, r_bk=r_bk))
        return dict(r0=r0, v=v, rk_sum=rk_sum, groups=groups)

    def advance(loc, tinvs):
        grps = loc['groups']
        ams, rms = [], []
        for g, grp in enumerate(grps):
            parts = [_mm(jnp.concatenate([grp['a_g'][s * ls:(s + 1) * ls], grp['r_g'][s * ls:(s + 1) * ls]], axis=0),
                         st_ref[s, g], NT) for s in range(nseg)]
            ams.append(jnp.concatenate([p[:ls] for p in parts], axis=0))
            rms.append(jnp.concatenate([p[ls:] for p in parts], axis=0))
        us = [unstack(SOLVE_MM_B(tinv, (stack_masked(am) + grp['kv_u']).reshape(4, CHUNK, GRP), BNN)
                      .reshape(nrow, GRP)) for grp, tinv, am in zip(grps, tinvs, ams)]
        ys = [rm + unstack(_mm(grp['r_bk'], jnp.concatenate([u, grp['v_g']], axis=0)) * head_rows)
              for grp, u, rm in zip(grps, us, rms)]
        for g, (grp, u) in enumerate(zip(grps, us)):
            v_g = grp['v_g']
            for s in range(nseg):
                if nseg == 1:
                    u_s, v_s = u, v_g
                else:
                    rows = _iota2((CHUNK, 1), 0) // ls
                    u_s = jnp.where(rows == s, u, 0.0)
                    v_s = jnp.where(rows == s, v_g, 0.0)
                upd = _mm(jnp.concatenate([u_s, v_s], axis=0),
                          jnp.concatenate([grp['b_h'], grp['k_h']], axis=0), TN)
                dec = jnp.exp(grp['gtot'][s * ls:s * ls + 1])
                st_ref[s, g] = st_ref[s, g] * dec + jnp.where(bd_state, upd, 0.0)
        y = jnp.concatenate(ys, axis=1)
        mean = _head_sums([y], bd)[0] * (1.0 / N_B)
        yc = y - mean
        var = _head_sums([yc * yc], bd)[0] * (1.0 / N_B)
        yn = yc * lax.rsqrt(var + GN_EPS) * gg_ref[...] + gb_ref[...]
        out = yn + loc['rk_sum'] * loc['v']
        for s in range(nseg):
            zz = z_ref[s, pl.ds(loc['r0'], ls), :]
            o_ref[s, pl.ds(loc['r0'], ls), :] = out[s * ls:(s + 1) * ls] * _silu(zz)

    nchunk = lt // ls
    per_trip = min(CHUNKS_PER_TRIP_B, nchunk)

    def trip(i, carry):
        locs = [local(i * per_trip + j) for j in range(per_trip)]
        tinvs = _inv_unit_lower([grp['neg_aab'] for loc in locs for grp in loc['groups']], ls, SOLVE_MM_B)
        for j, loc in enumerate(locs):
            advance(loc, tinvs[j * ngrp:(j + 1) * ngrp])
        return carry

    lax.fori_loop(0, nchunk // per_trip, trip, 0)

    @pl.when(l_idx == pl.num_programs(1) - 1)
    def _():
        for s in range(nseg):
            for h in range(H_B):
                g, j = divmod(h, 4)
                sout_ref[s, h] = st_ref[s, g, j * N_B:(j + 1) * N_B, j * N_B:(j + 1) * N_B]


def _rwkv(proj3, prev8, s0, layer, vecs, w2p, a2p, bd, *, nseg, lt, ls, l_valid):
    b, l, _ = proj3.shape
    ngrp = W_BRANCH // GRP
    kern = functools.partial(_rwkv_kernel, nseg=nseg, lt=lt, ls=ls, l_valid=l_valid, l_total=l)
    full = lambda shape: pl.BlockSpec(shape, lambda i, j: (0,) * len(shape))
    blk = lambda col, w: pl.BlockSpec((nseg, lt, w), lambda i, j: (i, j, col // w))
    mu, w0, a0, kkp, kap, rk, gg, gb = vecs
    return pl.pallas_call(
        kern,
        grid=(b // nseg, l // lt),
        in_specs=[blk(C_RKV, W_BRANCH), blk(C_RKV + W_BRANCH, W_BRANCH), blk(C_RKV + 2 * W_BRANCH, W_BRANCH),
                  blk(C_WLAL, 2 * LORA), blk(C_Z_B, W_BRANCH),
                  pl.BlockSpec((nseg, HIST, SHIFT_B), lambda i, j: (i, 0, 0)),
                  pl.BlockSpec((None, nseg, H_B, N_B, N_B), lambda i, j: (layer, i, 0, 0, 0)),
                  full((1, SHIFT_B)), full((1, W_BRANCH)), full((2 * LORA, W_BRANCH)),
                  full((1, W_BRANCH)), full((2 * LORA, W_BRANCH)), full((1, W_BRANCH)),
                  full((1, W_BRANCH)), full((1, W_BRANCH)), full((1, W_BRANCH)), full((1, W_BRANCH)),
                  full((W_BRANCH, W_BRANCH))],
        out_specs=[pl.BlockSpec((nseg, lt, W_BRANCH), lambda i, j: (i, j, 0)),
                   pl.BlockSpec((nseg, H_B, N_B, N_B), lambda i, j: (i, 0, 0, 0))],
        out_shape=[jax.ShapeDtypeStruct((b, l, W_BRANCH), F32),
                   jax.ShapeDtypeStruct((b, H_B, N_B, N_B), F32)],
        scratch_shapes=[pltpu.VMEM((nseg, lt + HIST, SHIFT_B), F32),
                        pltpu.VMEM((nseg, ngrp, GRP, GRP), F32)],
        compiler_params=pltpu.CompilerParams(dimension_semantics=("arbitrary", "arbitrary"),
                                             vmem_limit_bytes=VMEM_LIMIT),
    )(proj3, proj3, proj3, proj3, proj3, prev8, s0, mu, w0, w2p, a0, a2p, kkp, kap, rk, gg, gb, bd)


def _rope_table_kernel(inv_ref, cos_ref, sin_ref, *, pos0):
    n = cos_ref.shape[0]
    pos = (pos0 + pl.program_id(0) * n + _iota2((n, 1), 0)).astype(F32)
    ang = pos * inv_ref[...]
    cos_ref[...] = jnp.cos(ang)
    sin_ref[...] = jnp.sin(ang)


def _rope_table(inv, n, pos0, tile=512):
    tile = min(tile, n)
    blk = pl.BlockSpec((tile, LANES), lambda i: (i, 0))
    return pl.pallas_call(
        functools.partial(_rope_table_kernel, pos0=pos0),
        grid=(n // tile,),
        in_specs=[pl.BlockSpec((1, LANES), lambda i: (0, 0))],
        out_specs=[blk, blk],
        out_shape=[jax.ShapeDtypeStruct((n, LANES), F32)] * 2,
        compiler_params=pltpu.CompilerParams(dimension_semantics=("arbitrary",), vmem_limit_bytes=VMEM_LIMIT),
    )(inv)


def _qkprep_kernel(q_ref, k_ref, qn_ref, kn_ref, cos_ref, sin_ref, bd_ref, qo_ref, ko_ref):
    cos = cos_ref[...]
    sin = sin_ref[...]
    d = _iota2((1, LANES), 1) % D_C
    half = ROPE_DIM // 2
    c_main = jnp.where(d < ROPE_DIM, cos, 1.0)
    c_up = jnp.where(d < half, -sin, 0.0)
    c_dn = jnp.where((d >= half) & (d < ROPE_DIM), sin, 0.0)
    bd = bd_ref[...]

    def one(x_ref, g_ref, out_ref):
        x = x_ref[...]
        y = x * lax.rsqrt(_head_sum(x * x, bd) * (1.0 / D_C) + NORM_EPS) * g_ref[...]
        for blk in range(W_BRANCH // LANES):
            yb = y[:, blk * LANES:(blk + 1) * LANES]
            up = pltpu.roll(yb, LANES - half, 1)
            dn = pltpu.roll(yb, half, 1)
            out_ref[:, blk * LANES:(blk + 1) * LANES] = yb * c_main + up * c_up + dn * c_dn

    one(q_ref, qn_ref, qo_ref)
    one(k_ref, kn_ref, ko_ref)


def _qkprep(proj2, qn, kn, rope, bd, *, tm):
    t = proj2.shape[0]
    cos, sin = rope
    period = cos.shape[0] // tm
    full = lambda shape: pl.BlockSpec(shape, lambda i: (0,) * len(shape))
    table = pl.BlockSpec((tm, LANES), lambda i: (i % period, 0))
    return pl.pallas_call(
        _qkprep_kernel,
        grid=(t // tm,),
        in_specs=[pl.BlockSpec((tm, W_BRANCH), lambda i: (i, C_QKV_C // W_BRANCH)),
                  pl.BlockSpec((tm, W_BRANCH), lambda i: (i, C_QKV_C // W_BRANCH + 1)),
                  full((1, W_BRANCH)), full((1, W_BRANCH)), table, table, full((W_BRANCH, W_BRANCH))],
        out_specs=[pl.BlockSpec((tm, W_BRANCH), lambda i: (i, 0)),
                   pl.BlockSpec((tm, W_BRANCH), lambda i: (i, 0))],
        out_shape=[jax.ShapeDtypeStruct((t, W_BRANCH), F32), jax.ShapeDtypeStruct((t, W_BRANCH), F32)],
        compiler_params=pltpu.CompilerParams(dimension_semantics=("arbitrary",), vmem_limit_bytes=VMEM_LIMIT),
    )(proj2, proj2, qn, kn, cos, sin, bd)


def _rank_select(gate, nblk, limit, axis):
    idx = _iota2(gate.shape, axis)
    rank = jnp.zeros(gate.shape, F32)
    for m in range(nblk):
        gm = gate[m:m + 1, :] if axis == 0 else gate[:, m:m + 1]
        beats = (gm > gate) | ((gm == gate) & (m < idx))
        rank = rank + jnp.where(beats & (m < limit), 1.0, 0.0)
    return jnp.where((idx < limit) & (rank < MOBA_TOPK), 1.0, 0.0)


HEADS_PER_STEP = LANES // D_C


def _moba_prompt_kernel(q_ref, k_ref, v_ref, o_ref, vo_ref, *, nblk):
    blk = MOBA_BLOCK
    l = nblk * blk
    scale = D_C ** -0.5
    k = k_ref[...]
    q_t = q_ref[...].T
    v = v_ref[...]
    vo_ref[...] = v
    v_t = v.T.astype(BF16)
    kmean = jnp.concatenate([jnp.mean(k[n * blk:(n + 1) * blk], axis=0, keepdims=True) for n in range(nblk)],
                            axis=0)
    own = _iota2((1, l), 1) // blk
    kb = k.astype(BF16)
    causal = _iota2((blk, blk), 0) <= _iota2((blk, blk), 1)
    row_head = _iota2((LANES, 1), 0) // D_C
    units = [(e, j) for e in range(HEADS_PER_STEP) for j in range(nblk)]
    sels, qbs = [], []
    for e in range(HEADS_PER_STEP):
        q_e = jnp.where(row_head == e, q_t, 0.0)
        sels.append(_rank_select(_mm32(kmean, q_e), nblk, own, 0))
        qbs.append((q_e * scale).astype(BF16))
    scores = [jnp.dot(kb[:(j + 1) * blk], qbs[e][:, j * blk:(j + 1) * blk], preferred_element_type=F32)
              for e, j in units]
    probs, denoms = [], []
    for (e, j), s in zip(units, scores):
        cols = slice(j * blk, (j + 1) * blk)
        parts = [jnp.where(sels[e][n:n + 1, cols] > 0.0, s[n * blk:(n + 1) * blk], -jnp.inf) for n in range(j)]
        parts.append(jnp.where(causal, s[j * blk:], -jnp.inf))
        s = jnp.concatenate(parts, axis=0)
        p = jnp.exp(s - jnp.max(s, axis=0, keepdims=True))
        denoms.append(jnp.sum(p, axis=0, keepdims=True))
        probs.append(p.astype(BF16))
    tiles = [jnp.dot(v_t[e * D_C:(e + 1) * D_C, :(j + 1) * blk], p, preferred_element_type=F32) / d
             for (e, j), p, d in zip(units, probs, denoms)]
    outs = [jnp.concatenate(tiles[e * nblk:(e + 1) * nblk], axis=1) for e in range(HEADS_PER_STEP)]
    o_ref[...] = jnp.concatenate(outs, axis=0).T


def _moba_prompt(q3, k3, proj3):
    b, l, _ = q3.shape
    nblk = l // MOBA_BLOCK
    assert (4.0 ** round(np.log(D_C) / np.log(4.0))) == D_C
    kern = functools.partial(_moba_prompt_kernel, nblk=nblk)
    v_col = (C_QKV_C + 2 * W_BRANCH) // LANES
    cols = pl.BlockSpec((None, l, LANES), lambda i, hp: (i, 0, hp))
    return pl.pallas_call(
        kern,
        grid=(b, W_BRANCH // LANES),
        in_specs=[cols, cols, pl.BlockSpec((None, l, LANES), lambda i, hp: (i, 0, v_col + hp))],
        out_specs=[cols, cols],
        out_shape=[jax.ShapeDtypeStruct((b, l, W_BRANCH), F32)] * 2,
        compiler_params=pltpu.CompilerParams(dimension_semantics=("arbitrary", "arbitrary"),
                                             vmem_limit_bytes=VMEM_LIMIT),
    )(q3, k3, proj3)


SEQ_PER_STEP = 2


def _moba_decode_kernel(pt_ref, q_ref, kn_ref, vn_ref, *refs, npage, page, lq):
    nseq = q_ref.shape[0]
    k_refs = refs[:nseq * npage]
    v_refs = refs[nseq * npage:2 * nseq * npage]
    o_ref = refs[2 * nseq * npage]
    del pt_ref
    scale = D_C ** -0.5
    nrow = lq * H_C
    nblk = npage * page // MOBA_BLOCK
    lp = q_ref.shape[1]
    seqs = range(nseq)

    lane_head = _iota2((H_C, W_BRANCH), 1) // D_C
    own_head = (lane_head == _iota2((H_C, W_BRANCH), 0)).astype(F32)
    own_rows = jnp.concatenate([own_head] * lq, axis=0)
    lane = _iota2((1, LANES), 1)
    t_key = _iota2((nrow, LANES), 1)
    t_qry = _iota2((nrow, LANES), 0) // H_C
    pad = jnp.zeros((LANES - lp, W_BRANCH), F32)

    qbds = [jnp.concatenate([q_ref[r, t:t + 1, :] * own_head for t in range(lq)], axis=0) for r in seqs]
    qbs = [(qbd * scale).astype(BF16) for qbd in qbds]
    k_ts = [jnp.concatenate([k_refs[r * npage + i][...].reshape(W_BRANCH, page) for i in range(npage)], axis=1)
            for r in seqs]
    v_ts = [jnp.concatenate([v_refs[r * npage + i][...].reshape(W_BRANCH, page) for i in range(npage)], axis=1)
            for r in seqs]
    s_pasts = [jnp.dot(qb, k_t.astype(BF16), preferred_element_type=F32) for qb, k_t in zip(qbs, k_ts)]

    def block_key_sums(k_t):
        ksum = jnp.zeros((W_BRANCH, LANES), F32)
        for n in range(nblk):
            k_n = k_t[:, n * MOBA_BLOCK:(n + 1) * MOBA_BLOCK]
            folded = sum(k_n[:, i * LANES:(i + 1) * LANES] for i in range(MOBA_BLOCK // LANES))
            ksum = ksum + jnp.sum(folded, axis=-1, keepdims=True) * (lane == n).astype(F32)
        return ksum

    ksums = [block_key_sums(k_t) for k_t in k_ts]
    gates = [_mm3(qbd, ksum) * (1.0 / MOBA_BLOCK) for qbd, ksum in zip(qbds, ksums)]
    sels = [_rank_select(gate, nblk, nblk, 1) for gate in gates]
    s_pasts = [jnp.concatenate(
        [jnp.where(sel[:, n:n + 1] > 0.0, s[:, n * MOBA_BLOCK:(n + 1) * MOBA_BLOCK], -jnp.inf)
         for n in range(nblk)], axis=1) for sel, s in zip(sels, s_pasts)]

    k_news = [jnp.concatenate([kn_ref[r], pad], axis=0) for r in seqs]
    v_news = [jnp.concatenate([vn_ref[r], pad], axis=0) for r in seqs]
    s_owns = [jnp.where(t_key <= t_qry, lax.dot_general(qb, k_new.astype(BF16), NT, preferred_element_type=F32),
                        -jnp.inf) for qb, k_new in zip(qbs, k_news)]
    ms = [jnp.maximum(jnp.max(sp, axis=-1, keepdims=True), jnp.max(so, axis=-1, keepdims=True))
          for sp, so in zip(s_pasts, s_owns)]
    p_pasts = [jnp.exp(sp - m) for sp, m in zip(s_pasts, ms)]
    p_owns = [jnp.exp(so - m) for so, m in zip(s_owns, ms)]
    denoms = [jnp.sum(pp, axis=-1, keepdims=True) + jnp.sum(po, axis=-1, keepdims=True)
              for pp, po in zip(p_pasts, p_owns)]
    accs = [lax.dot_general(pp.astype(BF16), v_t.astype(BF16), NT, preferred_element_type=F32)
            for pp, v_t in zip(p_pasts, v_ts)]
    accs = [acc + jnp.dot(po.astype(BF16), v_new.astype(BF16), preferred_element_type=F32)
            for acc, po, v_new in zip(accs, p_owns, v_news)]
    for r in seqs:
        out = accs[r] / denoms[r]
        out = (out * own_rows).reshape(lq, H_C, W_BRANCH).sum(axis=1)
        o_ref[r] = jnp.concatenate([out, jnp.zeros((lp - lq, W_BRANCH), F32)], axis=0)


def _moba_decode(page_table, q3, k3, v3, cache_k, cache_v, layer, lq):
    b, lp, _ = q3.shape
    npage = page_table.shape[1]
    page = cache_k.shape[-1]
    nseq = SEQ_PER_STEP
    kern = functools.partial(_moba_decode_kernel, npage=npage, page=page, lq=lq)
    tok = pl.BlockSpec((nseq, lp, W_BRANCH), lambda i, pt: (i, 0, 0))

    def page_spec(r, p_i):
        return pl.BlockSpec((None, None, H_C, D_C, page), lambda i, pt: (layer, pt[i * nseq + r, p_i], 0, 0, 0))

    pages = [page_spec(r, p_i) for r in range(nseq) for p_i in range(npage)]
    grid_spec = pltpu.PrefetchScalarGridSpec(
        num_scalar_prefetch=1,
        grid=(b // nseq,),
        in_specs=[tok, tok, tok] + pages * 2,
        out_specs=pl.BlockSpec((nseq, lp, W_BRANCH), lambda i, pt: (i, 0, 0)),
    )
    return pl.pallas_call(
        kern,
        grid_spec=grid_spec,
        out_shape=jax.ShapeDtypeStruct((b, lp, W_BRANCH), F32),
        compiler_params=pltpu.CompilerParams(dimension_semantics=("arbitrary",), vmem_limit_bytes=VMEM_LIMIT),
    )(page_table, q3, k3, v3, *([cache_k] * (nseq * npage)), *([cache_v] * (nseq * npage)))


def _merge_kernel(oa_ref, ob_ref, oc_ref, zc_ref, gate_ref, x_ref, ple_ref, wb_ref, bg_ref, wo_ref,
                  wpg_ref, bpg_ref, wple_ref, y_ref):
    branches = (oa_ref[...], ob_ref[...], oc_ref[...] * _silu(zc_ref[...]))
    up = None
    for n, o in enumerate(branches):
        sl = slice(n * D_MODEL, (n + 1) * D_MODEL)
        gate = _sigmoid(gate_ref[:, sl] + bg_ref[:, sl])
        term = gate * jnp.dot(o.astype(BF16), wb_ref[n], preferred_element_type=F32)
        up = term if up is None else up + term
    y = x_ref[...] + jnp.dot(up.astype(BF16), wo_ref[...], preferred_element_type=F32)
    pg = _sigmoid(jnp.dot(y.astype(BF16), wpg_ref[...], preferred_element_type=F32) + bpg_ref[...])
    y_ref[...] = y + pg * jnp.dot(ple_ref[...].astype(BF16), wple_ref[...], preferred_element_type=F32)


def _merge(oa, ob, oc, proj2, x2, ple3, ple_layer, wb, bg, wo, wpg, bpg, wple, w_layer, tm):
    t = x2.shape[0]
    full = lambda shape: pl.BlockSpec((None,) + shape, lambda i: (w_layer,) + (0,) * len(shape))
    row = lambda w: pl.BlockSpec((tm, w), lambda i: (i, 0))
    return pl.pallas_call(
        _merge_kernel,
        grid=(t // tm,),
        in_specs=[row(W_BRANCH), row(W_BRANCH), row(W_BRANCH),
                  pl.BlockSpec((tm, W_BRANCH), lambda i: (i, C_Z_C // W_BRANCH)),
                  pl.BlockSpec((tm, N_BRANCH * D_MODEL), lambda i: (i, C_GATE // (N_BRANCH * D_MODEL))),
                  row(D_MODEL), pl.BlockSpec((None, tm, D_PLE), lambda i: (ple_layer, i, 0)),
                  full((N_BRANCH, W_BRANCH, D_MODEL)), full((1, N_BRANCH * D_MODEL)), full((D_MODEL, D_MODEL)),
                  full((D_MODEL, D_MODEL)), full((1, D_MODEL)), full((D_PLE, D_MODEL))],
        out_specs=row(D_MODEL),
        out_shape=jax.ShapeDtypeStruct((t, D_MODEL), F32),
        compiler_params=pltpu.CompilerParams(dimension_semantics=("arbitrary",), vmem_limit_bytes=VMEM_LIMIT),
    )(oa, ob, oc, proj2, proj2, x2, ple3, wb, bg, wo, wpg, bpg, wple)


def _arrange_w_in(w):
    offs = np.cumsum((0,) + IN_SPLITS)
    qkv_a, z_a, beta, alpha, f_b, z_b, qkv_c, z_c, gate = [w[:, offs[i]:offs[i + 1]] for i in range(9)]
    d = w.shape[0]
    ba = jnp.concatenate([beta, alpha, jnp.zeros((d, LANES - 2 * H_A), w.dtype)], axis=1)
    tail = jnp.zeros((d, N_PROJ - C_BA - LANES), w.dtype)
    out = jnp.concatenate([qkv_a, z_a, f_b[:, :3 * W_BRANCH], z_b, qkv_c, z_c, gate,
                           f_b[:, 3 * W_BRANCH:], ba, tail], axis=1)
    return out.astype(BF16)


def _rope_frequencies():
    half = ROPE_DIM // 2
    inv = ROPE_THETA ** (-jnp.arange(half, dtype=F32) / half)
    d = np.arange(LANES) % D_C
    return jnp.where(d < ROPE_DIM, inv[d % half], 0.0).reshape(1, LANES)


def _layer_params(i, p):
    row = lambda v: v.reshape(1, -1)
    lanes_ba = lambda v: jnp.zeros((1, LANES), F32).at[0, H_A:2 * H_A].set(v)
    lora_pad = jnp.zeros((LORA, W_BRANCH), F32)
    head = np.arange(W_BRANCH) // N_B
    return dict(
        norm_g=row(p['norm_g'][i]),
        w_in=_arrange_w_in(p['w_in'][i]),
        conv=p['conv_a'][i],
        pa=lanes_ba(p['a_log'][i]), pb=lanes_ba(p['dt_bias'][i]), onorm=row(p['onorm_a'][i]),
        rw_vecs=(row(p['mu_b'][i]), row(p['w0_b'][i]), row(p['a0_b'][i]), row(p['kk_b'][i]), row(p['ka_b'][i]),
                 row(p['rk_b'][i]), row(p['gn_g_b'][i]), row(p['gn_b_b'][i])),
        w2p=jnp.concatenate([p['w2_b'][i], lora_pad], axis=0).astype(BF16),
        a2p=jnp.concatenate([lora_pad, p['a2_b'][i]], axis=0).astype(BF16),
        bd=jnp.asarray(head[:, None] == head[None, :], BF16),
        qn=row(jnp.tile(p['qn_c'][i], H_C)), kn=row(jnp.tile(p['kn_c'][i], H_C)),
    )


def _hist_rows(rows):
    b, n, c = rows.shape
    return jnp.concatenate([jnp.zeros((b, HIST - n, c), F32), rows], axis=1)


def _layer(x3, ple3, ple_layer, lp, *, l_valid, rope, dn_s, rw_s, state_layer, dn_buf, rw_prev, nseg, lt, ls, tm, attend):
    b, l, d = x3.shape
    t = b * l
    proj2 = _proj(x3.reshape(t, d), lp['norm_g'], lp['w_in'], min(t, 1024), 1920)
    proj3 = proj2.reshape(b, l, N_PROJ)

    oa, dn_new = _gdn(proj3, _hist_rows(dn_buf), dn_s, state_layer, lp['conv'], lp['pa'], lp['pb'], lp['onorm'],
                      nseg=nseg, lt=lt, ls=ls, l_valid=l_valid)
    ob, rw_new = _rwkv(proj3, _hist_rows(rw_prev[:, None, :]), rw_s, state_layer, lp['rw_vecs'],
                       lp['w2p'], lp['a2p'], lp['bd'], nseg=nseg, lt=lt, ls=ls, l_valid=l_valid)
    qr, kr = _qkprep(proj2, lp['qn'], lp['kn'], rope, lp['bd'], tm=min(tm, t))
    oc, v2 = attend(qr, kr, proj3)

    y2 = _merge(oa.reshape(t, W_BRANCH), ob.reshape(t, W_BRANCH), oc.reshape(t, W_BRANCH), proj2,
                x3.reshape(t, d), ple3, ple_layer, lp['wb'], lp['bg'], lp['wo'], lp['wpg'],
                lp['bpg'], lp['wple'], lp['w_layer'], min(tm, t))

    f_last = jnp.concatenate([proj3[:, l_valid - 1, C_RKV:C_RKV + 3 * W_BRANCH],
                              proj3[:, l_valid - 1, C_WLAL:C_WLAL + 2 * LORA]], axis=-1)
    new_state = (kr.reshape(b, l, H_C, D_C)[:, :l_valid], v2.reshape(b, l, H_C, D_C)[:, :l_valid], dn_new,
                 proj3[:, l_valid - (CONV_W - 1):l_valid, C_QKV_A:C_QKV_A + QKV_A],
                 rw_new, f_last)
    return y2.reshape(b, l, d), new_state


def _attend_prompt(b, l):
    def attend(qr, kr, proj3):
        shp = (b, l, W_BRANCH)
        o3, v3 = _moba_prompt(qr.reshape(shp), kr.reshape(shp), proj3)
        return o3.reshape(b * l, W_BRANCH), v3.reshape(b * l, W_BRANCH)
    return attend


def _attend_decode(b, lp_, lq, page_table, cache_k4, cache_v4, layer):
    def attend(qr, kr, proj3):
        shp = (b, lp_, W_BRANCH)
        v3 = proj3[:, :, C_QKV_C + 2 * W_BRANCH:C_QKV_C + 3 * W_BRANCH]
        o3 = _moba_decode(page_table, qr.reshape(shp), kr.reshape(shp), v3, cache_k4, cache_v4, layer, lq)
        return o3.reshape(b * lp_, W_BRANCH), v3.reshape(b * lp_, W_BRANCH)
    return attend


DEC_PAD = 8


def kernel(x_prompt, x_sample, cache_k, cache_v, state_dn, state_dn_conv, state_rwkv, state_rwkv_shift,
           page_table, p_prompt, p_sample, norm_g, w_in, conv_a, a_log, dt_bias, onorm_a, mu_b, w0_b, w2_b,
           a0_b, a2_b, kk_b, ka_b, rk_b, gn_g_b, gn_b_b, qn_c, kn_c, w_branch, b_gate, w_out, w_ple, w_pg,
           b_pg):
    params = dict(norm_g=norm_g, w_in=w_in, conv_a=conv_a, a_log=a_log, dt_bias=dt_bias, onorm_a=onorm_a,
                  mu_b=mu_b, w0_b=w0_b, w2_b=w2_b, a0_b=a0_b, a2_b=a2_b, kk_b=kk_b, ka_b=ka_b, rk_b=rk_b,
                  gn_g_b=gn_g_b, gn_b_b=gn_b_b, qn_c=qn_c, kn_c=kn_c, w_branch=w_branch, b_gate=b_gate,
                  w_out=w_out, w_ple=w_ple, w_pg=w_pg, b_pg=b_pg)
    depth = norm_g.shape[0]
    b_p, l_p, _ = x_prompt.shape
    b_s, l_s, _ = x_sample.shape
    n_phys, page = cache_k.shape[1], cache_k.shape[2]
    past_len = page_table.shape[1] * page
    assert l_p % 512 == 0 and CONV_W - 1 <= l_s <= DEC_PAD and b_s % (CHUNK // DEC_PAD) == 0
    assert past_len % MOBA_BLOCK == 0 and MOBA_BLOCK % page == 0

    pad_rows = lambda a: jnp.pad(a, ((0, 0), (0, DEC_PAD - l_s), (0, 0)))
    xs = pad_rows(x_sample)
    cache_k4 = cache_k.transpose(0, 1, 3, 4, 2)
    cache_v4 = cache_v.transpose(0, 1, 3, 4, 2)
    inv = _rope_frequencies()
    tm_s = min(512, b_s * DEC_PAD)
    rope_p = _rope_table(inv, l_p, 0)
    rope_s = tuple(jnp.tile(tab, (tm_s // DEC_PAD, 1)) for tab in _rope_table(inv, DEC_PAD, past_len))
    xp = x_prompt
    new_p, new_s = [], []
    merge_w = dict(wb=w_branch.astype(BF16), bg=b_gate[:, None, :], wo=w_out.astype(BF16),
                   wpg=w_pg.astype(BF16), bpg=b_pg[:, None, :], wple=w_ple.astype(BF16))
    for i in range(depth):
        lp = dict(_layer_params(i, params), w_layer=i, **merge_w)
        xp, st_p = _layer(xp, p_prompt.reshape(depth, b_p * l_p, D_PLE), i, lp, l_valid=l_p, rope=rope_p,
                          dn_s=jnp.zeros((1, b_p, H_A, DK_A, DV_A), F32),
                          rw_s=jnp.zeros((1, b_p, H_B, N_B, N_B), F32), state_layer=0,
                          dn_buf=jnp.zeros((b_p, CONV_W - 1, QKV_A), F32), rw_prev=jnp.zeros((b_p, SHIFT_B), F32),
                          nseg=1, lt=512, ls=CHUNK, tm=512, attend=_attend_prompt(b_p, l_p))
        new_p.append(st_p)
        xs, st_s = _layer(xs, pad_rows(p_sample[i]).reshape(1, b_s * DEC_PAD, D_PLE), 0, lp, l_valid=l_s, rope=rope_s,
                          dn_s=state_dn, rw_s=state_rwkv, state_layer=i, dn_buf=state_dn_conv[i],
                          rw_prev=state_rwkv_shift[i], nseg=CHUNK // DEC_PAD, lt=DEC_PAD, ls=DEC_PAD, tm=512,
                          attend=_attend_decode(b_s, DEC_PAD, l_s, page_table, cache_k4, cache_v4, i))
        new_s.append(st_s)
    outs_p = [jnp.stack(t) for t in zip(*new_p)]
    outs_s = [jnp.stack(t) for t in zip(*new_s)]
    return (xp, xs[:, :l_s], *outs_p, *outs_s)
```

```python
import functools

import numpy as np
import jax
import jax.numpy as jnp
from jax import lax
from jax.experimental import pallas as pl
from jax.experimental.pallas import tpu as pltpu

F32 = jnp.float32
BF16 = jnp.bfloat16

D_MODEL = 1024
D_PLE = 256
N_BRANCH = 3
W_BRANCH = 512
NORM_EPS = 1e-6
DK_A = 128
DV_A = 128
H_A = 4
QKV_A = 1536
CONV_W = 4
N_B = 64
H_B = 8
LORA = 64
SHIFT_B = 3 * W_BRANCH + 2 * LORA
GN_EPS = 64e-5
D_C = 64
H_C = 8
MOBA_BLOCK = 256
MOBA_TOPK = 3
ROPE_DIM = 16
ROPE_THETA = 500000.0
IN_SPLITS = (QKV_A, W_BRANCH, H_A, H_A, SHIFT_B, W_BRANCH, 3 * W_BRANCH, W_BRANCH, N_BRANCH * D_MODEL)

C_QKV_A = 0
C_Z_A = 1536
C_RKV = 2048
C_Z_B = 3584
C_QKV_C = 4096
C_Z_C = 5632
C_GATE = 6144
C_WLAL = 9216
C_BA = 9344
N_PROJ = 9600

CHUNK = 64
HIST = 8
CHUNKS_PER_TRIP = 8
LANES = 128
VMEM_LIMIT = 62 * 1024 * 1024

NN = (((1,), (0,)), ((), ()))
BNN = (((2,), (1,)), ((0,), (0,)))
NT = (((1,), (1,)), ((), ()))
TN = (((0,), (0,)), ((), ()))
HIGHEST = lax.Precision.HIGHEST


def _mm(a, b, dims=NN):
    return lax.dot_general(a.astype(BF16), b.astype(BF16), dims, preferred_element_type=F32)


def _mm32(a, b, dims=NN):
    return lax.dot_general(a, b, dims, precision=HIGHEST, preferred_element_type=F32)


def _sigmoid(x):
    return 1.0 / (1.0 + jnp.exp(-x))


def _silu(x):
    return x * _sigmoid(x)


def _softplus(x):
    return jnp.maximum(x, 0.0) + jnp.log1p(jnp.exp(-jnp.abs(x)))


def _iota2(shape, dim):
    return lax.broadcasted_iota(jnp.int32, shape, dim)


def _split(x):
    hi = x.astype(BF16)
    return hi, (x - hi.astype(F32)).astype(BF16)


def _dotb(a, b, dims=NN):
    return lax.dot_general(a, b, dims, preferred_element_type=F32)


def _mm_sel(sel, x):
    hi, lo = _split(x)
    return _dotb(sel, hi) + _dotb(sel, lo)


def _mm3(a, b, dims=NN):
    ah, al = _split(a)
    bh, bl = _split(b)
    return _dotb(ah, bh, dims) + (_dotb(ah, bl, dims) + _dotb(al, bh, dims))


SOLVE_MM = _mm


def _inv_unit_lower(lmats, nilpotent, mm):
    n = lmats[0].shape[-1]
    dims = NN if lmats[0].ndim == 2 else BNN
    eye = (_iota2((n, n), 0) == _iota2((n, n), 1)).astype(F32)
    ps = [-lmat for lmat in lmats]
    accs = [eye + p for p in ps]
    span = 2
    while span < nilpotent:
        ps = [mm(p, p, dims) for p in ps]
        accs = [acc + mm(acc, p, dims) for acc, p in zip(accs, ps)]
        span *= 2
    return accs


def _head_sums(xs, bd):
    n = xs[0].shape[0]
    out = jnp.dot(jnp.concatenate([x.astype(BF16) for x in xs], axis=0), bd, preferred_element_type=F32)
    return [out[i * n:(i + 1) * n] for i in range(len(xs))]


def _head_sum(x, bd):
    return _head_sums([x], bd)[0]


def _proj_kernel(x_ref, g_ref, w_ref, o_ref):
    x = x_ref[...]
    h = x * lax.rsqrt(jnp.mean(x * x, axis=-1, keepdims=True) + NORM_EPS) * g_ref[...]
    o_ref[...] = jnp.dot(h.astype(BF16), w_ref[...], preferred_element_type=F32)


def _proj(x2d, g, w, tm, tn):
    t, d = x2d.shape
    n = w.shape[1]
    return pl.pallas_call(
        _proj_kernel,
        grid=(n // tn, t // tm),
        in_specs=[pl.BlockSpec((tm, d), lambda j, i: (i, 0)),
                  pl.BlockSpec((1, d), lambda j, i: (0, 0)),
                  pl.BlockSpec((d, tn), lambda j, i: (0, j))],
        out_specs=pl.BlockSpec((tm, tn), lambda j, i: (i, j)),
        out_shape=jax.ShapeDtypeStruct((t, n), F32),
        compiler_params=pltpu.CompilerParams(dimension_semantics=("arbitrary", "arbitrary"),
                                             vmem_limit_bytes=VMEM_LIMIT),
    )(x2d, g, w)


def _chunk_masks(n, ls):
    r = _iota2((n, n), 0)
    c = _iota2((n, n), 1)
    same = (r // ls) == (c // ls)
    return same, same & (r >= c), same & (r > c)


def _valid_rows(l_idx, lt, r0, ls, l_valid):
    t = l_idx * lt + r0 + (_iota2((CHUNK, 1), 0) % ls)
    return t < l_valid


def _gdn_parts(qkv_ref, z_ref, ba_ref, buf_ref, s0_ref, cw_ref, pa_ref, pb_ref, on_ref,
               o_ref, sout_ref, xs_ref, st_ref, *, l_idx, nseg, lt, ls, l_valid, l_total):
    nrow = H_A * CHUNK

    @pl.when(l_idx == 0)
    def _():
        xs_ref[:, 0:HIST, :] = buf_ref[...]
        st_ref[...] = s0_ref[...]

    @pl.when(l_idx > 0)
    def _():
        xs_ref[:, 0:HIST, :] = xs_ref[:, lt:lt + HIST, :]

    xs_ref[:, HIST:lt + HIST, :] = qkv_ref[...]

    same_r, c_incl, c_strict = _chunk_masks(CHUNK, ls)
    tri_r = c_incl.astype(BF16)
    same_r = same_r.astype(BF16)
    eye_c = (_iota2((CHUNK, CHUNK), 0) == _iota2((CHUNK, CHUNK), 1)).astype(F32)
    cw = cw_ref[...]
    neg_a = -jnp.exp(pa_ref[...])
    pb = pb_ref[...]
    onorm = on_ref[...]

    def conv(s, r0, col):
        win = xs_ref[s, pl.ds(r0, ls + HIST), col:col + LANES]
        w = cw[:, col:col + LANES]
        acc = pltpu.roll(win, 3, 0)[HIST:] * w[0:1] + pltpu.roll(win, 2, 0)[HIST:] * w[1:2]
        acc = acc + pltpu.roll(win, 1, 0)[HIST:] * w[2:3]
        acc = acc + win[HIST:] * w[3:4]
        return _silu(acc)

    def part(r0, col):
        return jnp.concatenate([conv(s, r0, col) for s in range(nseg)], axis=0)

    def local(c):
        r0 = pl.multiple_of(c * ls, ls)
        ba = jnp.concatenate([ba_ref[s, pl.ds(r0, ls), :] for s in range(nseg)], axis=0)
        beta_all = _sigmoid(ba)
        g_all = neg_a * _softplus(ba + pb)
        if l_valid < l_total:
            ok = _valid_rows(l_idx, lt, r0, ls, l_valid)
            beta_all = jnp.where(ok, beta_all, 0.0)
            g_all = jnp.where(ok, g_all, 0.0)
        gcum = _mm_sel(tri_r, g_all)
        if nseg == 1:
            gtot = jnp.broadcast_to(gcum[CHUNK - 1:CHUNK], gcum.shape)
        else:
            gtot = _mm_sel(same_r, g_all)

        qs, ks, vs, betas, gcs, gts, lmats, qks = [], [], [], [], [], [], [], []
        for h in range(H_A):
            q = part(r0, h * DK_A)
            k = part(r0, H_A * DK_A + h * DK_A)
            v = part(r0, 2 * H_A * DK_A + h * DV_A)
            q = q * lax.rsqrt(jnp.sum(q * q, axis=-1, keepdims=True) + 1e-6) * (DK_A ** -0.5)
            k = k * lax.rsqrt(jnp.sum(k * k, axis=-1, keepdims=True) + 1e-6)
            beta_h = beta_all[:, h:h + 1]
            g_h = gcum[:, H_A + h:H_A + h + 1]
            g_row = jnp.sum(eye_c * g_h, axis=0, keepdims=True)
            decay = jnp.exp(jnp.where(c_incl, g_h - g_row, -jnp.inf))
            kq_k = _mm(jnp.concatenate([k, q], axis=0), k, NT)
            lmats.append(jnp.where(c_strict, beta_h * kq_k[:CHUNK] * decay, 0.0))
            qks.append(jnp.where(c_incl, kq_k[CHUNK:] * decay, 0.0))
            qs.append(q)
            ks.append(k)
            vs.append(v)
            betas.append(beta_h)
            gcs.append(g_h)
            gts.append(gtot[:, H_A + h:H_A + h + 1])
        q_st = jnp.concatenate(qs, axis=0)
        k_st = jnp.concatenate(ks, axis=0)
        v_st = jnp.concatenate(vs, axis=0)
        beta = jnp.concatenate(betas, axis=0)
        g_col = jnp.concatenate(gcs, axis=0)
        gt_col = jnp.concatenate(gts, axis=0)
        return dict(
            r0=r0, gt_col=gt_col,
            lmat=jnp.stack(lmats),
            rhs=jnp.concatenate([v_st * beta, k_st * (beta * jnp.exp(g_col))], axis=1),
            qk=jnp.stack(qks),
            qg=q_st * jnp.exp(g_col),
            kd=k_st * jnp.exp(gt_col - g_col))

    def advance(loc, sol):
        r0, gt_col, qk, qg, kd = loc['r0'], loc['gt_col'], loc['qk'], loc['qg'], loc['kd']
        bv = sol[:, :DV_A]
        wk = sol[:, DV_A:]
        wks, qgs = [], []
        for h in range(H_A):
            for s in range(nseg):
                a = h * CHUNK + s * ls
                both = _mm(jnp.concatenate([wk[a:a + ls], qg[a:a + ls]], axis=0), st_ref[s, h])
                wks.append(both[:ls])
                qgs.append(both[ls:])
        u = bv - jnp.concatenate(wks, axis=0)
        o = jnp.concatenate(qgs, axis=0) + _mm(qk, u.reshape(H_A, CHUNK, DV_A), BNN).reshape(nrow, DV_A)

        for h in range(H_A):
            a = h * CHUNK
            for s in range(nseg):
                b = a + s * ls
                if nseg == 1:
                    kd_s = kd[a:a + CHUNK]
                else:
                    rows = _iota2((CHUNK, 1), 0) // ls
                    kd_s = jnp.where(rows == s, kd[a:a + CHUNK], 0.0)
                upd = _mm(kd_s, u[a:a + CHUNK], TN)
                st_ref[s, h] = st_ref[s, h] * jnp.exp(gt_col[b:b + 1]) + upd
            oh = o[a:a + CHUNK]
            oh = oh * lax.rsqrt(jnp.mean(oh * oh, axis=-1, keepdims=True) + NORM_EPS) * onorm
            for s in range(nseg):
                zz = z_ref[s, pl.ds(r0, ls), h * DV_A:(h + 1) * DV_A]
                o_ref[s, pl.ds(r0, ls), h * DV_A:(h + 1) * DV_A] = oh[s * ls:(s + 1) * ls] * _silu(zz)

    def prepare(loc, invs):
        return SOLVE_MM(invs[0], loc['rhs'].reshape(H_A, CHUNK, DK_A + DV_A), BNN).reshape(nrow, DK_A + DV_A)

    def epilogue():
        sout_ref[...] = st_ref[...]

    return dict(local=local, lmats=lambda loc: [loc['lmat']], prepare=prepare, advance=advance,
                epilogue=epilogue)


GRP = 4 * N_B


def _rwkv_parts(r_ref, k_ref, v_ref, wa_ref, z_ref, prev_ref, s0_ref, mu_ref, w0_ref, w2_ref,
                a0_ref, a2_ref, kkp_ref, kap_ref, rk_ref, gg_ref, gb_ref, bd_ref,
                o_ref, sout_ref, fs_ref, st_ref, *, l_idx, nseg, lt, ls, l_valid, l_total):
    nrow = 4 * CHUNK
    ngrp = W_BRANCH // GRP

    @pl.when(l_idx == 0)
    def _():
        fs_ref[:, 0:HIST, :] = prev_ref[...]
        st_ref[...] = jnp.zeros(st_ref.shape, F32)
        for s in range(nseg):
            for h in range(H_B):
                g, j = divmod(h, 4)
                st_ref[s, g, j * N_B:(j + 1) * N_B, j * N_B:(j + 1) * N_B] = s0_ref[s, h]

    @pl.when(l_idx > 0)
    def _():
        fs_ref[:, 0:HIST, :] = fs_ref[:, lt:lt + HIST, :]

    fs_ref[:, HIST:lt + HIST, 0:W_BRANCH] = r_ref[...]
    fs_ref[:, HIST:lt + HIST, W_BRANCH:2 * W_BRANCH] = k_ref[...]
    fs_ref[:, HIST:lt + HIST, 2 * W_BRANCH:3 * W_BRANCH] = v_ref[...]
    fs_ref[:, HIST:lt + HIST, 3 * W_BRANCH:SHIFT_B] = wa_ref[...]

    same_r, tri_r, _ = _chunk_masks(CHUNK, ls)
    tri_r = tri_r.astype(BF16)
    same_r = same_r.astype(BF16)
    lane_head = _iota2((1, GRP), 1) // N_B
    head_masks = [(lane_head == j).astype(F32) for j in range(4)]
    bd_state = (_iota2((GRP, GRP), 0) // N_B) == (_iota2((GRP, GRP), 1) // N_B)
    head_rows = ((_iota2((nrow, GRP), 0) // CHUNK) == (_iota2((nrow, GRP), 1) // N_B)).astype(F32)
    t_row = _iota2((nrow, 2 * CHUNK), 0) % CHUNK
    t_col = _iota2((nrow, 2 * CHUNK), 1) % CHUNK
    t_same = (t_row // ls) == (t_col // ls)
    t_incl = t_same & (t_col <= t_row)
    t_strict = t_same & (t_col < t_row)
    mu = mu_ref[...]
    bd = bd_ref[...]

    def mixed(r0, col, width):
        outs = []
        for s in range(nseg):
            win = fs_ref[s, pl.ds(r0, ls + HIST), col:col + width]
            cur = win[HIST:]
            prev = pltpu.roll(win, 1, 0)[HIST:]
            outs.append(cur + (prev - cur) * mu[:, col:col + width])
        return jnp.concatenate(outs, axis=0)

    def stack_masked(x):
        return jnp.concatenate([x * head_masks[j] for j in range(4)], axis=0)

    def unstack(x):
        return x[0:CHUNK] + x[CHUNK:2 * CHUNK] + x[2 * CHUNK:3 * CHUNK] + x[3 * CHUNK:4 * CHUNK]

    def local(c):
        r0 = pl.multiple_of(c * ls, ls)
        r = mixed(r0, 0, W_BRANCH)
        k = mixed(r0, W_BRANCH, W_BRANCH)
        v = mixed(r0, 2 * W_BRANCH, W_BRANCH)
        wa = mixed(r0, 3 * W_BRANCH, 2 * LORA)
        w_raw = w0_ref[...] + _mm(jnp.tanh(wa), w2_ref[...])
        logw = -jnp.exp(-_softplus(-w_raw) - 0.5)
        a_sig = _sigmoid(a0_ref[...] + _mm(wa, a2_ref[...]))
        kk = k * kkp_ref[...]
        k_mod = k * (1.0 + (a_sig - 1.0) * kap_ref[...])
        kk_ss, rk_sum = _head_sums([kk * kk, r * k_mod * rk_ref[...]], bd)
        kk = kk * lax.rsqrt(kk_ss + 1e-6)
        b_vec = kk * a_sig
        if l_valid < l_total:
            ok = _valid_rows(l_idx, lt, r0, ls, l_valid)
            logw = jnp.where(ok, logw, 0.0)
            b_vec = jnp.where(ok, b_vec, 0.0)
            k_mod = jnp.where(ok, k_mod, 0.0)
        gcum = _mm_sel(tri_r, logw)
        if nseg == 1:
            gtot = jnp.broadcast_to(gcum[CHUNK - 1:CHUNK], gcum.shape)
        else:
            gtot = _mm_sel(same_r, logw)
        e_neg = jnp.exp(-gcum)
        e_end = jnp.exp(gtot - gcum)
        a_t = -kk * jnp.exp(gcum - logw)
        b_t = b_vec * e_neg
        k_t = k_mod * e_neg
        r_t = r * jnp.exp(gcum)
        b_h = b_vec * e_end
        k_h = k_mod * e_end

        groups = []
        for g in range(ngrp):
            sl = slice(g * GRP, (g + 1) * GRP)
            a_g, b_g, k_g, r_g, v_g = a_t[:, sl], b_t[:, sl], k_t[:, sl], r_t[:, sl], v[:, sl]
            a_st = stack_masked(a_g)
            r_st = stack_masked(r_g)
            both = _mm(jnp.concatenate([a_st, r_st], axis=0), jnp.concatenate([b_g, k_g], axis=0), NT)
            a_bk = jnp.where(t_strict, both[:nrow], 0.0)
            r_bk = jnp.where(t_incl, both[nrow:], 0.0)
            zero_v = jnp.concatenate([jnp.zeros_like(v_g), v_g], axis=0)
            groups.append(dict(
                a_g=a_g, r_g=r_g, v_g=v_g, b_h=b_h[:, sl], k_h=k_h[:, sl], gtot=gtot[:, sl],
                neg_aab=-a_bk[:, :CHUNK].reshape(4, CHUNK, CHUNK),
                kv_u=_mm(a_bk, zero_v) * head_rows, r_bk=r_bk))
        return dict(r0=r0, v=v, rk_sum=rk_sum, groups=groups)

    def advance(loc, tinvs):
        grps = loc['groups']
        ams, rms = [], []
        for g, grp in enumerate(grps):
            parts = [_mm(jnp.concatenate([grp['a_g'][s * ls:(s + 1) * ls], grp['r_g'][s * ls:(s + 1) * ls]], axis=0),
                         st_ref[s, g], NT) for s in range(nseg)]
            ams.append(jnp.concatenate([p[:ls] for p in parts], axis=0))
            rms.append(jnp.concatenate([p[ls:] for p in parts], axis=0))
        us = [unstack(SOLVE_MM(tinv, (stack_masked(am) + grp['kv_u']).reshape(4, CHUNK, GRP), BNN)
                      .reshape(nrow, GRP)) for grp, tinv, am in zip(grps, tinvs, ams)]
        ys = [rm + unstack(_mm(grp['r_bk'], jnp.concatenate([u, grp['v_g']], axis=0)) * head_rows)
              for grp, u, rm in zip(grps, us, rms)]
        for g, (grp, u) in enumerate(zip(grps, us)):
            v_g = grp['v_g']
            for s in range(nseg):
                if nseg == 1:
                    u_s, v_s = u, v_g
                else:
                    rows = _iota2((CHUNK, 1), 0) // ls
                    u_s = jnp.where(rows == s, u, 0.0)
                    v_s = jnp.where(rows == s, v_g, 0.0)
                upd = _mm(jnp.concatenate([u_s, v_s], axis=0),
                          jnp.concatenate([grp['b_h'], grp['k_h']], axis=0), TN)
                dec = jnp.exp(grp['gtot'][s * ls:s * ls + 1])
                st_ref[s, g] = st_ref[s, g] * dec + jnp.where(bd_state, upd, 0.0)
        y = jnp.concatenate(ys, axis=1)
        mean = _head_sums([y], bd)[0] * (1.0 / N_B)
        yc = y - mean
        var = _head_sums([yc * yc], bd)[0] * (1.0 / N_B)
        yn = yc * lax.rsqrt(var + GN_EPS) * gg_ref[...] + gb_ref[...]
        out = yn + loc['rk_sum'] * loc['v']
        for s in range(nseg):
            zz = z_ref[s, pl.ds(loc['r0'], ls), :]
            o_ref[s, pl.ds(loc['r0'], ls), :] = out[s * ls:(s + 1) * ls] * _silu(zz)

    def epilogue():
        for s in range(nseg):
            for h in range(H_B):
                g, j = divmod(h, 4)
                sout_ref[s, h] = st_ref[s, g, j * N_B:(j + 1) * N_B, j * N_B:(j + 1) * N_B]

    return dict(local=local, lmats=lambda loc: [grp['neg_aab'] for grp in loc['groups']],
                prepare=lambda loc, invs: invs, advance=advance, epilogue=epilogue)


N_GDN_IN, N_RWKV_IN = 9, 18


def _mixers_kernel(*refs, nseg, lt, ls, l_valid, l_total):
    gdn_in = refs[:N_GDN_IN]
    rwkv_in = refs[N_GDN_IN:N_GDN_IN + N_RWKV_IN]
    oa_ref, dn_ref, ob_ref, rw_ref, xs_ref, st_a_ref, fs_ref, st_b_ref = refs[N_GDN_IN + N_RWKV_IN:]
    l_idx = pl.program_id(1)
    cfg = dict(l_idx=l_idx, nseg=nseg, lt=lt, ls=ls, l_valid=l_valid, l_total=l_total)
    mixers = [_gdn_parts(*gdn_in, oa_ref, dn_ref, xs_ref, st_a_ref, **cfg),
              _rwkv_parts(*rwkv_in, ob_ref, rw_ref, fs_ref, st_b_ref, **cfg)]
    nchunk = lt // ls
    per_trip = min(CHUNKS_PER_TRIP, nchunk)

    def trip(i, carry):
        chunks = [i * per_trip + j for j in range(per_trip)]
        locs = [[m['local'](c) for c in chunks] for m in mixers]
        lmats = [[m['lmats'](loc) for loc in mlocs] for m, mlocs in zip(mixers, locs)]
        invs = _inv_unit_lower([x for per_mixer in lmats for per_loc in per_mixer for x in per_loc], ls, SOLVE_MM)
        prepared, at = [], 0
        for m, mlocs, mlm in zip(mixers, locs, lmats):
            row = []
            for loc, lm in zip(mlocs, mlm):
                row.append(m['prepare'](loc, invs[at:at + len(lm)]))
                at += len(lm)
            prepared.append(row)
        for j in range(per_trip):
            for m, mlocs, mprep in zip(mixers, locs, prepared):
                m['advance'](mlocs[j], mprep[j])
        return carry

    lax.fori_loop(0, nchunk // per_trip, trip, 0)

    @pl.when(l_idx == pl.num_programs(1) - 1)
    def _():
        for m in mixers:
            m['epilogue']()


def _mixers(proj3, buf8, dn_s0, prev8, rw_s0, layer, lp, *, nseg, lt, ls, l_valid):
    b, l, _ = proj3.shape
    ngrp = W_BRANCH // GRP
    kern = functools.partial(_mixers_kernel, nseg=nseg, lt=lt, ls=ls, l_valid=l_valid, l_total=l)
    full = lambda shape: pl.BlockSpec(shape, lambda i, j: (0,) * len(shape))
    blk = lambda col, w: pl.BlockSpec((nseg, lt, w), lambda i, j: (i, j, col // w))
    mu, w0, a0, kkp, kap, rk, gg, gb = lp['rw_vecs']
    gdn_specs = [blk(C_QKV_A, QKV_A), blk(C_Z_A, W_BRANCH), blk(C_BA, LANES),
                 pl.BlockSpec((nseg, HIST, QKV_A), lambda i, j: (i, 0, 0)),
                 pl.BlockSpec((None, nseg, H_A, DK_A, DV_A), lambda i, j: (layer, i, 0, 0, 0)),
                 full((CONV_W, QKV_A)), full((1, LANES)), full((1, LANES)), full((1, DV_A))]
    rwkv_specs = [blk(C_RKV, W_BRANCH), blk(C_RKV + W_BRANCH, W_BRANCH), blk(C_RKV + 2 * W_BRANCH, W_BRANCH),
                  blk(C_WLAL, 2 * LORA), blk(C_Z_B, W_BRANCH),
                  pl.BlockSpec((nseg, HIST, SHIFT_B), lambda i, j: (i, 0, 0)),
                  pl.BlockSpec((None, nseg, H_B, N_B, N_B), lambda i, j: (layer, i, 0, 0, 0)),
                  full((1, SHIFT_B)), full((1, W_BRANCH)), full((2 * LORA, W_BRANCH)),
                  full((1, W_BRANCH)), full((2 * LORA, W_BRANCH)), full((1, W_BRANCH)),
                  full((1, W_BRANCH)), full((1, W_BRANCH)), full((1, W_BRANCH)), full((1, W_BRANCH)),
                  full((W_BRANCH, W_BRANCH))]
    assert len(gdn_specs) == N_GDN_IN and len(rwkv_specs) == N_RWKV_IN
    tile_out = pl.BlockSpec((nseg, lt, W_BRANCH), lambda i, j: (i, j, 0))
    return pl.pallas_call(
        kern,
        grid=(b // nseg, l // lt),
        in_specs=gdn_specs + rwkv_specs,
        out_specs=[tile_out, pl.BlockSpec((nseg, H_A, DK_A, DV_A), lambda i, j: (i, 0, 0, 0)),
                   tile_out, pl.BlockSpec((nseg, H_B, N_B, N_B), lambda i, j: (i, 0, 0, 0))],
        out_shape=[jax.ShapeDtypeStruct((b, l, W_BRANCH), F32), jax.ShapeDtypeStruct((b, H_A, DK_A, DV_A), F32),
                   jax.ShapeDtypeStruct((b, l, W_BRANCH), F32), jax.ShapeDtypeStruct((b, H_B, N_B, N_B), F32)],
        scratch_shapes=[pltpu.VMEM((nseg, lt + HIST, QKV_A), F32), pltpu.VMEM((nseg, H_A, DK_A, DV_A), F32),
                        pltpu.VMEM((nseg, lt + HIST, SHIFT_B), F32), pltpu.VMEM((nseg, ngrp, GRP, GRP), F32)],
        compiler_params=pltpu.CompilerParams(dimension_semantics=("arbitrary", "arbitrary"),
                                             vmem_limit_bytes=VMEM_LIMIT),
    )(proj3, proj3, proj3, buf8, dn_s0, lp['conv'], lp['pa'], lp['pb'], lp['onorm'],
      proj3, proj3, proj3, proj3, proj3, prev8, rw_s0, mu, w0, lp['w2p'], a0, lp['a2p'], kkp, kap, rk, gg, gb,
      lp['bd'])


def _rope_table_kernel(inv_ref, cos_ref, sin_ref, *, pos0):
    n = cos_ref.shape[0]
    pos = (pos0 + pl.program_id(0) * n + _iota2((n, 1), 0)).astype(F32)
    ang = pos * inv_ref[...]
    cos_ref[...] = jnp.cos(ang)
    sin_ref[...] = jnp.sin(ang)


def _rope_table(inv, n, pos0, tile=512):
    tile = min(tile, n)
    blk = pl.BlockSpec((tile, LANES), lambda i: (i, 0))
    return pl.pallas_call(
        functools.partial(_rope_table_kernel, pos0=pos0),
        grid=(n // tile,),
        in_specs=[pl.BlockSpec((1, LANES), lambda i: (0, 0))],
        out_specs=[blk, blk],
        out_shape=[jax.ShapeDtypeStruct((n, LANES), F32)] * 2,
        compiler_params=pltpu.CompilerParams(dimension_semantics=("arbitrary",), vmem_limit_bytes=VMEM_LIMIT),
    )(inv)


def _qkprep_kernel(q_ref, k_ref, qn_ref, kn_ref, cos_ref, sin_ref, bd_ref, qo_ref, ko_ref):
    cos = cos_ref[...]
    sin = sin_ref[...]
    d = _iota2((1, LANES), 1) % D_C
    half = ROPE_DIM // 2
    c_main = jnp.where(d < ROPE_DIM, cos, 1.0)
    c_up = jnp.where(d < half, -sin, 0.0)
    c_dn = jnp.where((d >= half) & (d < ROPE_DIM), sin, 0.0)
    bd = bd_ref[...]

    def one(x_ref, g_ref, out_ref):
        x = x_ref[...]
        y = x * lax.rsqrt(_head_sum(x * x, bd) * (1.0 / D_C) + NORM_EPS) * g_ref[...]
        for blk in range(W_BRANCH // LANES):
            yb = y[:, blk * LANES:(blk + 1) * LANES]
            up = pltpu.roll(yb, LANES - half, 1)
            dn = pltpu.roll(yb, half, 1)
            out_ref[:, blk * LANES:(blk + 1) * LANES] = yb * c_main + up * c_up + dn * c_dn

    one(q_ref, qn_ref, qo_ref)
    one(k_ref, kn_ref, ko_ref)


def _qkprep(proj2, qn, kn, rope, bd, *, tm):
    t = proj2.shape[0]
    cos, sin = rope
    period = cos.shape[0] // tm
    full = lambda shape: pl.BlockSpec(shape, lambda i: (0,) * len(shape))
    table = pl.BlockSpec((tm, LANES), lambda i: (i % period, 0))
    return pl.pallas_call(
        _qkprep_kernel,
        grid=(t // tm,),
        in_specs=[pl.BlockSpec((tm, W_BRANCH), lambda i: (i, C_QKV_C // W_BRANCH)),
                  pl.BlockSpec((tm, W_BRANCH), lambda i: (i, C_QKV_C // W_BRANCH + 1)),
                  full((1, W_BRANCH)), full((1, W_BRANCH)), table, table, full((W_BRANCH, W_BRANCH))],
        out_specs=[pl.BlockSpec((tm, W_BRANCH), lambda i: (i, 0)),
                   pl.BlockSpec((tm, W_BRANCH), lambda i: (i, 0))],
        out_shape=[jax.ShapeDtypeStruct((t, W_BRANCH), F32), jax.ShapeDtypeStruct((t, W_BRANCH), F32)],
        compiler_params=pltpu.CompilerParams(dimension_semantics=("arbitrary",), vmem_limit_bytes=VMEM_LIMIT),
    )(proj2, proj2, qn, kn, cos, sin, bd)


def _rank_select(gate, nblk, limit, axis):
    idx = _iota2(gate.shape, axis)
    rank = jnp.zeros(gate.shape, F32)
    for m in range(nblk):
        gm = gate[m:m + 1, :] if axis == 0 else gate[:, m:m + 1]
        beats = (gm > gate) | ((gm == gate) & (m < idx))
        rank = rank + jnp.where(beats & (m < limit), 1.0, 0.0)
    return jnp.where((idx < limit) & (rank < MOBA_TOPK), 1.0, 0.0)


HEADS_PER_STEP = LANES // D_C


def _moba_prompt_kernel(q_ref, k_ref, v_ref, o_ref, vo_ref, *, nblk):
    blk = MOBA_BLOCK
    l = nblk * blk
    scale = D_C ** -0.5
    k = k_ref[...]
    q_t = q_ref[...].T
    v = v_ref[...]
    vo_ref[...] = v
    v_t = v.T.astype(BF16)
    kmean = jnp.concatenate([jnp.mean(k[n * blk:(n + 1) * blk], axis=0, keepdims=True) for n in range(nblk)],
                            axis=0)
    own = _iota2((1, l), 1) // blk
    kb = k.astype(BF16)
    causal = _iota2((blk, blk), 0) <= _iota2((blk, blk), 1)
    row_head = _iota2((LANES, 1), 0) // D_C
    units = [(e, j) for e in range(HEADS_PER_STEP) for j in range(nblk)]
    sels, qbs = [], []
    for e in range(HEADS_PER_STEP):
        q_e = jnp.where(row_head == e, q_t, 0.0)
        sels.append(_rank_select(_mm32(kmean, q_e), nblk, own, 0))
        qbs.append((q_e * scale).astype(BF16))
    scores = [jnp.dot(kb[:(j + 1) * blk], qbs[e][:, j * blk:(j + 1) * blk], preferred_element_type=F32)
              for e, j in units]
    probs, denoms = [], []
    for (e, j), s in zip(units, scores):
        cols = slice(j * blk, (j + 1) * blk)
        parts = [jnp.where(sels[e][n:n + 1, cols] > 0.0, s[n * blk:(n + 1) * blk], -jnp.inf) for n in range(j)]
        parts.append(jnp.where(causal, s[j * blk:], -jnp.inf))
        s = jnp.concatenate(parts, axis=0)
        p = jnp.exp(s - jnp.max(s, axis=0, keepdims=True))
        denoms.append(jnp.sum(p, axis=0, keepdims=True))
        probs.append(p.astype(BF16))
    tiles = [jnp.dot(v_t[e * D_C:(e + 1) * D_C, :(j + 1) * blk], p, preferred_element_type=F32) / d
             for (e, j), p, d in zip(units, probs, denoms)]
    outs = [jnp.concatenate(tiles[e * nblk:(e + 1) * nblk], axis=1) for e in range(HEADS_PER_STEP)]
    o_ref[...] = jnp.concatenate(outs, axis=0).T


def _moba_prompt(q3, k3, proj3):
    b, l, _ = q3.shape
    nblk = l // MOBA_BLOCK
    assert (4.0 ** round(np.log(D_C) / np.log(4.0))) == D_C
    kern = functools.partial(_moba_prompt_kernel, nblk=nblk)
    v_col = (C_QKV_C + 2 * W_BRANCH) // LANES
    cols = pl.BlockSpec((None, l, LANES), lambda i, hp: (i, 0, hp))
    return pl.pallas_call(
        kern,
        grid=(b, W_BRANCH // LANES),
        in_specs=[cols, cols, pl.BlockSpec((None, l, LANES), lambda i, hp: (i, 0, v_col + hp))],
        out_specs=[cols, cols],
        out_shape=[jax.ShapeDtypeStruct((b, l, W_BRANCH), F32)] * 2,
        compiler_params=pltpu.CompilerParams(dimension_semantics=("arbitrary", "arbitrary"),
                                             vmem_limit_bytes=VMEM_LIMIT),
    )(q3, k3, proj3)


SEQ_PER_STEP = 2


def _moba_decode_kernel(pt_ref, q_ref, kn_ref, vn_ref, *refs, npage, page, lq):
    nseq = q_ref.shape[0]
    k_refs = refs[:nseq * npage]
    v_refs = refs[nseq * npage:2 * nseq * npage]
    o_ref = refs[2 * nseq * npage]
    del pt_ref
    scale = D_C ** -0.5
    nrow = lq * H_C
    nblk = npage * page // MOBA_BLOCK
    lp = q_ref.shape[1]
    seqs = range(nseq)

    lane_head = _iota2((H_C, W_BRANCH), 1) // D_C
    own_head = (lane_head == _iota2((H_C, W_BRANCH), 0)).astype(F32)
    own_rows = jnp.concatenate([own_head] * lq, axis=0)
    lane = _iota2((1, LANES), 1)
    t_key = _iota2((nrow, LANES), 1)
    t_qry = _iota2((nrow, LANES), 0) // H_C
    pad = jnp.zeros((LANES - lp, W_BRANCH), F32)

    qbds = [jnp.concatenate([q_ref[r, t:t + 1, :] * own_head for t in range(lq)], axis=0) for r in seqs]
    qbs = [(qbd * scale).astype(BF16) for qbd in qbds]
    k_ts = [jnp.concatenate([k_refs[r * npage + i][...].reshape(W_BRANCH, page) for i in range(npage)], axis=1)
            for r in seqs]
    v_ts = [jnp.concatenate([v_refs[r * npage + i][...].reshape(W_BRANCH, page) for i in range(npage)], axis=1)
            for r in seqs]
    s_pasts = [jnp.dot(qb, k_t.astype(BF16), preferred_element_type=F32) for qb, k_t in zip(qbs, k_ts)]

    def block_key_sums(k_t):
        ksum = jnp.zeros((W_BRANCH, LANES), F32)
        for n in range(nblk):
            k_n = k_t[:, n * MOBA_BLOCK:(n + 1) * MOBA_BLOCK]
            folded = sum(k_n[:, i * LANES:(i + 1) * LANES] for i in range(MOBA_BLOCK // LANES))
            ksum = ksum + jnp.sum(folded, axis=-1, keepdims=True) * (lane == n).astype(F32)
        return ksum

    ksums = [block_key_sums(k_t) for k_t in k_ts]
    gates = [_mm3(qbd, ksum) * (1.0 / MOBA_BLOCK) for qbd, ksum in zip(qbds, ksums)]
    sels = [_rank_select(gate, nblk, nblk, 1) for gate in gates]
    s_pasts = [jnp.concatenate(
        [jnp.where(sel[:, n:n + 1] > 0.0, s[:, n * MOBA_BLOCK:(n + 1) * MOBA_BLOCK], -jnp.inf)
         for n in range(nblk)], axis=1) for sel, s in zip(sels, s_pasts)]

    k_news = [jnp.concatenate([kn_ref[r], pad], axis=0) for r in seqs]
    v_news = [jnp.concatenate([vn_ref[r], pad], axis=0) for r in seqs]
    s_owns = [jnp.where(t_key <= t_qry, lax.dot_general(qb, k_new.astype(BF16), NT, preferred_element_type=F32),
                        -jnp.inf) for qb, k_new in zip(qbs, k_news)]
    ms = [jnp.maximum(jnp.max(sp, axis=-1, keepdims=True), jnp.max(so, axis=-1, keepdims=True))
          for sp, so in zip(s_pasts, s_owns)]
    p_pasts = [jnp.exp(sp - m) for sp, m in zip(s_pasts, ms)]
    p_owns = [jnp.exp(so - m) for so, m in zip(s_owns, ms)]
    denoms = [jnp.sum(pp, axis=-1, keepdims=True) + jnp.sum(po, axis=-1, keepdims=True)
              for pp, po in zip(p_pasts, p_owns)]
    accs = [lax.dot_general(pp.astype(BF16), v_t.astype(BF16), NT, preferred_element_type=F32)
            for pp, v_t in zip(p_pasts, v_ts)]
    accs = [acc + jnp.dot(po.astype(BF16), v_new.astype(BF16), preferred_element_type=F32)
            for acc, po, v_new in zip(accs, p_owns, v_news)]
    for r in seqs:
        out = accs[r] / denoms[r]
        out = (out * own_rows).reshape(lq, H_C, W_BRANCH).sum(axis=1)
        o_ref[r] = jnp.concatenate([out, jnp.zeros((lp - lq, W_BRANCH), F32)], axis=0)


def _moba_decode(page_table, q3, k3, v3, cache_k, cache_v, layer, lq):
    b, lp, _ = q3.shape
    npage = page_table.shape[1]
    page = cache_k.shape[-1]
    nseq = SEQ_PER_STEP
    kern = functools.partial(_moba_decode_kernel, npage=npage, page=page, lq=lq)
    tok = pl.BlockSpec((nseq, lp, W_BRANCH), lambda i, pt: (i, 0, 0))

    def page_spec(r, p_i):
        return pl.BlockSpec((None, None, H_C, D_C, page), lambda i, pt: (layer, pt[i * nseq + r, p_i], 0, 0, 0))

    pages = [page_spec(r, p_i) for r in range(nseq) for p_i in range(npage)]
    grid_spec = pltpu.PrefetchScalarGridSpec(
        num_scalar_prefetch=1,
        grid=(b // nseq,),
        in_specs=[tok, tok, tok] + pages * 2,
        out_specs=pl.BlockSpec((nseq, lp, W_BRANCH), lambda i, pt: (i, 0, 0)),
    )
    return pl.pallas_call(
        kern,
        grid_spec=grid_spec,
        out_shape=jax.ShapeDtypeStruct((b, lp, W_BRANCH), F32),
        compiler_params=pltpu.CompilerParams(dimension_semantics=("arbitrary",), vmem_limit_bytes=VMEM_LIMIT),
    )(page_table, q3, k3, v3, *([cache_k] * (nseq * npage)), *([cache_v] * (nseq * npage)))


def _merge_kernel(oa_ref, ob_ref, oc_ref, zc_ref, gate_ref, x_ref, ple_ref, wb_ref, bg_ref, wo_ref,
                  wpg_ref, bpg_ref, wple_ref, y_ref):
    branches = (oa_ref[...], ob_ref[...], oc_ref[...] * _silu(zc_ref[...]))
    up = None
    for n, o in enumerate(branches):
        sl = slice(n * D_MODEL, (n + 1) * D_MODEL)
        gate = _sigmoid(gate_ref[:, sl] + bg_ref[:, sl])
        term = gate * jnp.dot(o.astype(BF16), wb_ref[n], preferred_element_type=F32)
        up = term if up is None else up + term
    y = x_ref[...] + jnp.dot(up.astype(BF16), wo_ref[...], preferred_element_type=F32)
    pg = _sigmoid(jnp.dot(y.astype(BF16), wpg_ref[...], preferred_element_type=F32) + bpg_ref[...])
    y_ref[...] = y + pg * jnp.dot(ple_ref[...].astype(BF16), wple_ref[...], preferred_element_type=F32)


def _merge(oa, ob, oc, proj2, x2, ple3, ple_layer, wb, bg, wo, wpg, bpg, wple, w_layer, tm):
    t = x2.shape[0]
    full = lambda shape: pl.BlockSpec((None,) + shape, lambda i: (w_layer,) + (0,) * len(shape))
    row = lambda w: pl.BlockSpec((tm, w), lambda i: (i, 0))
    return pl.pallas_call(
        _merge_kernel,
        grid=(t // tm,),
        in_specs=[row(W_BRANCH), row(W_BRANCH), row(W_BRANCH),
                  pl.BlockSpec((tm, W_BRANCH), lambda i: (i, C_Z_C // W_BRANCH)),
                  pl.BlockSpec((tm, N_BRANCH * D_MODEL), lambda i: (i, C_GATE // (N_BRANCH * D_MODEL))),
                  row(D_MODEL), pl.BlockSpec((None, tm, D_PLE), lambda i: (ple_layer, i, 0)),
                  full((N_BRANCH, W_BRANCH, D_MODEL)), full((1, N_BRANCH * D_MODEL)), full((D_MODEL, D_MODEL)),
                  full((D_MODEL, D_MODEL)), full((1, D_MODEL)), full((D_PLE, D_MODEL))],
        out_specs=row(D_MODEL),
        out_shape=jax.ShapeDtypeStruct((t, D_MODEL), F32),
        compiler_params=pltpu.CompilerParams(dimension_semantics=("arbitrary",), vmem_limit_bytes=VMEM_LIMIT),
    )(oa, ob, oc, proj2, proj2, x2, ple3, wb, bg, wo, wpg, bpg, wple)


def _arrange_w_in(w):
    offs = np.cumsum((0,) + IN_SPLITS)
    qkv_a, z_a, beta, alpha, f_b, z_b, qkv_c, z_c, gate = [w[:, offs[i]:offs[i + 1]] for i in range(9)]
    d = w.shape[0]
    ba = jnp.concatenate([beta, alpha, jnp.zeros((d, LANES - 2 * H_A), w.dtype)], axis=1)
    tail = jnp.zeros((d, N_PROJ - C_BA - LANES), w.dtype)
    out = jnp.concatenate([qkv_a, z_a, f_b[:, :3 * W_BRANCH], z_b, qkv_c, z_c, gate,
                           f_b[:, 3 * W_BRANCH:], ba, tail], axis=1)
    return out.astype(BF16)


def _rope_frequencies():
    half = ROPE_DIM // 2
    inv = ROPE_THETA ** (-jnp.arange(half, dtype=F32) / half)
    d = np.arange(LANES) % D_C
    return jnp.where(d < ROPE_DIM, inv[d % half], 0.0).reshape(1, LANES)


def _layer_params(i, p):
    row = lambda v: v.reshape(1, -1)
    lanes_ba = lambda v: jnp.zeros((1, LANES), F32).at[0, H_A:2 * H_A].set(v)
    lora_pad = jnp.zeros((LORA, W_BRANCH), F32)
    head = np.arange(W_BRANCH) // N_B
    return dict(
        norm_g=row(p['norm_g'][i]),
        w_in=_arrange_w_in(p['w_in'][i]),
        conv=p['conv_a'][i],
        pa=lanes_ba(p['a_log'][i]), pb=lanes_ba(p['dt_bias'][i]), onorm=row(p['onorm_a'][i]),
        rw_vecs=(row(p['mu_b'][i]), row(p['w0_b'][i]), row(p['a0_b'][i]), row(p['kk_b'][i]), row(p['ka_b'][i]),
                 row(p['rk_b'][i]), row(p['gn_g_b'][i]), row(p['gn_b_b'][i])),
        w2p=jnp.concatenate([p['w2_b'][i], lora_pad], axis=0).astype(BF16),
        a2p=jnp.concatenate([lora_pad, p['a2_b'][i]], axis=0).astype(BF16),
        bd=jnp.asarray(head[:, None] == head[None, :], BF16),
        qn=row(jnp.tile(p['qn_c'][i], H_C)), kn=row(jnp.tile(p['kn_c'][i], H_C)),
    )


def _hist_rows(rows):
    b, n, c = rows.shape
    return jnp.concatenate([jnp.zeros((b, HIST - n, c), F32), rows], axis=1)


def _layer(x3, ple3, ple_layer, lp, *, l_valid, rope, dn_s, rw_s, state_layer, dn_buf, rw_prev, nseg, lt, ls, tm, attend):
    b, l, d = x3.shape
    t = b * l
    proj2 = _proj(x3.reshape(t, d), lp['norm_g'], lp['w_in'], min(t, 1024), 1920)
    proj3 = proj2.reshape(b, l, N_PROJ)

    oa, dn_new, ob, rw_new = _mixers(proj3, _hist_rows(dn_buf), dn_s, _hist_rows(rw_prev[:, None, :]), rw_s,
                                     state_layer, lp, nseg=nseg, lt=lt, ls=ls, l_valid=l_valid)
    qr, kr = _qkprep(proj2, lp['qn'], lp['kn'], rope, lp['bd'], tm=min(tm, t))
    oc, v2 = attend(qr, kr, proj3)

    y2 = _merge(oa.reshape(t, W_BRANCH), ob.reshape(t, W_BRANCH), oc.reshape(t, W_BRANCH), proj2,
                x3.reshape(t, d), ple3, ple_layer, lp['wb'], lp['bg'], lp['wo'], lp['wpg'],
                lp['bpg'], lp['wple'], lp['w_layer'], min(tm, t))

    f_last = jnp.concatenate([proj3[:, l_valid - 1, C_RKV:C_RKV + 3 * W_BRANCH],
                              proj3[:, l_valid - 1, C_WLAL:C_WLAL + 2 * LORA]], axis=-1)
    new_state = (kr.reshape(b, l, H_C, D_C)[:, :l_valid], v2.reshape(b, l, H_C, D_C)[:, :l_valid], dn_new,
                 proj3[:, l_valid - (CONV_W - 1):l_valid, C_QKV_A:C_QKV_A + QKV_A],
                 rw_new, f_last)
    return y2.reshape(b, l, d), new_state


def _attend_prompt(b, l):
    def attend(qr, kr, proj3):
        shp = (b, l, W_BRANCH)
        o3, v3 = _moba_prompt(qr.reshape(shp), kr.reshape(shp), proj3)
        return o3.reshape(b * l, W_BRANCH), v3.reshape(b * l, W_BRANCH)
    return attend


def _attend_decode(b, lp_, lq, page_table, cache_k4, cache_v4, layer):
    def attend(qr, kr, proj3):
        shp = (b, lp_, W_BRANCH)
        v3 = proj3[:, :, C_QKV_C + 2 * W_BRANCH:C_QKV_C + 3 * W_BRANCH]
        o3 = _moba_decode(page_table, qr.reshape(shp), kr.reshape(shp), v3, cache_k4, cache_v4, layer, lq)
        return o3.reshape(b * lp_, W_BRANCH), v3.reshape(b * lp_, W_BRANCH)
    return attend


DEC_PAD = 8


def kernel(x_prompt, x_sample, cache_k, cache_v, state_dn, state_dn_conv, state_rwkv, state_rwkv_shift,
           page_table, p_prompt, p_sample, norm_g, w_in, conv_a, a_log, dt_bias, onorm_a, mu_b, w0_b, w2_b,
           a0_b, a2_b, kk_b, ka_b, rk_b, gn_g_b, gn_b_b, qn_c, kn_c, w_branch, b_gate, w_out, w_ple, w_pg,
           b_pg):
    params = dict(norm_g=norm_g, w_in=w_in, conv_a=conv_a, a_log=a_log, dt_bias=dt_bias, onorm_a=onorm_a,
                  mu_b=mu_b, w0_b=w0_b, w2_b=w2_b, a0_b=a0_b, a2_b=a2_b, kk_b=kk_b, ka_b=ka_b, rk_b=rk_b,
                  gn_g_b=gn_g_b, gn_b_b=gn_b_b, qn_c=qn_c, kn_c=kn_c, w_branch=w_branch, b_gate=b_gate,
                  w_out=w_out, w_ple=w_ple, w_pg=w_pg, b_pg=b_pg)
    depth = norm_g.shape[0]
    b_p, l_p, _ = x_prompt.shape
    b_s, l_s, _ = x_sample.shape
    n_phys, page = cache_k.shape[1], cache_k.shape[2]
    past_len = page_table.shape[1] * page
    assert l_p % 512 == 0 and CONV_W - 1 <= l_s <= DEC_PAD and b_s % (CHUNK // DEC_PAD) == 0
    assert past_len % MOBA_BLOCK == 0 and MOBA_BLOCK % page == 0

    pad_rows = lambda a: jnp.pad(a, ((0, 0), (0, DEC_PAD - l_s), (0, 0)))
    xs = pad_rows(x_sample)
    cache_k4 = cache_k.transpose(0, 1, 3, 4, 2)
    cache_v4 = cache_v.transpose(0, 1, 3, 4, 2)
    inv = _rope_frequencies()
    tm_s = min(512, b_s * DEC_PAD)
    rope_p = _rope_table(inv, l_p, 0)
    rope_s = tuple(jnp.tile(tab, (tm_s // DEC_PAD, 1)) for tab in _rope_table(inv, DEC_PAD, past_len))
    xp = x_prompt
    new_p, new_s = [], []
    merge_w = dict(wb=w_branch.astype(BF16), bg=b_gate[:, None, :], wo=w_out.astype(BF16),
                   wpg=w_pg.astype(BF16), bpg=b_pg[:, None, :], wple=w_ple.astype(BF16))
    for i in range(depth):
        lp = dict(_layer_params(i, params), w_layer=i, **merge_w)
        xp, st_p = _layer(xp, p_prompt.reshape(depth, b_p * l_p, D_PLE), i, lp, l_valid=l_p, rope=rope_p,
                          dn_s=jnp.zeros((1, b_p, H_A, DK_A, DV_A), F32),
                          rw_s=jnp.zeros((1, b_p, H_B, N_B, N_B), F32), state_layer=0,
                          dn_buf=jnp.zeros((b_p, CONV_W - 1, QKV_A), F32), rw_prev=jnp.zeros((b_p, SHIFT_B), F32),
                          nseg=1, lt=512, ls=CHUNK, tm=512, attend=_attend_prompt(b_p, l_p))
        new_p.append(st_p)
        xs, st_s = _layer(xs, pad_rows(p_sample[i]).reshape(1, b_s * DEC_PAD, D_PLE), 0, lp, l_valid=l_s, rope=rope_s,
                          dn_s=state_dn, rw_s=state_rwkv, state_layer=i, dn_buf=state_dn_conv[i],
                          rw_prev=state_rwkv_shift[i], nseg=CHUNK // DEC_PAD, lt=DEC_PAD, ls=DEC_PAD, tm=512,
                          attend=_attend_decode(b_s, DEC_PAD, l_s, page_table, cache_k4, cache_v4, i))
        new_s.append(st_s)
    outs_p = [jnp.stack(t) for t in zip(*new_p)]
    outs_s = [jnp.stack(t) for t in zip(*new_s)]
    return (xp, xs[:, :l_s], *outs_p, *outs_s)
```

```python
import functools

import numpy as np
import jax
import jax.numpy as jnp
from jax import lax
from jax.experimental import pallas as pl
from jax.experimental.pallas import tpu as pltpu

F32 = jnp.float32
BF16 = jnp.bfloat16

D_MODEL = 1024
D_PLE = 256
N_BRANCH = 3
W_BRANCH = 512
NORM_EPS = 1e-6
DK_A = 128
DV_A = 128
H_A = 4
QKV_A = 1536
CONV_W = 4
N_B = 64
H_B = 8
LORA = 64
SHIFT_B = 3 * W_BRANCH + 2 * LORA
GN_EPS = 64e-5
D_C = 64
H_C = 8
MOBA_BLOCK = 256
MOBA_TOPK = 3
ROPE_DIM = 16
ROPE_THETA = 500000.0
IN_SPLITS = (QKV_A, W_BRANCH, H_A, H_A, SHIFT_B, W_BRANCH, 3 * W_BRANCH, W_BRANCH, N_BRANCH * D_MODEL)

C_QKV_A = 0
C_Z_A = 1536
C_RKV = 2048
C_Z_B = 3584
C_QKV_C = 4096
C_Z_C = 5632
C_GATE = 6144
C_WLAL = 9216
C_BA = 9344
N_PROJ = 9600

CHUNK = 64
HIST = 8
MAX_CHUNKS_PER_TILE = 8
LANES = 128
VMEM_LIMIT = 62 * 1024 * 1024

NN = (((1,), (0,)), ((), ()))
BNN = (((2,), (1,)), ((0,), (0,)))
NT = (((1,), (1,)), ((), ()))
TN = (((0,), (0,)), ((), ()))
HIGHEST = lax.Precision.HIGHEST


def _mm(a, b, dims=NN):
    return lax.dot_general(a.astype(BF16), b.astype(BF16), dims, preferred_element_type=F32)


def _mm32(a, b, dims=NN):
    return lax.dot_general(a, b, dims, precision=HIGHEST, preferred_element_type=F32)


def _sigmoid(x):
    return 1.0 / (1.0 + jnp.exp(-x))


def _silu(x):
    return x * _sigmoid(x)


def _softplus(x):
    return jnp.maximum(x, 0.0) + jnp.log(1.0 + jnp.exp(-jnp.abs(x)))


def _iota2(shape, dim):
    return lax.broadcasted_iota(jnp.int32, shape, dim)


def _split(x):
    hi = x.astype(BF16)
    return hi, (x - hi.astype(F32)).astype(BF16)


def _dotb(a, b, dims=NN):
    return lax.dot_general(a, b, dims, preferred_element_type=F32)


def _mm_sel(sel, x):
    hi, lo = _split(x)
    return _dotb(sel, hi) + _dotb(sel, lo)


def _mm3(a, b, dims=NN):
    ah, al = _split(a)
    bh, bl = _split(b)
    return _dotb(ah, bh, dims) + (_dotb(ah, bl, dims) + _dotb(al, bh, dims))


SOLVE_MM = _mm


def _inv_unit_lower(lmats, nilpotent, mm):
    n = lmats[0].shape[-1]
    dims = NN if lmats[0].ndim == 2 else BNN
    eye = (_iota2((n, n), 0) == _iota2((n, n), 1)).astype(F32)
    ps = [-lmat for lmat in lmats]
    accs = [eye + p for p in ps]
    span = 2
    while span < nilpotent:
        ps = [mm(p, p, dims) for p in ps]
        accs = [acc + mm(acc, p, dims) for acc, p in zip(accs, ps)]
        span *= 2
    return accs


def _head_sums(xs, bd):
    n = xs[0].shape[0]
    out = jnp.dot(jnp.concatenate([x.astype(BF16) for x in xs], axis=0), bd, preferred_element_type=F32)
    return [out[i * n:(i + 1) * n] for i in range(len(xs))]


def _head_sum(x, bd):
    return _head_sums([x], bd)[0]


def _proj_kernel(x_ref, g_ref, w_ref, o_ref):
    x = x_ref[...]
    h = x * lax.rsqrt(jnp.mean(x * x, axis=-1, keepdims=True) + NORM_EPS) * g_ref[...]
    o_ref[...] = jnp.dot(h.astype(BF16), w_ref[...], preferred_element_type=F32)


def _proj(x2d, g, w, tm, tn):
    t, d = x2d.shape
    n = w.shape[1]
    return pl.pallas_call(
        _proj_kernel,
        grid=(n // tn, t // tm),
        in_specs=[pl.BlockSpec((tm, d), lambda j, i: (i, 0)),
                  pl.BlockSpec((1, d), lambda j, i: (0, 0)),
                  pl.BlockSpec((d, tn), lambda j, i: (0, j))],
        out_specs=pl.BlockSpec((tm, tn), lambda j, i: (i, j)),
        out_shape=jax.ShapeDtypeStruct((t, n), F32),
        compiler_params=pltpu.CompilerParams(dimension_semantics=("arbitrary", "arbitrary"),
                                             vmem_limit_bytes=VMEM_LIMIT),
    )(x2d, g, w)


def _chunk_masks(n, ls):
    r = _iota2((n, n), 0)
    c = _iota2((n, n), 1)
    same = (r // ls) == (c // ls)
    return same, same & (r >= c), same & (r > c)


def _valid_rows(l_idx, lt, r0, ls, l_valid):
    t = l_idx * lt + r0 + (_iota2((CHUNK, 1), 0) % ls)
    return t < l_valid


def _gdn_parts(qkv_ref, z_ref, ba_ref, buf_ref, s0_ref, cw_ref, pa_ref, pb_ref, on_ref,
               o_ref, sout_ref, xs_ref, st_ref, *, l_idx, nseg, lt, ls, l_valid, l_total):
    nrow = H_A * CHUNK

    @pl.when(l_idx == 0)
    def _():
        xs_ref[...] = buf_ref[...]
        st_ref[...] = s0_ref[...]

    same_r, c_incl, c_strict = _chunk_masks(CHUNK, ls)
    tri_r = c_incl.astype(BF16)
    same_r = same_r.astype(BF16)
    eye_c = (_iota2((CHUNK, CHUNK), 0) == _iota2((CHUNK, CHUNK), 1)).astype(F32)
    cw = cw_ref[...]
    neg_a = -jnp.exp(pa_ref[...])
    pb = pb_ref[...]
    onorm = on_ref[...]

    def conv(s, r0, col):
        cols = slice(col, col + LANES)
        if r0 == 0:
            win = jnp.concatenate([xs_ref[s, :, cols], qkv_ref[s, 0:ls, cols]], axis=0)
        else:
            win = qkv_ref[s, r0 - HIST:r0 + ls, cols]
        w = cw[:, cols]
        acc = pltpu.roll(win, 3, 0)[HIST:] * w[0:1] + pltpu.roll(win, 2, 0)[HIST:] * w[1:2]
        acc = acc + pltpu.roll(win, 1, 0)[HIST:] * w[2:3]
        acc = acc + win[HIST:] * w[3:4]
        return _silu(acc)

    def part(r0, col):
        return jnp.concatenate([conv(s, r0, col) for s in range(nseg)], axis=0)

    def local(c):
        r0 = c * ls
        ba = jnp.concatenate([ba_ref[s, r0:r0 + ls, :] for s in range(nseg)], axis=0)
        beta_all = _sigmoid(ba)
        g_all = neg_a * _softplus(ba + pb)
        if l_valid < l_total:
            ok = _valid_rows(l_idx, lt, r0, ls, l_valid)
            beta_all = jnp.where(ok, beta_all, 0.0)
            g_all = jnp.where(ok, g_all, 0.0)
        gcum = _mm_sel(tri_r, g_all)
        if nseg == 1:
            gtot = jnp.broadcast_to(gcum[CHUNK - 1:CHUNK], gcum.shape)
        else:
            gtot = _mm_sel(same_r, g_all)

        qs, ks, vs, betas, gcs, gts, lmats, qks = [], [], [], [], [], [], [], []
        for h in range(H_A):
            q = part(r0, h * DK_A)
            k = part(r0, H_A * DK_A + h * DK_A)
            v = part(r0, 2 * H_A * DK_A + h * DV_A)
            q = q * lax.rsqrt(jnp.sum(q * q, axis=-1, keepdims=True) + 1e-6) * (DK_A ** -0.5)
            k = k * lax.rsqrt(jnp.sum(k * k, axis=-1, keepdims=True) + 1e-6)
            beta_h = beta_all[:, h:h + 1]
            g_h = gcum[:, H_A + h:H_A + h + 1]
            g_row = jnp.sum(eye_c * g_h, axis=0, keepdims=True)
            decay = jnp.exp(jnp.where(c_incl, g_h - g_row, -jnp.inf))
            kq_k = _mm(jnp.concatenate([k, q], axis=0), k, NT)
            lmats.append(jnp.where(c_strict, beta_h * kq_k[:CHUNK] * decay, 0.0))
            qks.append(jnp.where(c_incl, kq_k[CHUNK:] * decay, 0.0))
            qs.append(q)
            ks.append(k)
            vs.append(v)
            betas.append(beta_h)
            gcs.append(g_h)
            gts.append(gtot[:, H_A + h:H_A + h + 1])
        q_st = jnp.concatenate(qs, axis=0)
        k_st = jnp.concatenate(ks, axis=0)
        v_st = jnp.concatenate(vs, axis=0)
        beta = jnp.concatenate(betas, axis=0)
        g_col = jnp.concatenate(gcs, axis=0)
        gt_col = jnp.concatenate(gts, axis=0)
        return dict(
            r0=r0, gt_col=gt_col,
            lmat=jnp.stack(lmats),
            rhs=jnp.concatenate([v_st * beta, k_st * (beta * jnp.exp(g_col))], axis=1),
            qk=jnp.stack(qks),
            qg=q_st * jnp.exp(g_col),
            kd=k_st * jnp.exp(gt_col - g_col))

    def advance(loc, sol):
        r0, gt_col, qk, qg, kd = loc['r0'], loc['gt_col'], loc['qk'], loc['qg'], loc['kd']
        bv = sol[:, :DV_A]
        wk = sol[:, DV_A:]
        wks, qgs = [], []
        for h in range(H_A):
            for s in range(nseg):
                a = h * CHUNK + s * ls
                both = _mm(jnp.concatenate([wk[a:a + ls], qg[a:a + ls]], axis=0), st_ref[s, h])
                wks.append(both[:ls])
                qgs.append(both[ls:])
        u = bv - jnp.concatenate(wks, axis=0)
        o = jnp.concatenate(qgs, axis=0) + _mm(qk, u.reshape(H_A, CHUNK, DV_A), BNN).reshape(nrow, DV_A)

        for h in range(H_A):
            a = h * CHUNK
            for s in range(nseg):
                b = a + s * ls
                if nseg == 1:
                    kd_s = kd[a:a + CHUNK]
                else:
                    rows = _iota2((CHUNK, 1), 0) // ls
                    kd_s = jnp.where(rows == s, kd[a:a + CHUNK], 0.0)
                upd = _mm(kd_s, u[a:a + CHUNK], TN)
                st_ref[s, h] = st_ref[s, h] * jnp.exp(gt_col[b:b + 1]) + upd
            oh = o[a:a + CHUNK]
            oh = oh * lax.rsqrt(jnp.mean(oh * oh, axis=-1, keepdims=True) + NORM_EPS) * onorm
            for s in range(nseg):
                zz = z_ref[s, r0:r0 + ls, h * DV_A:(h + 1) * DV_A]
                o_ref[s, r0:r0 + ls, h * DV_A:(h + 1) * DV_A] = oh[s * ls:(s + 1) * ls] * _silu(zz)

    def prepare(loc, invs):
        return SOLVE_MM(invs[0], loc['rhs'].reshape(H_A, CHUNK, DK_A + DV_A), BNN).reshape(nrow, DK_A + DV_A)

    def keep_history():
        xs_ref[...] = qkv_ref[:, lt - HIST:lt, :]

    def epilogue():
        sout_ref[...] = st_ref[...]

    return dict(local=local, lmats=lambda loc: [loc['lmat']], prepare=prepare, advance=advance,
                keep_history=keep_history, epilogue=epilogue)


GRP = 4 * N_B


def _rwkv_parts(r_ref, k_ref, v_ref, wa_ref, z_ref, prev_ref, s0_ref, mu_ref, w0_ref, w2_ref,
                a0_ref, a2_ref, kkp_ref, kap_ref, rk_ref, gg_ref, gb_ref, bd_ref,
                o_ref, sout_ref, fs_ref, st_ref, *, l_idx, nseg, lt, ls, l_valid, l_total):
    nrow = 4 * CHUNK
    ngrp = W_BRANCH // GRP

    @pl.when(l_idx == 0)
    def _():
        fs_ref[...] = prev_ref[...]
        st_ref[...] = jnp.zeros(st_ref.shape, F32)
        for s in range(nseg):
            for h in range(H_B):
                g, j = divmod(h, 4)
                st_ref[s, g, j * N_B:(j + 1) * N_B, j * N_B:(j + 1) * N_B] = s0_ref[s, h]

    sources = {0: r_ref, W_BRANCH: k_ref, 2 * W_BRANCH: v_ref, 3 * W_BRANCH: wa_ref}

    same_r, tri_r, _ = _chunk_masks(CHUNK, ls)
    tri_r = tri_r.astype(BF16)
    same_r = same_r.astype(BF16)
    lane_head = _iota2((1, GRP), 1) // N_B
    head_masks = [(lane_head == j).astype(F32) for j in range(4)]
    bd_state = (_iota2((GRP, GRP), 0) // N_B) == (_iota2((GRP, GRP), 1) // N_B)
    head_rows = ((_iota2((nrow, GRP), 0) // CHUNK) == (_iota2((nrow, GRP), 1) // N_B)).astype(F32)
    t_row = _iota2((nrow, 2 * CHUNK), 0) % CHUNK
    t_col = _iota2((nrow, 2 * CHUNK), 1) % CHUNK
    t_same = (t_row // ls) == (t_col // ls)
    t_incl = t_same & (t_col <= t_row)
    t_strict = t_same & (t_col < t_row)
    mu = mu_ref[...]
    bd = bd_ref[...]

    def mixed(r0, col, width):
        outs = []
        for s in range(nseg):
            if r0 == 0:
                win = jnp.concatenate([fs_ref[s, :, col:col + width], sources[col][s, 0:ls, :]], axis=0)
            else:
                win = sources[col][s, r0 - HIST:r0 + ls, :]
            cur = win[HIST:]
            prev = pltpu.roll(win, 1, 0)[HIST:]
            outs.append(cur + (prev - cur) * mu[:, col:col + width])
        return jnp.concatenate(outs, axis=0)

    def stack_masked(x):
        return jnp.concatenate([x * head_masks[j] for j in range(4)], axis=0)

    def unstack(x):
        return x[0:CHUNK] + x[CHUNK:2 * CHUNK] + x[2 * CHUNK:3 * CHUNK] + x[3 * CHUNK:4 * CHUNK]

    def local(c):
        r0 = c * ls
        r = mixed(r0, 0, W_BRANCH)
        k = mixed(r0, W_BRANCH, W_BRANCH)
        v = mixed(r0, 2 * W_BRANCH, W_BRANCH)
        wa = mixed(r0, 3 * W_BRANCH, 2 * LORA)
        w_raw = w0_ref[...] + _mm(jnp.tanh(wa), w2_ref[...])
        logw = -jnp.exp(-_softplus(-w_raw) - 0.5)
        a_sig = _sigmoid(a0_ref[...] + _mm(wa, a2_ref[...]))
        kk = k * kkp_ref[...]
        k_mod = k * (1.0 + (a_sig - 1.0) * kap_ref[...])
        kk_ss, rk_sum = _head_sums([kk * kk, r * k_mod * rk_ref[...]], bd)
        kk = kk * lax.rsqrt(kk_ss + 1e-6)
        b_vec = kk * a_sig
        if l_valid < l_total:
            ok = _valid_rows(l_idx, lt, r0, ls, l_valid)
            logw = jnp.where(ok, logw, 0.0)
            b_vec = jnp.where(ok, b_vec, 0.0)
            k_mod = jnp.where(ok, k_mod, 0.0)
        gcum = _mm_sel(tri_r, logw)
        if nseg == 1:
            gtot = jnp.broadcast_to(gcum[CHUNK - 1:CHUNK], gcum.shape)
        else:
            gtot = _mm_sel(same_r, logw)
        e_neg = jnp.exp(-gcum)
        e_end = jnp.exp(gtot - gcum)
        a_t = -kk * jnp.exp(gcum - logw)
        b_t = b_vec * e_neg
        k_t = k_mod * e_neg
        r_t = r * jnp.exp(gcum)
        b_h = b_vec * e_end
        k_h = k_mod * e_end

        groups = []
        for g in range(ngrp):
            sl = slice(g * GRP, (g + 1) * GRP)
            a_g, b_g, k_g, r_g, v_g = a_t[:, sl], b_t[:, sl], k_t[:, sl], r_t[:, sl], v[:, sl]
            a_st = stack_masked(a_g)
            r_st = stack_masked(r_g)
            both = _mm(jnp.concatenate([a_st, r_st], axis=0), jnp.concatenate([b_g, k_g], axis=0), NT)
            a_bk = jnp.where(t_strict, both[:nrow], 0.0)
            r_bk = jnp.where(t_incl, both[nrow:], 0.0)
            zero_v = jnp.concatenate([jnp.zeros_like(v_g), v_g], axis=0)
            groups.append(dict(
                a_g=a_g, r_g=r_g, v_g=v_g, b_h=b_h[:, sl], k_h=k_h[:, sl], gtot=gtot[:, sl],
                neg_aab=-a_bk[:, :CHUNK].reshape(4, CHUNK, CHUNK),
                kv_u=_mm(a_bk, zero_v) * head_rows, r_bk=r_bk))
        return dict(r0=r0, v=v, rk_sum=rk_sum, groups=groups)

    def advance(loc, tinvs):
        grps = loc['groups']
        ams, rms = [], []
        for g, grp in enumerate(grps):
            parts = [_mm(jnp.concatenate([grp['a_g'][s * ls:(s + 1) * ls], grp['r_g'][s * ls:(s + 1) * ls]], axis=0),
                         st_ref[s, g], NT) for s in range(nseg)]
            ams.append(jnp.concatenate([p[:ls] for p in parts], axis=0))
            rms.append(jnp.concatenate([p[ls:] for p in parts], axis=0))
        us = [unstack(SOLVE_MM(tinv, (stack_masked(am) + grp['kv_u']).reshape(4, CHUNK, GRP), BNN)
                      .reshape(nrow, GRP)) for grp, tinv, am in zip(grps, tinvs, ams)]
        ys = [rm + unstack(_mm(grp['r_bk'], jnp.concatenate([u, grp['v_g']], axis=0)) * head_rows)
              for grp, u, rm in zip(grps, us, rms)]
        for g, (grp, u) in enumerate(zip(grps, us)):
            v_g = grp['v_g']
            for s in range(nseg):
                if nseg == 1:
                    u_s, v_s = u, v_g
                else:
                    rows = _iota2((CHUNK, 1), 0) // ls
                    u_s = jnp.where(rows == s, u, 0.0)
                    v_s = jnp.where(rows == s, v_g, 0.0)
                upd = _mm(jnp.concatenate([u_s, v_s], axis=0),
                          jnp.concatenate([grp['b_h'], grp['k_h']], axis=0), TN)
                dec = jnp.exp(grp['gtot'][s * ls:s * ls + 1])
                st_ref[s, g] = st_ref[s, g] * dec + jnp.where(bd_state, upd, 0.0)
        y = jnp.concatenate(ys, axis=1)
        mean = _head_sums([y], bd)[0] * (1.0 / N_B)
        yc = y - mean
        var = _head_sums([yc * yc], bd)[0] * (1.0 / N_B)
        yn = yc * lax.rsqrt(var + GN_EPS) * gg_ref[...] + gb_ref[...]
        out = yn + loc['rk_sum'] * loc['v']
        for s in range(nseg):
            zz = z_ref[s, loc['r0']:loc['r0'] + ls, :]
            o_ref[s, loc['r0']:loc['r0'] + ls, :] = out[s * ls:(s + 1) * ls] * _silu(zz)

    def keep_history():
        for col, ref in sources.items():
            fs_ref[:, :, col:col + ref.shape[-1]] = ref[:, lt - HIST:lt, :]

    def epilogue():
        for s in range(nseg):
            for h in range(H_B):
                g, j = divmod(h, 4)
                sout_ref[s, h] = st_ref[s, g, j * N_B:(j + 1) * N_B, j * N_B:(j + 1) * N_B]

    return dict(local=local, lmats=lambda loc: [grp['neg_aab'] for grp in loc['groups']],
                prepare=lambda loc, invs: invs, advance=advance, keep_history=keep_history, epilogue=epilogue)


N_GDN_IN, N_RWKV_IN = 9, 18


def _mixers_kernel(*refs, nseg, lt, ls, l_valid, l_total):
    gdn_in = refs[:N_GDN_IN]
    rwkv_in = refs[N_GDN_IN:N_GDN_IN + N_RWKV_IN]
    oa_ref, dn_ref, ob_ref, rw_ref, xs_ref, st_a_ref, fs_ref, st_b_ref = refs[N_GDN_IN + N_RWKV_IN:]
    l_idx = pl.program_id(1)
    cfg = dict(l_idx=l_idx, nseg=nseg, lt=lt, ls=ls, l_valid=l_valid, l_total=l_total)
    mixers = [_gdn_parts(*gdn_in, oa_ref, dn_ref, xs_ref, st_a_ref, **cfg),
              _rwkv_parts(*rwkv_in, ob_ref, rw_ref, fs_ref, st_b_ref, **cfg)]
    nchunk = lt // ls
    assert nchunk <= MAX_CHUNKS_PER_TILE

    def tile():
        chunks = range(nchunk)
        locs = [[m['local'](c) for c in chunks] for m in mixers]
        lmats = [[m['lmats'](loc) for loc in mlocs] for m, mlocs in zip(mixers, locs)]
        invs = _inv_unit_lower([x for per_mixer in lmats for per_loc in per_mixer for x in per_loc], ls, SOLVE_MM)
        prepared, at = [], 0
        for m, mlocs, mlm in zip(mixers, locs, lmats):
            row = []
            for loc, lm in zip(mlocs, mlm):
                row.append(m['prepare'](loc, invs[at:at + len(lm)]))
                at += len(lm)
            prepared.append(row)
        for j in chunks:
            for m, mlocs, mprep in zip(mixers, locs, prepared):
                m['advance'](mlocs[j], mprep[j])
        for m in mixers:
            m['keep_history']()

    tile()

    @pl.when(l_idx == pl.num_programs(1) - 1)
    def _():
        for m in mixers:
            m['epilogue']()


def _mixers(proj3, buf8, dn_s0, prev8, rw_s0, layer, lp, *, nseg, lt, ls, l_valid):
    b, l, _ = proj3.shape
    ngrp = W_BRANCH // GRP
    kern = functools.partial(_mixers_kernel, nseg=nseg, lt=lt, ls=ls, l_valid=l_valid, l_total=l)
    full = lambda shape: pl.BlockSpec(shape, lambda i, j: (0,) * len(shape))
    blk = lambda col, w: pl.BlockSpec((nseg, lt, w), lambda i, j: (i, j, col // w))
    mu, w0, a0, kkp, kap, rk, gg, gb = lp['rw_vecs']
    gdn_specs = [blk(C_QKV_A, QKV_A), blk(C_Z_A, W_BRANCH), blk(C_BA, LANES),
                 pl.BlockSpec((nseg, HIST, QKV_A), lambda i, j: (i, 0, 0)),
                 pl.BlockSpec((None, nseg, H_A, DK_A, DV_A), lambda i, j: (layer, i, 0, 0, 0)),
                 full((CONV_W, QKV_A)), full((1, LANES)), full((1, LANES)), full((1, DV_A))]
    rwkv_specs = [blk(C_RKV, W_BRANCH), blk(C_RKV + W_BRANCH, W_BRANCH), blk(C_RKV + 2 * W_BRANCH, W_BRANCH),
                  blk(C_WLAL, 2 * LORA), blk(C_Z_B, W_BRANCH),
                  pl.BlockSpec((nseg, HIST, SHIFT_B), lambda i, j: (i, 0, 0)),
                  pl.BlockSpec((None, nseg, H_B, N_B, N_B), lambda i, j: (layer, i, 0, 0, 0)),
                  full((1, SHIFT_B)), full((1, W_BRANCH)), full((2 * LORA, W_BRANCH)),
                  full((1, W_BRANCH)), full((2 * LORA, W_BRANCH)), full((1, W_BRANCH)),
                  full((1, W_BRANCH)), full((1, W_BRANCH)), full((1, W_BRANCH)), full((1, W_BRANCH)),
                  full((W_BRANCH, W_BRANCH))]
    assert len(gdn_specs) == N_GDN_IN and len(rwkv_specs) == N_RWKV_IN
    tile_out = pl.BlockSpec((nseg, lt, W_BRANCH), lambda i, j: (i, j, 0))
    return pl.pallas_call(
        kern,
        grid=(b // nseg, l // lt),
        in_specs=gdn_specs + rwkv_specs,
        out_specs=[tile_out, pl.BlockSpec((nseg, H_A, DK_A, DV_A), lambda i, j: (i, 0, 0, 0)),
                   tile_out, pl.BlockSpec((nseg, H_B, N_B, N_B), lambda i, j: (i, 0, 0, 0))],
        out_shape=[jax.ShapeDtypeStruct((b, l, W_BRANCH), F32), jax.ShapeDtypeStruct((b, H_A, DK_A, DV_A), F32),
                   jax.ShapeDtypeStruct((b, l, W_BRANCH), F32), jax.ShapeDtypeStruct((b, H_B, N_B, N_B), F32)],
        scratch_shapes=[pltpu.VMEM((nseg, HIST, QKV_A), F32), pltpu.VMEM((nseg, H_A, DK_A, DV_A), F32),
                        pltpu.VMEM((nseg, HIST, SHIFT_B), F32), pltpu.VMEM((nseg, ngrp, GRP, GRP), F32)],
        compiler_params=pltpu.CompilerParams(dimension_semantics=("arbitrary", "arbitrary"),
                                             vmem_limit_bytes=VMEM_LIMIT),
    )(proj3, proj3, proj3, buf8, dn_s0, lp['conv'], lp['pa'], lp['pb'], lp['onorm'],
      proj3, proj3, proj3, proj3, proj3, prev8, rw_s0, mu, w0, lp['w2p'], a0, lp['a2p'], kkp, kap, rk, gg, gb,
      lp['bd'])


def _rope_table_kernel(inv_ref, cos_ref, sin_ref, *, pos0):
    n = cos_ref.shape[0]
    pos = (pos0 + pl.program_id(0) * n + _iota2((n, 1), 0)).astype(F32)
    ang = pos * inv_ref[...]
    cos_ref[...] = jnp.cos(ang)
    sin_ref[...] = jnp.sin(ang)


def _rope_table(inv, n, pos0, tile=512):
    tile = min(tile, n)
    blk = pl.BlockSpec((tile, LANES), lambda i: (i, 0))
    return pl.pallas_call(
        functools.partial(_rope_table_kernel, pos0=pos0),
        grid=(n // tile,),
        in_specs=[pl.BlockSpec((1, LANES), lambda i: (0, 0))],
        out_specs=[blk, blk],
        out_shape=[jax.ShapeDtypeStruct((n, LANES), F32)] * 2,
        compiler_params=pltpu.CompilerParams(dimension_semantics=("arbitrary",), vmem_limit_bytes=VMEM_LIMIT),
    )(inv)


def _qkprep_kernel(q_ref, k_ref, qn_ref, kn_ref, cos_ref, sin_ref, bd_ref, qo_ref, ko_ref):
    cos = cos_ref[...]
    sin = sin_ref[...]
    d = _iota2((1, LANES), 1) % D_C
    half = ROPE_DIM // 2
    c_main = jnp.where(d < ROPE_DIM, cos, 1.0)
    c_up = jnp.where(d < half, -sin, 0.0)
    c_dn = jnp.where((d >= half) & (d < ROPE_DIM), sin, 0.0)
    bd = bd_ref[...]

    def one(x_ref, g_ref, out_ref):
        x = x_ref[...]
        y = x * lax.rsqrt(_head_sum(x * x, bd) * (1.0 / D_C) + NORM_EPS) * g_ref[...]
        for blk in range(W_BRANCH // LANES):
            yb = y[:, blk * LANES:(blk + 1) * LANES]
            up = pltpu.roll(yb, LANES - half, 1)
            dn = pltpu.roll(yb, half, 1)
            out_ref[:, blk * LANES:(blk + 1) * LANES] = yb * c_main + up * c_up + dn * c_dn

    one(q_ref, qn_ref, qo_ref)
    one(k_ref, kn_ref, ko_ref)


def _qkprep(proj2, qn, kn, rope, bd, *, tm):
    t = proj2.shape[0]
    cos, sin = rope
    period = cos.shape[0] // tm
    full = lambda shape: pl.BlockSpec(shape, lambda i: (0,) * len(shape))
    table = pl.BlockSpec((tm, LANES), lambda i: (i % period, 0))
    return pl.pallas_call(
        _qkprep_kernel,
        grid=(t // tm,),
        in_specs=[pl.BlockSpec((tm, W_BRANCH), lambda i: (i, C_QKV_C // W_BRANCH)),
                  pl.BlockSpec((tm, W_BRANCH), lambda i: (i, C_QKV_C // W_BRANCH + 1)),
                  full((1, W_BRANCH)), full((1, W_BRANCH)), table, table, full((W_BRANCH, W_BRANCH))],
        out_specs=[pl.BlockSpec((tm, W_BRANCH), lambda i: (i, 0)),
                   pl.BlockSpec((tm, W_BRANCH), lambda i: (i, 0))],
        out_shape=[jax.ShapeDtypeStruct((t, W_BRANCH), F32), jax.ShapeDtypeStruct((t, W_BRANCH), F32)],
        compiler_params=pltpu.CompilerParams(dimension_semantics=("arbitrary",), vmem_limit_bytes=VMEM_LIMIT),
    )(proj2, proj2, qn, kn, cos, sin, bd)


def _rank_select(gate, nblk, limit, axis):
    idx = _iota2(gate.shape, axis)
    rank = jnp.zeros(gate.shape, F32)
    for m in range(nblk):
        gm = gate[m:m + 1, :] if axis == 0 else gate[:, m:m + 1]
        beats = (gm > gate) | ((gm == gate) & (m < idx))
        rank = rank + jnp.where(beats & (m < limit), 1.0, 0.0)
    return jnp.where((idx < limit) & (rank < MOBA_TOPK), 1.0, 0.0)


HEADS_PER_STEP = LANES // D_C


def _moba_prompt_kernel(q_ref, k_ref, v_ref, o_ref, vo_ref, *, nblk):
    blk = MOBA_BLOCK
    l = nblk * blk
    scale = D_C ** -0.5
    k = k_ref[...]
    q_t = q_ref[...].T
    v = v_ref[...]
    vo_ref[...] = v
    v_t = v.T.astype(BF16)
    kmean = jnp.concatenate([jnp.mean(k[n * blk:(n + 1) * blk], axis=0, keepdims=True) for n in range(nblk)],
                            axis=0)
    own = _iota2((1, l), 1) // blk
    kb = k.astype(BF16)
    causal = _iota2((blk, blk), 0) <= _iota2((blk, blk), 1)
    row_head = _iota2((LANES, 1), 0) // D_C
    units = [(e, j) for e in range(HEADS_PER_STEP) for j in range(nblk)]
    sels, qbs = [], []
    for e in range(HEADS_PER_STEP):
        q_e = jnp.where(row_head == e, q_t, 0.0)
        sels.append(_rank_select(_mm32(kmean, q_e), nblk, own, 0))
        qbs.append((q_e * scale).astype(BF16))
    scores = [jnp.dot(kb[:(j + 1) * blk], qbs[e][:, j * blk:(j + 1) * blk], preferred_element_type=F32)
              for e, j in units]
    probs, denoms = [], []
    for (e, j), s in zip(units, scores):
        cols = slice(j * blk, (j + 1) * blk)
        parts = [jnp.where(sels[e][n:n + 1, cols] > 0.0, s[n * blk:(n + 1) * blk], -jnp.inf) for n in range(j)]
        parts.append(jnp.where(causal, s[j * blk:], -jnp.inf))
        s = jnp.concatenate(parts, axis=0)
        p = jnp.exp(s - jnp.max(s, axis=0, keepdims=True))
        denoms.append(jnp.sum(p, axis=0, keepdims=True))
        probs.append(p.astype(BF16))
    tiles = [jnp.dot(v_t[e * D_C:(e + 1) * D_C, :(j + 1) * blk], p, preferred_element_type=F32) / d
             for (e, j), p, d in zip(units, probs, denoms)]
    outs = [jnp.concatenate(tiles[e * nblk:(e + 1) * nblk], axis=1) for e in range(HEADS_PER_STEP)]
    o_ref[...] = jnp.concatenate(outs, axis=0).T


def _moba_prompt(q3, k3, proj3):
    b, l, _ = q3.shape
    nblk = l // MOBA_BLOCK
    assert (4.0 ** round(np.log(D_C) / np.log(4.0))) == D_C
    kern = functools.partial(_moba_prompt_kernel, nblk=nblk)
    v_col = (C_QKV_C + 2 * W_BRANCH) // LANES
    cols = pl.BlockSpec((None, l, LANES), lambda i, hp: (i, 0, hp))
    return pl.pallas_call(
        kern,
        grid=(b, W_BRANCH // LANES),
        in_specs=[cols, cols, pl.BlockSpec((None, l, LANES), lambda i, hp: (i, 0, v_col + hp))],
        out_specs=[cols, cols],
        out_shape=[jax.ShapeDtypeStruct((b, l, W_BRANCH), F32)] * 2,
        compiler_params=pltpu.CompilerParams(dimension_semantics=("arbitrary", "arbitrary"),
                                             vmem_limit_bytes=VMEM_LIMIT),
    )(q3, k3, proj3)


SEQ_PER_STEP = 2


def _moba_decode_kernel(pt_ref, q_ref, kn_ref, vn_ref, *refs, npage, page, lq):
    nseq = q_ref.shape[0]
    k_refs = refs[:nseq * npage]
    v_refs = refs[nseq * npage:2 * nseq * npage]
    o_ref = refs[2 * nseq * npage]
    del pt_ref
    scale = D_C ** -0.5
    nrow = lq * H_C
    nblk = npage * page // MOBA_BLOCK
    lp = q_ref.shape[1]
    seqs = range(nseq)

    lane_head = _iota2((H_C, W_BRANCH), 1) // D_C
    own_head = (lane_head == _iota2((H_C, W_BRANCH), 0)).astype(F32)
    own_rows = jnp.concatenate([own_head] * lq, axis=0)
    lane = _iota2((1, LANES), 1)
    t_key = _iota2((nrow, LANES), 1)
    t_qry = _iota2((nrow, LANES), 0) // H_C
    pad = jnp.zeros((LANES - lp, W_BRANCH), F32)

    qbds = [jnp.concatenate([q_ref[r, t:t + 1, :] * own_head for t in range(lq)], axis=0) for r in seqs]
    qbs = [(qbd * scale).astype(BF16) for qbd in qbds]
    k_ts = [jnp.concatenate([k_refs[r * npage + i][...].reshape(W_BRANCH, page) for i in range(npage)], axis=1)
            for r in seqs]
    v_ts = [jnp.concatenate([v_refs[r * npage + i][...].reshape(W_BRANCH, page) for i in range(npage)], axis=1)
            for r in seqs]
    s_pasts = [jnp.dot(qb, k_t.astype(BF16), preferred_element_type=F32) for qb, k_t in zip(qbs, k_ts)]

    def block_key_sums(k_t):
        ksum = jnp.zeros((W_BRANCH, LANES), F32)
        for n in range(nblk):
            k_n = k_t[:, n * MOBA_BLOCK:(n + 1) * MOBA_BLOCK]
            folded = sum(k_n[:, i * LANES:(i + 1) * LANES] for i in range(MOBA_BLOCK // LANES))
            ksum = ksum + jnp.sum(folded, axis=-1, keepdims=True) * (lane == n).astype(F32)
        return ksum

    ksums = [block_key_sums(k_t) for k_t in k_ts]
    gates = [_mm3(qbd, ksum) * (1.0 / MOBA_BLOCK) for qbd, ksum in zip(qbds, ksums)]
    sels = [_rank_select(gate, nblk, nblk, 1) for gate in gates]
    s_pasts = [jnp.concatenate(
        [jnp.where(sel[:, n:n + 1] > 0.0, s[:, n * MOBA_BLOCK:(n + 1) * MOBA_BLOCK], -jnp.inf)
         for n in range(nblk)], axis=1) for sel, s in zip(sels, s_pasts)]

    k_news = [jnp.concatenate([kn_ref[r], pad], axis=0) for r in seqs]
    v_news = [jnp.concatenate([vn_ref[r], pad], axis=0) for r in seqs]
    s_owns = [jnp.where(t_key <= t_qry, lax.dot_general(qb, k_new.astype(BF16), NT, preferred_element_type=F32),
                        -jnp.inf) for qb, k_new in zip(qbs, k_news)]
    ms = [jnp.maximum(jnp.max(sp, axis=-1, keepdims=True), jnp.max(so, axis=-1, keepdims=True))
          for sp, so in zip(s_pasts, s_owns)]
    p_pasts = [jnp.exp(sp - m) for sp, m in zip(s_pasts, ms)]
    p_owns = [jnp.exp(so - m) for so, m in zip(s_owns, ms)]
    denoms = [jnp.sum(pp, axis=-1, keepdims=True) + jnp.sum(po, axis=-1, keepdims=True)
              for pp, po in zip(p_pasts, p_owns)]
    accs = [lax.dot_general(pp.astype(BF16), v_t.astype(BF16), NT, preferred_element_type=F32)
            for pp, v_t in zip(p_pasts, v_ts)]
    accs = [acc + jnp.dot(po.astype(BF16), v_new.astype(BF16), preferred_element_type=F32)
            for acc, po, v_new in zip(accs, p_owns, v_news)]
    for r in seqs:
        out = accs[r] / denoms[r]
        out = (out * own_rows).reshape(lq, H_C, W_BRANCH).sum(axis=1)
        o_ref[r] = jnp.concatenate([out, jnp.zeros((lp - lq, W_BRANCH), F32)], axis=0)


def _moba_decode(page_table, q3, k3, v3, cache_k, cache_v, layer, lq):
    b, lp, _ = q3.shape
    npage = page_table.shape[1]
    page = cache_k.shape[-1]
    nseq = SEQ_PER_STEP
    kern = functools.partial(_moba_decode_kernel, npage=npage, page=page, lq=lq)
    tok = pl.BlockSpec((nseq, lp, W_BRANCH), lambda i, pt: (i, 0, 0))

    def page_spec(r, p_i):
        return pl.BlockSpec((None, None, H_C, D_C, page), lambda i, pt: (layer, pt[i * nseq + r, p_i], 0, 0, 0))

    pages = [page_spec(r, p_i) for r in range(nseq) for p_i in range(npage)]
    grid_spec = pltpu.PrefetchScalarGridSpec(
        num_scalar_prefetch=1,
        grid=(b // nseq,),
        in_specs=[tok, tok, tok] + pages * 2,
        out_specs=pl.BlockSpec((nseq, lp, W_BRANCH), lambda i, pt: (i, 0, 0)),
    )
    return pl.pallas_call(
        kern,
        grid_spec=grid_spec,
        out_shape=jax.ShapeDtypeStruct((b, lp, W_BRANCH), F32),
        compiler_params=pltpu.CompilerParams(dimension_semantics=("arbitrary",), vmem_limit_bytes=VMEM_LIMIT),
    )(page_table, q3, k3, v3, *([cache_k] * (nseq * npage)), *([cache_v] * (nseq * npage)))


def _merge_kernel(oa_ref, ob_ref, oc_ref, zc_ref, gate_ref, x_ref, ple_ref, wb_ref, bg_ref, wo_ref,
                  wpg_ref, bpg_ref, wple_ref, y_ref):
    branches = (oa_ref[...], ob_ref[...], oc_ref[...] * _silu(zc_ref[...]))
    up = None
    for n, o in enumerate(branches):
        sl = slice(n * D_MODEL, (n + 1) * D_MODEL)
        gate = _sigmoid(gate_ref[:, sl] + bg_ref[:, sl])
        term = gate * jnp.dot(o.astype(BF16), wb_ref[n], preferred_element_type=F32)
        up = term if up is None else up + term
    y = x_ref[...] + jnp.dot(up.astype(BF16), wo_ref[...], preferred_element_type=F32)
    pg = _sigmoid(jnp.dot(y.astype(BF16), wpg_ref[...], preferred_element_type=F32) + bpg_ref[...])
    y_ref[...] = y + pg * jnp.dot(ple_ref[...].astype(BF16), wple_ref[...], preferred_element_type=F32)


def _merge(oa, ob, oc, proj2, x2, ple3, ple_layer, wb, bg, wo, wpg, bpg, wple, w_layer, tm):
    t = x2.shape[0]
    full = lambda shape: pl.BlockSpec((None,) + shape, lambda i: (w_layer,) + (0,) * len(shape))
    row = lambda w: pl.BlockSpec((tm, w), lambda i: (i, 0))
    return pl.pallas_call(
        _merge_kernel,
        grid=(t // tm,),
        in_specs=[row(W_BRANCH), row(W_BRANCH), row(W_BRANCH),
                  pl.BlockSpec((tm, W_BRANCH), lambda i: (i, C_Z_C // W_BRANCH)),
                  pl.BlockSpec((tm, N_BRANCH * D_MODEL), lambda i: (i, C_GATE // (N_BRANCH * D_MODEL))),
                  row(D_MODEL), pl.BlockSpec((None, tm, D_PLE), lambda i: (ple_layer, i, 0)),
                  full((N_BRANCH, W_BRANCH, D_MODEL)), full((1, N_BRANCH * D_MODEL)), full((D_MODEL, D_MODEL)),
                  full((D_MODEL, D_MODEL)), full((1, D_MODEL)), full((D_PLE, D_MODEL))],
        out_specs=row(D_MODEL),
        out_shape=jax.ShapeDtypeStruct((t, D_MODEL), F32),
        compiler_params=pltpu.CompilerParams(dimension_semantics=("arbitrary",), vmem_limit_bytes=VMEM_LIMIT),
    )(oa, ob, oc, proj2, proj2, x2, ple3, wb, bg, wo, wpg, bpg, wple)


def _arrange_w_in(w):
    offs = np.cumsum((0,) + IN_SPLITS)
    qkv_a, z_a, beta, alpha, f_b, z_b, qkv_c, z_c, gate = [w[:, offs[i]:offs[i + 1]] for i in range(9)]
    d = w.shape[0]
    ba = jnp.concatenate([beta, alpha, jnp.zeros((d, LANES - 2 * H_A), w.dtype)], axis=1)
    tail = jnp.zeros((d, N_PROJ - C_BA - LANES), w.dtype)
    out = jnp.concatenate([qkv_a, z_a, f_b[:, :3 * W_BRANCH], z_b, qkv_c, z_c, gate,
                           f_b[:, 3 * W_BRANCH:], ba, tail], axis=1)
    return out.astype(BF16)


def _rope_frequencies():
    half = ROPE_DIM // 2
    inv = ROPE_THETA ** (-jnp.arange(half, dtype=F32) / half)
    d = np.arange(LANES) % D_C
    return jnp.where(d < ROPE_DIM, inv[d % half], 0.0).reshape(1, LANES)


def _layer_params(i, p):
    row = lambda v: v.reshape(1, -1)
    lanes_ba = lambda v: jnp.zeros((1, LANES), F32).at[0, H_A:2 * H_A].set(v)
    lora_pad = jnp.zeros((LORA, W_BRANCH), F32)
    head = np.arange(W_BRANCH) // N_B
    return dict(
        norm_g=row(p['norm_g'][i]),
        w_in=_arrange_w_in(p['w_in'][i]),
        conv=p['conv_a'][i],
        pa=lanes_ba(p['a_log'][i]), pb=lanes_ba(p['dt_bias'][i]), onorm=row(p['onorm_a'][i]),
        rw_vecs=(row(p['mu_b'][i]), row(p['w0_b'][i]), row(p['a0_b'][i]), row(p['kk_b'][i]), row(p['ka_b'][i]),
                 row(p['rk_b'][i]), row(p['gn_g_b'][i]), row(p['gn_b_b'][i])),
        w2p=jnp.concatenate([p['w2_b'][i], lora_pad], axis=0).astype(BF16),
        a2p=jnp.concatenate([lora_pad, p['a2_b'][i]], axis=0).astype(BF16),
        bd=jnp.asarray(head[:, None] == head[None, :], BF16),
        qn=row(jnp.tile(p['qn_c'][i], H_C)), kn=row(jnp.tile(p['kn_c'][i], H_C)),
    )


def _hist_rows(rows):
    b, n, c = rows.shape
    return jnp.concatenate([jnp.zeros((b, HIST - n, c), F32), rows], axis=1)


def _layer(x3, ple3, ple_layer, lp, *, l_valid, rope, dn_s, rw_s, state_layer, dn_buf, rw_prev, nseg, lt, ls, tm, attend):
    b, l, d = x3.shape
    t = b * l
    proj2 = _proj(x3.reshape(t, d), lp['norm_g'], lp['w_in'], min(t, 1024), 1920)
    proj3 = proj2.reshape(b, l, N_PROJ)

    oa, dn_new, ob, rw_new = _mixers(proj3, _hist_rows(dn_buf), dn_s, _hist_rows(rw_prev[:, None, :]), rw_s,
                                     state_layer, lp, nseg=nseg, lt=lt, ls=ls, l_valid=l_valid)
    qr, kr = _qkprep(proj2, lp['qn'], lp['kn'], rope, lp['bd'], tm=min(tm, t))
    oc, v2 = attend(qr, kr, proj3)

    y2 = _merge(oa.reshape(t, W_BRANCH), ob.reshape(t, W_BRANCH), oc.reshape(t, W_BRANCH), proj2,
                x3.reshape(t, d), ple3, ple_layer, lp['wb'], lp['bg'], lp['wo'], lp['wpg'],
                lp['bpg'], lp['wple'], lp['w_layer'], min(tm, t))

    f_last = jnp.concatenate([proj3[:, l_valid - 1, C_RKV:C_RKV + 3 * W_BRANCH],
                              proj3[:, l_valid - 1, C_WLAL:C_WLAL + 2 * LORA]], axis=-1)
    new_state = (kr.reshape(b, l, H_C, D_C)[:, :l_valid], v2.reshape(b, l, H_C, D_C)[:, :l_valid], dn_new,
                 proj3[:, l_valid - (CONV_W - 1):l_valid, C_QKV_A:C_QKV_A + QKV_A],
                 rw_new, f_last)
    return y2.reshape(b, l, d), new_state


def _attend_prompt(b, l):
    def attend(qr, kr, proj3):
        shp = (b, l, W_BRANCH)
        o3, v3 = _moba_prompt(qr.reshape(shp), kr.reshape(shp), proj3)
        return o3.reshape(b * l, W_BRANCH), v3.reshape(b * l, W_BRANCH)
    return attend


def _attend_decode(b, lp_, lq, page_table, cache_k4, cache_v4, layer):
    def attend(qr, kr, proj3):
        shp = (b, lp_, W_BRANCH)
        v3 = proj3[:, :, C_QKV_C + 2 * W_BRANCH:C_QKV_C + 3 * W_BRANCH]
        o3 = _moba_decode(page_table, qr.reshape(shp), kr.reshape(shp), v3, cache_k4, cache_v4, layer, lq)
        return o3.reshape(b * lp_, W_BRANCH), v3.reshape(b * lp_, W_BRANCH)
    return attend


DEC_PAD = 8


def kernel(x_prompt, x_sample, cache_k, cache_v, state_dn, state_dn_conv, state_rwkv, state_rwkv_shift,
           page_table, p_prompt, p_sample, norm_g, w_in, conv_a, a_log, dt_bias, onorm_a, mu_b, w0_b, w2_b,
           a0_b, a2_b, kk_b, ka_b, rk_b, gn_g_b, gn_b_b, qn_c, kn_c, w_branch, b_gate, w_out, w_ple, w_pg,
           b_pg):
    params = dict(norm_g=norm_g, w_in=w_in, conv_a=conv_a, a_log=a_log, dt_bias=dt_bias, onorm_a=onorm_a,
                  mu_b=mu_b, w0_b=w0_b, w2_b=w2_b, a0_b=a0_b, a2_b=a2_b, kk_b=kk_b, ka_b=ka_b, rk_b=rk_b,
                  gn_g_b=gn_g_b, gn_b_b=gn_b_b, qn_c=qn_c, kn_c=kn_c, w_branch=w_branch, b_gate=b_gate,
                  w_out=w_out, w_ple=w_ple, w_pg=w_pg, b_pg=b_pg)
    depth = norm_g.shape[0]
    b_p, l_p, _ = x_prompt.shape
    b_s, l_s, _ = x_sample.shape
    n_phys, page = cache_k.shape[1], cache_k.shape[2]
    past_len = page_table.shape[1] * page
    assert l_p % 512 == 0 and CONV_W - 1 <= l_s <= DEC_PAD and b_s % (CHUNK // DEC_PAD) == 0
    assert past_len % MOBA_BLOCK == 0 and MOBA_BLOCK % page == 0

    pad_rows = lambda a: jnp.pad(a, ((0, 0), (0, DEC_PAD - l_s), (0, 0)))
    xs = pad_rows(x_sample)
    cache_k4 = cache_k.transpose(0, 1, 3, 4, 2)
    cache_v4 = cache_v.transpose(0, 1, 3, 4, 2)
    inv = _rope_frequencies()
    tm_s = min(512, b_s * DEC_PAD)
    rope_p = _rope_table(inv, l_p, 0)
    rope_s = tuple(jnp.tile(tab, (tm_s // DEC_PAD, 1)) for tab in _rope_table(inv, DEC_PAD, past_len))
    xp = x_prompt
    new_p, new_s = [], []
    merge_w = dict(wb=w_branch.astype(BF16), bg=b_gate[:, None, :], wo=w_out.astype(BF16),
                   wpg=w_pg.astype(BF16), bpg=b_pg[:, None, :], wple=w_ple.astype(BF16))
    for i in range(depth):
        lp = dict(_layer_params(i, params), w_layer=i, **merge_w)
        xp, st_p = _layer(xp, p_prompt.reshape(depth, b_p * l_p, D_PLE), i, lp, l_valid=l_p, rope=rope_p,
                          dn_s=jnp.zeros((1, b_p, H_A, DK_A, DV_A), F32),
                          rw_s=jnp.zeros((1, b_p, H_B, N_B, N_B), F32), state_layer=0,
                          dn_buf=jnp.zeros((b_p, CONV_W - 1, QKV_A), F32), rw_prev=jnp.zeros((b_p, SHIFT_B), F32),
                          nseg=1, lt=512, ls=CHUNK, tm=512, attend=_attend_prompt(b_p, l_p))
        new_p.append(st_p)
        xs, st_s = _layer(xs, pad_rows(p_sample[i]).reshape(1, b_s * DEC_PAD, D_PLE), 0, lp, l_valid=l_s, rope=rope_s,
                          dn_s=state_dn, rw_s=state_rwkv, state_layer=i, dn_buf=state_dn_conv[i],
                          rw_prev=state_rwkv_shift[i], nseg=CHUNK // DEC_PAD, lt=DEC_PAD, ls=DEC_PAD, tm=512,
                          attend=_attend_decode(b_s, DEC_PAD, l_s, page_table, cache_k4, cache_v4, i))
        new_s.append(st_s)
    outs_p = [jnp.stack(t) for t in zip(*new_p)]
    outs_s = [jnp.stack(t) for t in zip(*new_s)]
    return (xp, xs[:, :l_s], *outs_p, *outs_s)
```

```python
import functools

import numpy as np
import jax
import jax.numpy as jnp
from jax import lax
from jax.experimental import pallas as pl
from jax.experimental.pallas import tpu as pltpu

F32 = jnp.float32
BF16 = jnp.bfloat16

D_MODEL = 1024
D_PLE = 256
N_BRANCH = 3
W_BRANCH = 512
NORM_EPS = 1e-6
DK_A = 128
DV_A = 128
H_A = 4
QKV_A = 1536
CONV_W = 4
N_B = 64
H_B = 8
LORA = 64
SHIFT_B = 3 * W_BRANCH + 2 * LORA
GN_EPS = 64e-5
D_C = 64
H_C = 8
MOBA_BLOCK = 256
MOBA_TOPK = 3
ROPE_DIM = 16
ROPE_THETA = 500000.0
IN_SPLITS = (QKV_A, W_BRANCH, H_A, H_A, SHIFT_B, W_BRANCH, 3 * W_BRANCH, W_BRANCH, N_BRANCH * D_MODEL)

C_QKV_A = 0
C_Z_A = 1536
C_RKV = 2048
C_Z_B = 3584
C_QKV_C = 4096
C_Z_C = 5632
C_GATE = 6144
C_WLAL = 9216
C_BA = 9344
N_PROJ = 9600

CHUNK = 64
HIST = 8
MAX_CHUNKS_PER_TILE = 8
LANES = 128
VMEM_LIMIT = 62 * 1024 * 1024

NN = (((1,), (0,)), ((), ()))
BNN = (((2,), (1,)), ((0,), (0,)))
NT = (((1,), (1,)), ((), ()))
TN = (((0,), (0,)), ((), ()))
HIGHEST = lax.Precision.HIGHEST


def _mm(a, b, dims=NN):
    return lax.dot_general(a.astype(BF16), b.astype(BF16), dims, preferred_element_type=F32)


def _mm32(a, b, dims=NN):
    return lax.dot_general(a, b, dims, precision=HIGHEST, preferred_element_type=F32)


def _sigmoid(x):
    return 0.5 * jnp.tanh(0.5 * x) + 0.5


def _silu(x):
    return x * _sigmoid(x)


def _softplus(x):
    return jnp.maximum(x, 0.0) + jnp.log(1.0 + jnp.exp(-jnp.abs(x)))


def _iota2(shape, dim):
    return lax.broadcasted_iota(jnp.int32, shape, dim)


def _split(x):
    hi = x.astype(BF16)
    return hi, (x - hi.astype(F32)).astype(BF16)


def _dotb(a, b, dims=NN):
    return lax.dot_general(a, b, dims, preferred_element_type=F32)


def _mm_sel(sel, x):
    hi, lo = _split(x)
    return _dotb(sel, hi) + _dotb(sel, lo)


def _mm3(a, b, dims=NN):
    ah, al = _split(a)
    bh, bl = _split(b)
    return _dotb(ah, bh, dims) + (_dotb(ah, bl, dims) + _dotb(al, bh, dims))


SOLVE_MM = _mm


def _inv_unit_lower(lmats, nilpotent, mm):
    n = lmats[0].shape[-1]
    dims = NN if lmats[0].ndim == 2 else BNN
    eye = (_iota2((n, n), 0) == _iota2((n, n), 1)).astype(F32)
    ps = [-lmat for lmat in lmats]
    accs = [eye + p for p in ps]
    span = 2
    while span < nilpotent:
        ps = [mm(p, p, dims) for p in ps]
        accs = [acc + mm(acc, p, dims) for acc, p in zip(accs, ps)]
        span *= 2
    return accs


def _head_sums(xs, bd):
    n = xs[0].shape[0]
    out = jnp.dot(jnp.concatenate([x.astype(BF16) for x in xs], axis=0), bd, preferred_element_type=F32)
    return [out[i * n:(i + 1) * n] for i in range(len(xs))]


def _head_sum(x, bd):
    return _head_sums([x], bd)[0]


def _proj_kernel(x_ref, g_ref, w_ref, o_ref):
    x = x_ref[...]
    h = x * lax.rsqrt(jnp.mean(x * x, axis=-1, keepdims=True) + NORM_EPS) * g_ref[...]
    o_ref[...] = jnp.dot(h.astype(BF16), w_ref[...], preferred_element_type=F32)


def _proj(x2d, g, w, tm, tn):
    t, d = x2d.shape
    n = w.shape[1]
    return pl.pallas_call(
        _proj_kernel,
        grid=(n // tn, t // tm),
        in_specs=[pl.BlockSpec((tm, d), lambda j, i: (i, 0)),
                  pl.BlockSpec((1, d), lambda j, i: (0, 0)),
                  pl.BlockSpec((d, tn), lambda j, i: (0, j))],
        out_specs=pl.BlockSpec((tm, tn), lambda j, i: (i, j)),
        out_shape=jax.ShapeDtypeStruct((t, n), F32),
        compiler_params=pltpu.CompilerParams(dimension_semantics=("arbitrary", "arbitrary"),
                                             vmem_limit_bytes=VMEM_LIMIT),
    )(x2d, g, w)


def _chunk_masks(n, ls):
    r = _iota2((n, n), 0)
    c = _iota2((n, n), 1)
    same = (r // ls) == (c // ls)
    return same, same & (r >= c), same & (r > c)


def _valid_rows(l_idx, lt, r0, ls, l_valid):
    t = l_idx * lt + r0 + (_iota2((CHUNK, 1), 0) % ls)
    return t < l_valid


def _gdn_parts(qkv_ref, z_ref, ba_ref, buf_ref, s0_ref, cw_ref, pa_ref, pb_ref, on_ref,
               o_ref, sout_ref, xs_ref, st_ref, *, l_idx, nseg, lt, ls, l_valid, l_total):
    nrow = H_A * CHUNK

    @pl.when(l_idx == 0)
    def _():
        xs_ref[...] = buf_ref[...]
        st_ref[...] = s0_ref[...]

    same_r, c_incl, c_strict = _chunk_masks(CHUNK, ls)
    tri_r = c_incl.astype(BF16)
    same_r = same_r.astype(BF16)
    eye_c = (_iota2((CHUNK, CHUNK), 0) == _iota2((CHUNK, CHUNK), 1)).astype(F32)
    cw = cw_ref[...]
    neg_a = -jnp.exp(pa_ref[...])
    pb = pb_ref[...]
    onorm = on_ref[...]

    def conv(s, r0, col):
        cols = slice(col, col + LANES)
        if r0 == 0:
            win = jnp.concatenate([xs_ref[s, :, cols], qkv_ref[s, 0:ls, cols]], axis=0)
        else:
            win = qkv_ref[s, r0 - HIST:r0 + ls, cols]
        w = cw[:, cols]
        acc = pltpu.roll(win, 3, 0)[HIST:] * w[0:1] + pltpu.roll(win, 2, 0)[HIST:] * w[1:2]
        acc = acc + pltpu.roll(win, 1, 0)[HIST:] * w[2:3]
        acc = acc + win[HIST:] * w[3:4]
        return _silu(acc)

    def part(r0, col):
        return jnp.concatenate([conv(s, r0, col) for s in range(nseg)], axis=0)

    def local(c):
        r0 = c * ls
        ba = jnp.concatenate([ba_ref[s, r0:r0 + ls, :] for s in range(nseg)], axis=0)
        beta_all = _sigmoid(ba)
        g_all = neg_a * _softplus(ba + pb)
        if l_valid < l_total:
            ok = _valid_rows(l_idx, lt, r0, ls, l_valid)
            beta_all = jnp.where(ok, beta_all, 0.0)
            g_all = jnp.where(ok, g_all, 0.0)
        gcum = _mm_sel(tri_r, g_all)
        if nseg == 1:
            gtot = jnp.broadcast_to(gcum[CHUNK - 1:CHUNK], gcum.shape)
        else:
            gtot = _mm_sel(same_r, g_all)

        qs, ks, vs, betas, gcs, gts, lmats, qks = [], [], [], [], [], [], [], []
        for h in range(H_A):
            q = part(r0, h * DK_A)
            k = part(r0, H_A * DK_A + h * DK_A)
            v = part(r0, 2 * H_A * DK_A + h * DV_A)
            q = q * lax.rsqrt(jnp.sum(q * q, axis=-1, keepdims=True) + 1e-6) * (DK_A ** -0.5)
            k = k * lax.rsqrt(jnp.sum(k * k, axis=-1, keepdims=True) + 1e-6)
            beta_h = beta_all[:, h:h + 1]
            g_h = gcum[:, H_A + h:H_A + h + 1]
            g_row = jnp.sum(eye_c * g_h, axis=0, keepdims=True)
            decay = jnp.exp(jnp.where(c_incl, g_h - g_row, -jnp.inf))
            kq_k = _mm(jnp.concatenate([k, q], axis=0), k, NT)
            lmats.append(jnp.where(c_strict, beta_h * kq_k[:CHUNK] * decay, 0.0))
            qks.append(jnp.where(c_incl, kq_k[CHUNK:] * decay, 0.0))
            qs.append(q)
            ks.append(k)
            vs.append(v)
            betas.append(beta_h)
            gcs.append(g_h)
            gts.append(gtot[:, H_A + h:H_A + h + 1])
        q_st = jnp.concatenate(qs, axis=0)
        k_st = jnp.concatenate(ks, axis=0)
        v_st = jnp.concatenate(vs, axis=0)
        beta = jnp.concatenate(betas, axis=0)
        g_col = jnp.concatenate(gcs, axis=0)
        gt_col = jnp.concatenate(gts, axis=0)
        return dict(
            r0=r0, gt_col=gt_col,
            lmat=jnp.stack(lmats),
            rhs=jnp.concatenate([v_st * beta, k_st * (beta * jnp.exp(g_col))], axis=1),
            qk=jnp.stack(qks),
            qg=q_st * jnp.exp(g_col),
            kd=k_st * jnp.exp(gt_col - g_col))

    def advance(loc, sol):
        r0, gt_col, qk, qg, kd = loc['r0'], loc['gt_col'], loc['qk'], loc['qg'], loc['kd']
        bv = sol[:, :DV_A]
        wk = sol[:, DV_A:]
        wks, qgs = [], []
        for h in range(H_A):
            for s in range(nseg):
                a = h * CHUNK + s * ls
                both = _mm(jnp.concatenate([wk[a:a + ls], qg[a:a + ls]], axis=0), st_ref[s, h])
                wks.append(both[:ls])
                qgs.append(both[ls:])
        u = bv - jnp.concatenate(wks, axis=0)
        o = jnp.concatenate(qgs, axis=0) + _mm(qk, u.reshape(H_A, CHUNK, DV_A), BNN).reshape(nrow, DV_A)

        for h in range(H_A):
            a = h * CHUNK
            for s in range(nseg):
                b = a + s * ls
                if nseg == 1:
                    kd_s = kd[a:a + CHUNK]
                else:
                    rows = _iota2((CHUNK, 1), 0) // ls
                    kd_s = jnp.where(rows == s, kd[a:a + CHUNK], 0.0)
                upd = _mm(kd_s, u[a:a + CHUNK], TN)
                st_ref[s, h] = st_ref[s, h] * jnp.exp(gt_col[b:b + 1]) + upd
            oh = o[a:a + CHUNK]
            oh = oh * lax.rsqrt(jnp.mean(oh * oh, axis=-1, keepdims=True) + NORM_EPS) * onorm
            for s in range(nseg):
                zz = z_ref[s, r0:r0 + ls, h * DV_A:(h + 1) * DV_A]
                o_ref[s, r0:r0 + ls, h * DV_A:(h + 1) * DV_A] = oh[s * ls:(s + 1) * ls] * _silu(zz)

    def prepare(loc, invs):
        return SOLVE_MM(invs[0], loc['rhs'].reshape(H_A, CHUNK, DK_A + DV_A), BNN).reshape(nrow, DK_A + DV_A)

    def keep_history():
        xs_ref[...] = qkv_ref[:, lt - HIST:lt, :]

    def epilogue():
        sout_ref[...] = st_ref[...]

    return dict(local=local, lmats=lambda loc: [loc['lmat']], prepare=prepare, advance=advance,
                keep_history=keep_history, epilogue=epilogue)


GRP = 4 * N_B


def _rwkv_parts(r_ref, k_ref, v_ref, wa_ref, z_ref, prev_ref, s0_ref, mu_ref, w0_ref, w2_ref,
                a0_ref, a2_ref, kkp_ref, kap_ref, rk_ref, gg_ref, gb_ref, bd_ref,
                o_ref, sout_ref, fs_ref, st_ref, *, l_idx, nseg, lt, ls, l_valid, l_total):
    nrow = 4 * CHUNK
    ngrp = W_BRANCH // GRP

    @pl.when(l_idx == 0)
    def _():
        fs_ref[...] = prev_ref[...]
        st_ref[...] = jnp.zeros(st_ref.shape, F32)
        for s in range(nseg):
            for h in range(H_B):
                g, j = divmod(h, 4)
                st_ref[s, g, j * N_B:(j + 1) * N_B, j * N_B:(j + 1) * N_B] = s0_ref[s, h]

    sources = {0: r_ref, W_BRANCH: k_ref, 2 * W_BRANCH: v_ref, 3 * W_BRANCH: wa_ref}

    same_r, tri_r, _ = _chunk_masks(CHUNK, ls)
    tri_r = tri_r.astype(BF16)
    same_r = same_r.astype(BF16)
    lane_head = _iota2((1, GRP), 1) // N_B
    head_masks = [(lane_head == j).astype(F32) for j in range(4)]
    bd_state = (_iota2((GRP, GRP), 0) // N_B) == (_iota2((GRP, GRP), 1) // N_B)
    head_rows = ((_iota2((nrow, GRP), 0) // CHUNK) == (_iota2((nrow, GRP), 1) // N_B)).astype(F32)
    t_row = _iota2((nrow, 2 * CHUNK), 0) % CHUNK
    t_col = _iota2((nrow, 2 * CHUNK), 1) % CHUNK
    t_same = (t_row // ls) == (t_col // ls)
    t_incl = t_same & (t_col <= t_row)
    t_strict = t_same & (t_col < t_row)
    mu = mu_ref[...]
    bd = bd_ref[...]

    def mixed(r0, col, width):
        outs = []
        for s in range(nseg):
            if r0 == 0:
                win = jnp.concatenate([fs_ref[s, :, col:col + width], sources[col][s, 0:ls, :]], axis=0)
            else:
                win = sources[col][s, r0 - HIST:r0 + ls, :]
            cur = win[HIST:]
            prev = pltpu.roll(win, 1, 0)[HIST:]
            outs.append(cur + (prev - cur) * mu[:, col:col + width])
        return jnp.concatenate(outs, axis=0)

    def stack_masked(x):
        return jnp.concatenate([x * head_masks[j] for j in range(4)], axis=0)

    def unstack(x):
        return x[0:CHUNK] + x[CHUNK:2 * CHUNK] + x[2 * CHUNK:3 * CHUNK] + x[3 * CHUNK:4 * CHUNK]

    def local(c):
        r0 = c * ls
        r = mixed(r0, 0, W_BRANCH)
        k = mixed(r0, W_BRANCH, W_BRANCH)
        v = mixed(r0, 2 * W_BRANCH, W_BRANCH)
        wa = mixed(r0, 3 * W_BRANCH, 2 * LORA)
        w_raw = w0_ref[...] + _mm(jnp.tanh(wa), w2_ref[...])
        logw = -jnp.exp(-_softplus(-w_raw) - 0.5)
        a_sig = _sigmoid(a0_ref[...] + _mm(wa, a2_ref[...]))
        kk = k * kkp_ref[...]
        k_mod = k * (1.0 + (a_sig - 1.0) * kap_ref[...])
        kk_ss, rk_sum = _head_sums([kk * kk, r * k_mod * rk_ref[...]], bd)
        kk = kk * lax.rsqrt(kk_ss + 1e-6)
        b_vec = kk * a_sig
        if l_valid < l_total:
            ok = _valid_rows(l_idx, lt, r0, ls, l_valid)
            logw = jnp.where(ok, logw, 0.0)
            b_vec = jnp.where(ok, b_vec, 0.0)
            k_mod = jnp.where(ok, k_mod, 0.0)
        gcum = _mm_sel(tri_r, logw)
        if nseg == 1:
            gtot = jnp.broadcast_to(gcum[CHUNK - 1:CHUNK], gcum.shape)
        else:
            gtot = _mm_sel(same_r, logw)
        e_neg = jnp.exp(-gcum)
        e_end = jnp.exp(gtot - gcum)
        a_t = -kk * jnp.exp(gcum - logw)
        b_t = b_vec * e_neg
        k_t = k_mod * e_neg
        r_t = r * jnp.exp(gcum)
        b_h = b_vec * e_end
        k_h = k_mod * e_end

        groups = []
        for g in range(ngrp):
            sl = slice(g * GRP, (g + 1) * GRP)
            a_g, b_g, k_g, r_g, v_g = a_t[:, sl], b_t[:, sl], k_t[:, sl], r_t[:, sl], v[:, sl]
            a_st = stack_masked(a_g)
            r_st = stack_masked(r_g)
            both = _mm(jnp.concatenate([a_st, r_st], axis=0), jnp.concatenate([b_g, k_g], axis=0), NT)
            a_bk = jnp.where(t_strict, both[:nrow], 0.0)
            r_bk = jnp.where(t_incl, both[nrow:], 0.0)
            zero_v = jnp.concatenate([jnp.zeros_like(v_g), v_g], axis=0)
            groups.append(dict(
                a_g=a_g, r_g=r_g, v_g=v_g, b_h=b_h[:, sl], k_h=k_h[:, sl], gtot=gtot[:, sl],
                neg_aab=-a_bk[:, :CHUNK].reshape(4, CHUNK, CHUNK),
                kv_u=_mm(a_bk, zero_v) * head_rows, r_bk=r_bk))
        return dict(r0=r0, v=v, rk_sum=rk_sum, groups=groups)

    def advance(loc, tinvs):
        grps = loc['groups']
        ams, rms = [], []
        for g, grp in enumerate(grps):
            parts = [_mm(jnp.concatenate([grp['a_g'][s * ls:(s + 1) * ls], grp['r_g'][s * ls:(s + 1) * ls]], axis=0),
                         st_ref[s, g], NT) for s in range(nseg)]
            ams.append(jnp.concatenate([p[:ls] for p in parts], axis=0))
            rms.append(jnp.concatenate([p[ls:] for p in parts], axis=0))
        us = [unstack(SOLVE_MM(tinv, (stack_masked(am) + grp['kv_u']).reshape(4, CHUNK, GRP), BNN)
                      .reshape(nrow, GRP)) for grp, tinv, am in zip(grps, tinvs, ams)]
        ys = [rm + unstack(_mm(grp['r_bk'], jnp.concatenate([u, grp['v_g']], axis=0)) * head_rows)
              for grp, u, rm in zip(grps, us, rms)]
        for g, (grp, u) in enumerate(zip(grps, us)):
            v_g = grp['v_g']
            for s in range(nseg):
                if nseg == 1:
                    u_s, v_s = u, v_g
                else:
                    rows = _iota2((CHUNK, 1), 0) // ls
                    u_s = jnp.where(rows == s, u, 0.0)
                    v_s = jnp.where(rows == s, v_g, 0.0)
                upd = _mm(jnp.concatenate([u_s, v_s], axis=0),
                          jnp.concatenate([grp['b_h'], grp['k_h']], axis=0), TN)
                dec = jnp.exp(grp['gtot'][s * ls:s * ls + 1])
                st_ref[s, g] = st_ref[s, g] * dec + jnp.where(bd_state, upd, 0.0)
        y = jnp.concatenate(ys, axis=1)
        mean = _head_sums([y], bd)[0] * (1.0 / N_B)
        yc = y - mean
        var = _head_sums([yc * yc], bd)[0] * (1.0 / N_B)
        yn = yc * lax.rsqrt(var + GN_EPS) * gg_ref[...] + gb_ref[...]
        out = yn + loc['rk_sum'] * loc['v']
        for s in range(nseg):
            zz = z_ref[s, loc['r0']:loc['r0'] + ls, :]
            o_ref[s, loc['r0']:loc['r0'] + ls, :] = out[s * ls:(s + 1) * ls] * _silu(zz)

    def keep_history():
        for col, ref in sources.items():
            fs_ref[:, :, col:col + ref.shape[-1]] = ref[:, lt - HIST:lt, :]

    def epilogue():
        for s in range(nseg):
            for h in range(H_B):
                g, j = divmod(h, 4)
                sout_ref[s, h] = st_ref[s, g, j * N_B:(j + 1) * N_B, j * N_B:(j + 1) * N_B]

    return dict(local=local, lmats=lambda loc: [grp['neg_aab'] for grp in loc['groups']],
                prepare=lambda loc, invs: invs, advance=advance, keep_history=keep_history, epilogue=epilogue)


N_GDN_IN, N_RWKV_IN = 9, 18


def _mixers_kernel(*refs, nseg, lt, ls, l_valid, l_total):
    gdn_in = refs[:N_GDN_IN]
    rwkv_in = refs[N_GDN_IN:N_GDN_IN + N_RWKV_IN]
    oa_ref, dn_ref, ob_ref, rw_ref, xs_ref, st_a_ref, fs_ref, st_b_ref = refs[N_GDN_IN + N_RWKV_IN:]
    l_idx = pl.program_id(1)
    cfg = dict(l_idx=l_idx, nseg=nseg, lt=lt, ls=ls, l_valid=l_valid, l_total=l_total)
    mixers = [_gdn_parts(*gdn_in, oa_ref, dn_ref, xs_ref, st_a_ref, **cfg),
              _rwkv_parts(*rwkv_in, ob_ref, rw_ref, fs_ref, st_b_ref, **cfg)]
    nchunk = lt // ls
    assert nchunk <= MAX_CHUNKS_PER_TILE

    def tile():
        chunks = range(nchunk)
        locs = [[m['local'](c) for c in chunks] for m in mixers]
        lmats = [[m['lmats'](loc) for loc in mlocs] for m, mlocs in zip(mixers, locs)]
        invs = _inv_unit_lower([x for per_mixer in lmats for per_loc in per_mixer for x in per_loc], ls, SOLVE_MM)
        prepared, at = [], 0
        for m, mlocs, mlm in zip(mixers, locs, lmats):
            row = []
            for loc, lm in zip(mlocs, mlm):
                row.append(m['prepare'](loc, invs[at:at + len(lm)]))
                at += len(lm)
            prepared.append(row)
        for j in chunks:
            for m, mlocs, mprep in zip(mixers, locs, prepared):
                m['advance'](mlocs[j], mprep[j])
        for m in mixers:
            m['keep_history']()

    tile()

    @pl.when(l_idx == pl.num_programs(1) - 1)
    def _():
        for m in mixers:
            m['epilogue']()


def _mixers(proj3, buf8, dn_s0, prev8, rw_s0, layer, lp, *, nseg, lt, ls, l_valid):
    b, l, _ = proj3.shape
    ngrp = W_BRANCH // GRP
    kern = functools.partial(_mixers_kernel, nseg=nseg, lt=lt, ls=ls, l_valid=l_valid, l_total=l)
    full = lambda shape: pl.BlockSpec(shape, lambda i, j: (0,) * len(shape))
    blk = lambda col, w: pl.BlockSpec((nseg, lt, w), lambda i, j: (i, j, col // w))
    mu, w0, a0, kkp, kap, rk, gg, gb = lp['rw_vecs']
    gdn_specs = [blk(C_QKV_A, QKV_A), blk(C_Z_A, W_BRANCH), blk(C_BA, LANES),
                 pl.BlockSpec((nseg, HIST, QKV_A), lambda i, j: (i, 0, 0)),
                 pl.BlockSpec((None, nseg, H_A, DK_A, DV_A), lambda i, j: (layer, i, 0, 0, 0)),
                 full((CONV_W, QKV_A)), full((1, LANES)), full((1, LANES)), full((1, DV_A))]
    rwkv_specs = [blk(C_RKV, W_BRANCH), blk(C_RKV + W_BRANCH, W_BRANCH), blk(C_RKV + 2 * W_BRANCH, W_BRANCH),
                  blk(C_WLAL, 2 * LORA), blk(C_Z_B, W_BRANCH),
                  pl.BlockSpec((nseg, HIST, SHIFT_B), lambda i, j: (i, 0, 0)),
                  pl.BlockSpec((None, nseg, H_B, N_B, N_B), lambda i, j: (layer, i, 0, 0, 0)),
                  full((1, SHIFT_B)), full((1, W_BRANCH)), full((2 * LORA, W_BRANCH)),
                  full((1, W_BRANCH)), full((2 * LORA, W_BRANCH)), full((1, W_BRANCH)),
                  full((1, W_BRANCH)), full((1, W_BRANCH)), full((1, W_BRANCH)), full((1, W_BRANCH)),
                  full((W_BRANCH, W_BRANCH))]
    assert len(gdn_specs) == N_GDN_IN and len(rwkv_specs) == N_RWKV_IN
    tile_out = pl.BlockSpec((nseg, lt, W_BRANCH), lambda i, j: (i, j, 0))
    return pl.pallas_call(
        kern,
        grid=(b // nseg, l // lt),
        in_specs=gdn_specs + rwkv_specs,
        out_specs=[tile_out, pl.BlockSpec((nseg, H_A, DK_A, DV_A), lambda i, j: (i, 0, 0, 0)),
                   tile_out, pl.BlockSpec((nseg, H_B, N_B, N_B), lambda i, j: (i, 0, 0, 0))],
        out_shape=[jax.ShapeDtypeStruct((b, l, W_BRANCH), F32), jax.ShapeDtypeStruct((b, H_A, DK_A, DV_A), F32),
                   jax.ShapeDtypeStruct((b, l, W_BRANCH), F32), jax.ShapeDtypeStruct((b, H_B, N_B, N_B), F32)],
        scratch_shapes=[pltpu.VMEM((nseg, HIST, QKV_A), F32), pltpu.VMEM((nseg, H_A, DK_A, DV_A), F32),
                        pltpu.VMEM((nseg, HIST, SHIFT_B), F32), pltpu.VMEM((nseg, ngrp, GRP, GRP), F32)],
        compiler_params=pltpu.CompilerParams(dimension_semantics=("arbitrary", "arbitrary"),
                                             vmem_limit_bytes=VMEM_LIMIT),
    )(proj3, proj3, proj3, buf8, dn_s0, lp['conv'], lp['pa'], lp['pb'], lp['onorm'],
      proj3, proj3, proj3, proj3, proj3, prev8, rw_s0, mu, w0, lp['w2p'], a0, lp['a2p'], kkp, kap, rk, gg, gb,
      lp['bd'])


def _rope_table_kernel(inv_ref, cos_ref, sin_ref, *, pos0):
    n = cos_ref.shape[0]
    pos = (pos0 + pl.program_id(0) * n + _iota2((n, 1), 0)).astype(F32)
    ang = pos * inv_ref[...]
    cos_ref[...] = jnp.cos(ang)
    sin_ref[...] = jnp.sin(ang)


def _rope_table(inv, n, pos0, tile=512):
    tile = min(tile, n)
    blk = pl.BlockSpec((tile, LANES), lambda i: (i, 0))
    return pl.pallas_call(
        functools.partial(_rope_table_kernel, pos0=pos0),
        grid=(n // tile,),
        in_specs=[pl.BlockSpec((1, LANES), lambda i: (0, 0))],
        out_specs=[blk, blk],
        out_shape=[jax.ShapeDtypeStruct((n, LANES), F32)] * 2,
        compiler_params=pltpu.CompilerParams(dimension_semantics=("arbitrary",), vmem_limit_bytes=VMEM_LIMIT),
    )(inv)


def _qkprep_kernel(q_ref, k_ref, qn_ref, kn_ref, cos_ref, sin_ref, bd_ref, qo_ref, ko_ref):
    cos = cos_ref[...]
    sin = sin_ref[...]
    d = _iota2((1, LANES), 1) % D_C
    half = ROPE_DIM // 2
    c_main = jnp.where(d < ROPE_DIM, cos, 1.0)
    c_up = jnp.where(d < half, -sin, 0.0)
    c_dn = jnp.where((d >= half) & (d < ROPE_DIM), sin, 0.0)
    bd = bd_ref[...]

    def one(x_ref, g_ref, out_ref):
        x = x_ref[...]
        y = x * lax.rsqrt(_head_sum(x * x, bd) * (1.0 / D_C) + NORM_EPS) * g_ref[...]
        for blk in range(W_BRANCH // LANES):
            yb = y[:, blk * LANES:(blk + 1) * LANES]
            up = pltpu.roll(yb, LANES - half, 1)
            dn = pltpu.roll(yb, half, 1)
            out_ref[:, blk * LANES:(blk + 1) * LANES] = yb * c_main + up * c_up + dn * c_dn

    one(q_ref, qn_ref, qo_ref)
    one(k_ref, kn_ref, ko_ref)


def _qkprep(proj2, qn, kn, rope, bd, *, tm):
    t = proj2.shape[0]
    cos, sin = rope
    period = cos.shape[0] // tm
    full = lambda shape: pl.BlockSpec(shape, lambda i: (0,) * len(shape))
    table = pl.BlockSpec((tm, LANES), lambda i: (i % period, 0))
    return pl.pallas_call(
        _qkprep_kernel,
        grid=(t // tm,),
        in_specs=[pl.BlockSpec((tm, W_BRANCH), lambda i: (i, C_QKV_C // W_BRANCH)),
                  pl.BlockSpec((tm, W_BRANCH), lambda i: (i, C_QKV_C // W_BRANCH + 1)),
                  full((1, W_BRANCH)), full((1, W_BRANCH)), table, table, full((W_BRANCH, W_BRANCH))],
        out_specs=[pl.BlockSpec((tm, W_BRANCH), lambda i: (i, 0)),
                   pl.BlockSpec((tm, W_BRANCH), lambda i: (i, 0))],
        out_shape=[jax.ShapeDtypeStruct((t, W_BRANCH), F32), jax.ShapeDtypeStruct((t, W_BRANCH), F32)],
        compiler_params=pltpu.CompilerParams(dimension_semantics=("arbitrary",), vmem_limit_bytes=VMEM_LIMIT),
    )(proj2, proj2, qn, kn, cos, sin, bd)


def _rank_select(gate, nblk, limit, axis):
    idx = _iota2(gate.shape, axis)
    rank = jnp.zeros(gate.shape, F32)
    for m in range(nblk):
        gm = gate[m:m + 1, :] if axis == 0 else gate[:, m:m + 1]
        beats = (gm > gate) | ((gm == gate) & (m < idx))
        rank = rank + jnp.where(beats & (m < limit), 1.0, 0.0)
    return jnp.where((idx < limit) & (rank < MOBA_TOPK), 1.0, 0.0)


HEADS_PER_STEP = LANES // D_C


def _moba_prompt_kernel(q_ref, k_ref, v_ref, o_ref, vo_ref, *, nblk):
    blk = MOBA_BLOCK
    l = nblk * blk
    scale = D_C ** -0.5
    k = k_ref[...]
    q_t = q_ref[...].T
    v = v_ref[...]
    vo_ref[...] = v
    v_t = v.T.astype(BF16)
    kmean = jnp.concatenate([jnp.mean(k[n * blk:(n + 1) * blk], axis=0, keepdims=True) for n in range(nblk)],
                            axis=0)
    own = _iota2((1, l), 1) // blk
    kb = k.astype(BF16)
    causal = _iota2((blk, blk), 0) <= _iota2((blk, blk), 1)
    row_head = _iota2((LANES, 1), 0) // D_C
    units = [(e, j) for e in range(HEADS_PER_STEP) for j in range(nblk)]
    sels, qbs = [], []
    for e in range(HEADS_PER_STEP):
        q_e = jnp.where(row_head == e, q_t, 0.0)
        sels.append(_rank_select(_mm32(kmean, q_e), nblk, own, 0))
        qbs.append((q_e * scale).astype(BF16))
    scores = [jnp.dot(kb[:(j + 1) * blk], qbs[e][:, j * blk:(j + 1) * blk], preferred_element_type=F32)
              for e, j in units]
    probs, denoms = [], []
    for (e, j), s in zip(units, scores):
        cols = slice(j * blk, (j + 1) * blk)
        parts = [jnp.where(sels[e][n:n + 1, cols] > 0.0, s[n * blk:(n + 1) * blk], -jnp.inf) for n in range(j)]
        parts.append(jnp.where(causal, s[j * blk:], -jnp.inf))
        s = jnp.concatenate(parts, axis=0)
        p = jnp.exp(s - jnp.max(s, axis=0, keepdims=True))
        denoms.append(jnp.sum(p, axis=0, keepdims=True))
        probs.append(p.astype(BF16))
    tiles = [jnp.dot(v_t[e * D_C:(e + 1) * D_C, :(j + 1) * blk], p, preferred_element_type=F32) / d
             for (e, j), p, d in zip(units, probs, denoms)]
    outs = [jnp.concatenate(tiles[e * nblk:(e + 1) * nblk], axis=1) for e in range(HEADS_PER_STEP)]
    o_ref[...] = jnp.concatenate(outs, axis=0).T


def _moba_prompt(q3, k3, proj3):
    b, l, _ = q3.shape
    nblk = l // MOBA_BLOCK
    assert (4.0 ** round(np.log(D_C) / np.log(4.0))) == D_C
    kern = functools.partial(_moba_prompt_kernel, nblk=nblk)
    v_col = (C_QKV_C + 2 * W_BRANCH) // LANES
    cols = pl.BlockSpec((None, l, LANES), lambda i, hp: (i, 0, hp))
    return pl.pallas_call(
        kern,
        grid=(b, W_BRANCH // LANES),
        in_specs=[cols, cols, pl.BlockSpec((None, l, LANES), lambda i, hp: (i, 0, v_col + hp))],
        out_specs=[cols, cols],
        out_shape=[jax.ShapeDtypeStruct((b, l, W_BRANCH), F32)] * 2,
        compiler_params=pltpu.CompilerParams(dimension_semantics=("arbitrary", "arbitrary"),
                                             vmem_limit_bytes=VMEM_LIMIT),
    )(q3, k3, proj3)


SEQ_PER_STEP = 2


def _moba_decode_kernel(pt_ref, q_ref, kn_ref, vn_ref, *refs, npage, page, lq):
    nseq = q_ref.shape[0]
    k_refs = refs[:nseq * npage]
    v_refs = refs[nseq * npage:2 * nseq * npage]
    o_ref = refs[2 * nseq * npage]
    del pt_ref
    scale = D_C ** -0.5
    nrow = lq * H_C
    nblk = npage * page // MOBA_BLOCK
    lp = q_ref.shape[1]
    seqs = range(nseq)

    lane_head = _iota2((H_C, W_BRANCH), 1) // D_C
    own_head = (lane_head == _iota2((H_C, W_BRANCH), 0)).astype(F32)
    own_rows = jnp.concatenate([own_head] * lq, axis=0)
    lane = _iota2((1, LANES), 1)
    t_key = _iota2((nrow, LANES), 1)
    t_qry = _iota2((nrow, LANES), 0) // H_C
    pad = jnp.zeros((LANES - lp, W_BRANCH), F32)

    qbds = [jnp.concatenate([q_ref[r, t:t + 1, :] * own_head for t in range(lq)], axis=0) for r in seqs]
    qbs = [(qbd * scale).astype(BF16) for qbd in qbds]
    k_ts = [jnp.concatenate([k_refs[r * npage + i][...].reshape(W_BRANCH, page) for i in range(npage)], axis=1)
            for r in seqs]
    v_ts = [jnp.concatenate([v_refs[r * npage + i][...].reshape(W_BRANCH, page) for i in range(npage)], axis=1)
            for r in seqs]
    s_pasts = [jnp.dot(qb, k_t.astype(BF16), preferred_element_type=F32) for qb, k_t in zip(qbs, k_ts)]

    def block_key_sums(k_t):
        ksum = jnp.zeros((W_BRANCH, LANES), F32)
        for n in range(nblk):
            k_n = k_t[:, n * MOBA_BLOCK:(n + 1) * MOBA_BLOCK]
            folded = sum(k_n[:, i * LANES:(i + 1) * LANES] for i in range(MOBA_BLOCK // LANES))
            ksum = ksum + jnp.sum(folded, axis=-1, keepdims=True) * (lane == n).astype(F32)
        return ksum

    ksums = [block_key_sums(k_t) for k_t in k_ts]
    gates = [_mm3(qbd, ksum) * (1.0 / MOBA_BLOCK) for qbd, ksum in zip(qbds, ksums)]
    sels = [_rank_select(gate, nblk, nblk, 1) for gate in gates]
    s_pasts = [jnp.concatenate(
        [jnp.where(sel[:, n:n + 1] > 0.0, s[:, n * MOBA_BLOCK:(n + 1) * MOBA_BLOCK], -jnp.inf)
         for n in range(nblk)], axis=1) for sel, s in zip(sels, s_pasts)]

    k_news = [jnp.concatenate([kn_ref[r], pad], axis=0) for r in seqs]
    v_news = [jnp.concatenate([vn_ref[r], pad], axis=0) for r in seqs]
    s_owns = [jnp.where(t_key <= t_qry, lax.dot_general(qb, k_new.astype(BF16), NT, preferred_element_type=F32),
                        -jnp.inf) for qb, k_new in zip(qbs, k_news)]
    ms = [jnp.maximum(jnp.max(sp, axis=-1, keepdims=True), jnp.max(so, axis=-1, keepdims=True))
          for sp, so in zip(s_pasts, s_owns)]
    p_pasts = [jnp.exp(sp - m) for sp, m in zip(s_pasts, ms)]
    p_owns = [jnp.exp(so - m) for so, m in zip(s_owns, ms)]
    denoms = [jnp.sum(pp, axis=-1, keepdims=True) + jnp.sum(po, axis=-1, keepdims=True)
              for pp, po in zip(p_pasts, p_owns)]
    accs = [lax.dot_general(pp.astype(BF16), v_t.astype(BF16), NT, preferred_element_type=F32)
            for pp, v_t in zip(p_pasts, v_ts)]
    accs = [acc + jnp.dot(po.astype(BF16), v_new.astype(BF16), preferred_element_type=F32)
            for acc, po, v_new in zip(accs, p_owns, v_news)]
    for r in seqs:
        out = accs[r] / denoms[r]
        out = (out * own_rows).reshape(lq, H_C, W_BRANCH).sum(axis=1)
        o_ref[r] = jnp.concatenate([out, jnp.zeros((lp - lq, W_BRANCH), F32)], axis=0)


def _moba_decode(page_table, q3, k3, v3, cache_k, cache_v, layer, lq):
    b, lp, _ = q3.shape
    npage = page_table.shape[1]
    page = cache_k.shape[-1]
    nseq = SEQ_PER_STEP
    kern = functools.partial(_moba_decode_kernel, npage=npage, page=page, lq=lq)
    tok = pl.BlockSpec((nseq, lp, W_BRANCH), lambda i, pt: (i, 0, 0))

    def page_spec(r, p_i):
        return pl.BlockSpec((None, None, H_C, D_C, page), lambda i, pt: (layer, pt[i * nseq + r, p_i], 0, 0, 0))

    pages = [page_spec(r, p_i) for r in range(nseq) for p_i in range(npage)]
    grid_spec = pltpu.PrefetchScalarGridSpec(
        num_scalar_prefetch=1,
        grid=(b // nseq,),
        in_specs=[tok, tok, tok] + pages * 2,
        out_specs=pl.BlockSpec((nseq, lp, W_BRANCH), lambda i, pt: (i, 0, 0)),
    )
    return pl.pallas_call(
        kern,
        grid_spec=grid_spec,
        out_shape=jax.ShapeDtypeStruct((b, lp, W_BRANCH), F32),
        compiler_params=pltpu.CompilerParams(dimension_semantics=("arbitrary",), vmem_limit_bytes=VMEM_LIMIT),
    )(page_table, q3, k3, v3, *([cache_k] * (nseq * npage)), *([cache_v] * (nseq * npage)))


def _merge_kernel(oa_ref, ob_ref, oc_ref, zc_ref, gate_ref, x_ref, ple_ref, wb_ref, bg_ref, wo_ref,
                  wpg_ref, bpg_ref, wple_ref, y_ref):
    branches = (oa_ref[...], ob_ref[...], oc_ref[...] * _silu(zc_ref[...]))
    up = None
    for n, o in enumerate(branches):
        sl = slice(n * D_MODEL, (n + 1) * D_MODEL)
        gate = _sigmoid(gate_ref[:, sl] + bg_ref[:, sl])
        term = gate * jnp.dot(o.astype(BF16), wb_ref[n], preferred_element_type=F32)
        up = term if up is None else up + term
    y = x_ref[...] + jnp.dot(up.astype(BF16), wo_ref[...], preferred_element_type=F32)
    pg = _sigmoid(jnp.dot(y.astype(BF16), wpg_ref[...], preferred_element_type=F32) + bpg_ref[...])
    y_ref[...] = y + pg * jnp.dot(ple_ref[...].astype(BF16), wple_ref[...], preferred_element_type=F32)


def _merge(oa, ob, oc, proj2, x2, ple3, ple_layer, wb, bg, wo, wpg, bpg, wple, w_layer, tm):
    t = x2.shape[0]
    full = lambda shape: pl.BlockSpec((None,) + shape, lambda i: (w_layer,) + (0,) * len(shape))
    row = lambda w: pl.BlockSpec((tm, w), lambda i: (i, 0))
    return pl.pallas_call(
        _merge_kernel,
        grid=(t // tm,),
        in_specs=[row(W_BRANCH), row(W_BRANCH), row(W_BRANCH),
                  pl.BlockSpec((tm, W_BRANCH), lambda i: (i, C_Z_C // W_BRANCH)),
                  pl.BlockSpec((tm, N_BRANCH * D_MODEL), lambda i: (i, C_GATE // (N_BRANCH * D_MODEL))),
                  row(D_MODEL), pl.BlockSpec((None, tm, D_PLE), lambda i: (ple_layer, i, 0)),
                  full((N_BRANCH, W_BRANCH, D_MODEL)), full((1, N_BRANCH * D_MODEL)), full((D_MODEL, D_MODEL)),
                  full((D_MODEL, D_MODEL)), full((1, D_MODEL)), full((D_PLE, D_MODEL))],
        out_specs=row(D_MODEL),
        out_shape=jax.ShapeDtypeStruct((t, D_MODEL), F32),
        compiler_params=pltpu.CompilerParams(dimension_semantics=("arbitrary",), vmem_limit_bytes=VMEM_LIMIT),
    )(oa, ob, oc, proj2, proj2, x2, ple3, wb, bg, wo, wpg, bpg, wple)


def _arrange_w_in(w):
    offs = np.cumsum((0,) + IN_SPLITS)
    qkv_a, z_a, beta, alpha, f_b, z_b, qkv_c, z_c, gate = [w[:, offs[i]:offs[i + 1]] for i in range(9)]
    d = w.shape[0]
    ba = jnp.concatenate([beta, alpha, jnp.zeros((d, LANES - 2 * H_A), w.dtype)], axis=1)
    tail = jnp.zeros((d, N_PROJ - C_BA - LANES), w.dtype)
    out = jnp.concatenate([qkv_a, z_a, f_b[:, :3 * W_BRANCH], z_b, qkv_c, z_c, gate,
                           f_b[:, 3 * W_BRANCH:], ba, tail], axis=1)
    return out.astype(BF16)


def _rope_frequencies():
    half = ROPE_DIM // 2
    inv = ROPE_THETA ** (-jnp.arange(half, dtype=F32) / half)
    d = np.arange(LANES) % D_C
    return jnp.where(d < ROPE_DIM, inv[d % half], 0.0).reshape(1, LANES)


def _layer_params(i, p):
    row = lambda v: v.reshape(1, -1)
    lanes_ba = lambda v: jnp.zeros((1, LANES), F32).at[0, H_A:2 * H_A].set(v)
    lora_pad = jnp.zeros((LORA, W_BRANCH), F32)
    head = np.arange(W_BRANCH) // N_B
    return dict(
        norm_g=row(p['norm_g'][i]),
        w_in=_arrange_w_in(p['w_in'][i]),
        conv=p['conv_a'][i],
        pa=lanes_ba(p['a_log'][i]), pb=lanes_ba(p['dt_bias'][i]), onorm=row(p['onorm_a'][i]),
        rw_vecs=(row(p['mu_b'][i]), row(p['w0_b'][i]), row(p['a0_b'][i]), row(p['kk_b'][i]), row(p['ka_b'][i]),
                 row(p['rk_b'][i]), row(p['gn_g_b'][i]), row(p['gn_b_b'][i])),
        w2p=jnp.concatenate([p['w2_b'][i], lora_pad], axis=0).astype(BF16),
        a2p=jnp.concatenate([lora_pad, p['a2_b'][i]], axis=0).astype(BF16),
        bd=jnp.asarray(head[:, None] == head[None, :], BF16),
        qn=row(jnp.tile(p['qn_c'][i], H_C)), kn=row(jnp.tile(p['kn_c'][i], H_C)),
    )


def _hist_rows(rows):
    b, n, c = rows.shape
    return jnp.concatenate([jnp.zeros((b, HIST - n, c), F32), rows], axis=1)


def _layer(x3, ple3, ple_layer, lp, *, l_valid, rope, dn_s, rw_s, state_layer, dn_buf, rw_prev, nseg, lt, ls, tm, attend):
    b, l, d = x3.shape
    t = b * l
    proj2 = _proj(x3.reshape(t, d), lp['norm_g'], lp['w_in'], min(t, 1024), 1920)
    proj3 = proj2.reshape(b, l, N_PROJ)

    oa, dn_new, ob, rw_new = _mixers(proj3, _hist_rows(dn_buf), dn_s, _hist_rows(rw_prev[:, None, :]), rw_s,
                                     state_layer, lp, nseg=nseg, lt=lt, ls=ls, l_valid=l_valid)
    qr, kr = _qkprep(proj2, lp['qn'], lp['kn'], rope, lp['bd'], tm=min(tm, t))
    oc, v2 = attend(qr, kr, proj3)

    y2 = _merge(oa.reshape(t, W_BRANCH), ob.reshape(t, W_BRANCH), oc.reshape(t, W_BRANCH), proj2,
                x3.reshape(t, d), ple3, ple_layer, lp['wb'], lp['bg'], lp['wo'], lp['wpg'],
                lp['bpg'], lp['wple'], lp['w_layer'], min(tm, t))

    f_last = jnp.concatenate([proj3[:, l_valid - 1, C_RKV:C_RKV + 3 * W_BRANCH],
                              proj3[:, l_valid - 1, C_WLAL:C_WLAL + 2 * LORA]], axis=-1)
    new_state = (kr.reshape(b, l, H_C, D_C)[:, :l_valid], v2.reshape(b, l, H_C, D_C)[:, :l_valid], dn_new,
                 proj3[:, l_valid - (CONV_W - 1):l_valid, C_QKV_A:C_QKV_A + QKV_A],
                 rw_new, f_last)
    return y2.reshape(b, l, d), new_state


def _attend_prompt(b, l):
    def attend(qr, kr, proj3):
        shp = (b, l, W_BRANCH)
        o3, v3 = _moba_prompt(qr.reshape(shp), kr.reshape(shp), proj3)
        return o3.reshape(b * l, W_BRANCH), v3.reshape(b * l, W_BRANCH)
    return attend


def _attend_decode(b, lp_, lq, page_table, cache_k4, cache_v4, layer):
    def attend(qr, kr, proj3):
        shp = (b, lp_, W_BRANCH)
        v3 = proj3[:, :, C_QKV_C + 2 * W_BRANCH:C_QKV_C + 3 * W_BRANCH]
        o3 = _moba_decode(page_table, qr.reshape(shp), kr.reshape(shp), v3, cache_k4, cache_v4, layer, lq)
        return o3.reshape(b * lp_, W_BRANCH), v3.reshape(b * lp_, W_BRANCH)
    return attend


DEC_PAD = 8


def kernel(x_prompt, x_sample, cache_k, cache_v, state_dn, state_dn_conv, state_rwkv, state_rwkv_shift,
           page_table, p_prompt, p_sample, norm_g, w_in, conv_a, a_log, dt_bias, onorm_a, mu_b, w0_b, w2_b,
           a0_b, a2_b, kk_b, ka_b, rk_b, gn_g_b, gn_b_b, qn_c, kn_c, w_branch, b_gate, w_out, w_ple, w_pg,
           b_pg):
    params = dict(norm_g=norm_g, w_in=w_in, conv_a=conv_a, a_log=a_log, dt_bias=dt_bias, onorm_a=onorm_a,
                  mu_b=mu_b, w0_b=w0_b, w2_b=w2_b, a0_b=a0_b, a2_b=a2_b, kk_b=kk_b, ka_b=ka_b, rk_b=rk_b,
                  gn_g_b=gn_g_b, gn_b_b=gn_b_b, qn_c=qn_c, kn_c=kn_c, w_branch=w_branch, b_gate=b_gate,
                  w_out=w_out, w_ple=w_ple, w_pg=w_pg, b_pg=b_pg)
    depth = norm_g.shape[0]
    b_p, l_p, _ = x_prompt.shape
    b_s, l_s, _ = x_sample.shape
    n_phys, page = cache_k.shape[1], cache_k.shape[2]
    past_len = page_table.shape[1] * page
    assert l_p % 512 == 0 and CONV_W - 1 <= l_s <= DEC_PAD and b_s % (CHUNK // DEC_PAD) == 0
    assert past_len % MOBA_BLOCK == 0 and MOBA_BLOCK % page == 0

    pad_rows = lambda a: jnp.pad(a, ((0, 0), (0, DEC_PAD - l_s), (0, 0)))
    xs = pad_rows(x_sample)
    cache_k4 = cache_k.transpose(0, 1, 3, 4, 2)
    cache_v4 = cache_v.transpose(0, 1, 3, 4, 2)
    inv = _rope_frequencies()
    tm_s = min(512, b_s * DEC_PAD)
    rope_p = _rope_table(inv, l_p, 0)
    rope_s = tuple(jnp.tile(tab, (tm_s // DEC_PAD, 1)) for tab in _rope_table(inv, DEC_PAD, past_len))
    xp = x_prompt
    new_p, new_s = [], []
    merge_w = dict(wb=w_branch.astype(BF16), bg=b_gate[:, None, :], wo=w_out.astype(BF16),
                   wpg=w_pg.astype(BF16), bpg=b_pg[:, None, :], wple=w_ple.astype(BF16))
    for i in range(depth):
        lp = dict(_layer_params(i, params), w_layer=i, **merge_w)
        xp, st_p = _layer(xp, p_prompt.reshape(depth, b_p * l_p, D_PLE), i, lp, l_valid=l_p, rope=rope_p,
                          dn_s=jnp.zeros((1, b_p, H_A, DK_A, DV_A), F32),
                          rw_s=jnp.zeros((1, b_p, H_B, N_B, N_B), F32), state_layer=0,
                          dn_buf=jnp.zeros((b_p, CONV_W - 1, QKV_A), F32), rw_prev=jnp.zeros((b_p, SHIFT_B), F32),
                          nseg=1, lt=512, ls=CHUNK, tm=512, attend=_attend_prompt(b_p, l_p))
        new_p.append(st_p)
        xs, st_s = _layer(xs, pad_rows(p_sample[i]).reshape(1, b_s * DEC_PAD, D_PLE), 0, lp, l_valid=l_s, rope=rope_s,
                          dn_s=state_dn, rw_s=state_rwkv, state_layer=i, dn_buf=state_dn_conv[i],
                          rw_prev=state_rwkv_shift[i], nseg=CHUNK // DEC_PAD, lt=DEC_PAD, ls=DEC_PAD, tm=512,
                          attend=_attend_decode(b_s, DEC_PAD, l_s, page_table, cache_k4, cache_v4, i))
        new_s.append(st_s)
    outs_p = [jnp.stack(t) for t in zip(*new_p)]
    outs_s = [jnp.stack(t) for t in zip(*new_s)]
    return (xp, xs[:, :l_s], *outs_p, *outs_s)
```

```python
import functools

import numpy as np
import jax
import jax.numpy as jnp
from jax import lax
from jax.experimental import pallas as pl
from jax.experimental.pallas import tpu as pltpu

F32 = jnp.float32
BF16 = jnp.bfloat16

D_MODEL = 1024
D_PLE = 256
N_BRANCH = 3
W_BRANCH = 512
NORM_EPS = 1e-6
DK_A = 128
DV_A = 128
H_A = 4
QKV_A = 1536
CONV_W = 4
N_B = 64
H_B = 8
LORA = 64
SHIFT_B = 3 * W_BRANCH + 2 * LORA
GN_EPS = 64e-5
D_C = 64
H_C = 8
MOBA_BLOCK = 256
MOBA_TOPK = 3
ROPE_DIM = 16
ROPE_THETA = 500000.0
IN_SPLITS = (QKV_A, W_BRANCH, H_A, H_A, SHIFT_B, W_BRANCH, 3 * W_BRANCH, W_BRANCH, N_BRANCH * D_MODEL)

C_QKV_A = 0
C_Z_A = 1536
C_RKV = 2048
C_Z_B = 3584
C_QKV_C = 4096
C_Z_C = 5632
C_GATE = 6144
C_WLAL = 9216
C_BA = 9344
N_PROJ = 9600

CHUNK = 64
HIST = 8
MAX_CHUNKS_PER_TILE = 8
LANES = 128
VMEM_LIMIT = 62 * 1024 * 1024

NN = (((1,), (0,)), ((), ()))
BNN = (((2,), (1,)), ((0,), (0,)))
NT = (((1,), (1,)), ((), ()))
TN = (((0,), (0,)), ((), ()))
HIGHEST = lax.Precision.HIGHEST


def _mm(a, b, dims=NN):
    return lax.dot_general(a.astype(BF16), b.astype(BF16), dims, preferred_element_type=F32)


def _mm32(a, b, dims=NN):
    return lax.dot_general(a, b, dims, precision=HIGHEST, preferred_element_type=F32)


def _sigmoid(x):
    return 0.5 * jnp.tanh(0.5 * x) + 0.5


def _silu(x):
    return x * _sigmoid(x)


def _softplus(x):
    return jnp.maximum(x, 0.0) + jnp.log(1.0 + jnp.exp(-jnp.abs(x)))


def _iota2(shape, dim):
    return lax.broadcasted_iota(jnp.int32, shape, dim)


def _split(x):
    hi = x.astype(BF16)
    return hi, (x - hi.astype(F32)).astype(BF16)


def _dotb(a, b, dims=NN):
    return lax.dot_general(a, b, dims, preferred_element_type=F32)


def _mm_sel(sel, x):
    hi, lo = _split(x)
    return _dotb(sel, hi) + _dotb(sel, lo)


def _mm3(a, b, dims=NN):
    ah, al = _split(a)
    bh, bl = _split(b)
    return _dotb(ah, bh, dims) + (_dotb(ah, bl, dims) + _dotb(al, bh, dims))


SOLVE_MM = _mm


def _inv_unit_lower(lmats, nilpotent, mm):
    n = lmats[0].shape[-1]
    dims = NN if lmats[0].ndim == 2 else BNN
    eye = (_iota2((n, n), 0) == _iota2((n, n), 1)).astype(F32)
    ps = [-lmat for lmat in lmats]
    accs = [eye + p for p in ps]
    span = 2
    while span < nilpotent:
        ps = [mm(p, p, dims) for p in ps]
        accs = [acc + mm(acc, p, dims) for acc, p in zip(accs, ps)]
        span *= 2
    return accs


def _head_sums(xs, bd):
    n = xs[0].shape[0]
    out = jnp.dot(jnp.concatenate([x.astype(BF16) for x in xs], axis=0), bd, preferred_element_type=F32)
    return [out[i * n:(i + 1) * n] for i in range(len(xs))]


def _head_sum(x, bd):
    return _head_sums([x], bd)[0]


def _proj_kernel(x_ref, g_ref, w_ref, o_ref):
    x = x_ref[...]
    h = x * lax.rsqrt(jnp.mean(x * x, axis=-1, keepdims=True) + NORM_EPS) * g_ref[...]
    o_ref[...] = jnp.dot(h.astype(BF16), w_ref[...], preferred_element_type=F32)


def _proj(x2d, g, w, tm, tn):
    t, d = x2d.shape
    n = w.shape[1]
    return pl.pallas_call(
        _proj_kernel,
        grid=(n // tn, t // tm),
        in_specs=[pl.BlockSpec((tm, d), lambda j, i: (i, 0)),
                  pl.BlockSpec((1, d), lambda j, i: (0, 0)),
                  pl.BlockSpec((d, tn), lambda j, i: (0, j))],
        out_specs=pl.BlockSpec((tm, tn), lambda j, i: (i, j)),
        out_shape=jax.ShapeDtypeStruct((t, n), F32),
        compiler_params=pltpu.CompilerParams(dimension_semantics=("arbitrary", "arbitrary"),
                                             vmem_limit_bytes=VMEM_LIMIT),
    )(x2d, g, w)


def _chunk_masks(n, ls):
    r = _iota2((n, n), 0)
    c = _iota2((n, n), 1)
    same = (r // ls) == (c // ls)
    return same, same & (r >= c), same & (r > c)


def _valid_rows(l_idx, lt, r0, ls, l_valid):
    t = l_idx * lt + r0 + (_iota2((CHUNK, 1), 0) % ls)
    return t < l_valid


def _gdn_parts(qkv_ref, z_ref, ba_ref, buf_ref, s0_ref, cw_ref, pa_ref, pb_ref, on_ref,
               o_ref, sout_ref, xs_ref, st_ref, *, l_idx, nseg, lt, ls, l_valid, l_total):
    nrow = H_A * CHUNK

    @pl.when(l_idx == 0)
    def _():
        xs_ref[...] = buf_ref[...]
        st_ref[...] = s0_ref[...]

    same_r, c_incl, c_strict = _chunk_masks(CHUNK, ls)
    tri_r = c_incl.astype(BF16)
    same_r = same_r.astype(BF16)
    eye_c = (_iota2((CHUNK, CHUNK), 0) == _iota2((CHUNK, CHUNK), 1)).astype(F32)
    cw = cw_ref[...]
    neg_a = -jnp.exp(pa_ref[...])
    pb = pb_ref[...]
    onorm = on_ref[...]

    def conv(s, r0, col):
        cols = slice(col, col + LANES)
        if r0 == 0:
            win = jnp.concatenate([xs_ref[s, :, cols], qkv_ref[s, 0:ls, cols]], axis=0)
        else:
            win = qkv_ref[s, r0 - HIST:r0 + ls, cols]
        w = cw[:, cols]
        acc = pltpu.roll(win, 3, 0)[HIST:] * w[0:1] + pltpu.roll(win, 2, 0)[HIST:] * w[1:2]
        acc = acc + pltpu.roll(win, 1, 0)[HIST:] * w[2:3]
        acc = acc + win[HIST:] * w[3:4]
        return _silu(acc)

    def part(r0, col):
        return jnp.concatenate([conv(s, r0, col) for s in range(nseg)], axis=0)

    def local(c):
        r0 = c * ls
        ba = jnp.concatenate([ba_ref[s, r0:r0 + ls, :] for s in range(nseg)], axis=0)
        beta_all = _sigmoid(ba)
        g_all = neg_a * _softplus(ba + pb)
        if l_valid < l_total:
            ok = _valid_rows(l_idx, lt, r0, ls, l_valid)
            beta_all = jnp.where(ok, beta_all, 0.0)
            g_all = jnp.where(ok, g_all, 0.0)
        gcum = _mm_sel(tri_r, g_all)
        if nseg == 1:
            gtot = jnp.broadcast_to(gcum[CHUNK - 1:CHUNK], gcum.shape)
        else:
            gtot = _mm_sel(same_r, g_all)

        qs, ks, vs, betas, gcs, gts, lmats, qks = [], [], [], [], [], [], [], []
        for h in range(H_A):
            q = part(r0, h * DK_A)
            k = part(r0, H_A * DK_A + h * DK_A)
            v = part(r0, 2 * H_A * DK_A + h * DV_A)
            q = q * lax.rsqrt(jnp.sum(q * q, axis=-1, keepdims=True) + 1e-6) * (DK_A ** -0.5)
            k = k * lax.rsqrt(jnp.sum(k * k, axis=-1, keepdims=True) + 1e-6)
            beta_h = beta_all[:, h:h + 1]
            g_h = gcum[:, H_A + h:H_A + h + 1]
            g_row = jnp.sum(eye_c * g_h, axis=0, keepdims=True)
            decay = jnp.exp(jnp.where(c_incl, g_h - g_row, -jnp.inf))
            kq_k = _mm(jnp.concatenate([k, q], axis=0), k, NT)
            lmats.append(jnp.where(c_strict, beta_h * kq_k[:CHUNK] * decay, 0.0))
            qks.append(jnp.where(c_incl, kq_k[CHUNK:] * decay, 0.0))
            qs.append(q)
            ks.append(k)
            vs.append(v)
            betas.append(beta_h)
            gcs.append(g_h)
            gts.append(gtot[:, H_A + h:H_A + h + 1])
        q_st = jnp.concatenate(qs, axis=0)
        k_st = jnp.concatenate(ks, axis=0)
        v_st = jnp.concatenate(vs, axis=0)
        beta = jnp.concatenate(betas, axis=0)
        g_col = jnp.concatenate(gcs, axis=0)
        gt_col = jnp.concatenate(gts, axis=0)
        return dict(
            r0=r0, gt_col=gt_col,
            lmat=jnp.stack(lmats),
            rhs=jnp.concatenate([v_st * beta, k_st * (beta * jnp.exp(g_col))], axis=1),
            qk=jnp.stack(qks),
            qg=q_st * jnp.exp(g_col),
            kd=k_st * jnp.exp(gt_col - g_col))

    def advance(loc, sol):
        r0, gt_col, qk, qg, kd = loc['r0'], loc['gt_col'], loc['qk'], loc['qg'], loc['kd']
        bv = sol[:, :DV_A]
        wk = sol[:, DV_A:]
        wks, qgs = [], []
        for h in range(H_A):
            for s in range(nseg):
                a = h * CHUNK + s * ls
                both = _mm(jnp.concatenate([wk[a:a + ls], qg[a:a + ls]], axis=0), st_ref[s, h])
                wks.append(both[:ls])
                qgs.append(both[ls:])
        u = bv - jnp.concatenate(wks, axis=0)
        o = jnp.concatenate(qgs, axis=0) + _mm(qk, u.reshape(H_A, CHUNK, DV_A), BNN).reshape(nrow, DV_A)

        for h in range(H_A):
            a = h * CHUNK
            for s in range(nseg):
                b = a + s * ls
                if nseg == 1:
                    kd_s = kd[a:a + CHUNK]
                else:
                    rows = _iota2((CHUNK, 1), 0) // ls
                    kd_s = jnp.where(rows == s, kd[a:a + CHUNK], 0.0)
                upd = _mm(kd_s, u[a:a + CHUNK], TN)
                st_ref[s, h] = st_ref[s, h] * jnp.exp(gt_col[b:b + 1]) + upd
            oh = o[a:a + CHUNK]
            oh = oh * lax.rsqrt(jnp.mean(oh * oh, axis=-1, keepdims=True) + NORM_EPS) * onorm
            for s in range(nseg):
                zz = z_ref[s, r0:r0 + ls, h * DV_A:(h + 1) * DV_A]
                o_ref[s, r0:r0 + ls, h * DV_A:(h + 1) * DV_A] = oh[s * ls:(s + 1) * ls] * _silu(zz)

    def prepare(loc, invs):
        return SOLVE_MM(invs[0], loc['rhs'].reshape(H_A, CHUNK, DK_A + DV_A), BNN).reshape(nrow, DK_A + DV_A)

    def keep_history():
        xs_ref[...] = qkv_ref[:, lt - HIST:lt, :]

    def epilogue():
        sout_ref[...] = st_ref[...]

    return dict(local=local, lmats=lambda loc: [loc['lmat']], prepare=prepare, advance=advance,
                keep_history=keep_history, epilogue=epilogue)


GRP = 4 * N_B


def _rwkv_parts(r_ref, k_ref, v_ref, wa_ref, z_ref, prev_ref, s0_ref, mu_ref, w0_ref, w2_ref,
                a0_ref, a2_ref, kkp_ref, kap_ref, rk_ref, gg_ref, gb_ref, bd_ref,
                o_ref, sout_ref, fs_ref, st_ref, *, l_idx, nseg, lt, ls, l_valid, l_total):
    nrow = 4 * CHUNK
    ngrp = W_BRANCH // GRP

    @pl.when(l_idx == 0)
    def _():
        fs_ref[...] = prev_ref[...]
        st_ref[...] = jnp.zeros(st_ref.shape, F32)
        for s in range(nseg):
            for h in range(H_B):
                g, j = divmod(h, 4)
                st_ref[s, g, j * N_B:(j + 1) * N_B, j * N_B:(j + 1) * N_B] = s0_ref[s, h]

    sources = {0: r_ref, W_BRANCH: k_ref, 2 * W_BRANCH: v_ref, 3 * W_BRANCH: wa_ref}

    same_r, tri_r, _ = _chunk_masks(CHUNK, ls)
    tri_r = tri_r.astype(BF16)
    same_r = same_r.astype(BF16)
    lane_head = _iota2((1, GRP), 1) // N_B
    head_masks = [(lane_head == j).astype(F32) for j in range(4)]
    bd_state = (_iota2((GRP, GRP), 0) // N_B) == (_iota2((GRP, GRP), 1) // N_B)
    head_rows = ((_iota2((nrow, GRP), 0) // CHUNK) == (_iota2((nrow, GRP), 1) // N_B)).astype(F32)
    t_row = _iota2((nrow, 2 * CHUNK), 0) % CHUNK
    t_col = _iota2((nrow, 2 * CHUNK), 1) % CHUNK
    t_same = (t_row // ls) == (t_col // ls)
    t_incl = t_same & (t_col <= t_row)
    t_strict = t_same & (t_col < t_row)
    mu = mu_ref[...]
    bd = bd_ref[...]

    def mixed(r0, col, width):
        outs = []
        for s in range(nseg):
            if r0 == 0:
                win = jnp.concatenate([fs_ref[s, :, col:col + width], sources[col][s, 0:ls, :]], axis=0)
            else:
                win = sources[col][s, r0 - HIST:r0 + ls, :]
            cur = win[HIST:]
            prev = pltpu.roll(win, 1, 0)[HIST:]
            outs.append(cur + (prev - cur) * mu[:, col:col + width])
        return jnp.concatenate(outs, axis=0)

    def stack_masked(x):
        return jnp.concatenate([x * head_masks[j] for j in range(4)], axis=0)

    def unstack(x):
        return x[0:CHUNK] + x[CHUNK:2 * CHUNK] + x[2 * CHUNK:3 * CHUNK] + x[3 * CHUNK:4 * CHUNK]

    def local(c):
        r0 = c * ls
        r = mixed(r0, 0, W_BRANCH)
        k = mixed(r0, W_BRANCH, W_BRANCH)
        v = mixed(r0, 2 * W_BRANCH, W_BRANCH)
        wa = mixed(r0, 3 * W_BRANCH, 2 * LORA)
        w_raw = w0_ref[...] + _mm(jnp.tanh(wa), w2_ref[...])
        logw = -jnp.exp(-_softplus(-w_raw) - 0.5)
        a_sig = _sigmoid(a0_ref[...] + _mm(wa, a2_ref[...]))
        kk = k * kkp_ref[...]
        k_mod = k * (1.0 + (a_sig - 1.0) * kap_ref[...])
        kk_ss, rk_sum = _head_sums([kk * kk, r * k_mod * rk_ref[...]], bd)
        kk = kk * lax.rsqrt(kk_ss + 1e-6)
        b_vec = kk * a_sig
        if l_valid < l_total:
            ok = _valid_rows(l_idx, lt, r0, ls, l_valid)
            logw = jnp.where(ok, logw, 0.0)
            b_vec = jnp.where(ok, b_vec, 0.0)
            k_mod = jnp.where(ok, k_mod, 0.0)
        gcum = _mm_sel(tri_r, logw)
        if nseg == 1:
            gtot = jnp.broadcast_to(gcum[CHUNK - 1:CHUNK], gcum.shape)
        else:
            gtot = _mm_sel(same_r, logw)
        e_neg = jnp.exp(-gcum)
        e_end = jnp.exp(gtot - gcum)
        a_t = -kk * jnp.exp(gcum - logw)
        b_t = b_vec * e_neg
        k_t = k_mod * e_neg
        r_t = r * jnp.exp(gcum)
        b_h = b_vec * e_end
        k_h = k_mod * e_end

        groups = []
        for g in range(ngrp):
            sl = slice(g * GRP, (g + 1) * GRP)
            a_g, b_g, k_g, r_g, v_g = a_t[:, sl], b_t[:, sl], k_t[:, sl], r_t[:, sl], v[:, sl]
            a_st = stack_masked(a_g)
            r_st = stack_masked(r_g)
            both = _mm(jnp.concatenate([a_st, r_st], axis=0), jnp.concatenate([b_g, k_g], axis=0), NT)
            a_bk = jnp.where(t_strict, both[:nrow], 0.0)
            r_bk = jnp.where(t_incl, both[nrow:], 0.0)
            zero_v = jnp.concatenate([jnp.zeros_like(v_g), v_g], axis=0)
            groups.append(dict(
                a_g=a_g, r_g=r_g, v_g=v_g, b_h=b_h[:, sl], k_h=k_h[:, sl], gtot=gtot[:, sl],
                neg_aab=-a_bk[:, :CHUNK].reshape(4, CHUNK, CHUNK),
                kv_u=_mm(a_bk, zero_v) * head_rows, r_bk=r_bk))
        return dict(r0=r0, v=v, rk_sum=rk_sum, groups=groups)

    def advance(loc, tinvs):
        grps = loc['groups']
        ams, rms = [], []
        for g, grp in enumerate(grps):
            parts = [_mm(jnp.concatenate([grp['a_g'][s * ls:(s + 1) * ls], grp['r_g'][s * ls:(s + 1) * ls]], axis=0),
                         st_ref[s, g], NT) for s in range(nseg)]
            ams.append(jnp.concatenate([p[:ls] for p in parts], axis=0))
            rms.append(jnp.concatenate([p[ls:] for p in parts], axis=0))
        us = [unstack(SOLVE_MM(tinv, (stack_masked(am) + grp['kv_u']).reshape(4, CHUNK, GRP), BNN)
                      .reshape(nrow, GRP)) for grp, tinv, am in zip(grps, tinvs, ams)]
        ys = [rm + unstack(_mm(grp['r_bk'], jnp.concatenate([u, grp['v_g']], axis=0)) * head_rows)
              for grp, u, rm in zip(grps, us, rms)]
        for g, (grp, u) in enumerate(zip(grps, us)):
            v_g = grp['v_g']
            for s in range(nseg):
                if nseg == 1:
                    u_s, v_s = u, v_g
                else:
                    rows = _iota2((CHUNK, 1), 0) // ls
                    u_s = jnp.where(rows == s, u, 0.0)
                    v_s = jnp.where(rows == s, v_g, 0.0)
                upd = _mm(jnp.concatenate([u_s, v_s], axis=0),
                          jnp.concatenate([grp['b_h'], grp['k_h']], axis=0), TN)
                dec = jnp.exp(grp['gtot'][s * ls:s * ls + 1])
                st_ref[s, g] = st_ref[s, g] * dec + jnp.where(bd_state, upd, 0.0)
        y = jnp.concatenate(ys, axis=1)
        mean = _head_sums([y], bd)[0] * (1.0 / N_B)
        yc = y - mean
        var = _head_sums([yc * yc], bd)[0] * (1.0 / N_B)
        yn = yc * lax.rsqrt(var + GN_EPS) * gg_ref[...] + gb_ref[...]
        out = yn + loc['rk_sum'] * loc['v']
        for s in range(nseg):
            zz = z_ref[s, loc['r0']:loc['r0'] + ls, :]
            o_ref[s, loc['r0']:loc['r0'] + ls, :] = out[s * ls:(s + 1) * ls] * _silu(zz)

    def keep_history():
        for col, ref in sources.items():
            fs_ref[:, :, col:col + ref.shape[-1]] = ref[:, lt - HIST:lt, :]

    def epilogue():
        for s in range(nseg):
            for h in range(H_B):
                g, j = divmod(h, 4)
                sout_ref[s, h] = st_ref[s, g, j * N_B:(j + 1) * N_B, j * N_B:(j + 1) * N_B]

    return dict(local=local, lmats=lambda loc: [grp['neg_aab'] for grp in loc['groups']],
                prepare=lambda loc, invs: invs, advance=advance, keep_history=keep_history, epilogue=epilogue)


N_GDN_IN, N_RWKV_IN = 9, 18


def _mixers_kernel(*refs, nseg, lt, ls, l_valid, l_total):
    gdn_in = refs[:N_GDN_IN]
    rwkv_in = refs[N_GDN_IN:N_GDN_IN + N_RWKV_IN]
    oa_ref, dn_ref, ob_ref, rw_ref, xs_ref, st_a_ref, fs_ref, st_b_ref = refs[N_GDN_IN + N_RWKV_IN:]
    l_idx = pl.program_id(1)
    cfg = dict(l_idx=l_idx, nseg=nseg, lt=lt, ls=ls, l_valid=l_valid, l_total=l_total)
    mixers = [_gdn_parts(*gdn_in, oa_ref, dn_ref, xs_ref, st_a_ref, **cfg),
              _rwkv_parts(*rwkv_in, ob_ref, rw_ref, fs_ref, st_b_ref, **cfg)]
    nchunk = lt // ls
    assert nchunk <= MAX_CHUNKS_PER_TILE

    def tile():
        chunks = range(nchunk)
        locs = [[m['local'](c) for c in chunks] for m in mixers]
        lmats = [[m['lmats'](loc) for loc in mlocs] for m, mlocs in zip(mixers, locs)]
        invs = _inv_unit_lower([x for per_mixer in lmats for per_loc in per_mixer for x in per_loc], ls, SOLVE_MM)
        prepared, at = [], 0
        for m, mlocs, mlm in zip(mixers, locs, lmats):
            row = []
            for loc, lm in zip(mlocs, mlm):
                row.append(m['prepare'](loc, invs[at:at + len(lm)]))
                at += len(lm)
            prepared.append(row)
        for j in chunks:
            for m, mlocs, mprep in zip(mixers, locs, prepared):
                m['advance'](mlocs[j], mprep[j])
        for m in mixers:
            m['keep_history']()

    tile()

    @pl.when(l_idx == pl.num_programs(1) - 1)
    def _():
        for m in mixers:
            m['epilogue']()


def _mixers(proj3, buf8, dn_s0, prev8, rw_s0, layer, lp, *, nseg, lt, ls, l_valid):
    b, l, _ = proj3.shape
    ngrp = W_BRANCH // GRP
    kern = functools.partial(_mixers_kernel, nseg=nseg, lt=lt, ls=ls, l_valid=l_valid, l_total=l)
    full = lambda shape: pl.BlockSpec(shape, lambda i, j: (0,) * len(shape))
    blk = lambda col, w: pl.BlockSpec((nseg, lt, w), lambda i, j: (i, j, col // w))
    mu, w0, a0, kkp, kap, rk, gg, gb = lp['rw_vecs']
    gdn_specs = [blk(C_QKV_A, QKV_A), blk(C_Z_A, W_BRANCH), blk(C_BA, LANES),
                 pl.BlockSpec((nseg, HIST, QKV_A), lambda i, j: (i, 0, 0)),
                 pl.BlockSpec((None, nseg, H_A, DK_A, DV_A), lambda i, j: (layer, i, 0, 0, 0)),
                 full((CONV_W, QKV_A)), full((1, LANES)), full((1, LANES)), full((1, DV_A))]
    rwkv_specs = [blk(C_RKV, W_BRANCH), blk(C_RKV + W_BRANCH, W_BRANCH), blk(C_RKV + 2 * W_BRANCH, W_BRANCH),
                  blk(C_WLAL, 2 * LORA), blk(C_Z_B, W_BRANCH),
                  pl.BlockSpec((nseg, HIST, SHIFT_B), lambda i, j: (i, 0, 0)),
                  pl.BlockSpec((None, nseg, H_B, N_B, N_B), lambda i, j: (layer, i, 0, 0, 0)),
                  full((1, SHIFT_B)), full((1, W_BRANCH)), full((2 * LORA, W_BRANCH)),
                  full((1, W_BRANCH)), full((2 * LORA, W_BRANCH)), full((1, W_BRANCH)),
                  full((1, W_BRANCH)), full((1, W_BRANCH)), full((1, W_BRANCH)), full((1, W_BRANCH)),
                  full((W_BRANCH, W_BRANCH))]
    assert len(gdn_specs) == N_GDN_IN and len(rwkv_specs) == N_RWKV_IN
    tile_out = pl.BlockSpec((nseg, lt, W_BRANCH), lambda i, j: (i, j, 0))
    return pl.pallas_call(
        kern,
        grid=(b // nseg, l // lt),
        in_specs=gdn_specs + rwkv_specs,
        out_specs=[tile_out, pl.BlockSpec((nseg, H_A, DK_A, DV_A), lambda i, j: (i, 0, 0, 0)),
                   tile_out, pl.BlockSpec((nseg, H_B, N_B, N_B), lambda i, j: (i, 0, 0, 0))],
        out_shape=[jax.ShapeDtypeStruct((b, l, W_BRANCH), F32), jax.ShapeDtypeStruct((b, H_A, DK_A, DV_A), F32),
                   jax.ShapeDtypeStruct((b, l, W_BRANCH), F32), jax.ShapeDtypeStruct((b, H_B, N_B, N_B), F32)],
        scratch_shapes=[pltpu.VMEM((nseg, HIST, QKV_A), F32), pltpu.VMEM((nseg, H_A, DK_A, DV_A), F32),
                        pltpu.VMEM((nseg, HIST, SHIFT_B), F32), pltpu.VMEM((nseg, ngrp, GRP, GRP), F32)],
        compiler_params=pltpu.CompilerParams(dimension_semantics=("arbitrary", "arbitrary"),
                                             vmem_limit_bytes=VMEM_LIMIT),
    )(proj3, proj3, proj3, buf8, dn_s0, lp['conv'], lp['pa'], lp['pb'], lp['onorm'],
      proj3, proj3, proj3, proj3, proj3, prev8, rw_s0, mu, w0, lp['w2p'], a0, lp['a2p'], kkp, kap, rk, gg, gb,
      lp['bd'])


def _rope_table_kernel(inv_ref, cos_ref, sin_ref, *, pos0):
    n = cos_ref.shape[0]
    pos = (pos0 + pl.program_id(0) * n + _iota2((n, 1), 0)).astype(F32)
    ang = pos * inv_ref[...]
    cos_ref[...] = jnp.cos(ang)
    sin_ref[...] = jnp.sin(ang)


def _rope_table(inv, n, pos0, tile=512):
    tile = min(tile, n)
    blk = pl.BlockSpec((tile, LANES), lambda i: (i, 0))
    return pl.pallas_call(
        functools.partial(_rope_table_kernel, pos0=pos0),
        grid=(n // tile,),
        in_specs=[pl.BlockSpec((1, LANES), lambda i: (0, 0))],
        out_specs=[blk, blk],
        out_shape=[jax.ShapeDtypeStruct((n, LANES), F32)] * 2,
        compiler_params=pltpu.CompilerParams(dimension_semantics=("arbitrary",), vmem_limit_bytes=VMEM_LIMIT),
    )(inv)


def _qkprep_kernel(q_ref, k_ref, qn_ref, kn_ref, cos_ref, sin_ref, bd_ref, qo_ref, ko_ref):
    cos = cos_ref[...]
    sin = sin_ref[...]
    d = _iota2((1, LANES), 1) % D_C
    half = ROPE_DIM // 2
    c_main = jnp.where(d < ROPE_DIM, cos, 1.0)
    c_up = jnp.where(d < half, -sin, 0.0)
    c_dn = jnp.where((d >= half) & (d < ROPE_DIM), sin, 0.0)
    bd = bd_ref[...]

    def one(x_ref, g_ref, out_ref):
        x = x_ref[...]
        y = x * lax.rsqrt(_head_sum(x * x, bd) * (1.0 / D_C) + NORM_EPS) * g_ref[...]
        for blk in range(W_BRANCH // LANES):
            yb = y[:, blk * LANES:(blk + 1) * LANES]
            up = pltpu.roll(yb, LANES - half, 1)
            dn = pltpu.roll(yb, half, 1)
            out_ref[:, blk * LANES:(blk + 1) * LANES] = yb * c_main + up * c_up + dn * c_dn

    one(q_ref, qn_ref, qo_ref)
    one(k_ref, kn_ref, ko_ref)


def _qkprep(proj2, qn, kn, rope, bd, *, tm):
    t = proj2.shape[0]
    cos, sin = rope
    period = cos.shape[0] // tm
    full = lambda shape: pl.BlockSpec(shape, lambda i: (0,) * len(shape))
    table = pl.BlockSpec((tm, LANES), lambda i: (i % period, 0))
    return pl.pallas_call(
        _qkprep_kernel,
        grid=(t // tm,),
        in_specs=[pl.BlockSpec((tm, W_BRANCH), lambda i: (i, C_QKV_C // W_BRANCH)),
                  pl.BlockSpec((tm, W_BRANCH), lambda i: (i, C_QKV_C // W_BRANCH + 1)),
                  full((1, W_BRANCH)), full((1, W_BRANCH)), table, table, full((W_BRANCH, W_BRANCH))],
        out_specs=[pl.BlockSpec((tm, W_BRANCH), lambda i: (i, 0)),
                   pl.BlockSpec((tm, W_BRANCH), lambda i: (i, 0))],
        out_shape=[jax.ShapeDtypeStruct((t, W_BRANCH), F32), jax.ShapeDtypeStruct((t, W_BRANCH), F32)],
        compiler_params=pltpu.CompilerParams(dimension_semantics=("arbitrary",), vmem_limit_bytes=VMEM_LIMIT),
    )(proj2, proj2, qn, kn, cos, sin, bd)


def _rank_select(gate, nblk, limit, axis):
    idx = _iota2(gate.shape, axis)
    rank = jnp.zeros(gate.shape, F32)
    for m in range(nblk):
        gm = gate[m:m + 1, :] if axis == 0 else gate[:, m:m + 1]
        beats = (gm > gate) | ((gm == gate) & (m < idx))
        rank = rank + jnp.where(beats & (m < limit), 1.0, 0.0)
    return jnp.where((idx < limit) & (rank < MOBA_TOPK), 1.0, 0.0)


HEADS_PER_STEP = LANES // D_C


def _moba_prompt_kernel(q_ref, k_ref, v_ref, o_ref, vo_ref, *, nblk):
    blk = MOBA_BLOCK
    l = nblk * blk
    scale = D_C ** -0.5
    k = k_ref[...]
    q_t = q_ref[...].T
    v = v_ref[...]
    vo_ref[...] = v
    v_t = v.T.astype(BF16)
    kmean = jnp.concatenate([jnp.mean(k[n * blk:(n + 1) * blk], axis=0, keepdims=True) for n in range(nblk)],
                            axis=0)
    own = _iota2((1, l), 1) // blk
    kb = k.astype(BF16)
    causal = _iota2((blk, blk), 0) <= _iota2((blk, blk), 1)
    row_head = _iota2((LANES, 1), 0) // D_C
    units = [(e, j) for e in range(HEADS_PER_STEP) for j in range(nblk)]
    sels, qbs = [], []
    for e in range(HEADS_PER_STEP):
        q_e = jnp.where(row_head == e, q_t, 0.0)
        sels.append(_rank_select(_mm32(kmean, q_e), nblk, own, 0))
        qbs.append((q_e * scale).astype(BF16))
    scores = [jnp.dot(kb[:(j + 1) * blk], qbs[e][:, j * blk:(j + 1) * blk], preferred_element_type=F32)
              for e, j in units]
    probs, denoms = [], []
    for (e, j), s in zip(units, scores):
        cols = slice(j * blk, (j + 1) * blk)
        parts = [jnp.where(sels[e][n:n + 1, cols] > 0.0, s[n * blk:(n + 1) * blk], -jnp.inf) for n in range(j)]
        parts.append(jnp.where(causal, s[j * blk:], -jnp.inf))
        s = jnp.concatenate(parts, axis=0)
        p = jnp.exp(s - jnp.max(s, axis=0, keepdims=True))
        denoms.append(jnp.sum(p, axis=0, keepdims=True))
        probs.append(p.astype(BF16))
    tiles = [jnp.dot(v_t[e * D_C:(e + 1) * D_C, :(j + 1) * blk], p, preferred_element_type=F32) / d
             for (e, j), p, d in zip(units, probs, denoms)]
    outs = [jnp.concatenate(tiles[e * nblk:(e + 1) * nblk], axis=1) for e in range(HEADS_PER_STEP)]
    o_ref[...] = jnp.concatenate(outs, axis=0).T


def _moba_prompt(q3, k3, proj3):
    b, l, _ = q3.shape
    nblk = l // MOBA_BLOCK
    assert (4.0 ** round(np.log(D_C) / np.log(4.0))) == D_C
    kern = functools.partial(_moba_prompt_kernel, nblk=nblk)
    v_col = (C_QKV_C + 2 * W_BRANCH) // LANES
    cols = pl.BlockSpec((None, l, LANES), lambda i, hp: (i, 0, hp))
    return pl.pallas_call(
        kern,
        grid=(b, W_BRANCH // LANES),
        in_specs=[cols, cols, pl.BlockSpec((None, l, LANES), lambda i, hp: (i, 0, v_col + hp))],
        out_specs=[cols, cols],
        out_shape=[jax.ShapeDtypeStruct((b, l, W_BRANCH), F32)] * 2,
        compiler_params=pltpu.CompilerParams(dimension_semantics=("arbitrary", "arbitrary"),
                                             vmem_limit_bytes=VMEM_LIMIT),
    )(q3, k3, proj3)


SEQ_PER_STEP = 2


def _moba_decode_kernel(pt_ref, q_ref, kn_ref, vn_ref, *refs, npage, page, lq):
    nseq = q_ref.shape[0]
    k_refs = refs[:nseq * npage]
    v_refs = refs[nseq * npage:2 * nseq * npage]
    o_ref = refs[2 * nseq * npage]
    del pt_ref
    scale = D_C ** -0.5
    nrow = lq * H_C
    nblk = npage * page // MOBA_BLOCK
    lp = q_ref.shape[1]
    seqs = range(nseq)

    lane_head = _iota2((H_C, W_BRANCH), 1) // D_C
    own_head = (lane_head == _iota2((H_C, W_BRANCH), 0)).astype(F32)
    own_rows = jnp.concatenate([own_head] * lq, axis=0)
    lane = _iota2((1, LANES), 1)
    t_key = _iota2((nrow, LANES), 1)
    t_qry = _iota2((nrow, LANES), 0) // H_C
    pad = jnp.zeros((LANES - lp, W_BRANCH), F32)

    qbds = [jnp.concatenate([q_ref[r, t:t + 1, :] * own_head for t in range(lq)], axis=0) for r in seqs]
    qbs = [(qbd * scale).astype(BF16) for qbd in qbds]
    k_ts = [jnp.concatenate([k_refs[r * npage + i][...].reshape(W_BRANCH, page) for i in range(npage)], axis=1)
            for r in seqs]
    v_ts = [jnp.concatenate([v_refs[r * npage + i][...].reshape(W_BRANCH, page) for i in range(npage)], axis=1)
            for r in seqs]
    s_pasts = [jnp.dot(qb, k_t.astype(BF16), preferred_element_type=F32) for qb, k_t in zip(qbs, k_ts)]

    def block_key_sums(k_t):
        ksum = jnp.zeros((W_BRANCH, LANES), F32)
        for n in range(nblk):
            k_n = k_t[:, n * MOBA_BLOCK:(n + 1) * MOBA_BLOCK]
            folded = sum(k_n[:, i * LANES:(i + 1) * LANES] for i in range(MOBA_BLOCK // LANES))
            ksum = ksum + jnp.sum(folded, axis=-1, keepdims=True) * (lane == n).astype(F32)
        return ksum

    ksums = [block_key_sums(k_t) for k_t in k_ts]
    gates = [_mm3(qbd, ksum) * (1.0 / MOBA_BLOCK) for qbd, ksum in zip(qbds, ksums)]
    sels = [_rank_select(gate, nblk, nblk, 1) for gate in gates]
    s_pasts = [jnp.concatenate(
        [jnp.where(sel[:, n:n + 1] > 0.0, s[:, n * MOBA_BLOCK:(n + 1) * MOBA_BLOCK], -jnp.inf)
         for n in range(nblk)], axis=1) for sel, s in zip(sels, s_pasts)]

    k_news = [jnp.concatenate([kn_ref[r], pad], axis=0) for r in seqs]
    v_news = [jnp.concatenate([vn_ref[r], pad], axis=0) for r in seqs]
    s_owns = [jnp.where(t_key <= t_qry, lax.dot_general(qb, k_new.astype(BF16), NT, preferred_element_type=F32),
                        -jnp.inf) for qb, k_new in zip(qbs, k_news)]
    ms = [jnp.maximum(jnp.max(sp, axis=-1, keepdims=True), jnp.max(so, axis=-1, keepdims=True))
          for sp, so in zip(s_pasts, s_owns)]
    p_pasts = [jnp.exp(sp - m) for sp, m in zip(s_pasts, ms)]
    p_owns = [jnp.exp(so - m) for so, m in zip(s_owns, ms)]
    denoms = [jnp.sum(pp, axis=-1, keepdims=True) + jnp.sum(po, axis=-1, keepdims=True)
              for pp, po in zip(p_pasts, p_owns)]
    accs = [lax.dot_general(pp.astype(BF16), v_t.astype(BF16), NT, preferred_element_type=F32)
            for pp, v_t in zip(p_pasts, v_ts)]
    accs = [acc + jnp.dot(po.astype(BF16), v_new.astype(BF16), preferred_element_type=F32)
            for acc, po, v_new in zip(accs, p_owns, v_news)]
    for r in seqs:
        out = accs[r] / denoms[r]
        out = (out * own_rows).reshape(lq, H_C, W_BRANCH).sum(axis=1)
        o_ref[r] = jnp.concatenate([out, jnp.zeros((lp - lq, W_BRANCH), F32)], axis=0)


def _moba_decode(page_table, q3, k3, v3, cache_k, cache_v, layer, lq):
    b, lp, _ = q3.shape
    npage = page_table.shape[1]
    page = cache_k.shape[-1]
    nseq = SEQ_PER_STEP
    kern = functools.partial(_moba_decode_kernel, npage=npage, page=page, lq=lq)
    tok = pl.BlockSpec((nseq, lp, W_BRANCH), lambda i, pt: (i, 0, 0))

    def page_spec(r, p_i):
        return pl.BlockSpec((None, None, H_C, D_C, page), lambda i, pt: (layer, pt[i * nseq + r, p_i], 0, 0, 0))

    pages = [page_spec(r, p_i) for r in range(nseq) for p_i in range(npage)]
    grid_spec = pltpu.PrefetchScalarGridSpec(
        num_scalar_prefetch=1,
        grid=(b // nseq,),
        in_specs=[tok, tok, tok] + pages * 2,
        out_specs=pl.BlockSpec((nseq, lp, W_BRANCH), lambda i, pt: (i, 0, 0)),
    )
    return pl.pallas_call(
        kern,
        grid_spec=grid_spec,
        out_shape=jax.ShapeDtypeStruct((b, lp, W_BRANCH), F32),
        compiler_params=pltpu.CompilerParams(dimension_semantics=("arbitrary",), vmem_limit_bytes=VMEM_LIMIT),
    )(page_table, q3, k3, v3, *([cache_k] * (nseq * npage)), *([cache_v] * (nseq * npage)))


def _merge_kernel(oa_ref, ob_ref, oc_ref, zc_ref, gate_ref, x_ref, ple_ref, wb_ref, bg_ref, wo_ref,
                  wpg_ref, bpg_ref, wple_ref, y_ref):
    branches = (oa_ref[...], ob_ref[...], oc_ref[...] * _silu(zc_ref[...]))
    up = None
    for n, o in enumerate(branches):
        sl = slice(n * D_MODEL, (n + 1) * D_MODEL)
        gate = _sigmoid(gate_ref[:, sl] + bg_ref[:, sl])
        term = gate * jnp.dot(o.astype(BF16), wb_ref[n], preferred_element_type=F32)
        up = term if up is None else up + term
    y = x_ref[...] + jnp.dot(up.astype(BF16), wo_ref[...], preferred_element_type=F32)
    pg = _sigmoid(jnp.dot(y.astype(BF16), wpg_ref[...], preferred_element_type=F32) + bpg_ref[...])
    y_ref[...] = y + pg * jnp.dot(ple_ref[...].astype(BF16), wple_ref[...], preferred_element_type=F32)


def _merge(oa, ob, oc, proj2, x2, ple3, ple_layer, wb, bg, wo, wpg, bpg, wple, w_layer, tm):
    t = x2.shape[0]
    full = lambda shape: pl.BlockSpec((None,) + shape, lambda i: (w_layer,) + (0,) * len(shape))
    row = lambda w: pl.BlockSpec((tm, w), lambda i: (i, 0))
    return pl.pallas_call(
        _merge_kernel,
        grid=(t // tm,),
        in_specs=[row(W_BRANCH), row(W_BRANCH), row(W_BRANCH),
                  pl.BlockSpec((tm, W_BRANCH), lambda i: (i, C_Z_C // W_BRANCH)),
                  pl.BlockSpec((tm, N_BRANCH * D_MODEL), lambda i: (i, C_GATE // (N_BRANCH * D_MODEL))),
                  row(D_MODEL), pl.BlockSpec((None, tm, D_PLE), lambda i: (ple_layer, i, 0)),
                  full((N_BRANCH, W_BRANCH, D_MODEL)), full((1, N_BRANCH * D_MODEL)), full((D_MODEL, D_MODEL)),
                  full((D_MODEL, D_MODEL)), full((1, D_MODEL)), full((D_PLE, D_MODEL))],
        out_specs=row(D_MODEL),
        out_shape=jax.ShapeDtypeStruct((t, D_MODEL), F32),
        compiler_params=pltpu.CompilerParams(dimension_semantics=("arbitrary",), vmem_limit_bytes=VMEM_LIMIT),
    )(oa, ob, oc, proj2, proj2, x2, ple3, wb, bg, wo, wpg, bpg, wple)


def _arrange_w_in(w):
    offs = np.cumsum((0,) + IN_SPLITS)
    qkv_a, z_a, beta, alpha, f_b, z_b, qkv_c, z_c, gate = [w[:, offs[i]:offs[i + 1]] for i in range(9)]
    d = w.shape[0]
    ba = jnp.concatenate([beta, alpha, jnp.zeros((d, LANES - 2 * H_A), w.dtype)], axis=1)
    tail = jnp.zeros((d, N_PROJ - C_BA - LANES), w.dtype)
    out = jnp.concatenate([qkv_a, z_a, f_b[:, :3 * W_BRANCH], z_b, qkv_c, z_c, gate,
                           f_b[:, 3 * W_BRANCH:], ba, tail], axis=1)
    return out.astype(BF16)


def _rope_frequencies():
    half = ROPE_DIM // 2
    inv = ROPE_THETA ** (-jnp.arange(half, dtype=F32) / half)
    d = np.arange(LANES) % D_C
    return jnp.where(d < ROPE_DIM, inv[d % half], 0.0).reshape(1, LANES)


def _layer_params(i, p):
    row = lambda v: v.reshape(1, -1)
    lanes_ba = lambda v: jnp.zeros((1, LANES), F32).at[0, H_A:2 * H_A].set(v)
    lora_pad = jnp.zeros((LORA, W_BRANCH), F32)
    head = np.arange(W_BRANCH) // N_B
    return dict(
        norm_g=row(p['norm_g'][i]),
        w_in=_arrange_w_in(p['w_in'][i]),
        conv=p['conv_a'][i],
        pa=lanes_ba(p['a_log'][i]), pb=lanes_ba(p['dt_bias'][i]), onorm=row(p['onorm_a'][i]),
        rw_vecs=(row(p['mu_b'][i]), row(p['w0_b'][i]), row(p['a0_b'][i]), row(p['kk_b'][i]), row(p['ka_b'][i]),
                 row(p['rk_b'][i]), row(p['gn_g_b'][i]), row(p['gn_b_b'][i])),
        w2p=jnp.concatenate([p['w2_b'][i], lora_pad], axis=0).astype(BF16),
        a2p=jnp.concatenate([lora_pad, p['a2_b'][i]], axis=0).astype(BF16),
        bd=jnp.asarray(head[:, None] == head[None, :], BF16),
        qn=row(jnp.tile(p['qn_c'][i], H_C)), kn=row(jnp.tile(p['kn_c'][i], H_C)),
    )


def _hist_rows(rows):
    b, n, c = rows.shape
    return jnp.concatenate([jnp.zeros((b, HIST - n, c), F32), rows], axis=1)


def _layer(x3, ple3, ple_layer, lp, *, l_valid, rope, dn_s, rw_s, state_layer, dn_buf, rw_prev, nseg, lt, ls, tm, attend):
    b, l, d = x3.shape
    t = b * l
    proj2 = _proj(x3.reshape(t, d), lp['norm_g'], lp['w_in'], min(t, 1024), 3200)
    proj3 = proj2.reshape(b, l, N_PROJ)

    oa, dn_new, ob, rw_new = _mixers(proj3, _hist_rows(dn_buf), dn_s, _hist_rows(rw_prev[:, None, :]), rw_s,
                                     state_layer, lp, nseg=nseg, lt=lt, ls=ls, l_valid=l_valid)
    qr, kr = _qkprep(proj2, lp['qn'], lp['kn'], rope, lp['bd'], tm=min(tm, t))
    oc, v2 = attend(qr, kr, proj3)

    y2 = _merge(oa.reshape(t, W_BRANCH), ob.reshape(t, W_BRANCH), oc.reshape(t, W_BRANCH), proj2,
                x3.reshape(t, d), ple3, ple_layer, lp['wb'], lp['bg'], lp['wo'], lp['wpg'],
                lp['bpg'], lp['wple'], lp['w_layer'], min(tm, t))

    f_last = jnp.concatenate([proj3[:, l_valid - 1, C_RKV:C_RKV + 3 * W_BRANCH],
                              proj3[:, l_valid - 1, C_WLAL:C_WLAL + 2 * LORA]], axis=-1)
    new_state = (kr.reshape(b, l, H_C, D_C)[:, :l_valid], v2.reshape(b, l, H_C, D_C)[:, :l_valid], dn_new,
                 proj3[:, l_valid - (CONV_W - 1):l_valid, C_QKV_A:C_QKV_A + QKV_A],
                 rw_new, f_last)
    return y2.reshape(b, l, d), new_state


def _attend_prompt(b, l):
    def attend(qr, kr, proj3):
        shp = (b, l, W_BRANCH)
        o3, v3 = _moba_prompt(qr.reshape(shp), kr.reshape(shp), proj3)
        return o3.reshape(b * l, W_BRANCH), v3.reshape(b * l, W_BRANCH)
    return attend


def _attend_decode(b, lp_, lq, page_table, cache_k4, cache_v4, layer):
    def attend(qr, kr, proj3):
        shp = (b, lp_, W_BRANCH)
        v3 = proj3[:, :, C_QKV_C + 2 * W_BRANCH:C_QKV_C + 3 * W_BRANCH]
        o3 = _moba_decode(page_table, qr.reshape(shp), kr.reshape(shp), v3, cache_k4, cache_v4, layer, lq)
        return o3.reshape(b * lp_, W_BRANCH), v3.reshape(b * lp_, W_BRANCH)
    return attend


DEC_PAD = 8


def kernel(x_prompt, x_sample, cache_k, cache_v, state_dn, state_dn_conv, state_rwkv, state_rwkv_shift,
           page_table, p_prompt, p_sample, norm_g, w_in, conv_a, a_log, dt_bias, onorm_a, mu_b, w0_b, w2_b,
           a0_b, a2_b, kk_b, ka_b, rk_b, gn_g_b, gn_b_b, qn_c, kn_c, w_branch, b_gate, w_out, w_ple, w_pg,
           b_pg):
    params = dict(norm_g=norm_g, w_in=w_in, conv_a=conv_a, a_log=a_log, dt_bias=dt_bias, onorm_a=onorm_a,
                  mu_b=mu_b, w0_b=w0_b, w2_b=w2_b, a0_b=a0_b, a2_b=a2_b, kk_b=kk_b, ka_b=ka_b, rk_b=rk_b,
                  gn_g_b=gn_g_b, gn_b_b=gn_b_b, qn_c=qn_c, kn_c=kn_c, w_branch=w_branch, b_gate=b_gate,
                  w_out=w_out, w_ple=w_ple, w_pg=w_pg, b_pg=b_pg)
    depth = norm_g.shape[0]
    b_p, l_p, _ = x_prompt.shape
    b_s, l_s, _ = x_sample.shape
    n_phys, page = cache_k.shape[1], cache_k.shape[2]
    past_len = page_table.shape[1] * page
    assert l_p % 512 == 0 and CONV_W - 1 <= l_s <= DEC_PAD and b_s % (CHUNK // DEC_PAD) == 0
    assert past_len % MOBA_BLOCK == 0 and MOBA_BLOCK % page == 0

    pad_rows = lambda a: jnp.pad(a, ((0, 0), (0, DEC_PAD - l_s), (0, 0)))
    xs = pad_rows(x_sample)
    cache_k4 = cache_k.transpose(0, 1, 3, 4, 2)
    cache_v4 = cache_v.transpose(0, 1, 3, 4, 2)
    inv = _rope_frequencies()
    tm_s = min(512, b_s * DEC_PAD)
    rope_p = _rope_table(inv, l_p, 0)
    rope_s = tuple(jnp.tile(tab, (tm_s // DEC_PAD, 1)) for tab in _rope_table(inv, DEC_PAD, past_len))
    xp = x_prompt
    new_p, new_s = [], []
    merge_w = dict(wb=w_branch.astype(BF16), bg=b_gate[:, None, :], wo=w_out.astype(BF16),
                   wpg=w_pg.astype(BF16), bpg=b_pg[:, None, :], wple=w_ple.astype(BF16))
    for i in range(depth):
        lp = dict(_layer_params(i, params), w_layer=i, **merge_w)
        xp, st_p = _layer(xp, p_prompt.reshape(depth, b_p * l_p, D_PLE), i, lp, l_valid=l_p, rope=rope_p,
                          dn_s=jnp.zeros((1, b_p, H_A, DK_A, DV_A), F32),
                          rw_s=jnp.zeros((1, b_p, H_B, N_B, N_B), F32), state_layer=0,
                          dn_buf=jnp.zeros((b_p, CONV_W - 1, QKV_A), F32), rw_prev=jnp.zeros((b_p, SHIFT_B), F32),
                          nseg=1, lt=512, ls=CHUNK, tm=512, attend=_attend_prompt(b_p, l_p))
        new_p.append(st_p)
        xs, st_s = _layer(xs, pad_rows(p_sample[i]).reshape(1, b_s * DEC_PAD, D_PLE), 0, lp, l_valid=l_s, rope=rope_s,
                          dn_s=state_dn, rw_s=state_rwkv, state_layer=i, dn_buf=state_dn_conv[i],
                          rw_prev=state_rwkv_shift[i], nseg=CHUNK // DEC_PAD, lt=DEC_PAD, ls=DEC_PAD, tm=512,
                          attend=_attend_decode(b_s, DEC_PAD, l_s, page_table, cache_k4, cache_v4, i))
        new_s.append(st_s)
    outs_p = [jnp.stack(t) for t in zip(*new_p)]
    outs_s = [jnp.stack(t) for t in zip(*new_s)]
    return (xp, xs[:, :l_s], *outs_p, *outs_s)
```
